```python
import math
import jax
import jax.numpy as jnp
from jax import lax
import numpy as np

D_MODEL = 1024
BATCH = 8
SEQ = 2048
DEPTH = 1
DEC_BATCH = 128
DEC_SEQ = 1
PAST_LEN = 8192
PAGE_SIZE = 128

D_MIX = D_MODEL
NSA_WIDTH = D_MIX // 2
SSM_WIDTH = D_MIX - NSA_WIDTH
HEAD_DIM = 64
N_HEADS = NSA_WIDTH // HEAD_DIM
N_KV = 2
GQA = N_HEADS // N_KV
KV_WIDTH = N_KV * HEAD_DIM
N_BRANCH = 3
CMP_BLOCK = 32
SEL_BLOCK = 64
CMP_PER_SEL = SEL_BLOCK // CMP_BLOCK
N_SEL = 16
WINDOW = 512
WIN_QBLOCK = 128
SEL_QCHUNK = 32
SSM_GROUP = 16
N_SSM_GROUPS = SSM_WIDTH // SSM_GROUP
SSM_STATE = 64
N_BUCKETS = 32
MAX_DISTANCE = 128
N_MEM = 256
MEM_HEADS = 4
MEM_HEAD_DIM = 128
MEM_WIDTH = MEM_HEADS * MEM_HEAD_DIM
D_FF = -(-(8 * D_MODEL) // (3 * 256)) * 256
EPS = 1e-6
NEG = -1e30
TINY = 1e-30
FORCE = 1e4
SCALE = HEAD_DIM ** -0.5
MEM_SCALE = MEM_HEAD_DIM ** -0.5

OFF_KVC = NSA_WIDTH
OFF_KVS = OFF_KVC + 2 * KV_WIDTH
OFF_KVW = OFF_KVS + 2 * KV_WIDTH
OFF_GATE = OFF_KVW + 2 * KV_WIDTH
OFF_U = OFF_GATE + N_HEADS * N_BRANCH
IN_COLS = OFF_U + SSM_WIDTH

kernel_name = 'hymba_nsa_s5_memxattn_decode_step'


def rmsnorm(x, g):
    xf = x.astype(jnp.float32)
    y = xf * lax.rsqrt(jnp.mean(xf * xf, axis=-1, keepdims=True) + EPS)
    return (y * g.astype(jnp.float32)).astype(x.dtype)


def masked_softmax(s, mask, axis=-1):
    s = jnp.where(mask, s.astype(jnp.float32), NEG)
    m = jnp.max(s, axis=axis, keepdims=True)
    e = jnp.where(mask, jnp.exp(s - m), 0.0)
    return e / jnp.maximum(jnp.sum(e, axis=axis, keepdims=True), TINY)


def t5_bucket(dist):
    n = jnp.maximum(dist, 0)
    exact = N_BUCKETS // 2
    nf = jnp.maximum(n, exact).astype(jnp.float32)
    big = exact + (jnp.log(nf / exact) / math.log(MAX_DISTANCE / exact) * (N_BUCKETS - exact)).astype(jnp.int32)
    return jnp.where(n < exact, n, jnp.minimum(big, N_BUCKETS - 1))


def project(h, w_in):
    z = h @ w_in
    B, S, _ = z.shape
    q = z[..., :NSA_WIDTH].reshape(B, S, N_KV, GQA, HEAD_DIM)
    kv_c = z[..., OFF_KVC:OFF_KVS].reshape(B, S, 2, N_KV, HEAD_DIM)
    kv_s = z[..., OFF_KVS:OFF_KVW].reshape(B, S, 2, N_KV, HEAD_DIM)
    kv_w = z[..., OFF_KVW:OFF_GATE].reshape(B, S, 2, N_KV, HEAD_DIM)
    gates = jax.nn.sigmoid(z[..., OFF_GATE:OFF_U].astype(jnp.float32)).reshape(B, S, N_KV, GQA, N_BRANCH).astype(z.dtype)
    u = z[..., OFF_U:IN_COLS]
    return q, kv_c, kv_s, kv_w, gates, u


def compress_blocks(rows, cmp_pos, cmp_proj):
    B, N = rows.shape[:2]
    blk = rows.reshape(B, N // CMP_BLOCK, CMP_BLOCK, 2, N_KV, HEAD_DIM)
    summ = jnp.einsum('bnlcgd,cld->bncgd', blk, cmp_pos)
    return jnp.einsum('bncgd,cde->bncge', summ, cmp_proj)


def cmp_attention(q, blocks, qpos, table):
    S = q.shape[1]
    NC = blocks.shape[1]
    end = jnp.arange(NC, dtype=jnp.int32) * CMP_BLOCK + (CMP_BLOCK - 1)
    dist = qpos[:, None] - end[None, :]
    bias = table[t5_bucket(dist)].reshape(S, NC, N_KV, GQA).transpose(0, 2, 3, 1)
    s = jnp.einsum('bsgrd,bcgd->bsgrc', q, blocks[:, :, 0]).astype(jnp.float32) * SCALE + bias.astype(jnp.float32)
    p = masked_softmax(s, (dist >= 0)[:, None, None, :], -1)
    o = jnp.einsum('bsgrc,bcgd->bsgrd', p.astype(q.dtype), blocks[:, :, 1])
    return o, p


def select_blocks(p, qpos):
    B, S, G, R, NC = p.shape
    nbs = NC // CMP_PER_SEL
    ps = p.sum(3).reshape(B, S, G, nbs, CMP_PER_SEL).sum(-1)
    j = jnp.arange(nbs, dtype=jnp.int32)
    cur = (qpos // SEL_BLOCK)[:, None]
    forced = ((j == 0) | (j == cur) | (j == cur - 1))[None, :, None, :]
    visible = (j <= cur)[None, :, None, :]
    score = jnp.where(visible, jnp.where(forced, ps + FORCE, ps), -jnp.inf)
    _, idx = lax.top_k(score, min(N_SEL, nbs))
    valid = idx <= cur[None, :, :, None]
    return idx.astype(jnp.int32), valid


def sel_attention(q, kv, idx, valid, qpos, table):
    tok = idx[..., None] * SEL_BLOCK + jnp.arange(SEL_BLOCK, dtype=jnp.int32)
    dist = qpos[None, :, None, None, None] - tok
    mask = valid[..., None] & (dist >= 0)
    gi = jnp.arange(N_KV)[None, None, :, None, None]
    bias = table.reshape(N_BUCKETS, N_KV, GQA)[t5_bucket(dist), gi]
    s = jnp.einsum('bsgrd,bsgkld->bsgrkl', q, kv[..., 0, :]).astype(jnp.float32) * SCALE + jnp.moveaxis(bias, -1, 3).astype(jnp.float32)
    p = masked_softmax(s, mask[:, :, :, None], axis=(-2, -1))
    return jnp.einsum('bsgrkl,bsgkld->bsgrd', p.astype(q.dtype), kv[..., 1, :])


def window_prompt(q, kvw, table):
    B, T = q.shape[:2]
    nq = T // WIN_QBLOCK
    span = WINDOW + WIN_QBLOCK
    kvp = jnp.pad(kvw, ((0, 0), (WINDOW, 0), (0, 0), (0, 0), (0, 0)))
    rows = jnp.arange(nq)[:, None] * WIN_QBLOCK + jnp.arange(span)[None, :]
    kb = kvp[:, rows]
    qb = q.reshape(B, nq, WIN_QBLOCK, N_KV, GQA, HEAD_DIM)
    dist = jnp.arange(WIN_QBLOCK)[:, None] + WINDOW - jnp.arange(span)[None, :]
    mask = ((dist >= 0) & (dist < WINDOW))[None] & (rows >= WINDOW)[:, None, :]
    bias = table[t5_bucket(dist)].reshape(WIN_QBLOCK, span, N_KV, GQA).transpose(2, 3, 0, 1)
    s = jnp.einsum('bnqgrd,bnkgd->bngrqk', qb, kb[:, :, :, 0]).astype(jnp.float32) * SCALE + bias.astype(jnp.float32)
    p = masked_softmax(s, mask[None, :, None, None], -1)
    o = jnp.einsum('bngrqk,bnkgd->bnqgrd', p.astype(q.dtype), kb[:, :, :, 1])
    return o.reshape(q.shape)


def window_sample(q, buf, kvw, past_len, table):
    S = q.shape[1]
    wb = buf.shape[1]
    keys = jnp.concatenate([buf.astype(kvw.dtype), kvw], axis=1)
    qpos = past_len + jnp.arange(S, dtype=jnp.int32)
    kpos = past_len - wb + jnp.arange(wb + S, dtype=jnp.int32)
    dist = qpos[:, None] - kpos[None, :]
    mask = (dist >= 0) & (dist < WINDOW)
    bias = table[t5_bucket(dist)].reshape(S, wb + S, N_KV, GQA).transpose(2, 3, 0, 1)
    s = jnp.einsum('bsgrd,bkgd->bgrsk', q, keys[:, :, 0]).astype(jnp.float32) * SCALE + bias.astype(jnp.float32)
    p = masked_softmax(s, mask, -1)
    o = jnp.einsum('bgrsk,bkgd->bsgrd', p.astype(q.dtype), keys[:, :, 1])
    return o, keys[:, S:]


def combine_branches(gates, o_c, o_s, o_w):
    o = gates[..., 0:1] * o_c + gates[..., 1:2] * o_s + gates[..., 2:3] * o_w
    return o.reshape(o.shape[0], o.shape[1], NSA_WIDTH)


def nsa_prompt(q, kvc, kvs, kvw, gates, cmp_pos, cmp_proj, table):
    B, T = q.shape[:2]
    qpos = jnp.arange(T, dtype=jnp.int32)
    padw = ((0, 0), (0, (-T) % SEL_BLOCK), (0, 0), (0, 0), (0, 0))
    blocks = compress_blocks(jnp.pad(kvc, padw), cmp_pos, cmp_proj)
    o_c, p = cmp_attention(q, blocks, qpos, table)
    idx, valid = select_blocks(p, qpos)
    kvb = jnp.pad(kvs, padw).reshape(B, -1, SEL_BLOCK, 2, N_KV, HEAD_DIM).transpose(0, 4, 1, 2, 3, 5)
    bi = jnp.arange(B)[:, None, None, None]
    gi = jnp.arange(N_KV)[None, None, :, None]
    n_chunk = T // SEL_QCHUNK

    def to_chunks(a):
        return jnp.moveaxis(a.reshape(B, n_chunk, SEL_QCHUNK, *a.shape[2:]), 1, 0)

    def one_chunk(args):
        qc, ic, vc, pc = args
        kv = kvb[bi, gi, ic]
        return sel_attention(qc, kv, ic, vc, pc, table)

    o_s = lax.map(one_chunk, (to_chunks(q), to_chunks(idx), to_chunks(valid), qpos.reshape(n_chunk, SEL_QCHUNK)))
    o_s = jnp.moveaxis(o_s, 0, 1).reshape(q.shape)
    o_w = window_prompt(q, kvw, table)
    return combine_branches(gates, o_c, o_s, o_w)


def nsa_sample(q, kvc, kvs, kvw, gates, pool_c, pool_s, buf_w, page_table, layer, cmp_pos, cmp_proj, table):
    DB, S = q.shape[:2]
    n_pages = page_table.shape[1]
    past_len = n_pages * PAGE_SIZE
    qpos = past_len + jnp.arange(S, dtype=jnp.int32)
    padw = ((0, 0), (0, (-S) % SEL_BLOCK), (0, 0), (0, 0), (0, 0))
    past_c = pool_c[layer, page_table].reshape(DB, past_len, 2, N_KV, HEAD_DIM)
    blocks = jnp.concatenate([compress_blocks(past_c, cmp_pos, cmp_proj),
                              compress_blocks(jnp.pad(kvc, padw), cmp_pos, cmp_proj)], axis=1)
    o_c, p = cmp_attention(q, blocks, qpos, table)
    idx, valid = select_blocks(p, qpos)
    nb_past = past_len // SEL_BLOCK
    bpp = PAGE_SIZE // SEL_BLOCK
    bi = jnp.arange(DB)[:, None, None, None]
    gi = jnp.arange(N_KV)[None, None, :, None]
    jp = jnp.minimum(idx, nb_past - 1)
    phys = page_table[bi, jp // bpp]
    pool_r = pool_s.reshape(pool_s.shape[0], pool_s.shape[1], bpp, SEL_BLOCK, 2, N_KV, HEAD_DIM)
    kv_past = pool_r[layer, phys, jp % bpp, :, :, gi, :]
    new_r = jnp.pad(kvs, padw).reshape(DB, -1, SEL_BLOCK, 2, N_KV, HEAD_DIM).transpose(0, 4, 1, 2, 3, 5)
    jn = jnp.clip(idx - nb_past, 0, new_r.shape[2] - 1)
    kv_new = new_r[bi, gi, jn]
    kv = jnp.where((idx < nb_past)[..., None, None, None], kv_past.astype(kv_new.dtype), kv_new)
    o_s = sel_attention(q, kv, idx, valid, qpos, table)
    o_w, new_buf = window_sample(q, buf_w, kvw, past_len, table)
    return combine_branches(gates, o_c, o_s, o_w), new_buf


def _ssm_combine(e1, e2):
    ar1, ai1, br1, bi1 = e1
    ar2, ai2, br2, bi2 = e2
    return (ar2 * ar1 - ai2 * ai1, ar2 * ai1 + ai2 * ar1,
            ar2 * br1 - ai2 * bi1 + br2, ar2 * bi1 + ai2 * br1 + bi2)


def ssm_apply(u, h0, ssm_p):
    a_re, a_im, b_re, b_im, c_re, c_im, d, log_dt, glu_w, glu_b = ssm_p
    f32 = jnp.float32
    a_re = a_re.astype(f32)
    a_im = a_im.astype(f32)
    dt = jnp.exp(log_dt.astype(f32))[:, None]
    mag = jnp.exp(a_re * dt)
    ab_re = mag * jnp.cos(a_im * dt)
    ab_im = mag * jnp.sin(a_im * dt)
    den = a_re * a_re + a_im * a_im
    co_re = ((ab_re - 1.0) * a_re + ab_im * a_im) / den
    co_im = (ab_im * a_re - (ab_re - 1.0) * a_im) / den
    b_re = b_re.astype(f32)
    b_im = b_im.astype(f32)
    bb_re = co_re[..., None] * b_re - co_im[..., None] * b_im
    bb_im = co_re[..., None] * b_im + co_im[..., None] * b_re
    B, S, _ = u.shape
    uf = u.astype(f32)
    ug = uf.reshape(B, S, N_SSM_GROUPS, SSM_GROUP)
    bu_re = jnp.einsum('bsgi,gpi->bsgp', ug, bb_re)
    bu_im = jnp.einsum('bsgi,gpi->bsgp', ug, bb_im)
    shape_a = (1, S, N_SSM_GROUPS, SSM_STATE)
    elems = (jnp.broadcast_to(ab_re, shape_a), jnp.broadcast_to(ab_im, shape_a), bu_re, bu_im)
    pa_re, pa_im, h_re, h_im = lax.associative_scan(_ssm_combine, elems, axis=1)
    if h0 is not None:
        h0_re = h0[0].astype(f32)[:, None]
        h0_im = h0[1].astype(f32)[:, None]
        h_re, h_im = (h_re + pa_re * h0_re - pa_im * h0_im,
                      h_im + pa_re * h0_im + pa_im * h0_re)
    y = (jnp.einsum('gip,bsgp->bsgi', c_re.astype(f32), h_re)
         - jnp.einsum('gip,bsgp->bsgi', c_im.astype(f32), h_im))
    y = y.reshape(B, S, SSM_WIDTH) + d.astype(f32) * uf
    g = jax.nn.gelu(y)
    out = g * jax.nn.sigmoid(g @ glu_w.astype(f32) + glu_b.astype(f32))
    return out.astype(u.dtype), h_re[:, -1], h_im[:, -1]


def merge_groups(o_nsa, o_ssm, grp_g, w_out):
    m = jnp.concatenate([rmsnorm(o_nsa, grp_g[:NSA_WIDTH]), rmsnorm(o_ssm, grp_g[NSA_WIDTH:])], axis=-1)
    return m @ w_out


def memory_kv(mem, g, w_kv):
    B, M, _ = mem.shape
    return (rmsnorm(mem, g) @ w_kv).reshape(B, M, 2, MEM_HEADS, MEM_HEAD_DIM)


def cross_attn(h, mkv, w_q, w_o):
    B, S, _ = h.shape
    q = (h @ w_q).reshape(B, S, MEM_HEADS, MEM_HEAD_DIM)
    s = jnp.einsum('bshd,bmhd->bhsm', q, mkv[:, :, 0].astype(q.dtype)).astype(jnp.float32) * MEM_SCALE
    p = jax.nn.softmax(s, axis=-1).astype(q.dtype)
    o = jnp.einsum('bhsm,bmhd->bshd', p, mkv[:, :, 1].astype(q.dtype)).reshape(B, S, MEM_WIDTH)
    return o @ w_o


def swiglu(h, w_i, w_o):
    z = h @ w_i
    return (jax.nn.silu(z[..., :D_FF]) * z[..., D_FF:]) @ w_o


def cross_and_ffn(x, mkv, ng, w_q, w_o, ffn_wi, ffn_wo):
    x = x + rmsnorm(cross_attn(rmsnorm(x, ng[2]), mkv, w_q, w_o), ng[3])
    return x + rmsnorm(swiglu(rmsnorm(x, ng[4]), ffn_wi, ffn_wo), ng[5])


def setup_inputs(seed: int = 0) -> dict:
    key = jax.random.key(seed)
    ks = iter(jax.random.split(key, 40))
    f32 = jnp.float32

    def nrm(shape, scale=1.0):
        return jax.random.normal(next(ks), shape, f32) * scale

    n_pages = PAST_LEN // PAGE_SIZE
    n_used = DEC_BATCH * n_pages
    n_phys = n_used + n_used // 4
    win_rows = min(WINDOW, PAST_LEN)
    n_idx = jnp.arange(SSM_STATE, dtype=f32)
    gshape = (DEPTH, N_SSM_GROUPS, SSM_STATE)
    return {
        'x_prompt': nrm((BATCH, SEQ, D_MODEL)),
        'x_sample': nrm((DEC_BATCH, DEC_SEQ, D_MODEL)),
        'cache_kv_cmp': nrm((DEPTH, n_phys, PAGE_SIZE, 2, N_KV, HEAD_DIM)),
        'cache_kv_sel': nrm((DEPTH, n_phys, PAGE_SIZE, 2, N_KV, HEAD_DIM)),
        'cache_kv_win': nrm((DEPTH, DEC_BATCH, win_rows, 2, N_KV, HEAD_DIM)),
        'state_ssm_re': nrm((DEPTH, DEC_BATCH, N_SSM_GROUPS, SSM_STATE), 0.3),
        'state_ssm_im': nrm((DEPTH, DEC_BATCH, N_SSM_GROUPS, SSM_STATE), 0.3),
        'cache_mem_kv': nrm((DEPTH, DEC_BATCH, N_MEM, 2, MEM_HEADS, MEM_HEAD_DIM)),
        'page_table': jax.random.permutation(next(ks), n_phys)[:n_used].reshape(DEC_BATCH, n_pages).astype(jnp.int32),
        'mem_prompt': nrm((BATCH, N_MEM, D_MODEL)),
        'w_in': nrm((DEPTH, D_MODEL, IN_COLS), D_MODEL ** -0.5),
        'w_out': nrm((DEPTH, D_MIX, D_MODEL), D_MIX ** -0.5),
        'norm_g': 1.0 + nrm((DEPTH, 6, D_MODEL), 0.02),
        'grp_norm_g': 1.0 + nrm((DEPTH, D_MIX), 0.02),
        'cmp_pos': (1.0 + nrm((DEPTH, 2, CMP_BLOCK, HEAD_DIM), 0.1)) * CMP_BLOCK ** -0.5,
        'cmp_proj': nrm((DEPTH, 2, HEAD_DIM, HEAD_DIM), HEAD_DIM ** -0.5),
        'rel_bias': nrm((N_BUCKETS, N_HEADS), 0.5),
        'ssm_a_re': -0.5 + nrm(gshape, 0.01),
        'ssm_a_im': jnp.broadcast_to(jnp.pi * n_idx, gshape) + nrm(gshape, 0.01),
        'ssm_b_re': nrm((DEPTH, N_SSM_GROUPS, SSM_STATE, SSM_GROUP), (2 * SSM_GROUP) ** -0.5),
        'ssm_b_im': nrm((DEPTH, N_SSM_GROUPS, SSM_STATE, SSM_GROUP), (2 * SSM_GROUP) ** -0.5),
        'ssm_c_re': nrm((DEPTH, N_SSM_GROUPS, SSM_GROUP, SSM_STATE), (2 * SSM_STATE) ** -0.5),
        'ssm_c_im': nrm((DEPTH, N_SSM_GROUPS, SSM_GROUP, SSM_STATE), (2 * SSM_STATE) ** -0.5),
        'ssm_d': nrm((DEPTH, SSM_WIDTH)),
        'ssm_log_dt': jax.random.uniform(next(ks), (DEPTH, N_SSM_GROUPS), f32, math.log(0.001), math.log(0.1)),
        'glu_w': nrm((DEPTH, SSM_WIDTH, SSM_WIDTH), SSM_WIDTH ** -0.5),
        'glu_b': nrm((DEPTH, SSM_WIDTH), 0.02),
        'mem_norm_g': 1.0 + nrm((DEPTH, D_MODEL), 0.02),
        'xq': nrm((DEPTH, D_MODEL, MEM_WIDTH), D_MODEL ** -0.5),
        'xkv': nrm((DEPTH, D_MODEL, 2 * MEM_WIDTH), D_MODEL ** -0.5),
        'xo': nrm((DEPTH, MEM_WIDTH, D_MODEL), MEM_WIDTH ** -0.5),
        'ffn_wi': nrm((DEPTH, D_MODEL, 2 * D_FF), D_MODEL ** -0.5),
        'ffn_wo': nrm((DEPTH, D_FF, D_MODEL), D_FF ** -0.5),
    }


def reference(x_prompt, x_sample, cache_kv_cmp, cache_kv_sel, cache_kv_win, state_ssm_re, state_ssm_im,
              cache_mem_kv, page_table, mem_prompt, w_in, w_out, norm_g, grp_norm_g, cmp_pos, cmp_proj,
              rel_bias, ssm_a_re, ssm_a_im, ssm_b_re, ssm_b_im, ssm_c_re, ssm_c_im, ssm_d, ssm_log_dt,
              glu_w, glu_b, mem_norm_g, xq, xkv, xo, ffn_wi, ffn_wo):
    xp = x_prompt
    xs = x_sample
    l_pc, l_ps, l_pw, l_pre, l_pim, l_pm = [], [], [], [], [], []
    l_sc, l_ss, l_sw, l_sre, l_sim = [], [], [], [], []
    for l in range(DEPTH):
        ssm_p = (ssm_a_re[l], ssm_a_im[l], ssm_b_re[l], ssm_b_im[l], ssm_c_re[l], ssm_c_im[l],
                 ssm_d[l], ssm_log_dt[l], glu_w[l], glu_b[l])
        h = rmsnorm(xp, norm_g[l, 0])
        q, kvc, kvs, kvw, gates, u = project(h, w_in[l])
        o_nsa = nsa_prompt(q, kvc, kvs, kvw, gates, cmp_pos[l], cmp_proj[l], rel_bias)
        o_ssm, h_re, h_im = ssm_apply(u, None, ssm_p)
        xp = xp + rmsnorm(merge_groups(o_nsa, o_ssm, grp_norm_g[l], w_out[l]), norm_g[l, 1])
        mkv = memory_kv(mem_prompt, mem_norm_g[l], xkv[l])
        xp = cross_and_ffn(xp, mkv, norm_g[l], xq[l], xo[l], ffn_wi[l], ffn_wo[l])
        T = kvw.shape[1]
        l_pc.append(kvc)
        l_ps.append(kvs)
        l_pw.append(kvw[:, T - min(WINDOW, T):])
        l_pre.append(h_re)
        l_pim.append(h_im)
        l_pm.append(mkv)
        h = rmsnorm(xs, norm_g[l, 0])
        q, kvc, kvs, kvw, gates, u = project(h, w_in[l])
        o_nsa, new_buf = nsa_sample(q, kvc, kvs, kvw, gates, cache_kv_cmp, cache_kv_sel, cache_kv_win[l],
                                    page_table, l, cmp_pos[l], cmp_proj[l], rel_bias)
        o_ssm, h_re, h_im = ssm_apply(u, (state_ssm_re[l], state_ssm_im[l]), ssm_p)
        xs = xs + rmsnorm(merge_groups(o_nsa, o_ssm, grp_norm_g[l], w_out[l]), norm_g[l, 1])
        xs = cross_and_ffn(xs, cache_mem_kv[l], norm_g[l], xq[l], xo[l], ffn_wi[l], ffn_wo[l])
        l_sc.append(kvc)
        l_ss.append(kvs)
        l_sw.append(new_buf)
        l_sre.append(h_re)
        l_sim.append(h_im)
    p_kv_cmp = jnp.stack(l_pc, axis=0)
    p_kv_sel = jnp.stack(l_ps, axis=0)
    p_kv_win = jnp.stack(l_pw, axis=0)
    p_ssm_re = jnp.stack(l_pre, axis=0)
    p_ssm_im = jnp.stack(l_pim, axis=0)
    p_mem_kv = jnp.stack(l_pm, axis=0)
    s_kv_cmp = jnp.stack(l_sc, axis=0)
    s_kv_sel = jnp.stack(l_ss, axis=0)
    s_kv_win = jnp.stack(l_sw, axis=0)
    s_ssm_re = jnp.stack(l_sre, axis=0)
    s_ssm_im = jnp.stack(l_sim, axis=0)
    return (xp, xs, p_kv_cmp, p_kv_sel, p_kv_win, p_ssm_re, p_ssm_im, p_mem_kv,
            s_kv_cmp, s_kv_sel, s_kv_win, s_ssm_re, s_ssm_im)
```

```python
import functools
import math

import numpy as np
import jax
import jax.numpy as jnp
from jax import lax
from jax.experimental import pallas as pl
from jax.experimental.pallas import tpu as pltpu

F32 = jnp.float32
BF16 = jnp.bfloat16
I32 = jnp.int32

D_MODEL = 1024
HEAD_DIM = 64
N_HEADS = 8
N_KV = 2
GQA = N_HEADS // N_KV
NSA_WIDTH = N_HEADS * HEAD_DIM
SSM_WIDTH = 512
KV_COLS = 2 * N_KV * HEAD_DIM
N_BRANCH = 3
CMP_BLOCK = 32
SEL_BLOCK = 64
N_SEL = 16
WINDOW = 512
PAGE_SIZE = 128
SSM_GROUP = 16
N_SSM_GROUPS = 32
SSM_STATE = 64
N_STATE = N_SSM_GROUPS * SSM_STATE
N_BUCKETS = 32
MAX_DISTANCE = 128
MEM_HEADS = 4
MEM_HEAD_DIM = 128
MEM_WIDTH = MEM_HEADS * MEM_HEAD_DIM
EPS = 1e-6
NEG = -1e30
MASKED_BELOW = -1e29
TINY = 1e-30
FORCE = 1e4
SCALE = HEAD_DIM ** -0.5
MEM_SCALE = MEM_HEAD_DIM ** -0.5

LANES = 128
Q_TILE = 128
VMEM_LIMIT = 56 * 1024 * 1024


def _bucket_starts():
    n = np.arange(0, 4 * MAX_DISTANCE)
    exact = N_BUCKETS // 2
    nf = np.maximum(n, exact).astype(np.float32)
    big = exact + (np.log(nf / exact) / np.float32(math.log(MAX_DISTANCE / exact))
                   * (N_BUCKETS - exact)).astype(np.int32)
    bucket = np.where(n < exact, n, np.minimum(big, N_BUCKETS - 1))
    return [int(np.argmax(bucket >= k)) for k in range(N_BUCKETS)]


_BUCKET_START = _bucket_starts()
CONST_BIAS_DIST = _BUCKET_START[N_BUCKETS - 1]


def _params(*sem):
    return pltpu.CompilerParams(dimension_semantics=sem, vmem_limit_bytes=VMEM_LIMIT)


def _bdot(a, b):
    return jnp.dot(a.astype(BF16), b.astype(BF16), preferred_element_type=F32)


def _bdot_nt(a, b):
    return lax.dot_general(a.astype(BF16), b.astype(BF16), (((1,), (1,)), ((), ())),
                           preferred_element_type=F32)


def _split(a):
    hi = a.astype(BF16)
    return hi, (a - hi.astype(F32)).astype(BF16)


def _dot3(a, b):
    ah, al = _split(a)
    bh, bl = _split(b)
    d = functools.partial(jnp.dot, preferred_element_type=F32)
    return d(ah, bh) + (d(ah, bl) + d(al, bh))


def _dot3_nt(a, b):
    ah, al = _split(a)
    bh, bl = _split(b)
    d = functools.partial(lax.dot_general, dimension_numbers=(((1,), (1,)), ((), ())),
                          preferred_element_type=F32)
    return d(ah, bh) + (d(ah, bl) + d(al, bh))


def _rms(x, g):
    return x * lax.rsqrt(jnp.mean(x * x, axis=-1, keepdims=True) + EPS) * g


def _masked_softmax(s, mask, axis=-1):
    s = jnp.where(mask, s, NEG)
    m = jnp.max(s, axis=axis, keepdims=True)
    e = jnp.where(mask, jnp.exp(s - m), 0.0)
    return e * (1.0 / jnp.maximum(jnp.sum(e, axis=axis, keepdims=True), TINY))


def _bias_of_dist(dist, tbl_ref, h):
    v = jnp.full(dist.shape, tbl_ref[0, h], F32)
    for k in range(1, N_BUCKETS):
        v = jnp.where(dist >= _BUCKET_START[k], tbl_ref[k, h], v)
    return v


def _iota2(shape, dim):
    return lax.broadcasted_iota(I32, shape, dim)


def _prompt_bias_kernel(tbl_ref, bsel_ref, bcmp_ref, bwin_ref, *, n_ktiles):
    h = pl.program_id(0)
    qi = pl.program_id(1)
    row = _iota2((Q_TILE, LANES), 0)
    col = _iota2((Q_TILE, LANES), 1)
    last = tbl_ref[N_BUCKETS - 1, h]

    for kt in range(n_ktiles):
        cols = slice(kt * LANES, (kt + 1) * LANES)

        @pl.when(kt > qi)
        def _(cols=cols):
            bsel_ref[0, :, cols] = jnp.full((Q_TILE, LANES), NEG, F32)

        @pl.when(kt < qi - 1)
        def _(cols=cols):
            bsel_ref[0, :, cols] = jnp.full((Q_TILE, LANES), last, F32)

        @pl.when((kt <= qi) & (kt >= qi - 1))
        def _(cols=cols, kt=kt):
            dist = (qi - kt) * LANES + row - col
            bsel_ref[0, :, cols] = jnp.where(dist >= 0, _bias_of_dist(dist, tbl_ref, h), NEG)

    n_cmp = n_ktiles * LANES // CMP_BLOCK
    dist = qi * Q_TILE + row - (col * CMP_BLOCK + (CMP_BLOCK - 1))
    bcmp_ref[0] = jnp.where((dist >= 0) & (col < n_cmp), _bias_of_dist(dist, tbl_ref, h), NEG)

    @pl.when(qi == 0)
    def _():
        for mt in range((WINDOW + Q_TILE) // LANES):
            dist = WINDOW + row - (col + mt * LANES)
            bwin_ref[0, :, mt * LANES:(mt + 1) * LANES] = jnp.where(
                (dist >= 0) & (dist < WINDOW), _bias_of_dist(dist, tbl_ref, h), NEG)


def _prompt_bias(rel_bias, seq):
    nq = seq // Q_TILE
    span = WINDOW + Q_TILE
    return pl.pallas_call(
        functools.partial(_prompt_bias_kernel, n_ktiles=seq // LANES),
        grid=(N_HEADS, nq),
        in_specs=[pl.BlockSpec(memory_space=pltpu.SMEM)],
        out_specs=[pl.BlockSpec((1, Q_TILE, seq), lambda h, qi: (h, qi, 0)),
                   pl.BlockSpec((1, Q_TILE, LANES), lambda h, qi: (h, qi, 0)),
                   pl.BlockSpec((1, Q_TILE, span), lambda h, qi: (h, 0, 0))],
        out_shape=[jax.ShapeDtypeStruct((N_HEADS, seq, seq), F32),
                   jax.ShapeDtypeStruct((N_HEADS, seq, LANES), F32),
                   jax.ShapeDtypeStruct((N_HEADS, Q_TILE, span), F32)],
        compiler_params=_params("arbitrary", "arbitrary"),
    )(rel_bias)


C_Q = 0
C_KVC = C_Q + NSA_WIDTH
C_KVS = C_KVC + KV_COLS
C_KVW = C_KVS + KV_COLS
C_KVS_G = C_KVW + KV_COLS
C_KVW_G = C_KVS_G + KV_COLS
C_GATE = C_KVW_G + KV_COLS
C_U = C_GATE + N_KV * LANES
C_END = C_U + SSM_WIDTH


def _pack_w_in(w_in):
    off_kvc = NSA_WIDTH
    off_gate = off_kvc + 3 * KV_COLS
    off_u = off_gate + N_HEADS * N_BRANCH
    kv = [w_in[:, off_kvc + i * KV_COLS: off_kvc + (i + 1) * KV_COLS] for i in range(3)]

    def by_group(w):
        return w.reshape(-1, 2, N_KV, HEAD_DIM).transpose(0, 2, 1, 3).reshape(-1, KV_COLS)

    gates = w_in[:, off_gate:off_u].reshape(-1, N_KV, GQA * N_BRANCH)
    gates = jnp.pad(gates, ((0, 0), (0, 0), (0, LANES - GQA * N_BRANCH))).reshape(-1, N_KV * LANES)
    cols = [w_in[:, :NSA_WIDTH], kv[0], kv[1], kv[2], by_group(kv[1]), by_group(kv[2]), gates,
            w_in[:, off_u:]]
    return jnp.concatenate(cols, axis=1).astype(BF16)


def _inproj_kernel(x_ref, g_ref, w_ref, q_ref, kvc_ref, kvs_ref, kvw_ref, kvsg_ref, kvwg_ref,
                   gate_ref, u_ref):
    h = _rms(x_ref[...], g_ref[...])
    z = jnp.dot(h.astype(BF16), w_ref[...], preferred_element_type=F32)
    q_ref[...] = z[:, C_Q:C_KVC]
    kvc_ref[...] = z[:, C_KVC:C_KVS]
    kvs_ref[...] = z[:, C_KVS:C_KVW]
    kvw_ref[...] = z[:, C_KVW:C_KVS_G]
    kvsg_ref[...] = z[:, C_KVS_G:C_KVW_G]
    kvwg_ref[...] = z[:, C_KVW_G:C_GATE]
    gate_ref[...] = jax.nn.sigmoid(z[:, C_GATE:C_U])
    u_ref[...] = z[:, C_U:C_END]


def _inproj(x, g, w_packed, n_seq, tile):
    rows = x.shape[0]
    seq = rows // n_seq
    nt = seq // tile
    row_map = lambda b, t: (b * nt + t, 0)
    widths = [NSA_WIDTH, KV_COLS, KV_COLS, KV_COLS, KV_COLS, KV_COLS, N_KV * LANES]
    out_specs = [pl.BlockSpec((tile, w), row_map) for w in widths]
    out_shape = [jax.ShapeDtypeStruct((rows, w), F32) for w in widths]
    out_specs.append(pl.BlockSpec((tile, SSM_WIDTH), lambda b, t: (t, b)))
    out_shape.append(jax.ShapeDtypeStruct((seq, n_seq * SSM_WIDTH), F32))
    return pl.pallas_call(
        _inproj_kernel,
        grid=(n_seq, nt),
        in_specs=[pl.BlockSpec((tile, D_MODEL), row_map),
                  pl.BlockSpec((1, D_MODEL), lambda b, t: (0, 0)),
                  pl.BlockSpec((D_MODEL, C_END), lambda b, t: (0, 0))],
        out_specs=out_specs,
        out_shape=out_shape,
        compiler_params=_params("arbitrary", "arbitrary"),
    )(x, g, w_packed)


def _pack_compress(cmp_pos, cmp_proj, group_major):
    pos = jnp.broadcast_to(cmp_pos.transpose(1, 0, 2)[:, :, None, :],
                           (CMP_BLOCK, 2, N_KV, HEAD_DIM)).reshape(CMP_BLOCK, KV_COLS)
    eye = jnp.eye(2 * N_KV, dtype=F32).reshape(2, N_KV, 2, N_KV)
    order = 'cgdGCe' if group_major else 'cgdCGe'
    proj = jnp.einsum('cde,cgCG->' + order, cmp_proj, eye).reshape(KV_COLS, KV_COLS)
    return pos, proj


def _block_sums(rows, pos):
    n = rows.shape[0] // CMP_BLOCK
    return jnp.sum(rows.reshape(n, CMP_BLOCK, KV_COLS) * pos[None], axis=1)


def _compress_prompt_kernel(kv_ref, pos_ref, proj_ref, o_ref):
    blk = _dot3(_block_sums(kv_ref[0], pos_ref[...]), proj_ref[...])
    o_ref[0] = jnp.concatenate([blk, jnp.zeros((LANES - blk.shape[0], KV_COLS), F32)], axis=0)


def _compress_prompt(kvc, pos, proj):
    n_seq, seq, _ = kvc.shape
    assert seq // CMP_BLOCK <= LANES
    return pl.pallas_call(
        _compress_prompt_kernel,
        grid=(n_seq,),
        in_specs=[pl.BlockSpec((1, seq, KV_COLS), lambda b: (b, 0, 0)),
                  pl.BlockSpec((CMP_BLOCK, KV_COLS), lambda b: (0, 0)),
                  pl.BlockSpec((KV_COLS, KV_COLS), lambda b: (0, 0))],
        out_specs=pl.BlockSpec((1, LANES, KV_COLS), lambda b: (b, 0, 0)),
        out_shape=jax.ShapeDtypeStruct((n_seq, LANES, KV_COLS), F32),
        compiler_params=_params("arbitrary"),
    )(kvc, pos, proj)


def _select_blocks(ps, t0):
    lane = _iota2(ps.shape, 1)
    cur = (t0 + _iota2(ps.shape, 0)) // SEL_BLOCK
    j = lane // (SEL_BLOCK // CMP_BLOCK)
    pair = ps + pltpu.roll(ps, LANES - 1, 1)
    cand = (lane % 2 == 0) & (j <= cur)
    forced = (j == 0) | (j == cur) | (j == cur - 1)
    score = jnp.where(cand, jnp.where(forced, pair + FORCE, pair), -jnp.inf)
    chosen = jnp.zeros(ps.shape, F32)
    for _ in range(N_SEL):
        m = jnp.max(score, axis=-1, keepdims=True)
        hit = (score == m) & (m > -jnp.inf)
        first = jnp.min(jnp.where(hit, lane, LANES), axis=-1, keepdims=True)
        pick = lane == first
        chosen = jnp.where(pick, 1.0, chosen)
        score = jnp.where(pick, -jnp.inf, score)
    return chosen + pltpu.roll(chosen, 1, 1)


def _nsa_prompt_kernel(q_ref, gate_ref, blk_ref, kvs_ref, kvw_ref, bsel_ref, bcmp_ref, bwin_ref,
                       expand_ref, o_ref):
    qi = pl.program_id(1)
    t0 = qi * Q_TILE
    q = q_ref[0]
    gate = gate_ref[0]
    span = WINDOW + Q_TILE

    k_cmp, v_cmp = blk_ref[0, :, :HEAD_DIM], blk_ref[0, :, HEAD_DIM:]
    o_cmp = []
    ps = jnp.zeros((Q_TILE, LANES), F32)
    for r in range(GQA):
        qr = q[:, r * HEAD_DIM:(r + 1) * HEAD_DIM]
        bias = bcmp_ref[r]
        p = _masked_softmax(_dot3_nt(qr, k_cmp) * SCALE + bias, bias > MASKED_BELOW)
        o_cmp.append(_bdot(p, v_cmp))
        ps = ps + p

    chosen = _select_blocks(ps, t0)
    sel_keys = jnp.dot(chosen.astype(BF16), expand_ref[...], preferred_element_type=F32) > 0.5

    k_sel, v_sel = kvs_ref[0, :, :HEAD_DIM], kvs_ref[0, :, HEAD_DIM:]
    kw = kvw_ref[0, pl.ds(pl.multiple_of(t0, Q_TILE), span), :]
    k_win, v_win = kw[:, :HEAD_DIM], kw[:, HEAD_DIM:]
    in_seq = (_iota2((Q_TILE, span), 1) + t0) >= WINDOW

    heads = []
    for r in range(GQA):
        qr = q[:, r * HEAD_DIM:(r + 1) * HEAD_DIM]
        bias = bsel_ref[r]
        p = _masked_softmax(_bdot_nt(qr, k_sel) * SCALE + bias, sel_keys & (bias > MASKED_BELOW))
        o_sel = _bdot(p, v_sel)
        bias = bwin_ref[r]
        p = _masked_softmax(_bdot_nt(qr, k_win) * SCALE + bias, in_seq & (bias > MASKED_BELOW))
        o_win = _bdot(p, v_win)
        g0 = r * N_BRANCH
        heads.append(gate[:, g0:g0 + 1] * o_cmp[r] + gate[:, g0 + 1:g0 + 2] * o_sel
                     + gate[:, g0 + 2:g0 + 3] * o_win)
    o_ref[0] = jnp.concatenate(heads, axis=1)


def _nsa_prompt(q, gates, blocks, kvs_g, kvw_g, bsel, bcmp, bwin):
    n_seq, seq, _ = q.shape
    nq = seq // Q_TILE
    span = WINDOW + Q_TILE
    key = np.arange(seq) // CMP_BLOCK
    expand = jnp.asarray(key[None, :] == np.arange(LANES)[:, None], dtype=BF16)
    gw = GQA * HEAD_DIM
    return pl.pallas_call(
        _nsa_prompt_kernel,
        grid=(N_KV, nq, n_seq),
        in_specs=[pl.BlockSpec((1, Q_TILE, gw), lambda g, qi, b: (b, qi, g)),
                  pl.BlockSpec((1, Q_TILE, LANES), lambda g, qi, b: (b, qi, g)),
                  pl.BlockSpec((1, LANES, 2 * HEAD_DIM), lambda g, qi, b: (b, 0, g)),
                  pl.BlockSpec((1, seq, 2 * HEAD_DIM), lambda g, qi, b: (b, 0, g)),
                  pl.BlockSpec((1, WINDOW + seq, 2 * HEAD_DIM), lambda g, qi, b: (b, 0, g)),
                  pl.BlockSpec((GQA, Q_TILE, seq), lambda g, qi, b: (g, qi, 0)),
                  pl.BlockSpec((GQA, Q_TILE, LANES), lambda g, qi, b: (g, qi, 0)),
                  pl.BlockSpec((GQA, Q_TILE, span), lambda g, qi, b: (g, 0, 0)),
                  pl.BlockSpec((LANES, seq), lambda g, qi, b: (0, 0))],
        out_specs=pl.BlockSpec((1, Q_TILE, gw), lambda g, qi, b: (b, qi, g)),
        out_shape=jax.ShapeDtypeStruct((n_seq, seq, NSA_WIDTH), F32),
        compiler_params=_params("arbitrary", "arbitrary", "arbitrary"),
    )(q, gates, blocks, kvs_g, kvw_g, bsel, bcmp, bwin, expand)


STATE_CHUNK = 512
SSM_KB = SSM_WIDTH // LANES
GROUPS_PER_KB = N_SSM_GROUPS // SSM_KB


def _ssm_disc_kernel(are_ref, aim_ref, ldt_ref, bre_ref, bim_ref, abr_ref, abi_ref, bbr_ref, bbi_ref):
    a_re, a_im = are_ref[...], aim_ref[...]
    dt = jnp.exp(ldt_ref[...])
    mag = jnp.exp(a_re * dt)
    ab_re = mag * jnp.cos(a_im * dt)
    ab_im = mag * jnp.sin(a_im * dt)
    den = a_re * a_re + a_im * a_im
    co_re = ((ab_re - 1.0) * a_re + ab_im * a_im) / den
    co_im = (ab_im * a_re - (ab_re - 1.0) * a_im) / den
    abr_ref[...] = ab_re
    abi_ref[...] = ab_im
    b_re, b_im = bre_ref[...], bim_ref[...]
    bbr_ref[...] = co_re[:, None, :] * b_re - co_im[:, None, :] * b_im
    bbi_ref[...] = co_re[:, None, :] * b_im + co_im[:, None, :] * b_re


def _block_diag(w):
    a, b = w.shape[1:]
    w = w.reshape(SSM_KB, GROUPS_PER_KB, a, b)
    eye = jnp.eye(GROUPS_PER_KB, dtype=w.dtype)
    return jnp.einsum('kgab,gh->kgahb', w, eye).reshape(SSM_KB, GROUPS_PER_KB * a, GROUPS_PER_KB * b)


def _ssm_weights(a_re, a_im, b_re, b_im, c_re, c_im, log_dt):
    ng, p = a_re.shape
    vm = pl.BlockSpec(memory_space=pltpu.VMEM)
    ab_re, ab_im, bb_re, bb_im = pl.pallas_call(
        _ssm_disc_kernel,
        in_specs=[vm] * 5, out_specs=[vm] * 4,
        out_shape=[jax.ShapeDtypeStruct((ng, p), F32)] * 2
        + [jax.ShapeDtypeStruct((ng, SSM_GROUP, p), F32)] * 2,
    )(a_re, a_im, log_dt.reshape(ng, 1), b_re.transpose(0, 2, 1), b_im.transpose(0, 2, 1))
    return (ab_re.reshape(1, N_STATE), ab_im.reshape(1, N_STATE),
            _block_diag(bb_re).astype(BF16), _block_diag(bb_im).astype(BF16),
            _block_diag(c_re.transpose(0, 2, 1)).astype(BF16),
            _block_diag(c_im.transpose(0, 2, 1)).astype(BF16))


def _ssm_kernel(u_ref, h0r_ref, h0i_ref, abr_ref, abi_ref, bbr_ref, bbi_ref, ccr_ref, cci_ref,
                d_ref, gw_ref, gb_ref, o_ref, hr_ref, hi_ref, sre, sim, *, nb, steps):
    @pl.when(pl.program_id(0) == 0)
    def _():
        hr_ref[...] = h0r_ref[...]
        hi_ref[...] = h0i_ref[...]

    u = u_ref[...]
    ub = u.astype(BF16)
    kw = N_STATE // SSM_KB
    for kb in range(SSM_KB):
        uk = ub[:, kb * LANES:(kb + 1) * LANES]
        sre[:, kb * kw:(kb + 1) * kw] = jnp.dot(uk, bbr_ref[kb], preferred_element_type=F32)
        sim[:, kb * kw:(kb + 1) * kw] = jnp.dot(uk, bbi_ref[kb], preferred_element_type=F32)

    if steps == 1:
        ar, ai = abr_ref[...], abi_ref[...]
        hr, hi = hr_ref[...], hi_ref[...]
        nr = ar * hr - ai * hi + sre[...]
        ni = ar * hi + ai * hr + sim[...]
        sre[...] = nr
        sim[...] = ni
        hr_ref[...] = nr
        hi_ref[...] = ni
    else:
        for cb in range(N_STATE // STATE_CHUNK):
            cols = slice(cb * STATE_CHUNK, (cb + 1) * STATE_CHUNK)
            ar = jnp.broadcast_to(abr_ref[:, cols], (nb, STATE_CHUNK))
            ai = jnp.broadcast_to(abi_ref[:, cols], (nb, STATE_CHUNK))

            def step(t, carry, cols=cols, ar=ar, ai=ai):
                hr, hi = carry
                rows = pl.ds(pl.multiple_of(t * nb, nb), nb)
                nr = ar * hr - ai * hi + sre[rows, cols]
                ni = ar * hi + ai * hr + sim[rows, cols]
                sre[rows, cols] = nr
                sim[rows, cols] = ni
                return nr, ni

            hr, hi = lax.fori_loop(0, steps, step, (hr_ref[:, cols], hi_ref[:, cols]))
            hr_ref[:, cols] = hr
            hi_ref[:, cols] = hi

    ys = []
    for kb in range(SSM_KB):
        cols = slice(kb * kw, (kb + 1) * kw)
        ys.append(_bdot(sre[:, cols], ccr_ref[kb]) - _bdot(sim[:, cols], cci_ref[kb]))
    y = jnp.concatenate(ys, axis=1) + d_ref[...] * u
    g = jax.nn.gelu(y)
    o_ref[...] = g * jax.nn.sigmoid(_bdot(g, gw_ref[...]) + gb_ref[...])


def _ssm(u_rows, h0_re, h0_im, wts, d, glu_w, glu_b, nb, steps_per_call):
    ab_re, ab_im, bb_re, bb_im, cc_re, cc_im = wts
    rows = nb * steps_per_call
    n_calls = u_rows.shape[0] // rows
    const2 = lambda i: (0, 0)
    const3 = lambda i: (0, 0, 0)
    kw = N_STATE // SSM_KB
    return pl.pallas_call(
        functools.partial(_ssm_kernel, nb=nb, steps=steps_per_call),
        grid=(n_calls,),
        in_specs=[pl.BlockSpec((rows, SSM_WIDTH), lambda i: (i, 0)),
                  pl.BlockSpec((nb, N_STATE), const2), pl.BlockSpec((nb, N_STATE), const2),
                  pl.BlockSpec((1, N_STATE), const2), pl.BlockSpec((1, N_STATE), const2),
                  pl.BlockSpec((SSM_KB, LANES, kw), const3), pl.BlockSpec((SSM_KB, LANES, kw), const3),
                  pl.BlockSpec((SSM_KB, kw, LANES), const3), pl.BlockSpec((SSM_KB, kw, LANES), const3),
                  pl.BlockSpec((1, SSM_WIDTH), const2),
                  pl.BlockSpec((SSM_WIDTH, SSM_WIDTH), const2),
                  pl.BlockSpec((1, SSM_WIDTH), const2)],
        out_specs=[pl.BlockSpec((rows, SSM_WIDTH), lambda i: (i, 0)),
                   pl.BlockSpec((nb, N_STATE), const2), pl.BlockSpec((nb, N_STATE), const2)],
        out_shape=[jax.ShapeDtypeStruct(u_rows.shape, F32),
                   jax.ShapeDtypeStruct((nb, N_STATE), F32), jax.ShapeDtypeStruct((nb, N_STATE), F32)],
        scratch_shapes=[pltpu.VMEM((rows, N_STATE), F32), pltpu.VMEM((rows, N_STATE), F32)],
        compiler_params=_params("arbitrary"),
    )(u_rows, h0_re, h0_im, ab_re, ab_im, bb_re, bb_im, cc_re, cc_im, d, glu_w, glu_b)


def _merge_kernel(onsa_ref, ossm_ref, x_ref, gg_ref, wout_ref, ng_ref, xq_ref, x1_ref, qx_ref):
    gg = gg_ref[...]
    a = _rms(onsa_ref[...], gg[:, :NSA_WIDTH])
    b = _rms(ossm_ref[...], gg[:, NSA_WIDTH:])
    m = (jnp.dot(a.astype(BF16), wout_ref[:NSA_WIDTH], preferred_element_type=F32)
         + jnp.dot(b.astype(BF16), wout_ref[NSA_WIDTH:], preferred_element_type=F32))
    x1 = x_ref[...] + _rms(m, ng_ref[1:2])
    x1_ref[...] = x1
    qx_ref[...] = _bdot(_rms(x1, ng_ref[2:3]), xq_ref[...])


def _merge(o_nsa, o_ssm, x, gg, w_out, ng, xq, n_seq, tile):
    rows = x.shape[0]
    nt = rows // n_seq // tile
    row_map = lambda b, t: (b * nt + t, 0)
    const = lambda b, t: (0, 0)
    return pl.pallas_call(
        _merge_kernel,
        grid=(n_seq, nt),
        in_specs=[pl.BlockSpec((tile, NSA_WIDTH), row_map),
                  pl.BlockSpec((tile, SSM_WIDTH), lambda b, t: (t, b)),
                  pl.BlockSpec((tile, D_MODEL), row_map),
                  pl.BlockSpec((1, D_MODEL), const),
                  pl.BlockSpec((D_MODEL, D_MODEL), const),
                  pl.BlockSpec((6, D_MODEL), const),
                  pl.BlockSpec((D_MODEL, MEM_WIDTH), const)],
        out_specs=[pl.BlockSpec((tile, D_MODEL), row_map), pl.BlockSpec((tile, MEM_WIDTH), row_map)],
        out_shape=[jax.ShapeDtypeStruct((rows, D_MODEL), F32),
                   jax.ShapeDtypeStruct((rows, MEM_WIDTH), F32)],
        compiler_params=_params("arbitrary", "arbitrary"),
    )(o_nsa, o_ssm, x, gg, w_out, ng, xq)


def _norm_matmul_kernel(x_ref, g_ref, w_ref, o_ref):
    o_ref[...] = _bdot(_rms(x_ref[...], g_ref[...]), w_ref[...])


def _norm_matmul(x, g, w, tile):
    rows, k = x.shape
    n = w.shape[1]
    return pl.pallas_call(
        _norm_matmul_kernel,
        grid=(rows // tile,),
        in_specs=[pl.BlockSpec((tile, k), lambda i: (i, 0)),
                  pl.BlockSpec((1, k), lambda i: (0, 0)),
                  pl.BlockSpec((k, n), lambda i: (0, 0))],
        out_specs=pl.BlockSpec((tile, n), lambda i: (i, 0)),
        out_shape=jax.ShapeDtypeStruct((rows, n), F32),
        compiler_params=_params("arbitrary"),
    )(x, g, w)


def _cross_prompt_kernel(qx_ref, mkv_ref, o_ref):
    qx = qx_ref[...]
    outs = []
    for h in range(MEM_HEADS):
        cols = slice(h * MEM_HEAD_DIM, (h + 1) * MEM_HEAD_DIM)
        k = mkv_ref[0, :, cols]
        v = mkv_ref[0, :, MEM_WIDTH + h * MEM_HEAD_DIM: MEM_WIDTH + (h + 1) * MEM_HEAD_DIM]
        s = _bdot_nt(qx[:, cols], k) * MEM_SCALE
        e = jnp.exp(s - jnp.max(s, axis=-1, keepdims=True))
        p = e * (1.0 / jnp.sum(e, axis=-1, keepdims=True))
        outs.append(_bdot(p, v))
    o_ref[...] = jnp.concatenate(outs, axis=1)


def _cross_prompt(qx, mkv, n_seq, tile):
    rows = qx.shape[0]
    nt = rows // n_seq // tile
    n_mem = mkv.shape[1]
    return pl.pallas_call(
        _cross_prompt_kernel,
        grid=(n_seq, nt),
        in_specs=[pl.BlockSpec((tile, MEM_WIDTH), lambda b, t: (b * nt + t, 0)),
                  pl.BlockSpec((1, n_mem, 2 * MEM_WIDTH), lambda b, t: (b, 0, 0))],
        out_specs=pl.BlockSpec((tile, MEM_WIDTH), lambda b, t: (b * nt + t, 0)),
        out_shape=jax.ShapeDtypeStruct((rows, MEM_WIDTH), F32),
        compiler_params=_params("arbitrary", "arbitrary"),
    )(qx, mkv)


SAMPLE_GROUP = 8


def _cross_sample_kernel(qx_ref, mkv_ref, o_ref):
    lane_head = _iota2((SAMPLE_GROUP, MEM_WIDTH), 1) // MEM_HEAD_DIM
    own = lane_head == _iota2((SAMPLE_GROUP, MEM_WIDTH), 0)
    qx = qx_ref[...]
    outs = []
    for i in range(SAMPLE_GROUP):
        qm = jnp.where(own, jnp.broadcast_to(qx[i:i + 1], own.shape), 0.0)
        s = _bdot_nt(qm, mkv_ref[i, :, :MEM_WIDTH]) * MEM_SCALE
        e = jnp.exp(s - jnp.max(s, axis=-1, keepdims=True))
        p = e * (1.0 / jnp.sum(e, axis=-1, keepdims=True))
        pv = _bdot(p, mkv_ref[i, :, MEM_WIDTH:])
        outs.append(jnp.sum(jnp.where(own, pv, 0.0), axis=0, keepdims=True))
    o_ref[...] = jnp.concatenate(outs, axis=0)


def _cross_sample(qx, mkv):
    rows = qx.shape[0]
    n_mem = mkv.shape[1]
    return pl.pallas_call(
        _cross_sample_kernel,
        grid=(rows // SAMPLE_GROUP,),
        in_specs=[pl.BlockSpec((SAMPLE_GROUP, MEM_WIDTH), lambda i: (i, 0)),
                  pl.BlockSpec((SAMPLE_GROUP, n_mem, 2 * MEM_WIDTH), lambda i: (i, 0, 0))],
        out_specs=pl.BlockSpec((SAMPLE_GROUP, MEM_WIDTH), lambda i: (i, 0)),
        out_shape=jax.ShapeDtypeStruct((rows, MEM_WIDTH), F32),
        compiler_params=_params("arbitrary"),
    )(qx, mkv)


def _ffn_kernel(x1_ref, oa_ref, ng_ref, xo_ref, wi_ref, wo_ref, o_ref, *, d_ff):
    x2 = x1_ref[...] + _rms(_bdot(oa_ref[...], xo_ref[...]), ng_ref[3:4])
    h = _rms(x2, ng_ref[4:5]).astype(BF16)
    z1 = jnp.dot(h, wi_ref[:, :d_ff], preferred_element_type=F32)
    z2 = jnp.dot(h, wi_ref[:, d_ff:], preferred_element_type=F32)
    y = _bdot(z1 * jax.nn.sigmoid(z1) * z2, wo_ref[...])
    o_ref[...] = x2 + _rms(y, ng_ref[5:6])


def _ffn(x1, oa, ng, xo, wi, wo, tile):
    rows = x1.shape[0]
    d_ff = wo.shape[0]
    const = lambda i: (0, 0)
    once = dict(pipeline_mode=pl.Buffered(1))
    return pl.pallas_call(
        functools.partial(_ffn_kernel, d_ff=d_ff),
        grid=(rows // tile,),
        in_specs=[pl.BlockSpec((tile, D_MODEL), lambda i: (i, 0)),
                  pl.BlockSpec((tile, MEM_WIDTH), lambda i: (i, 0)),
                  pl.BlockSpec((6, D_MODEL), const),
                  pl.BlockSpec((MEM_WIDTH, D_MODEL), const, **once),
                  pl.BlockSpec((D_MODEL, 2 * d_ff), const, **once),
                  pl.BlockSpec((d_ff, D_MODEL), const, **once)],
        out_specs=pl.BlockSpec((tile, D_MODEL), lambda i: (i, 0)),
        out_shape=jax.ShapeDtypeStruct((rows, D_MODEL), F32),
        compiler_params=_params("arbitrary"),
    )(x1, oa, ng, xo, wi, wo)


PAGES_PER_STEP = 16
NEW_TILE = 8


def _bias_rows(dist, tblt_ref):
    v = jnp.broadcast_to(tblt_ref[:, 0:1], dist.shape)
    for k in range(1, N_BUCKETS):
        v = jnp.where(dist >= _BUCKET_START[k], tblt_ref[:, k:k + 1], v)
    return v


def _sample_bias_kernel(tblt_ref, bw_ref, bc_ref, bn_ref, bs_ref, b0_ref, *, past_len):
    dist = (WINDOW - 1) - _iota2(bw_ref.shape, 1)
    bw_ref[...] = jnp.where((dist >= 0) & (dist < WINDOW), _bias_rows(dist, tblt_ref), NEG)
    dist = past_len - (_iota2(bc_ref.shape, 1) * CMP_BLOCK + (CMP_BLOCK - 1))
    bc_ref[...] = jnp.where(dist >= 0, _bias_rows(dist, tblt_ref), NEG)
    lane = _iota2(bn_ref.shape, 1)
    dist = past_len - ((past_len // CMP_BLOCK + lane) * CMP_BLOCK + (CMP_BLOCK - 1))
    bn_ref[...] = jnp.where((dist >= 0) & (lane < SEL_BLOCK // CMP_BLOCK),
                            _bias_rows(jnp.maximum(dist, 0), tblt_ref), NEG)
    blk = lax.broadcasted_iota(I32, bs_ref.shape, 0)
    dist = past_len - blk * SEL_BLOCK - lax.broadcasted_iota(I32, bs_ref.shape, 2)
    bs_ref[...] = _bias_rows(dist, tblt_ref)
    b0_ref[...] = _bias_rows(jnp.zeros(b0_ref.shape, I32), tblt_ref)


def _sample_bias(rel_bias, past_len):
    vm = pl.BlockSpec(memory_space=pltpu.VMEM)
    shapes = [(N_HEADS, WINDOW), (N_HEADS, past_len // CMP_BLOCK), (N_HEADS, LANES),
              (past_len // SEL_BLOCK, N_HEADS, SEL_BLOCK), (N_HEADS, LANES)]
    return pl.pallas_call(
        functools.partial(_sample_bias_kernel, past_len=past_len),
        in_specs=[vm], out_specs=[vm] * len(shapes),
        out_shape=[jax.ShapeDtypeStruct(s, F32) for s in shapes],
    )(rel_bias.T)


def _compress_sample_kernel(pt_ref, *refs):
    pages = refs[:PAGES_PER_STEP]
    pos_ref, proj_ref, o_ref = refs[PAGES_PER_STEP:]
    pos = pos_ref[...]
    sums = jnp.concatenate([_block_sums(pg[0], pos) for pg in pages], axis=0)
    o_ref[0] = _dot3(sums, proj_ref[...])


def _compress_sample(pool, page_table_flat, n_seq, n_pages, pos, proj):
    steps = n_pages // PAGES_PER_STEP
    blocks_per_step = PAGES_PER_STEP * PAGE_SIZE // CMP_BLOCK

    def page_map(i):
        return lambda b, s, pt: (pt[b * n_pages + s * PAGES_PER_STEP + i], 0, 0)

    grid_spec = pltpu.PrefetchScalarGridSpec(
        num_scalar_prefetch=1,
        grid=(n_seq, steps),
        in_specs=[pl.BlockSpec((1, PAGE_SIZE, KV_COLS), page_map(i)) for i in range(PAGES_PER_STEP)]
        + [pl.BlockSpec((CMP_BLOCK, KV_COLS), lambda b, s, pt: (0, 0)),
           pl.BlockSpec((KV_COLS, KV_COLS), lambda b, s, pt: (0, 0))],
        out_specs=pl.BlockSpec((1, blocks_per_step, KV_COLS), lambda b, s, pt: (b, s, 0)),
    )
    return pl.pallas_call(
        _compress_sample_kernel,
        grid_spec=grid_spec,
        out_shape=jax.ShapeDtypeStruct((n_seq, steps * blocks_per_step, KV_COLS), F32),
        compiler_params=_params("arbitrary", "arbitrary"),
    )(page_table_flat, *([pool] * PAGES_PER_STEP), pos, proj)


def _cmp_sample_kernel(qm_ref, blk_ref, kvc_ref, pos_ref, proj_ref, bc_ref, bn_ref, oc_ref, idx_ref,
                       *, past_len):
    half = N_KV * HEAD_DIM
    qm = qm_ref[0]
    n_cmp = blk_ref.shape[1]
    bias = bc_ref[...]
    mask = bias > MASKED_BELOW
    s = jnp.where(mask, _dot3_nt(qm, blk_ref[0, :, :half]) * SCALE + bias, NEG)

    new_sum = jnp.concatenate([kvc_ref[0] * pos_ref[0:1], jnp.zeros((NEW_TILE - 1, KV_COLS), F32)], axis=0)
    new_blk = _dot3(new_sum, proj_ref[...])
    bias_n = bn_ref[:, :NEW_TILE]
    mask_n = bias_n > MASKED_BELOW
    s_n = jnp.where(mask_n, _dot3_nt(qm, new_blk[:, :half]) * SCALE + bias_n, NEG)

    m = jnp.maximum(jnp.max(s, axis=-1, keepdims=True), jnp.max(s_n, axis=-1, keepdims=True))
    e = jnp.where(mask, jnp.exp(s - m), 0.0)
    e_n = jnp.where(mask_n, jnp.exp(s_n - m), 0.0)
    inv = 1.0 / jnp.maximum(jnp.sum(e, axis=-1, keepdims=True) + jnp.sum(e_n, axis=-1, keepdims=True), TINY)
    p = e * inv
    p_n = e_n * inv
    oc_ref[0] = _bdot(p, blk_ref[0, :, half:]) + _bdot(p_n, new_blk[:, half:])

    ps = jnp.concatenate(
        [jnp.broadcast_to(jnp.sum(p[g * GQA:(g + 1) * GQA], axis=0, keepdims=True), (GQA, n_cmp))
         for g in range(N_KV)], axis=0)
    lane = _iota2(ps.shape, 1)
    pair = ps + pltpu.roll(ps, n_cmp - 1, 1)
    cur = past_len // SEL_BLOCK
    j = lane // (SEL_BLOCK // CMP_BLOCK)
    forced = (j == 0) | (j == cur) | (j == cur - 1)
    score = jnp.where(lane % 2 == 0, jnp.where(forced, pair + FORCE, pair), -jnp.inf)
    slot = _iota2((N_HEADS, LANES), 1)
    ids = jnp.zeros((N_HEADS, LANES), I32)
    for it in range(N_SEL - 1):
        top = jnp.max(score, axis=-1, keepdims=True)
        first = jnp.min(jnp.where(score == top, lane, n_cmp), axis=-1, keepdims=True)
        ids = jnp.where(slot == it, first // (SEL_BLOCK // CMP_BLOCK), ids)
        score = jnp.where(lane == first, -jnp.inf, score)
    idx_ref[0] = jnp.where(slot == N_SEL - 1, cur, ids)


def _cmp_sample(qm, blocks, kvc_new, pos, proj, bc, bn, past_len):
    n_seq, n_cmp, _ = blocks.shape
    assert past_len // SEL_BLOCK > N_SEL and past_len % SEL_BLOCK == 0
    const = lambda b: (0, 0)
    return pl.pallas_call(
        functools.partial(_cmp_sample_kernel, past_len=past_len),
        grid=(n_seq,),
        in_specs=[pl.BlockSpec((1, N_HEADS, LANES), lambda b: (b, 0, 0)),
                  pl.BlockSpec((1, n_cmp, KV_COLS), lambda b: (b, 0, 0)),
                  pl.BlockSpec((1, 1, KV_COLS), lambda b: (b, 0, 0)),
                  pl.BlockSpec((CMP_BLOCK, KV_COLS), const),
                  pl.BlockSpec((KV_COLS, KV_COLS), const),
                  pl.BlockSpec((N_HEADS, n_cmp), const),
                  pl.BlockSpec((N_HEADS, LANES), const)],
        out_specs=[pl.BlockSpec((1, N_HEADS, LANES), lambda b: (b, 0, 0)),
                   pl.BlockSpec((1, N_HEADS, LANES), lambda b: (b, 0, 0))],
        out_shape=[jax.ShapeDtypeStruct((n_seq, N_HEADS, LANES), F32),
                   jax.ShapeDtypeStruct((n_seq, N_HEADS, LANES), I32)],
        compiler_params=_params("arbitrary"),
    )(qm, blocks, kvc_new, pos, proj, bc, bn)


def _sel_win_sample_kernel(idx_ref, pt_ref, *refs, n_past_blocks):
    n_pages = N_KV * N_SEL
    pages = refs[:n_pages]
    (buf_ref, qm_ref, kvs_ref, kvw_ref, gate_ref, oc_ref, bw_ref, bs_ref, b0_ref,
     o_ref, nbuf_ref) = refs[n_pages:]
    b = pl.program_id(0)
    half = N_KV * HEAD_DIM
    qm = qm_ref[0]

    buf = buf_ref[0]
    shifted = pltpu.roll(buf, WINDOW - 1, 0)
    nbuf = jnp.where(_iota2(buf.shape, 0) == WINDOW - 1, jnp.broadcast_to(kvw_ref[0], buf.shape), shifted)
    nbuf_ref[0] = nbuf
    bias = bw_ref[...]
    p = _masked_softmax(_bdot_nt(qm, nbuf[:, :half]) * SCALE + bias, bias > MASKED_BELOW)
    o_win = _bdot(p, nbuf[:, half:])

    o_sel = []
    for g in range(N_KV):
        kcols = slice(g * HEAD_DIM, (g + 1) * HEAD_DIM)
        vcols = slice(half + g * HEAD_DIM, half + (g + 1) * HEAD_DIM)
        rows = slice(g * GQA, (g + 1) * GQA)
        qg = qm[rows, kcols]
        ks, vs, biases = [], [], []
        has_new = False
        for k in range(N_SEL):
            i = idx_ref[(b * N_KV + g) * N_SEL + k]
            is_past = i < n_past_blocks
            has_new = jnp.logical_or(has_new, jnp.logical_not(is_past))
            blk = pages[g * N_SEL + k][0]
            ks.append(blk[:, kcols])
            vs.append(blk[:, vcols])
            tile = bs_ref[jnp.minimum(i, n_past_blocks - 1)]
            biases.append(jnp.where(is_past, tile[rows], NEG))
        bias = jnp.concatenate(biases, axis=1)
        mask = bias > MASKED_BELOW
        s = jnp.where(mask, _bdot_nt(qg, jnp.concatenate(ks, axis=0)) * SCALE + bias, NEG)
        k_new = kvs_ref[0][:, kcols]
        v_new = kvs_ref[0][:, vcols]
        s_new = jnp.where(has_new,
                          jnp.sum(qg * k_new, axis=-1, keepdims=True) * SCALE + b0_ref[rows, 0:1], NEG)
        m = jnp.maximum(jnp.max(s, axis=-1, keepdims=True), s_new)
        e = jnp.where(mask, jnp.exp(s - m), 0.0)
        e_new = jnp.where(has_new, jnp.exp(s_new - m), 0.0)
        inv = 1.0 / jnp.maximum(jnp.sum(e, axis=-1, keepdims=True) + e_new, TINY)
        o_sel.append(_bdot(e * inv, jnp.concatenate(vs, axis=0)) + (e_new * inv) * v_new)
    o_sel = jnp.concatenate(o_sel, axis=0)

    first_group = _iota2((N_HEADS, HEAD_DIM), 0) < GQA
    oc = oc_ref[0]
    o_cmp = jnp.where(first_group, oc[:, :HEAD_DIM], oc[:, HEAD_DIM:])
    o_win = jnp.where(first_group, o_win[:, :HEAD_DIM], o_win[:, HEAD_DIM:])
    gate = gate_ref[0]
    o_ref[0] = gate[:, 0:1] * o_cmp + gate[:, 1:2] * o_sel + gate[:, 2:3] * o_win


def _sel_win_sample(idx_flat, pt_flat, pool, buf, qm, kvs_new, kvw_new, gates, o_cmp, bw, bs, b0, n_pages):
    n_seq = buf.shape[0]
    n_past_blocks = bs.shape[0]
    bpp = PAGE_SIZE // SEL_BLOCK

    def page_map(g, k):
        def f(b, idx, pt):
            i = jnp.minimum(idx[(b * N_KV + g) * N_SEL + k], n_past_blocks - 1)
            return (pt[b * n_pages + i // bpp] * bpp + i % bpp, 0, 0)
        return f

    per_seq3 = lambda b, idx, pt: (b, 0, 0)
    const2 = lambda b, idx, pt: (0, 0)
    head_tile = pl.BlockSpec((1, N_HEADS, LANES), per_seq3)
    row_tile = pl.BlockSpec((1, 1, KV_COLS), per_seq3)
    grid_spec = pltpu.PrefetchScalarGridSpec(
        num_scalar_prefetch=2,
        grid=(n_seq,),
        in_specs=[pl.BlockSpec((1, SEL_BLOCK, KV_COLS), page_map(g, k))
                  for g in range(N_KV) for k in range(N_SEL)]
        + [pl.BlockSpec((1, WINDOW, KV_COLS), per_seq3), head_tile, row_tile, row_tile, head_tile,
           head_tile,
           pl.BlockSpec((N_HEADS, WINDOW), const2),
           pl.BlockSpec((n_past_blocks, N_HEADS, SEL_BLOCK), lambda b, idx, pt: (0, 0, 0)),
           pl.BlockSpec((N_HEADS, LANES), const2)],
        out_specs=[pl.BlockSpec((1, N_HEADS, HEAD_DIM), per_seq3),
                   pl.BlockSpec((1, WINDOW, KV_COLS), per_seq3)],
    )
    return pl.pallas_call(
        functools.partial(_sel_win_sample_kernel, n_past_blocks=n_past_blocks),
        grid_spec=grid_spec,
        out_shape=[jax.ShapeDtypeStruct((n_seq, N_HEADS, HEAD_DIM), F32),
                   jax.ShapeDtypeStruct(buf.shape, F32)],
        compiler_params=_params("arbitrary"),
    )(idx_flat, pt_flat, *([pool] * (N_KV * N_SEL)), buf, qm, kvs_new, kvw_new, gates, o_cmp, bw, bs, b0)


PROJ_TILE = 512
ROW_TILE = 256
SCAN_STEPS = 128


def kernel(x_prompt, x_sample, cache_kv_cmp, cache_kv_sel, cache_kv_win, state_ssm_re, state_ssm_im,
           cache_mem_kv, page_table, mem_prompt, w_in, w_out, norm_g, grp_norm_g, cmp_pos, cmp_proj,
           rel_bias, ssm_a_re, ssm_a_im, ssm_b_re, ssm_b_im, ssm_c_re, ssm_c_im, ssm_d, ssm_log_dt,
           glu_w, glu_b, mem_norm_g, xq, xkv, xo, ffn_wi, ffn_wo):
    depth = w_in.shape[0]
    n_seq, seq, _ = x_prompt.shape
    n_dec, dec_seq, _ = x_sample.shape
    assert dec_seq == 1, "the sample kernels handle one new token per sequence"
    n_pages = page_table.shape[1]
    past_len = n_pages * PAGE_SIZE
    n_phys = cache_kv_cmp.shape[1]
    n_mem = mem_prompt.shape[1]
    assert cache_kv_win.shape[2] == WINDOW and seq >= WINDOW

    bsel, bcmp, bwin = _prompt_bias(rel_bias, seq)
    bw_s, bc_s, bn_s, bs_s, b0_s = _sample_bias(rel_bias, past_len)
    pt_flat = page_table.reshape(-1)
    kv5 = (2, N_KV, HEAD_DIM)

    xp = x_prompt.reshape(n_seq * seq, D_MODEL)
    xs = x_sample.reshape(n_dec, D_MODEL)
    outs = [[] for _ in range(11)]
    for l in range(depth):
        ng = norm_g[l]
        w_packed = _pack_w_in(w_in[l])
        w_out_b, xq_b, xkv_b, xo_b = (w[l].astype(BF16) for w in (w_out, xq, xkv, xo))
        wi_b, wo_b, glu_w_b = ffn_wi[l].astype(BF16), ffn_wo[l].astype(BF16), glu_w[l].astype(BF16)
        gg = grp_norm_g[l][None]
        ssm_w = _ssm_weights(ssm_a_re[l], ssm_a_im[l], ssm_b_re[l], ssm_b_im[l], ssm_c_re[l], ssm_c_im[l],
                             ssm_log_dt[l])
        ssm_tail = (ssm_d[l][None], glu_w_b, glu_b[l][None])
        pos, proj_g = _pack_compress(cmp_pos[l], cmp_proj[l], group_major=True)
        _, proj_c = _pack_compress(cmp_pos[l], cmp_proj[l], group_major=False)

        q, kvc, kvs, kvw, kvs_g, kvw_g, gates, u = _inproj(xp, ng[0:1], w_packed, n_seq, PROJ_TILE)
        blocks = _compress_prompt(kvc.reshape(n_seq, seq, KV_COLS), pos, proj_g)
        kvw_pad = jnp.pad(kvw_g.reshape(n_seq, seq, KV_COLS), ((0, 0), (WINDOW, 0), (0, 0)))
        o_nsa = _nsa_prompt(q.reshape(n_seq, seq, NSA_WIDTH), gates.reshape(n_seq, seq, N_KV * LANES), blocks,
                            kvs_g.reshape(n_seq, seq, KV_COLS), kvw_pad, bsel, bcmp, bwin)
        zeros = jnp.zeros((n_seq, N_STATE), F32)
        o_ssm, h_re, h_im = _ssm(u.reshape(seq * n_seq, SSM_WIDTH), zeros, zeros, ssm_w, *ssm_tail,
                                 n_seq, SCAN_STEPS)
        x1, qx = _merge(o_nsa.reshape(n_seq * seq, NSA_WIDTH), o_ssm.reshape(seq, n_seq * SSM_WIDTH), xp,
                        gg, w_out_b, ng, xq_b, n_seq, ROW_TILE)
        mkv = _norm_matmul(mem_prompt.reshape(n_seq * n_mem, D_MODEL), mem_norm_g[l][None], xkv_b, ROW_TILE)
        oa = _cross_prompt(qx, mkv.reshape(n_seq, n_mem, 2 * MEM_WIDTH), n_seq, ROW_TILE)
        xp = _ffn(x1, oa, ng, xo_b, wi_b, wo_b, ROW_TILE)
        outs[0].append(kvc.reshape(n_seq, seq, *kv5))
        outs[1].append(kvs.reshape(n_seq, seq, *kv5))
        outs[2].append(kvw.reshape(n_seq, seq, *kv5)[:, seq - WINDOW:])
        outs[3].append(h_re.reshape(n_seq, N_SSM_GROUPS, SSM_STATE))
        outs[4].append(h_im.reshape(n_seq, N_SSM_GROUPS, SSM_STATE))
        outs[5].append(mkv.reshape(n_seq, n_mem, 2, MEM_HEADS, MEM_HEAD_DIM))

        q, kvc, kvs, kvw, _, _, gates, u = _inproj(xs, ng[0:1], w_packed, 1, n_dec)
        head_group = (jnp.arange(N_HEADS)[:, None] // GQA == jnp.arange(N_KV)[None, :]).astype(F32)
        qm = (q.reshape(n_dec, N_HEADS, 1, HEAD_DIM) * head_group[None, :, :, None]).reshape(n_dec, N_HEADS, LANES)
        gates_h = gates.reshape(n_dec, N_KV, LANES)[:, :, :GQA * N_BRANCH].reshape(n_dec, N_HEADS, N_BRANCH)
        gates_h = jnp.pad(gates_h, ((0, 0), (0, 0), (0, LANES - N_BRANCH)))
        blocks = _compress_sample(cache_kv_cmp[l].reshape(n_phys, PAGE_SIZE, KV_COLS), pt_flat, n_dec, n_pages,
                                  pos, proj_c)
        o_cmp, ids = _cmp_sample(qm, blocks, kvc.reshape(n_dec, 1, KV_COLS), pos, proj_c, bc_s, bn_s, past_len)
        idx_flat = ids[:, ::GQA, :N_SEL].reshape(-1)
        o_nsa, new_buf = _sel_win_sample(
            idx_flat, pt_flat, cache_kv_sel[l].reshape(n_phys * (PAGE_SIZE // SEL_BLOCK), SEL_BLOCK, KV_COLS),
            cache_kv_win[l].reshape(n_dec, WINDOW, KV_COLS), qm, kvs.reshape(n_dec, 1, KV_COLS),
            kvw.reshape(n_dec, 1, KV_COLS), gates_h, o_cmp, bw_s, bs_s, b0_s, n_pages)
        o_ssm, h_re, h_im = _ssm(u, state_ssm_re[l].reshape(n_dec, N_STATE), state_ssm_im[l].reshape(n_dec, N_STATE),
                                 ssm_w, *ssm_tail, n_dec, 1)
        x1, qx = _merge(o_nsa.reshape(n_dec, NSA_WIDTH), o_ssm, xs, gg, w_out_b, ng, xq_b, 1, n_dec)
        oa = _cross_sample(qx, cache_mem_kv[l].reshape(n_dec, n_mem, 2 * MEM_WIDTH))
        xs = _ffn(x1, oa, ng, xo_b, wi_b, wo_b, n_dec)
        outs[6].append(kvc.reshape(n_dec, 1, *kv5))
        outs[7].append(kvs.reshape(n_dec, 1, *kv5))
        outs[8].append(new_buf.reshape(n_dec, WINDOW, *kv5))
        outs[9].append(h_re.reshape(n_dec, N_SSM_GROUPS, SSM_STATE))
        outs[10].append(h_im.reshape(n_dec, N_SSM_GROUPS, SSM_STATE))

    stacked = [jnp.stack(o, axis=0) for o in outs]
    return (xp.reshape(x_prompt.shape), xs.reshape(x_sample.shape), *stacked)
```

```python
import functools
import math

import numpy as np
import jax
import jax.numpy as jnp
from jax import lax
from jax.experimental import pallas as pl
from jax.experimental.pallas import tpu as pltpu

F32 = jnp.float32
BF16 = jnp.bfloat16
I32 = jnp.int32

D_MODEL = 1024
HEAD_DIM = 64
N_HEADS = 8
N_KV = 2
GQA = N_HEADS // N_KV
NSA_WIDTH = N_HEADS * HEAD_DIM
SSM_WIDTH = 512
KV_COLS = 2 * N_KV * HEAD_DIM
KV_HALF = N_KV * HEAD_DIM
N_BRANCH = 3
CMP_BLOCK = 32
SEL_BLOCK = 64
N_SEL = 16
WINDOW = 512
PAGE_SIZE = 128
SSM_GROUP = 16
N_SSM_GROUPS = 32
SSM_STATE = 64
N_STATE = N_SSM_GROUPS * SSM_STATE
N_BUCKETS = 32
MAX_DISTANCE = 128
MEM_HEADS = 4
MEM_HEAD_DIM = 128
MEM_WIDTH = MEM_HEADS * MEM_HEAD_DIM
EPS = 1e-6
NEG = -1e30
MASKED_BELOW = -1e29
TINY = 1e-30
FORCE = 1e4
SCALE = HEAD_DIM ** -0.5
MEM_SCALE = MEM_HEAD_DIM ** -0.5

LANES = 128
Q_TILE = 128
FAR_TILES = 4
VMEM_LIMIT = 56 * 1024 * 1024

AUG_SEL = HEAD_DIM
AUG_ONE = AUG_SEL + 32
GATE_ROWS = 16


def _bucket_starts():
    n = np.arange(0, 4 * MAX_DISTANCE)
    exact = N_BUCKETS // 2
    nf = np.maximum(n, exact).astype(np.float32)
    big = exact + (np.log(nf / exact) / np.float32(math.log(MAX_DISTANCE / exact))
                   * (N_BUCKETS - exact)).astype(np.int32)
    bucket = np.where(n < exact, n, np.minimum(big, N_BUCKETS - 1))
    return [int(np.argmax(bucket >= k)) for k in range(N_BUCKETS)]


_BUCKET_START = _bucket_starts()
assert _BUCKET_START[-1] <= Q_TILE


def _params(*sem):
    return pltpu.CompilerParams(dimension_semantics=sem, vmem_limit_bytes=VMEM_LIMIT)


def _bdot(a, b):
    return jnp.dot(a.astype(BF16), b.astype(BF16), preferred_element_type=F32)


def _bdot_nt(a, b):
    return lax.dot_general(a.astype(BF16), b.astype(BF16), (((1,), (1,)), ((), ())),
                           preferred_element_type=F32)


def _split(a):
    hi = a.astype(BF16)
    return hi, (a - hi.astype(F32)).astype(BF16)


def _dot3(a, b):
    ah, al = _split(a)
    bh, bl = _split(b)
    d = functools.partial(jnp.dot, preferred_element_type=F32)
    return d(ah, bh) + (d(ah, bl) + d(al, bh))


def _dot3_nt(a, b):
    ah, al = _split(a)
    bh, bl = _split(b)
    d = functools.partial(lax.dot_general, dimension_numbers=(((1,), (1,)), ((), ())),
                          preferred_element_type=F32)
    return d(ah, bh) + (d(ah, bl) + d(al, bh))


def _rms(x, g):
    return x * lax.rsqrt(jnp.mean(x * x, axis=-1, keepdims=True) + EPS) * g


def _masked_softmax(s, mask, axis=-1):
    s = jnp.where(mask, s, NEG)
    m = jnp.max(s, axis=axis, keepdims=True)
    e = jnp.where(mask, jnp.exp(s - m), 0.0)
    return e * (1.0 / jnp.maximum(jnp.sum(e, axis=axis, keepdims=True), TINY))


def _bias_of_dist(dist, tbl_ref, h):
    v = jnp.full(dist.shape, tbl_ref[0, h], F32)
    for k in range(1, N_BUCKETS):
        v = jnp.where(dist >= _BUCKET_START[k], tbl_ref[k, h], v)
    return v


def _iota2(shape, dim):
    return lax.broadcasted_iota(I32, shape, dim)


def _rows_minor(a):
    n, _, rows = a.shape
    return a.reshape(n, 2, N_KV, HEAD_DIM, rows).transpose(0, 4, 1, 2, 3)


def _rows_minor_view(a):
    n, rows = a.shape[:2]
    return a.transpose(0, 2, 3, 4, 1).reshape(n, KV_COLS, rows)


def _cmp_block_of_row(row, n_cmp):
    return jnp.where(row < n_cmp // 2, 2 * row, 2 * (row - n_cmp // 2) + 1)


def _prompt_bias_kernel(tbl_ref, bcmp_ref, near_ref, *, n_cmp):
    h = pl.program_id(0)
    qi = pl.program_id(1)
    last = tbl_ref[N_BUCKETS - 1, h]

    blk = _cmp_block_of_row(_iota2((n_cmp, Q_TILE), 0), n_cmp)
    dist = qi * Q_TILE + _iota2((n_cmp, Q_TILE), 1) - (blk * CMP_BLOCK + (CMP_BLOCK - 1))
    bcmp_ref[0, 0] = jnp.where(dist >= 0, _bias_of_dist(dist, tbl_ref, h), NEG)

    @pl.when(qi == 0)
    def _():
        key = _iota2((Q_TILE, Q_TILE), 0)
        qry = _iota2((Q_TILE, Q_TILE), 1)
        d0 = qry - key
        near_ref[0, 0] = jnp.where(d0 >= 0, _bias_of_dist(d0, tbl_ref, h) - last, NEG)
        d1 = Q_TILE + qry - key
        near_ref[0, 1] = _bias_of_dist(d1, tbl_ref, h) - last
        near_ref[0, 2] = jnp.where(key > qry, 0.0, NEG)


def _prompt_bias(rel_bias, seq):
    nq = seq // Q_TILE
    n_cmp = seq // CMP_BLOCK
    return pl.pallas_call(
        functools.partial(_prompt_bias_kernel, n_cmp=n_cmp),
        grid=(N_HEADS, nq),
        in_specs=[pl.BlockSpec(memory_space=pltpu.SMEM)],
        out_specs=[pl.BlockSpec((1, 1, n_cmp, Q_TILE), lambda h, qi: (qi, h // GQA, 0, h % GQA)),
                   pl.BlockSpec((1, 3, Q_TILE, Q_TILE), lambda h, qi: (h // GQA, 0, 0, h % GQA))],
        out_shape=[jax.ShapeDtypeStruct((nq, N_KV, n_cmp, GQA * Q_TILE), F32),
                   jax.ShapeDtypeStruct((N_KV, 3, Q_TILE, GQA * Q_TILE), F32)],
        compiler_params=_params("arbitrary", "arbitrary"),
    )(rel_bias)


OFF_KV = NSA_WIDTH
OFF_GATE = OFF_KV + 3 * KV_COLS
OFF_U = OFF_GATE + N_HEADS * N_BRANCH


def _k_cols(w_in, which):
    base = OFF_KV + which * KV_COLS
    return [w_in[:, base + g * HEAD_DIM: base + (g + 1) * HEAD_DIM] for g in range(N_KV)]


def _pack_w_prompt(w_in):
    w_row = jnp.concatenate([w_in[:, OFF_KV:OFF_KV + KV_COLS]] + _k_cols(w_in, 1) + _k_cols(w_in, 2)
                            + [w_in[:, OFF_U:]], axis=1)
    gates = w_in[:, OFF_GATE:OFF_U].reshape(-1, N_KV, GQA * N_BRANCH)
    gates = jnp.pad(gates, ((0, 0), (0, 0), (0, GATE_ROWS - GQA * N_BRANCH))).reshape(-1, N_KV * GATE_ROWS)
    w_t = jnp.concatenate([w_in[:, :OFF_GATE], gates], axis=1).T
    return w_row.astype(BF16), w_t.astype(BF16)


def _inproj_prompt_kernel(x_ref, g_ref, wr_ref, wt_ref, kvc_ref, ksa_ref, kwa_ref, u_ref, qt_ref, kvct_ref,
                          kvst_ref, kvwt_ref, vst_ref, vwt_ref, gt_ref, *, tile):
    hb = _rms(x_ref[...], g_ref[...]).astype(BF16)
    z = jnp.dot(hb, wr_ref[...], preferred_element_type=F32)
    kvc_ref[...] = z[:, :KV_COLS]
    u_ref[...] = z[:, KV_COLS + 4 * HEAD_DIM:]

    pos = pl.program_id(1) * tile + _iota2((tile, LANES - HEAD_DIM), 0)
    col = _iota2((tile, LANES - HEAD_DIM), 1)
    ones = ((col >= AUG_ONE - HEAD_DIM) & (col < AUG_ONE - HEAD_DIM + 2)).astype(F32)
    aug_sel = jnp.where(col == pos // SEL_BLOCK, -NEG, ones)
    k0 = KV_COLS
    ksa_ref[...] = jnp.concatenate(
        [z[:, k0:k0 + HEAD_DIM], aug_sel, z[:, k0 + HEAD_DIM:k0 + 2 * HEAD_DIM], aug_sel], axis=1).astype(BF16)
    k0 = KV_COLS + 2 * HEAD_DIM
    kwa_ref[...] = jnp.concatenate(
        [z[:, k0:k0 + HEAD_DIM], ones, z[:, k0 + HEAD_DIM:k0 + 2 * HEAD_DIM], ones], axis=1).astype(BF16)

    zt = lax.dot_general(wt_ref[...], hb, (((1,), (1,)), ((), ())), preferred_element_type=F32)
    qt_ref[0] = zt[:NSA_WIDTH]
    kvct_ref[0] = zt[OFF_KV:OFF_KV + KV_COLS]
    kvst_ref[0] = zt[OFF_KV + KV_COLS:OFF_KV + 2 * KV_COLS]
    kvwt_ref[0] = zt[OFF_KV + 2 * KV_COLS:OFF_GATE]
    gt_ref[0] = jax.nn.sigmoid(zt[OFF_GATE:])
    v_sel = OFF_KV + KV_COLS + KV_HALF
    v_win = OFF_KV + 2 * KV_COLS + KV_HALF
    for j in range(tile // Q_TILE):
        cols = slice(j * Q_TILE, (j + 1) * Q_TILE)
        vst_ref[0, j] = zt[v_sel:v_sel + KV_HALF, cols].astype(BF16)
        vwt_ref[0, j] = zt[v_win:v_win + KV_HALF, cols].astype(BF16)


def _inproj_prompt(x, g, w_row, w_t, n_seq, tile):
    rows = x.shape[0]
    seq = rows // n_seq
    assert seq // SEL_BLOCK <= AUG_ONE - AUG_SEL
    nt = seq // tile
    row_map = lambda b, t: (b * nt + t, 0)
    t_map = lambda b, t: (b, 0, t)
    tiles = tile // Q_TILE
    n_gate = N_KV * GATE_ROWS
    out = [
        (pl.BlockSpec((tile, KV_COLS), row_map), jax.ShapeDtypeStruct((rows, KV_COLS), F32)),
        (pl.BlockSpec((tile, N_KV * LANES), row_map), jax.ShapeDtypeStruct((rows, N_KV * LANES), BF16)),
        (pl.BlockSpec((tile, N_KV * LANES), row_map), jax.ShapeDtypeStruct((rows, N_KV * LANES), BF16)),
        (pl.BlockSpec((tile, SSM_WIDTH), lambda b, t: (t, b)), jax.ShapeDtypeStruct((seq, n_seq * SSM_WIDTH), F32)),
        (pl.BlockSpec((1, NSA_WIDTH, tile), t_map), jax.ShapeDtypeStruct((n_seq, NSA_WIDTH, seq), F32)),
        (pl.BlockSpec((1, KV_COLS, tile), t_map), jax.ShapeDtypeStruct((n_seq, KV_COLS, seq), F32)),
        (pl.BlockSpec((1, KV_COLS, tile), t_map), jax.ShapeDtypeStruct((n_seq, KV_COLS, seq), F32)),
        (pl.BlockSpec((1, KV_COLS, tile), t_map), jax.ShapeDtypeStruct((n_seq, KV_COLS, seq), F32)),
        (pl.BlockSpec((1, tiles, KV_HALF, Q_TILE), lambda b, t: (b, t, 0, 0)),
         jax.ShapeDtypeStruct((n_seq, seq // Q_TILE, KV_HALF, Q_TILE), BF16)),
        (pl.BlockSpec((1, tiles, KV_HALF, Q_TILE), lambda b, t: (b, t, 0, 0)),
         jax.ShapeDtypeStruct((n_seq, seq // Q_TILE, KV_HALF, Q_TILE), BF16)),
        (pl.BlockSpec((1, n_gate, tile), t_map), jax.ShapeDtypeStruct((n_seq, n_gate, seq), F32)),
    ]
    return pl.pallas_call(
        functools.partial(_inproj_prompt_kernel, tile=tile),
        grid=(n_seq, nt),
        in_specs=[pl.BlockSpec((tile, D_MODEL), row_map),
                  pl.BlockSpec((1, D_MODEL), lambda b, t: (0, 0)),
                  pl.BlockSpec(w_row.shape, lambda b, t: (0, 0)),
                  pl.BlockSpec(w_t.shape, lambda b, t: (0, 0))],
        out_specs=[o[0] for o in out],
        out_shape=[o[1] for o in out],
        compiler_params=_params("arbitrary", "arbitrary"),
    )(x, g, w_row, w_t)


S_Q = 0
S_KV = S_Q + NSA_WIDTH
S_GATE = S_KV + 3 * KV_COLS
S_U = S_GATE + LANES
S_END = S_U + SSM_WIDTH


def _pack_w_sample(w_in):
    gates = jnp.pad(w_in[:, OFF_GATE:OFF_U], ((0, 0), (0, LANES - N_HEADS * N_BRANCH)))
    return jnp.concatenate([w_in[:, :OFF_GATE], gates, w_in[:, OFF_U:]], axis=1).astype(BF16)


def _inproj_sample_kernel(x_ref, g_ref, w_ref, q_ref, kv_ref, gate_ref, u_ref):
    z = jnp.dot(_rms(x_ref[...], g_ref[...]).astype(BF16), w_ref[...], preferred_element_type=F32)
    q_ref[...] = z[:, S_Q:S_KV]
    kv_ref[...] = z[:, S_KV:S_GATE]
    gate_ref[...] = jax.nn.sigmoid(z[:, S_GATE:S_U])
    u_ref[...] = z[:, S_U:S_END]


def _inproj_sample(x, g, w):
    rows = x.shape[0]
    vm = pl.BlockSpec(memory_space=pltpu.VMEM)
    widths = [NSA_WIDTH, 3 * KV_COLS, LANES, SSM_WIDTH]
    return pl.pallas_call(
        _inproj_sample_kernel,
        in_specs=[vm] * 3, out_specs=[vm] * 4,
        out_shape=[jax.ShapeDtypeStruct((rows, w_), F32) for w_ in widths],
        compiler_params=pltpu.CompilerParams(vmem_limit_bytes=VMEM_LIMIT),
    )(x, g, w)


def _pack_compress(cmp_pos, cmp_proj, group_major):
    pos = jnp.broadcast_to(cmp_pos.transpose(1, 0, 2)[:, :, None, :],
                           (CMP_BLOCK, 2, N_KV, HEAD_DIM)).reshape(CMP_BLOCK, KV_COLS)
    eye = jnp.eye(2 * N_KV, dtype=F32).reshape(2, N_KV, 2, N_KV)
    order = 'cgdGCe' if group_major else 'cgdCGe'
    proj = jnp.einsum('cde,cgCG->' + order, cmp_proj, eye).reshape(KV_COLS, KV_COLS)
    return pos, proj


def _compress_prompt_kernel(kv_ref, pos_ref, proj_ref, projt_ref, kblk_ref, vblkt_ref):
    x = kv_ref[0]
    n_sel = x.shape[0] // SEL_BLOCK
    x = x.reshape(n_sel, SEL_BLOCK // CMP_BLOCK, CMP_BLOCK, KV_COLS)
    pos = pos_ref[...][None]
    sums = jnp.concatenate([jnp.sum(x[:, i] * pos, axis=1) for i in range(SEL_BLOCK // CMP_BLOCK)], axis=0)
    blk = _dot3(sums, proj_ref[...])
    blk_t = _dot3_nt(projt_ref[...], sums)
    for g in range(N_KV):
        kblk_ref[0, g] = blk[:, g * KV_HALF:g * KV_HALF + HEAD_DIM]
        vblkt_ref[0, g] = blk_t[g * KV_HALF + HEAD_DIM:(g + 1) * KV_HALF]


def _compress_prompt(kvc, pos, proj):
    n_seq, seq, _ = kvc.shape
    n_cmp = seq // CMP_BLOCK
    return pl.pallas_call(
        _compress_prompt_kernel,
        grid=(n_seq,),
        in_specs=[pl.BlockSpec((1, seq, KV_COLS), lambda b: (b, 0, 0)),
                  pl.BlockSpec((CMP_BLOCK, KV_COLS), lambda b: (0, 0)),
                  pl.BlockSpec((KV_COLS, KV_COLS), lambda b: (0, 0)),
                  pl.BlockSpec((KV_COLS, KV_COLS), lambda b: (0, 0))],
        out_specs=[pl.BlockSpec((1, N_KV, n_cmp, HEAD_DIM), lambda b: (b, 0, 0, 0)),
                   pl.BlockSpec((1, N_KV, HEAD_DIM, n_cmp), lambda b: (b, 0, 0, 0))],
        out_shape=[jax.ShapeDtypeStruct((n_seq, N_KV, n_cmp, HEAD_DIM), F32),
                   jax.ShapeDtypeStruct((n_seq, N_KV, HEAD_DIM, n_cmp), F32)],
        compiler_params=_params("arbitrary"),
    )(kvc, pos, proj, proj.T)


def _select_blocks_t(pair, q0):
    row = _iota2(pair.shape, 0)
    n_sel = pair.shape[0]
    cur = (q0 + _iota2(pair.shape, 1)) // SEL_BLOCK
    forced = (row == 0) | (row == cur) | (row == cur - 1)
    score = jnp.where(row <= cur, jnp.where(forced, pair + FORCE, pair), -jnp.inf)
    chosen = jnp.zeros(pair.shape, F32)
    for _ in range(N_SEL):
        m = jnp.max(score, axis=0, keepdims=True)
        hit = (score == m) & (m > -jnp.inf)
        first = jnp.min(jnp.where(hit, row, n_sel), axis=0, keepdims=True)
        pick = row == first
        chosen = jnp.where(pick, 1.0, chosen)
        score = jnp.where(pick, -jnp.inf, score)
    return chosen


def _attend(state, k_aug, v_t, q_aug, extra):
    m, l, acc = state
    s = jnp.dot(k_aug, q_aug, preferred_element_type=F32)
    if extra is not None:
        s = s + extra
    m_new = jnp.maximum(m, jnp.max(s, axis=0, keepdims=True))
    alpha = jnp.exp(m - m_new)
    e = jnp.exp(s - m_new)
    l = alpha * l + jnp.sum(e, axis=0, keepdims=True)
    acc = alpha * acc + jnp.dot(v_t, e.astype(BF16), preferred_element_type=F32)
    return m_new, l, acc


def _nsa_prompt_kernel(tbl_ref, qt_ref, gt_ref, kblk_ref, vblkt_ref, bcmp_ref, near_ref, ks_ref, vst_ref,
                       kw_ref, vwt_ref, o_ref):
    g = pl.program_id(1)
    qi = pl.program_id(2)
    q0 = qi * Q_TILE
    win_tiles = WINDOW // Q_TILE
    v_rows = pl.ds(pl.multiple_of(g * HEAD_DIM, HEAD_DIM), HEAD_DIM)

    width = GQA * Q_TILE
    q = jnp.concatenate([qt_ref[0, r * HEAD_DIM:(r + 1) * HEAD_DIM, :] for r in range(GQA)], axis=1) * SCALE

    k_cmp, v_cmp_t = kblk_ref[0, 0], vblkt_ref[0, 0]
    n_cmp = k_cmp.shape[0]
    bias = bcmp_ref[0, 0]
    p = _masked_softmax(_dot3(k_cmp, q) + bias, bias > MASKED_BELOW, axis=0)
    o_cmp = _bdot(v_cmp_t, p)
    ps = p[:, :Q_TILE]
    for r in range(1, GQA):
        ps = ps + p[:, r * Q_TILE:(r + 1) * Q_TILE]
    chosen = _select_blocks_t(ps[:n_cmp // 2] + ps[n_cmp // 2:], q0)
    not_chosen = (chosen - 1.0).astype(BF16)
    pad_rows = (AUG_ONE - AUG_SEL) - not_chosen.shape[0]
    if pad_rows:
        not_chosen = jnp.concatenate([not_chosen, jnp.zeros((pad_rows, Q_TILE), BF16)], axis=0)

    row = _iota2((LANES - AUG_ONE, Q_TILE), 0)
    consts = []
    for r in range(GQA):
        last = jnp.full((LANES - AUG_ONE, Q_TILE), tbl_ref[N_BUCKETS - 1, g * GQA + r], F32)
        hi = last.astype(BF16).astype(F32)
        consts.append(jnp.where(row == 0, hi, jnp.where(row == 1, last - hi, 0.0)).astype(BF16))
    q_aug = jnp.concatenate([q.astype(BF16), jnp.concatenate([not_chosen] * GQA, axis=1),
                             jnp.concatenate(consts, axis=1)], axis=0)
    near0, near1, oldest = near_ref[0, 0], near_ref[0, 1], near_ref[0, 2]
    init = (jnp.full((1, width), NEG, F32), jnp.zeros((1, width), F32), jnp.zeros((HEAD_DIM, width), F32))

    def key_rows(ref, tile, n_tiles):
        return ref[0, pl.ds(pl.multiple_of(tile * Q_TILE, Q_TILE), n_tiles * Q_TILE), :]

    def value_cols(ref, tile, n_tiles):
        return jnp.concatenate([ref[0, tile + j, v_rows, :] for j in range(n_tiles)], axis=1)

    def tile_mask(ok):
        return jnp.full((Q_TILE, width), jnp.where(ok, 0.0, NEG), F32)

    extra = jnp.concatenate([near1 + tile_mask(qi >= 1), near0], axis=0)
    st = _attend(init, key_rows(ks_ref, qi, 2), value_cols(vst_ref, qi, 2), q_aug, extra)
    n_far = jnp.maximum(qi - 1, 0)

    def far_chunk(c, st):
        t0 = c * FAR_TILES
        extra = jnp.concatenate([tile_mask(t0 + j < n_far) for j in range(FAR_TILES)], axis=0)
        return _attend(st, key_rows(ks_ref, t0 + 1, FAR_TILES), value_cols(vst_ref, t0 + 1, FAR_TILES), q_aug, extra)

    st = lax.fori_loop(0, (n_far + FAR_TILES - 1) // FAR_TILES, far_chunk, st)
    o_sel = st[2] * (1.0 / jnp.maximum(st[1], TINY))

    extras = []
    for mt in range(win_tiles):
        ok = tile_mask(qi + mt >= win_tiles)
        extras.append(oldest + ok if mt == 0 else near1 + ok if mt == win_tiles - 1 else ok)
    extra = jnp.concatenate(extras + [near0], axis=0)
    st = _attend(init, key_rows(kw_ref, qi, win_tiles + 1), value_cols(vwt_ref, qi, win_tiles + 1), q_aug, extra)
    o_win = st[2] * (1.0 / jnp.maximum(st[1], TINY))

    def gate(branch):
        return jnp.concatenate([gt_ref[0, r * N_BRANCH + branch:r * N_BRANCH + branch + 1, :]
                                for r in range(GQA)], axis=1)

    o = gate(0) * o_cmp + gate(1) * o_sel + gate(2) * o_win
    o_ref[0] = jnp.concatenate([o[:, r * Q_TILE:(r + 1) * Q_TILE] for r in range(GQA)], axis=0).T


def _nsa_prompt(rel_bias, q_t, gates_t, kblk, vblk_t, bcmp, near, ks_aug, vs_t, kw_aug, vw_t):
    n_seq, _, seq = q_t.shape
    nq = seq // Q_TILE
    assert nq % FAR_TILES == 0
    n_cmp = kblk.shape[2]
    gw = GQA * HEAD_DIM
    wt = WINDOW // Q_TILE
    return pl.pallas_call(
        _nsa_prompt_kernel,
        grid=(n_seq, N_KV, nq),
        in_specs=[pl.BlockSpec(memory_space=pltpu.SMEM),
                  pl.BlockSpec((1, gw, Q_TILE), lambda b, g, qi: (b, g, qi)),
                  pl.BlockSpec((1, GATE_ROWS, Q_TILE), lambda b, g, qi: (b, g, qi)),
                  pl.BlockSpec((1, 1, n_cmp, HEAD_DIM), lambda b, g, qi: (b, g, 0, 0)),
                  pl.BlockSpec((1, 1, HEAD_DIM, n_cmp), lambda b, g, qi: (b, g, 0, 0)),
                  pl.BlockSpec((1, 1, n_cmp, GQA * Q_TILE), lambda b, g, qi: (qi, g, 0, 0)),
                  pl.BlockSpec((1, 3, Q_TILE, GQA * Q_TILE), lambda b, g, qi: (g, 0, 0, 0)),
                  pl.BlockSpec((1, Q_TILE + seq, LANES), lambda b, g, qi: (b, 0, g)),
                  pl.BlockSpec((1, 1 + nq, KV_HALF, Q_TILE), lambda b, g, qi: (b, 0, 0, 0)),
                  pl.BlockSpec((1, WINDOW + seq, LANES), lambda b, g, qi: (b, 0, g)),
                  pl.BlockSpec((1, wt + nq, KV_HALF, Q_TILE), lambda b, g, qi: (b, 0, 0, 0))],
        out_specs=pl.BlockSpec((1, Q_TILE, gw), lambda b, g, qi: (b, qi, g)),
        out_shape=jax.ShapeDtypeStruct((n_seq, seq, NSA_WIDTH), F32),
        compiler_params=_params("arbitrary", "arbitrary", "arbitrary"),
    )(rel_bias, q_t, gates_t, kblk, vblk_t, bcmp, near, ks_aug, vs_t, kw_aug, vw_t)


STATE_CHUNK = 512
SSM_KB = SSM_WIDTH // LANES
GROUPS_PER_KB = N_SSM_GROUPS // SSM_KB


def _ssm_disc_kernel(are_ref, aim_ref, ldt_ref, bre_ref, bim_ref, abr_ref, abi_ref, bbr_ref, bbi_ref):
    a_re, a_im = are_ref[...], aim_ref[...]
    dt = jnp.exp(ldt_ref[...])
    mag = jnp.exp(a_re * dt)
    ab_re = mag * jnp.cos(a_im * dt)
    ab_im = mag * jnp.sin(a_im * dt)
    den = a_re * a_re + a_im * a_im
    co_re = ((ab_re - 1.0) * a_re + ab_im * a_im) / den
    co_im = (ab_im * a_re - (ab_re - 1.0) * a_im) / den
    abr_ref[...] = ab_re
    abi_ref[...] = ab_im
    b_re, b_im = bre_ref[...], bim_ref[...]
    bbr_ref[...] = co_re[:, None, :] * b_re - co_im[:, None, :] * b_im
    bbi_ref[...] = co_re[:, None, :] * b_im + co_im[:, None, :] * b_re


def _block_diag(w):
    a, b = w.shape[1:]
    w = w.reshape(SSM_KB, GROUPS_PER_KB, a, b)
    eye = jnp.eye(GROUPS_PER_KB, dtype=w.dtype)
    return jnp.einsum('kgab,gh->kgahb', w, eye).reshape(SSM_KB, GROUPS_PER_KB * a, GROUPS_PER_KB * b)


def _ssm_weights(a_re, a_im, b_re, b_im, c_re, c_im, log_dt):
    ng, p = a_re.shape
    vm = pl.BlockSpec(memory_space=pltpu.VMEM)
    ab_re, ab_im, bb_re, bb_im = pl.pallas_call(
        _ssm_disc_kernel,
        in_specs=[vm] * 5, out_specs=[vm] * 4,
        out_shape=[jax.ShapeDtypeStruct((ng, p), F32)] * 2
        + [jax.ShapeDtypeStruct((ng, SSM_GROUP, p), F32)] * 2,
    )(a_re, a_im, log_dt.reshape(ng, 1), b_re.transpose(0, 2, 1), b_im.transpose(0, 2, 1))
    return (ab_re.reshape(1, N_STATE), ab_im.reshape(1, N_STATE),
            _block_diag(bb_re).astype(BF16), _block_diag(bb_im).astype(BF16),
            _block_diag(c_re.transpose(0, 2, 1)).astype(BF16),
            _block_diag(c_im.transpose(0, 2, 1)).astype(BF16))


def _ssm_kernel(u_ref, h0r_ref, h0i_ref, abr_ref, abi_ref, bbr_ref, bbi_ref, ccr_ref, cci_ref,
                d_ref, gw_ref, gb_ref, o_ref, hr_ref, hi_ref, sre, sim, *, nb, steps):
    @pl.when(pl.program_id(0) == 0)
    def _():
        hr_ref[...] = h0r_ref[...]
        hi_ref[...] = h0i_ref[...]

    u = u_ref[...]
    ub = u.astype(BF16)
    kw = N_STATE // SSM_KB
    for kb in range(SSM_KB):
        uk = ub[:, kb * LANES:(kb + 1) * LANES]
        sre[:, kb * kw:(kb + 1) * kw] = jnp.dot(uk, bbr_ref[kb], preferred_element_type=F32)
        sim[:, kb * kw:(kb + 1) * kw] = jnp.dot(uk, bbi_ref[kb], preferred_element_type=F32)

    if steps == 1:
        ar, ai = abr_ref[...], abi_ref[...]
        hr, hi = hr_ref[...], hi_ref[...]
        nr = ar * hr - ai * hi + sre[...]
        ni = ar * hi + ai * hr + sim[...]
        sre[...] = nr
        sim[...] = ni
        hr_ref[...] = nr
        hi_ref[...] = ni
    else:
        for cb in range(N_STATE // STATE_CHUNK):
            cols = slice(cb * STATE_CHUNK, (cb + 1) * STATE_CHUNK)
            ar = jnp.broadcast_to(abr_ref[:, cols], (nb, STATE_CHUNK))
            ai = jnp.broadcast_to(abi_ref[:, cols], (nb, STATE_CHUNK))

            def step(t, carry, cols=cols, ar=ar, ai=ai):
                hr, hi = carry
                rows = pl.ds(pl.multiple_of(t * nb, nb), nb)
                nr = ar * hr - ai * hi + sre[rows, cols]
                ni = ar * hi + ai * hr + sim[rows, cols]
                sre[rows, cols] = nr
                sim[rows, cols] = ni
                return nr, ni

            hr, hi = lax.fori_loop(0, steps, step, (hr_ref[:, cols], hi_ref[:, cols]))
            hr_ref[:, cols] = hr
            hi_ref[:, cols] = hi

    ys = []
    for kb in range(SSM_KB):
        cols = slice(kb * kw, (kb + 1) * kw)
        ys.append(_bdot(sre[:, cols], ccr_ref[kb]) - _bdot(sim[:, cols], cci_ref[kb]))
    y = jnp.concatenate(ys, axis=1) + d_ref[...] * u
    g = jax.nn.gelu(y)
    o_ref[...] = g * jax.nn.sigmoid(_bdot(g, gw_ref[...]) + gb_ref[...])


def _ssm(u_rows, h0_re, h0_im, wts, d, glu_w, glu_b, nb, steps_per_call):
    ab_re, ab_im, bb_re, bb_im, cc_re, cc_im = wts
    rows = nb * steps_per_call
    n_calls = u_rows.shape[0] // rows
    const2 = lambda i: (0, 0)
    const3 = lambda i: (0, 0, 0)
    kw = N_STATE // SSM_KB
    return pl.pallas_call(
        functools.partial(_ssm_kernel, nb=nb, steps=steps_per_call),
        grid=(n_calls,),
        in_specs=[pl.BlockSpec((rows, SSM_WIDTH), lambda i: (i, 0)),
                  pl.BlockSpec((nb, N_STATE), const2), pl.BlockSpec((nb, N_STATE), const2),
                  pl.BlockSpec((1, N_STATE), const2), pl.BlockSpec((1, N_STATE), const2),
                  pl.BlockSpec((SSM_KB, LANES, kw), const3), pl.BlockSpec((SSM_KB, LANES, kw), const3),
                  pl.BlockSpec((SSM_KB, kw, LANES), const3), pl.BlockSpec((SSM_KB, kw, LANES), const3),
                  pl.BlockSpec((1, SSM_WIDTH), const2),
                  pl.BlockSpec((SSM_WIDTH, SSM_WIDTH), const2),
                  pl.BlockSpec((1, SSM_WIDTH), const2)],
        out_specs=[pl.BlockSpec((rows, SSM_WIDTH), lambda i: (i, 0)),
                   pl.BlockSpec((nb, N_STATE), const2), pl.BlockSpec((nb, N_STATE), const2)],
        out_shape=[jax.ShapeDtypeStruct(u_rows.shape, F32),
                   jax.ShapeDtypeStruct((nb, N_STATE), F32), jax.ShapeDtypeStruct((nb, N_STATE), F32)],
        scratch_shapes=[pltpu.VMEM((rows, N_STATE), F32), pltpu.VMEM((rows, N_STATE), F32)],
        compiler_params=_params("arbitrary"),
    )(u_rows, h0_re, h0_im, ab_re, ab_im, bb_re, bb_im, cc_re, cc_im, d, glu_w, glu_b)


def _merge_kernel(onsa_ref, ossm_ref, x_ref, gg_ref, wout_ref, ng_ref, xq_ref, x1_ref, qx_ref):
    gg = gg_ref[...]
    a = _rms(onsa_ref[...], gg[:, :NSA_WIDTH])
    b = _rms(ossm_ref[...], gg[:, NSA_WIDTH:])
    m = (jnp.dot(a.astype(BF16), wout_ref[:NSA_WIDTH], preferred_element_type=F32)
         + jnp.dot(b.astype(BF16), wout_ref[NSA_WIDTH:], preferred_element_type=F32))
    x1 = x_ref[...] + _rms(m, ng_ref[1:2])
    x1_ref[...] = x1
    qx_ref[...] = _bdot(_rms(x1, ng_ref[2:3]), xq_ref[...])


def _merge(o_nsa, o_ssm, x, gg, w_out, ng, xq, n_seq, tile):
    rows = x.shape[0]
    nt = rows // n_seq // tile
    row_map = lambda b, t: (b * nt + t, 0)
    const = lambda b, t: (0, 0)
    return pl.pallas_call(
        _merge_kernel,
        grid=(n_seq, nt),
        in_specs=[pl.BlockSpec((tile, NSA_WIDTH), row_map),
                  pl.BlockSpec((tile, SSM_WIDTH), lambda b, t: (t, b)),
                  pl.BlockSpec((tile, D_MODEL), row_map),
                  pl.BlockSpec((1, D_MODEL), const),
                  pl.BlockSpec((D_MODEL, D_MODEL), const),
                  pl.BlockSpec((6, D_MODEL), const),
                  pl.BlockSpec((D_MODEL, MEM_WIDTH), const)],
        out_specs=[pl.BlockSpec((tile, D_MODEL), row_map), pl.BlockSpec((tile, MEM_WIDTH), row_map)],
        out_shape=[jax.ShapeDtypeStruct((rows, D_MODEL), F32),
                   jax.ShapeDtypeStruct((rows, MEM_WIDTH), F32)],
        compiler_params=_params("arbitrary", "arbitrary"),
    )(o_nsa, o_ssm, x, gg, w_out, ng, xq)


def _norm_matmul_kernel(x_ref, g_ref, w_ref, o_ref):
    o_ref[...] = _bdot(_rms(x_ref[...], g_ref[...]), w_ref[...])


def _norm_matmul(x, g, w, tile):
    rows, k = x.shape
    n = w.shape[1]
    return pl.pallas_call(
        _norm_matmul_kernel,
        grid=(rows // tile,),
        in_specs=[pl.BlockSpec((tile, k), lambda i: (i, 0)),
                  pl.BlockSpec((1, k), lambda i: (0, 0)),
                  pl.BlockSpec((k, n), lambda i: (0, 0))],
        out_specs=pl.BlockSpec((tile, n), lambda i: (i, 0)),
        out_shape=jax.ShapeDtypeStruct((rows, n), F32),
        compiler_params=_params("arbitrary"),
    )(x, g, w)


def _cross_prompt_kernel(qx_ref, mkv_ref, o_ref):
    qx = qx_ref[...]
    outs = []
    for h in range(MEM_HEADS):
        cols = slice(h * MEM_HEAD_DIM, (h + 1) * MEM_HEAD_DIM)
        k = mkv_ref[0, :, cols]
        v = mkv_ref[0, :, MEM_WIDTH + h * MEM_HEAD_DIM: MEM_WIDTH + (h + 1) * MEM_HEAD_DIM]
        s = _bdot_nt(qx[:, cols], k) * MEM_SCALE
        e = jnp.exp(s - jnp.max(s, axis=-1, keepdims=True))
        p = e * (1.0 / jnp.sum(e, axis=-1, keepdims=True))
        outs.append(_bdot(p, v))
    o_ref[...] = jnp.concatenate(outs, axis=1)


def _cross_prompt(qx, mkv, n_seq, tile):
    rows = qx.shape[0]
    nt = rows // n_seq // tile
    n_mem = mkv.shape[1]
    return pl.pallas_call(
        _cross_prompt_kernel,
        grid=(n_seq, nt),
        in_specs=[pl.BlockSpec((tile, MEM_WIDTH), lambda b, t: (b * nt + t, 0)),
                  pl.BlockSpec((1, n_mem, 2 * MEM_WIDTH), lambda b, t: (b, 0, 0))],
        out_specs=pl.BlockSpec((tile, MEM_WIDTH), lambda b, t: (b * nt + t, 0)),
        out_shape=jax.ShapeDtypeStruct((rows, MEM_WIDTH), F32),
        compiler_params=_params("arbitrary", "arbitrary"),
    )(qx, mkv)


SAMPLE_GROUP = 8


def _cross_sample_kernel(qx_ref, mkv_ref, o_ref, *, n_mem):
    per_tok = 2 * MEM_HEADS
    sub = _iota2((SAMPLE_GROUP, MEM_HEAD_DIM), 0)
    qx = qx_ref[...]
    outs = []
    for i in range(SAMPLE_GROUP):
        s = jnp.zeros((SAMPLE_GROUP, n_mem), F32)
        for h in range(MEM_HEADS):
            qh = jnp.broadcast_to(qx[i:i + 1, h * MEM_HEAD_DIM:(h + 1) * MEM_HEAD_DIM], sub.shape)
            k = mkv_ref[i, pl.ds(h, n_mem, stride=per_tok), :]
            s = s + _bdot_nt(jnp.where(sub == h, qh, 0.0), k)
        s = s * MEM_SCALE
        e = jnp.exp(s - jnp.max(s, axis=-1, keepdims=True))
        p = e * (1.0 / jnp.sum(e, axis=-1, keepdims=True))
        heads = []
        for h in range(MEM_HEADS):
            v = mkv_ref[i, pl.ds(MEM_HEADS + h, n_mem, stride=per_tok), :]
            heads.append(_bdot(p, v)[h:h + 1])
        outs.append(jnp.concatenate(heads, axis=1))
    o_ref[...] = jnp.concatenate(outs, axis=0)


def _cross_sample(qx, mkv, n_mem):
    rows = qx.shape[0]
    return pl.pallas_call(
        functools.partial(_cross_sample_kernel, n_mem=n_mem),
        grid=(rows // SAMPLE_GROUP,),
        in_specs=[pl.BlockSpec((SAMPLE_GROUP, MEM_WIDTH), lambda i: (i, 0)),
                  pl.BlockSpec((SAMPLE_GROUP,) + mkv.shape[1:], lambda i: (i, 0, 0))],
        out_specs=pl.BlockSpec((SAMPLE_GROUP, MEM_WIDTH), lambda i: (i, 0)),
        out_shape=jax.ShapeDtypeStruct((rows, MEM_WIDTH), F32),
        compiler_params=_params("arbitrary"),
    )(qx, mkv)


def _ffn_kernel(x1_ref, oa_ref, ng_ref, xo_ref, wi_ref, wo_ref, o_ref, *, d_ff):
    x2 = x1_ref[...] + _rms(_bdot(oa_ref[...], xo_ref[...]), ng_ref[3:4])
    h = _rms(x2, ng_ref[4:5]).astype(BF16)
    z1 = jnp.dot(h, wi_ref[:, :d_ff], preferred_element_type=F32)
    z2 = jnp.dot(h, wi_ref[:, d_ff:], preferred_element_type=F32)
    y = _bdot(z1 * jax.nn.sigmoid(z1) * z2, wo_ref[...])
    o_ref[...] = x2 + _rms(y, ng_ref[5:6])


def _ffn(x1, oa, ng, xo, wi, wo, tile):
    rows = x1.shape[0]
    d_ff = wo.shape[0]
    const = lambda i: (0, 0)
    once = dict(pipeline_mode=pl.Buffered(1))
    return pl.pallas_call(
        functools.partial(_ffn_kernel, d_ff=d_ff),
        grid=(rows // tile,),
        in_specs=[pl.BlockSpec((tile, D_MODEL), lambda i: (i, 0)),
                  pl.BlockSpec((tile, MEM_WIDTH), lambda i: (i, 0)),
                  pl.BlockSpec((6, D_MODEL), const),
                  pl.BlockSpec((MEM_WIDTH, D_MODEL), const, **once),
                  pl.BlockSpec((D_MODEL, 2 * d_ff), const, **once),
                  pl.BlockSpec((d_ff, D_MODEL), const, **once)],
        out_specs=pl.BlockSpec((tile, D_MODEL), lambda i: (i, 0)),
        out_shape=jax.ShapeDtypeStruct((rows, D_MODEL), F32),
        compiler_params=_params("arbitrary"),
    )(x1, oa, ng, xo, wi, wo)


PAGES_PER_STEP = 32
NEW_TILE = 8
CMP_GROUP = 8


def _bias_rows(dist, tblt_ref):
    v = jnp.broadcast_to(tblt_ref[:, 0:1], dist.shape)
    for k in range(1, N_BUCKETS):
        v = jnp.where(dist >= _BUCKET_START[k], tblt_ref[:, k:k + 1], v)
    return v


def _sample_bias_kernel(tblt_ref, bw_ref, bc_ref, bn_ref, bs_ref, b0_ref, *, past_len):
    dist = (WINDOW - 1) - _iota2(bw_ref.shape, 1)
    bw_ref[...] = jnp.where((dist >= 0) & (dist < WINDOW), _bias_rows(dist, tblt_ref), NEG)
    dist = past_len - (_iota2(bc_ref.shape, 1) * CMP_BLOCK + (CMP_BLOCK - 1))
    bc_ref[...] = jnp.where(dist >= 0, _bias_rows(dist, tblt_ref), NEG)
    lane = _iota2(bn_ref.shape, 1)
    dist = past_len - ((past_len // CMP_BLOCK + lane) * CMP_BLOCK + (CMP_BLOCK - 1))
    bn_ref[...] = jnp.where((dist >= 0) & (lane < SEL_BLOCK // CMP_BLOCK),
                            _bias_rows(jnp.maximum(dist, 0), tblt_ref), NEG)
    page = lax.broadcasted_iota(I32, bs_ref.shape, 0)
    dist = past_len - page * PAGE_SIZE - lax.broadcasted_iota(I32, bs_ref.shape, 2)
    bs_ref[...] = _bias_rows(dist, tblt_ref)
    b0_ref[...] = _bias_rows(jnp.zeros(b0_ref.shape, I32), tblt_ref)


def _sample_bias(rel_bias, past_len):
    vm = pl.BlockSpec(memory_space=pltpu.VMEM)
    shapes = [(N_HEADS, WINDOW), (N_HEADS, past_len // CMP_BLOCK), (N_HEADS, LANES),
              (past_len // PAGE_SIZE, N_HEADS, PAGE_SIZE), (N_HEADS, LANES)]
    return pl.pallas_call(
        functools.partial(_sample_bias_kernel, past_len=past_len),
        in_specs=[vm], out_specs=[vm] * len(shapes),
        out_shape=[jax.ShapeDtypeStruct(s, F32) for s in shapes],
    )(rel_bias.T)


def _compress_sample_kernel(pt_ref, *refs):
    pages = refs[:PAGES_PER_STEP]
    post_ref, fold_ref, projt_ref, o_ref = refs[PAGES_PER_STEP:]
    post = post_ref[...]
    his, los = [], []
    for pg in pages:
        hi, lo = _split(pg[0] * post)
        his.append(hi)
        los.append(lo)
    fold = fold_ref[...]
    d = functools.partial(jnp.dot, preferred_element_type=F32)
    sums_t = d(jnp.concatenate(his, axis=1), fold) + d(jnp.concatenate(los, axis=1), fold)
    o_ref[0] = _dot3(projt_ref[...], sums_t)


def _compress_sample(pool_t, page_table_flat, n_seq, n_pages, pos, proj):
    steps = n_pages // PAGES_PER_STEP
    blocks_per_page = PAGE_SIZE // CMP_BLOCK
    blocks_per_step = PAGES_PER_STEP * blocks_per_page
    assert blocks_per_step == LANES
    pos_t = jnp.tile(pos.T, (1, blocks_per_page))
    row = np.arange(PAGES_PER_STEP * PAGE_SIZE)
    fold = jnp.asarray(row[:, None] // CMP_BLOCK == np.arange(blocks_per_step)[None, :], dtype=BF16)

    def page_map(i):
        return lambda b, s, pt: (pt[b * n_pages + s * PAGES_PER_STEP + i], 0, 0)

    const = lambda b, s, pt: (0, 0)
    grid_spec = pltpu.PrefetchScalarGridSpec(
        num_scalar_prefetch=1,
        grid=(n_seq, steps),
        in_specs=[pl.BlockSpec((1, KV_COLS, PAGE_SIZE), page_map(i)) for i in range(PAGES_PER_STEP)]
        + [pl.BlockSpec((KV_COLS, PAGE_SIZE), const),
           pl.BlockSpec(fold.shape, const),
           pl.BlockSpec((KV_COLS, KV_COLS), const)],
        out_specs=pl.BlockSpec((1, KV_COLS, blocks_per_step), lambda b, s, pt: (b, 0, s)),
    )
    return pl.pallas_call(
        _compress_sample_kernel,
        grid_spec=grid_spec,
        out_shape=jax.ShapeDtypeStruct((n_seq, KV_COLS, steps * blocks_per_step), F32),
        compiler_params=_params("arbitrary", "arbitrary"),
    )(page_table_flat, *([pool_t] * PAGES_PER_STEP), pos_t, fold, proj.T)


def _cmp_sample_kernel(qm_ref, blk_ref, kvc_ref, pos_ref, proj_ref, bc_ref, bn_ref, oc_ref, idx_ref,
                       *, past_len):
    n_cmp = blk_ref.shape[2]
    bias = bc_ref[...]
    mask = bias > MASKED_BELOW
    bias_n = bn_ref[:, :NEW_TILE]
    mask_n = bias_n > MASKED_BELOW
    group_ps = []
    for i in range(CMP_GROUP):
        qm = qm_ref[i]
        s = jnp.where(mask, _dot3(qm, blk_ref[i, :KV_HALF, :]) * SCALE + bias, NEG)
        new_sum = jnp.concatenate([kvc_ref[i] * pos_ref[0:1], jnp.zeros((NEW_TILE - 1, KV_COLS), F32)], axis=0)
        new_blk = _dot3(new_sum, proj_ref[...])
        s_n = jnp.where(mask_n, _dot3_nt(qm, new_blk[:, :KV_HALF]) * SCALE + bias_n, NEG)
        m = jnp.maximum(jnp.max(s, axis=-1, keepdims=True), jnp.max(s_n, axis=-1, keepdims=True))
        e = jnp.where(mask, jnp.exp(s - m), 0.0)
        e_n = jnp.where(mask_n, jnp.exp(s_n - m), 0.0)
        inv = 1.0 / jnp.maximum(jnp.sum(e, axis=-1, keepdims=True) + jnp.sum(e_n, axis=-1, keepdims=True), TINY)
        p = e * inv
        oc_ref[i] = _bdot_nt(p, blk_ref[i, KV_HALF:, :]) + _bdot(e_n * inv, new_blk[:, KV_HALF:])
        group_ps += [jnp.sum(p[g * GQA:(g + 1) * GQA], axis=0, keepdims=True) for g in range(N_KV)]

    ps = jnp.concatenate(group_ps, axis=0)
    lane = _iota2(ps.shape, 1)
    pair = ps + pltpu.roll(ps, n_cmp - 1, 1)
    cur = past_len // SEL_BLOCK
    j = lane // (SEL_BLOCK // CMP_BLOCK)
    forced = (j == 0) | (j == cur) | (j == cur - 1)
    score = jnp.where(lane % 2 == 0, jnp.where(forced, pair + FORCE, pair), -jnp.inf)
    slot = _iota2((ps.shape[0], LANES), 1)
    ids = jnp.zeros((ps.shape[0], LANES), I32)
    for it in range(N_SEL - 1):
        top = jnp.max(score, axis=-1, keepdims=True)
        first = jnp.min(jnp.where(score == top, lane, n_cmp), axis=-1, keepdims=True)
        ids = jnp.where(slot == it, first // (SEL_BLOCK // CMP_BLOCK), ids)
        score = jnp.where(lane == first, -jnp.inf, score)
    idx_ref[...] = jnp.where(slot == N_SEL - 1, cur, ids)


def _cmp_sample(qm, blocks_t, kvc_new, pos, proj, bc, bn, past_len):
    n_seq, _, n_cmp = blocks_t.shape
    assert past_len // SEL_BLOCK > N_SEL and past_len % SEL_BLOCK == 0
    const = lambda b: (0, 0)
    per_seq = lambda b: (b, 0, 0)
    return pl.pallas_call(
        functools.partial(_cmp_sample_kernel, past_len=past_len),
        grid=(n_seq // CMP_GROUP,),
        in_specs=[pl.BlockSpec((CMP_GROUP, N_HEADS, LANES), per_seq),
                  pl.BlockSpec((CMP_GROUP, KV_COLS, n_cmp), per_seq),
                  pl.BlockSpec((CMP_GROUP, 1, KV_COLS), per_seq),
                  pl.BlockSpec((CMP_BLOCK, KV_COLS), const),
                  pl.BlockSpec((KV_COLS, KV_COLS), const),
                  pl.BlockSpec((N_HEADS, n_cmp), const),
                  pl.BlockSpec((N_HEADS, LANES), const)],
        out_specs=[pl.BlockSpec((CMP_GROUP, N_HEADS, LANES), per_seq),
                   pl.BlockSpec((CMP_GROUP * N_KV, LANES), lambda b: (b, 0))],
        out_shape=[jax.ShapeDtypeStruct((n_seq, N_HEADS, LANES), F32),
                   jax.ShapeDtypeStruct((n_seq * N_KV, LANES), I32)],
        compiler_params=_params("arbitrary"),
    )(qm, blocks_t, kvc_new, pos, proj, bc, bn)


def _sel_win_sample_kernel(idx_ref, pt_ref, *refs, n_past_blocks):
    n_sel_pages = N_KV * N_SEL
    pages = refs[:n_sel_pages]
    (buf_ref, qm_ref, kvs_ref, kvw_ref, gate_ref, oc_ref, bw_ref, bs_ref, b0_ref,
     o_ref, nbuf_ref) = refs[n_sel_pages:]
    b = pl.program_id(0)
    bpp = PAGE_SIZE // SEL_BLOCK
    qm = qm_ref[0]

    buf = buf_ref[0]
    shifted = pltpu.roll(buf, WINDOW - 1, 1)
    nbuf = jnp.where(_iota2(buf.shape, 1) == WINDOW - 1, jnp.broadcast_to(kvw_ref[0], buf.shape), shifted)
    nbuf_ref[0] = nbuf
    bias = bw_ref[...]
    p = _masked_softmax(_bdot(qm, nbuf[:KV_HALF]) * SCALE + bias, bias > MASKED_BELOW)
    o_win = _bdot_nt(p, nbuf[KV_HALF:])

    lane_half = _iota2((GQA, PAGE_SIZE), 1) // SEL_BLOCK
    o_sel = []
    for g in range(N_KV):
        cols = slice(g * HEAD_DIM, (g + 1) * HEAD_DIM)
        rows = slice(g * GQA, (g + 1) * GQA)
        qg = qm[rows, cols]
        ks, vs, biases = [], [], []
        has_new = False
        for k in range(N_SEL):
            i = idx_ref[(b * N_KV + g) * N_SEL + k]
            is_past = i < n_past_blocks
            has_new = jnp.logical_or(has_new, jnp.logical_not(is_past))
            i = jnp.minimum(i, n_past_blocks - 1)
            blk = pages[g * N_SEL + k]
            ks.append(blk[0, 0, 0])
            vs.append(blk[0, 1, 0])
            tile = bs_ref[i // bpp]
            biases.append(jnp.where(is_past & (lane_half == i % bpp), tile[rows], NEG))
        bias = jnp.concatenate(biases, axis=1)
        mask = bias > MASKED_BELOW
        s = jnp.where(mask, _bdot(qg, jnp.concatenate(ks, axis=1)) * SCALE + bias, NEG)
        k_new = kvs_ref[0][:, cols]
        v_new = kvs_ref[0][:, KV_HALF + g * HEAD_DIM:KV_HALF + (g + 1) * HEAD_DIM]
        s_new = jnp.where(has_new,
                          jnp.sum(qg * k_new, axis=-1, keepdims=True) * SCALE + b0_ref[rows, 0:1], NEG)
        m = jnp.maximum(jnp.max(s, axis=-1, keepdims=True), s_new)
        e = jnp.where(mask, jnp.exp(s - m), 0.0)
        e_new = jnp.where(has_new, jnp.exp(s_new - m), 0.0)
        inv = 1.0 / jnp.maximum(jnp.sum(e, axis=-1, keepdims=True) + e_new, TINY)
        o_sel.append(_bdot_nt(e * inv, jnp.concatenate(vs, axis=1)) + (e_new * inv) * v_new)
    o_sel = jnp.concatenate(o_sel, axis=0)

    first_group = _iota2((N_HEADS, HEAD_DIM), 0) < GQA
    oc = oc_ref[0]
    o_cmp = jnp.where(first_group, oc[:, :HEAD_DIM], oc[:, HEAD_DIM:])
    o_win = jnp.where(first_group, o_win[:, :HEAD_DIM], o_win[:, HEAD_DIM:])
    gate = gate_ref[0]
    o_ref[0] = gate[:, 0:1] * o_cmp + gate[:, 1:2] * o_sel + gate[:, 2:3] * o_win


def _sel_win_sample(idx_flat, pt_flat, pool_t, buf_t, qm, kvs_new, kvw_new_col, gates, o_cmp, bw, bs, b0,
                    n_pages):
    n_seq = buf_t.shape[0]
    bpp = PAGE_SIZE // SEL_BLOCK
    n_past_blocks = n_pages * bpp

    def page_map(g, k):
        def f(b, idx, pt):
            i = jnp.minimum(idx[(b * N_KV + g) * N_SEL + k], n_past_blocks - 1)
            return (pt[b * n_pages + i // bpp], 0, g, 0, 0)
        return f

    per_seq3 = lambda b, idx, pt: (b, 0, 0)
    const2 = lambda b, idx, pt: (0, 0)
    head_tile = pl.BlockSpec((1, N_HEADS, LANES), per_seq3)
    grid_spec = pltpu.PrefetchScalarGridSpec(
        num_scalar_prefetch=2,
        grid=(n_seq,),
        in_specs=[pl.BlockSpec((1, 2, 1, HEAD_DIM, PAGE_SIZE), page_map(g, k))
                  for g in range(N_KV) for k in range(N_SEL)]
        + [pl.BlockSpec((1, KV_COLS, WINDOW), per_seq3), head_tile,
           pl.BlockSpec((1, 1, KV_COLS), per_seq3), pl.BlockSpec((1, KV_COLS, 1), per_seq3), head_tile,
           head_tile,
           pl.BlockSpec((N_HEADS, WINDOW), const2),
           pl.BlockSpec((n_pages, N_HEADS, PAGE_SIZE), lambda b, idx, pt: (0, 0, 0)),
           pl.BlockSpec((N_HEADS, LANES), const2)],
        out_specs=[pl.BlockSpec((1, N_HEADS, HEAD_DIM), per_seq3),
                   pl.BlockSpec((1, KV_COLS, WINDOW), per_seq3)],
    )
    return pl.pallas_call(
        functools.partial(_sel_win_sample_kernel, n_past_blocks=n_past_blocks),
        grid_spec=grid_spec,
        out_shape=[jax.ShapeDtypeStruct((n_seq, N_HEADS, HEAD_DIM), F32),
                   jax.ShapeDtypeStruct(buf_t.shape, F32)],
        compiler_params=_params("arbitrary"),
    )(idx_flat, pt_flat, *([pool_t] * (N_KV * N_SEL)), buf_t, qm, kvs_new, kvw_new_col, gates, o_cmp, bw, bs, b0)


PROJ_TILE = 512
ROW_TILE = 256
SCAN_STEPS = 128


def kernel(x_prompt, x_sample, cache_kv_cmp, cache_kv_sel, cache_kv_win, state_ssm_re, state_ssm_im,
           cache_mem_kv, page_table, mem_prompt, w_in, w_out, norm_g, grp_norm_g, cmp_pos, cmp_proj,
           rel_bias, ssm_a_re, ssm_a_im, ssm_b_re, ssm_b_im, ssm_c_re, ssm_c_im, ssm_d, ssm_log_dt,
           glu_w, glu_b, mem_norm_g, xq, xkv, xo, ffn_wi, ffn_wo):
    depth = w_in.shape[0]
    n_seq, seq, _ = x_prompt.shape
    n_dec, dec_seq, _ = x_sample.shape
    assert dec_seq == 1, "the sample kernels handle one new token per sequence"
    n_pages = page_table.shape[1]
    past_len = n_pages * PAGE_SIZE
    n_mem = mem_prompt.shape[1]
    assert cache_kv_win.shape[2] == WINDOW and seq >= WINDOW and WINDOW % Q_TILE == 0
    win_tiles = WINDOW // Q_TILE

    bcmp, near = _prompt_bias(rel_bias, seq)
    bw_s, bc_s, bn_s, bs_s, b0_s = _sample_bias(rel_bias, past_len)
    pt_flat = page_table.reshape(-1)
    kv5 = (2, N_KV, HEAD_DIM)

    xp = x_prompt.reshape(n_seq * seq, D_MODEL)
    xs = x_sample.reshape(n_dec, D_MODEL)
    outs = [[] for _ in range(11)]
    for l in range(depth):
        ng = norm_g[l]
        w_row, w_t = _pack_w_prompt(w_in[l])
        w_sample = _pack_w_sample(w_in[l])
        w_out_b, xq_b, xkv_b, xo_b = (w[l].astype(BF16) for w in (w_out, xq, xkv, xo))
        wi_b, wo_b, glu_w_b = ffn_wi[l].astype(BF16), ffn_wo[l].astype(BF16), glu_w[l].astype(BF16)
        gg = grp_norm_g[l][None]
        ssm_w = _ssm_weights(ssm_a_re[l], ssm_a_im[l], ssm_b_re[l], ssm_b_im[l], ssm_c_re[l], ssm_c_im[l],
                             ssm_log_dt[l])
        ssm_tail = (ssm_d[l][None], glu_w_b, glu_b[l][None])
        pos, proj_g = _pack_compress(cmp_pos[l], cmp_proj[l], group_major=True)
        _, proj_c = _pack_compress(cmp_pos[l], cmp_proj[l], group_major=False)

        (kvc, ks_aug, kw_aug, u, q_t, kvc_t, kvs_t, kvw_t, vs_t, vw_t, gates_t) = _inproj_prompt(
            xp, ng[0:1], w_row, w_t, n_seq, PROJ_TILE)
        kblk, vblk_t = _compress_prompt(kvc.reshape(n_seq, seq, KV_COLS), pos, proj_g)
        kw_aug = jnp.pad(kw_aug.reshape(n_seq, seq, N_KV * LANES), ((0, 0), (WINDOW, 0), (0, 0)))
        vw_t = jnp.pad(vw_t, ((0, 0), (win_tiles, 0), (0, 0), (0, 0)))
        ks_aug = jnp.pad(ks_aug.reshape(n_seq, seq, N_KV * LANES), ((0, 0), (Q_TILE, 0), (0, 0)))
        vs_t = jnp.pad(vs_t, ((0, 0), (1, 0), (0, 0), (0, 0)))
        o_nsa = _nsa_prompt(rel_bias, q_t, gates_t, kblk, vblk_t, bcmp, near, ks_aug, vs_t, kw_aug, vw_t)
        zeros = jnp.zeros((n_seq, N_STATE), F32)
        o_ssm, h_re, h_im = _ssm(u.reshape(seq * n_seq, SSM_WIDTH), zeros, zeros, ssm_w, *ssm_tail,
                                 n_seq, SCAN_STEPS)
        x1, qx = _merge(o_nsa.reshape(n_seq * seq, NSA_WIDTH), o_ssm.reshape(seq, n_seq * SSM_WIDTH), xp,
                        gg, w_out_b, ng, xq_b, n_seq, ROW_TILE)
        mkv = _norm_matmul(mem_prompt.reshape(n_seq * n_mem, D_MODEL), mem_norm_g[l][None], xkv_b, ROW_TILE)
        oa = _cross_prompt(qx, mkv.reshape(n_seq, n_mem, 2 * MEM_WIDTH), n_seq, ROW_TILE)
        xp = _ffn(x1, oa, ng, xo_b, wi_b, wo_b, ROW_TILE)
        outs[0].append(_rows_minor(kvc_t))
        outs[1].append(_rows_minor(kvs_t))
        outs[2].append(_rows_minor(kvw_t[:, :, seq - WINDOW:]))
        outs[3].append(h_re.reshape(n_seq, N_SSM_GROUPS, SSM_STATE))
        outs[4].append(h_im.reshape(n_seq, N_SSM_GROUPS, SSM_STATE))
        outs[5].append(mkv.reshape(n_seq, n_mem, 2, MEM_HEADS, MEM_HEAD_DIM))

        q, kv, gates, u = _inproj_sample(xs, ng[0:1], w_sample)
        kvc, kvs, kvw = (kv[:, i * KV_COLS:(i + 1) * KV_COLS] for i in range(3))
        head_group = (jnp.arange(N_HEADS)[:, None] // GQA == jnp.arange(N_KV)[None, :]).astype(F32)
        qm = (q.reshape(n_dec, N_HEADS, 1, HEAD_DIM) * head_group[None, :, :, None]).reshape(n_dec, N_HEADS, LANES)
        gates_h = gates[:, :N_HEADS * N_BRANCH].reshape(n_dec, N_HEADS, N_BRANCH)
        gates_h = jnp.pad(gates_h, ((0, 0), (0, 0), (0, LANES - N_BRANCH)))
        blocks_t = _compress_sample(_rows_minor_view(cache_kv_cmp[l]), pt_flat, n_dec, n_pages, pos, proj_c)
        o_cmp, ids = _cmp_sample(qm, blocks_t, kvc.reshape(n_dec, 1, KV_COLS), pos, proj_c, bc_s, bn_s, past_len)
        idx_flat = ids[:, :N_SEL].reshape(-1)
        pool_sel = cache_kv_sel[l].transpose(0, 2, 3, 4, 1)
        o_nsa, new_buf_t = _sel_win_sample(
            idx_flat, pt_flat, pool_sel, _rows_minor_view(cache_kv_win[l]), qm, kvs.reshape(n_dec, 1, KV_COLS),
            kvw.reshape(n_dec, KV_COLS, 1), gates_h, o_cmp, bw_s, bs_s, b0_s, n_pages)
        o_ssm, h_re, h_im = _ssm(u, state_ssm_re[l].reshape(n_dec, N_STATE), state_ssm_im[l].reshape(n_dec, N_STATE),
                                 ssm_w, *ssm_tail, n_dec, 1)
        x1, qx = _merge(o_nsa.reshape(n_dec, NSA_WIDTH), o_ssm, xs, gg, w_out_b, ng, xq_b, 1, n_dec)
        oa = _cross_sample(qx, cache_mem_kv[l].reshape(n_dec, n_mem * 2 * MEM_HEADS, MEM_HEAD_DIM), n_mem)
        xs = _ffn(x1, oa, ng, xo_b, wi_b, wo_b, n_dec)
        outs[6].append(kvc.reshape(n_dec, 1, *kv5))
        outs[7].append(kvs.reshape(n_dec, 1, *kv5))
        outs[8].append(_rows_minor(new_buf_t))
        outs[9].append(h_re.reshape(n_dec, N_SSM_GROUPS, SSM_STATE))
        outs[10].append(h_im.reshape(n_dec, N_SSM_GROUPS, SSM_STATE))

    stacked = [jnp.stack(o, axis=0) for o in outs]
    return (xp.reshape(x_prompt.shape), xs.reshape(x_sample.shape), *stacked)
```

```python
import functools
import math

import numpy as np
import jax
import jax.numpy as jnp
from jax import lax
from jax.experimental import pallas as pl
from jax.experimental.pallas import tpu as pltpu

F32 = jnp.float32
BF16 = jnp.bfloat16
I32 = jnp.int32

D_MODEL = 1024
HEAD_DIM = 64
N_HEADS = 8
N_KV = 2
GQA = N_HEADS // N_KV
NSA_WIDTH = N_HEADS * HEAD_DIM
SSM_WIDTH = 512
KV_COLS = 2 * N_KV * HEAD_DIM
KV_HALF = N_KV * HEAD_DIM
N_BRANCH = 3
CMP_BLOCK = 32
SEL_BLOCK = 64
N_SEL = 16
WINDOW = 512
PAGE_SIZE = 128
SSM_GROUP = 16
N_SSM_GROUPS = 32
SSM_STATE = 64
N_STATE = N_SSM_GROUPS * SSM_STATE
N_BUCKETS = 32
MAX_DISTANCE = 128
MEM_HEADS = 4
MEM_HEAD_DIM = 128
MEM_WIDTH = MEM_HEADS * MEM_HEAD_DIM
EPS = 1e-6
NEG = -1e30
MASKED_BELOW = -1e29
TINY = 1e-30
FORCE = 1e4
SCALE = HEAD_DIM ** -0.5
MEM_SCALE = MEM_HEAD_DIM ** -0.5

LANES = 128
Q_TILE = 256
FAR_TILES = 2
VMEM_LIMIT = 56 * 1024 * 1024

AUG_SEL = HEAD_DIM
AUG_ONE = AUG_SEL + 32
GATE_ROWS = 16
VT_ROWS = HEAD_DIM + 16
LOG2E = math.log2(math.e)


def _bucket_starts():
    n = np.arange(0, 4 * MAX_DISTANCE)
    exact = N_BUCKETS // 2
    nf = np.maximum(n, exact).astype(np.float32)
    big = exact + (np.log(nf / exact) / np.float32(math.log(MAX_DISTANCE / exact))
                   * (N_BUCKETS - exact)).astype(np.int32)
    bucket = np.where(n < exact, n, np.minimum(big, N_BUCKETS - 1))
    return [int(np.argmax(bucket >= k)) for k in range(N_BUCKETS)]


_BUCKET_START = _bucket_starts()
assert _BUCKET_START[-1] <= Q_TILE


def _params(*sem):
    return pltpu.CompilerParams(dimension_semantics=sem, vmem_limit_bytes=VMEM_LIMIT)


def _bdot(a, b):
    return jnp.dot(a.astype(BF16), b.astype(BF16), preferred_element_type=F32)


def _bdot_nt(a, b):
    return lax.dot_general(a.astype(BF16), b.astype(BF16), (((1,), (1,)), ((), ())),
                           preferred_element_type=F32)


def _split(a):
    hi = a.astype(BF16)
    return hi, (a - hi.astype(F32)).astype(BF16)


def _dot3(a, b):
    ah, al = _split(a)
    bh, bl = _split(b)
    d = functools.partial(jnp.dot, preferred_element_type=F32)
    return d(ah, bh) + (d(ah, bl) + d(al, bh))


def _dot3_nt(a, b):
    ah, al = _split(a)
    bh, bl = _split(b)
    d = functools.partial(lax.dot_general, dimension_numbers=(((1,), (1,)), ((), ())),
                          preferred_element_type=F32)
    return d(ah, bh) + (d(ah, bl) + d(al, bh))


def _rms(x, g):
    return x * lax.rsqrt(jnp.mean(x * x, axis=-1, keepdims=True) + EPS) * g


def _masked_softmax(s, mask, axis=-1):
    s = jnp.where(mask, s, NEG)
    m = jnp.max(s, axis=axis, keepdims=True)
    e = jnp.where(mask, jnp.exp(s - m), 0.0)
    return e * (1.0 / jnp.maximum(jnp.sum(e, axis=axis, keepdims=True), TINY))


def _bias_of_dist(dist, tbl_ref, h):
    v = jnp.full(dist.shape, tbl_ref[0, h], F32)
    for k in range(1, N_BUCKETS):
        v = jnp.where(dist >= _BUCKET_START[k], tbl_ref[k, h], v)
    return v


def _iota2(shape, dim):
    return lax.broadcasted_iota(I32, shape, dim)


def _rows_minor(a):
    n, _, rows = a.shape
    return a.reshape(n, 2, N_KV, HEAD_DIM, rows).transpose(0, 4, 1, 2, 3)


def _rows_minor_view(a):
    n, rows = a.shape[:2]
    return a.transpose(0, 2, 3, 4, 1).reshape(n, KV_COLS, rows)


def _cmp_block_of_row(row, n_cmp):
    return jnp.where(row < n_cmp // 2, 2 * row, 2 * (row - n_cmp // 2) + 1)


def _prompt_bias_kernel(tbl_ref, bcmp_ref, near_ref, *, n_cmp, nq):
    h = pl.program_id(0)
    last = tbl_ref[N_BUCKETS - 1, h]

    blk = _cmp_block_of_row(_iota2((n_cmp, Q_TILE), 0), n_cmp)
    for qi in range(nq):
        dist = qi * Q_TILE + _iota2((n_cmp, Q_TILE), 1) - (blk * CMP_BLOCK + (CMP_BLOCK - 1))
        bcmp_ref[qi, 0] = jnp.where(dist >= 0, _bias_of_dist(dist, tbl_ref, h), NEG)

    key = _iota2((Q_TILE, Q_TILE), 0)
    qry = _iota2((Q_TILE, Q_TILE), 1)
    d0 = qry - key
    near_ref[0, 0] = jnp.where(d0 >= 0, (_bias_of_dist(d0, tbl_ref, h) - last) * LOG2E, NEG)
    d1 = Q_TILE + qry - key
    near_ref[0, 1] = (_bias_of_dist(d1, tbl_ref, h) - last) * LOG2E
    near_ref[0, 2] = jnp.where(key > qry, 0.0, NEG)


def _prompt_bias(rel_bias, seq):
    nq = seq // Q_TILE
    n_cmp = seq // CMP_BLOCK
    return pl.pallas_call(
        functools.partial(_prompt_bias_kernel, n_cmp=n_cmp, nq=nq),
        grid=(N_HEADS,),
        in_specs=[pl.BlockSpec(memory_space=pltpu.SMEM)],
        out_specs=[pl.BlockSpec((nq, 1, n_cmp, Q_TILE), lambda h: (0, h // GQA, 0, h % GQA)),
                   pl.BlockSpec((1, 3, Q_TILE, Q_TILE), lambda h: (h // GQA, 0, 0, h % GQA))],
        out_shape=[jax.ShapeDtypeStruct((nq, N_KV, n_cmp, GQA * Q_TILE), F32),
                   jax.ShapeDtypeStruct((N_KV, 3, Q_TILE, GQA * Q_TILE), F32)],
        compiler_params=_params("arbitrary"),
    )(rel_bias)


OFF_KV = NSA_WIDTH
OFF_GATE = OFF_KV + 3 * KV_COLS
OFF_U = OFF_GATE + N_HEADS * N_BRANCH


def _k_cols(w_in, which):
    base = OFF_KV + which * KV_COLS
    return [w_in[:, base + g * HEAD_DIM: base + (g + 1) * HEAD_DIM] for g in range(N_KV)]


def _pack_w_prompt(w_in):
    w_row = jnp.concatenate([w_in[:, OFF_KV:OFF_KV + KV_COLS]] + _k_cols(w_in, 1) + _k_cols(w_in, 2)
                            + [w_in[:, OFF_U:]], axis=1)
    gates = w_in[:, OFF_GATE:OFF_U].reshape(-1, N_KV, GQA * N_BRANCH)
    gates = jnp.pad(gates, ((0, 0), (0, 0), (0, GATE_ROWS - GQA * N_BRANCH))).reshape(-1, N_KV * GATE_ROWS)
    w_t = jnp.concatenate([w_in[:, :OFF_GATE], gates], axis=1).T
    return w_row.astype(BF16), w_t.astype(BF16)


def _inproj_prompt_kernel(x_ref, g_ref, wr_ref, wt_ref, kvc_ref, ksa_ref, kwa_ref, u_ref, qt_ref, kvct_ref,
                          kvst_ref, kvwt_ref, vst_ref, vwt_ref, gt_ref, *, tile):
    hb = _rms(x_ref[...], g_ref[...]).astype(BF16)
    z = jnp.dot(hb, wr_ref[...], preferred_element_type=F32)
    kvc_ref[...] = z[:, :KV_COLS]
    u_ref[...] = z[:, KV_COLS + 4 * HEAD_DIM:]

    pos = pl.program_id(1) * tile + _iota2((tile, LANES - HEAD_DIM), 0)
    col = _iota2((tile, LANES - HEAD_DIM), 1)
    ones = ((col >= AUG_ONE - HEAD_DIM) & (col < AUG_ONE - HEAD_DIM + 2)).astype(F32)
    aug_sel = jnp.where(col == pos // SEL_BLOCK, -NEG, ones)
    k0 = KV_COLS
    ksa_ref[...] = jnp.concatenate(
        [z[:, k0:k0 + HEAD_DIM], aug_sel, z[:, k0 + HEAD_DIM:k0 + 2 * HEAD_DIM], aug_sel], axis=1).astype(BF16)
    k0 = KV_COLS + 2 * HEAD_DIM
    kwa_ref[...] = jnp.concatenate(
        [z[:, k0:k0 + HEAD_DIM], ones, z[:, k0 + HEAD_DIM:k0 + 2 * HEAD_DIM], ones], axis=1).astype(BF16)

    zt = lax.dot_general(wt_ref[...], hb, (((1,), (1,)), ((), ())), preferred_element_type=F32)
    qt_ref[0] = zt[:NSA_WIDTH]
    kvct_ref[0] = zt[OFF_KV:OFF_KV + KV_COLS]
    kvst_ref[0] = zt[OFF_KV + KV_COLS:OFF_KV + 2 * KV_COLS]
    kvwt_ref[0] = zt[OFF_KV + 2 * KV_COLS:OFF_GATE]
    gt_ref[0] = jax.nn.sigmoid(zt[OFF_GATE:])
    v_sel = OFF_KV + KV_COLS + KV_HALF
    v_win = OFF_KV + 2 * KV_COLS + KV_HALF
    sum_rows = (_iota2((VT_ROWS - HEAD_DIM, Q_TILE), 0) == 0).astype(F32)
    for j in range(tile // Q_TILE):
        cols = slice(j * Q_TILE, (j + 1) * Q_TILE)
        for ref, v0 in ((vst_ref, v_sel), (vwt_ref, v_win)):
            parts = []
            for g in range(N_KV):
                parts += [zt[v0 + g * HEAD_DIM:v0 + (g + 1) * HEAD_DIM, cols], sum_rows]
            ref[0, j] = jnp.concatenate(parts, axis=0).astype(BF16)


def _inproj_prompt(x, g, w_row, w_t, n_seq, tile):
    rows = x.shape[0]
    seq = rows // n_seq
    assert seq // SEL_BLOCK <= AUG_ONE - AUG_SEL
    nt = seq // tile
    row_map = lambda b, t: (b * nt + t, 0)
    t_map = lambda b, t: (b, 0, t)
    tiles = tile // Q_TILE
    n_gate = N_KV * GATE_ROWS
    out = [
        (pl.BlockSpec((tile, KV_COLS), row_map), jax.ShapeDtypeStruct((rows, KV_COLS), F32)),
        (pl.BlockSpec((tile, N_KV * LANES), row_map), jax.ShapeDtypeStruct((rows, N_KV * LANES), BF16)),
        (pl.BlockSpec((tile, N_KV * LANES), row_map), jax.ShapeDtypeStruct((rows, N_KV * LANES), BF16)),
        (pl.BlockSpec((tile, SSM_WIDTH), lambda b, t: (t, b)), jax.ShapeDtypeStruct((seq, n_seq * SSM_WIDTH), F32)),
        (pl.BlockSpec((1, NSA_WIDTH, tile), t_map), jax.ShapeDtypeStruct((n_seq, NSA_WIDTH, seq), F32)),
        (pl.BlockSpec((1, KV_COLS, tile), t_map), jax.ShapeDtypeStruct((n_seq, KV_COLS, seq), F32)),
        (pl.BlockSpec((1, KV_COLS, tile), t_map), jax.ShapeDtypeStruct((n_seq, KV_COLS, seq), F32)),
        (pl.BlockSpec((1, KV_COLS, tile), t_map), jax.ShapeDtypeStruct((n_seq, KV_COLS, seq), F32)),
        (pl.BlockSpec((1, tiles, N_KV * VT_ROWS, Q_TILE), lambda b, t: (b, t, 0, 0)),
         jax.ShapeDtypeStruct((n_seq, seq // Q_TILE, N_KV * VT_ROWS, Q_TILE), BF16)),
        (pl.BlockSpec((1, tiles, N_KV * VT_ROWS, Q_TILE), lambda b, t: (b, t, 0, 0)),
         jax.ShapeDtypeStruct((n_seq, seq // Q_TILE, N_KV * VT_ROWS, Q_TILE), BF16)),
        (pl.BlockSpec((1, n_gate, tile), t_map), jax.ShapeDtypeStruct((n_seq, n_gate, seq), F32)),
    ]
    return pl.pallas_call(
        functools.partial(_inproj_prompt_kernel, tile=tile),
        grid=(n_seq, nt),
        in_specs=[pl.BlockSpec((tile, D_MODEL), row_map),
                  pl.BlockSpec((1, D_MODEL), lambda b, t: (0, 0)),
                  pl.BlockSpec(w_row.shape, lambda b, t: (0, 0)),
                  pl.BlockSpec(w_t.shape, lambda b, t: (0, 0))],
        out_specs=[o[0] for o in out],
        out_shape=[o[1] for o in out],
        compiler_params=_params("arbitrary", "arbitrary"),
    )(x, g, w_row, w_t)


S_Q = 0
S_KV = S_Q + NSA_WIDTH
S_GATE = S_KV + 3 * KV_COLS
S_U = S_GATE + LANES
S_END = S_U + SSM_WIDTH


def _pack_w_sample(w_in):
    gates = jnp.pad(w_in[:, OFF_GATE:OFF_U], ((0, 0), (0, LANES - N_HEADS * N_BRANCH)))
    return jnp.concatenate([w_in[:, :OFF_GATE], gates, w_in[:, OFF_U:]], axis=1).astype(BF16)


def _inproj_sample_kernel(x_ref, g_ref, w_ref, q_ref, kv_ref, gate_ref, u_ref):
    z = jnp.dot(_rms(x_ref[...], g_ref[...]).astype(BF16), w_ref[...], preferred_element_type=F32)
    q_ref[...] = z[:, S_Q:S_KV]
    kv_ref[...] = z[:, S_KV:S_GATE]
    gate_ref[...] = jax.nn.sigmoid(z[:, S_GATE:S_U])
    u_ref[...] = z[:, S_U:S_END]


def _inproj_sample(x, g, w):
    rows = x.shape[0]
    vm = pl.BlockSpec(memory_space=pltpu.VMEM)
    widths = [NSA_WIDTH, 3 * KV_COLS, LANES, SSM_WIDTH]
    return pl.pallas_call(
        _inproj_sample_kernel,
        in_specs=[vm] * 3, out_specs=[vm] * 4,
        out_shape=[jax.ShapeDtypeStruct((rows, w_), F32) for w_ in widths],
        compiler_params=pltpu.CompilerParams(vmem_limit_bytes=VMEM_LIMIT),
    )(x, g, w)


def _pack_compress(cmp_pos, cmp_proj, group_major):
    pos = jnp.broadcast_to(cmp_pos.transpose(1, 0, 2)[:, :, None, :],
                           (CMP_BLOCK, 2, N_KV, HEAD_DIM)).reshape(CMP_BLOCK, KV_COLS)
    eye = jnp.eye(2 * N_KV, dtype=F32).reshape(2, N_KV, 2, N_KV)
    order = 'cgdGCe' if group_major else 'cgdCGe'
    proj = jnp.einsum('cde,cgCG->' + order, cmp_proj, eye).reshape(KV_COLS, KV_COLS)
    return pos, proj


def _compress_prompt_kernel(kv_ref, pos_ref, proj_ref, projt_ref, kblk_ref, vblkt_ref):
    x = kv_ref[0]
    n_sel = x.shape[0] // SEL_BLOCK
    x = x.reshape(n_sel, SEL_BLOCK // CMP_BLOCK, CMP_BLOCK, KV_COLS)
    pos = pos_ref[...][None]
    sums = jnp.concatenate([jnp.sum(x[:, i] * pos, axis=1) for i in range(SEL_BLOCK // CMP_BLOCK)], axis=0)
    blk = _dot3(sums, proj_ref[...])
    blk_t = _dot3_nt(projt_ref[...], sums)
    for g in range(N_KV):
        kblk_ref[0, g] = blk[:, g * KV_HALF:g * KV_HALF + HEAD_DIM]
        vblkt_ref[0, g] = blk_t[g * KV_HALF + HEAD_DIM:(g + 1) * KV_HALF]


def _compress_prompt(kvc, pos, proj):
    n_seq, seq, _ = kvc.shape
    n_cmp = seq // CMP_BLOCK
    return pl.pallas_call(
        _compress_prompt_kernel,
        grid=(n_seq,),
        in_specs=[pl.BlockSpec((1, seq, KV_COLS), lambda b: (b, 0, 0)),
                  pl.BlockSpec((CMP_BLOCK, KV_COLS), lambda b: (0, 0)),
                  pl.BlockSpec((KV_COLS, KV_COLS), lambda b: (0, 0)),
                  pl.BlockSpec((KV_COLS, KV_COLS), lambda b: (0, 0))],
        out_specs=[pl.BlockSpec((1, N_KV, n_cmp, HEAD_DIM), lambda b: (b, 0, 0, 0)),
                   pl.BlockSpec((1, N_KV, HEAD_DIM, n_cmp), lambda b: (b, 0, 0, 0))],
        out_shape=[jax.ShapeDtypeStruct((n_seq, N_KV, n_cmp, HEAD_DIM), F32),
                   jax.ShapeDtypeStruct((n_seq, N_KV, HEAD_DIM, n_cmp), F32)],
        compiler_params=_params("arbitrary"),
    )(kvc, pos, proj, proj.T)


def _select_blocks_t(pair, q0):
    row = _iota2(pair.shape, 0)
    n_sel = pair.shape[0]
    cur = (q0 + _iota2(pair.shape, 1)) // SEL_BLOCK
    forced = (row == 0) | (row == cur) | (row == cur - 1)
    score = jnp.where(row <= cur, jnp.where(forced, pair + FORCE, pair), -jnp.inf)
    chosen = jnp.zeros(pair.shape, F32)
    for _ in range(N_SEL):
        m = jnp.max(score, axis=0, keepdims=True)
        hit = (score == m) & (m > -jnp.inf)
        first = jnp.min(jnp.where(hit, row, n_sel), axis=0, keepdims=True)
        pick = row == first
        chosen = jnp.where(pick, 1.0, chosen)
        score = jnp.where(pick, -jnp.inf, score)
    return chosen


def _attend(state, k_aug, v_t, q_aug, extra):
    m, acc = state
    s = jnp.dot(k_aug, q_aug, preferred_element_type=F32)
    if extra is not None:
        s = s + extra
    m_new = jnp.maximum(m, jnp.max(s, axis=0, keepdims=True))
    e = jnp.exp2(s - m_new)
    acc = jnp.exp2(m - m_new) * acc + jnp.dot(v_t, e.astype(BF16), preferred_element_type=F32)
    return m_new, acc


def _attend_result(state):
    acc = state[1]
    return acc[:HEAD_DIM] * (1.0 / jnp.maximum(acc[HEAD_DIM:HEAD_DIM + 1], TINY))


def _nsa_prompt_kernel(tbl_ref, qt_ref, gt_ref, kblk_ref, vblkt_ref, bcmp_ref, near_ref, ks_ref, vst_ref,
                       kw_ref, vwt_ref, o_ref):
    g = pl.program_id(1)
    qi = pl.program_id(2)
    q0 = qi * Q_TILE
    win_tiles = WINDOW // Q_TILE
    v_rows = pl.ds(pl.multiple_of(g * VT_ROWS, VT_ROWS), VT_ROWS)

    width = GQA * Q_TILE
    q = jnp.concatenate([qt_ref[0, r * HEAD_DIM:(r + 1) * HEAD_DIM, :] for r in range(GQA)], axis=1) * SCALE

    k_cmp, v_cmp_t = kblk_ref[0, 0], vblkt_ref[0, 0]
    n_cmp = k_cmp.shape[0]
    bias = bcmp_ref[0, 0]
    p = _masked_softmax(_dot3(k_cmp, q) + bias, bias > MASKED_BELOW, axis=0)
    o_cmp = _bdot(v_cmp_t, p)
    ps = p[:, :Q_TILE]
    for r in range(1, GQA):
        ps = ps + p[:, r * Q_TILE:(r + 1) * Q_TILE]
    chosen = _select_blocks_t(ps[:n_cmp // 2] + ps[n_cmp // 2:], q0)
    not_chosen = (chosen - 1.0).astype(BF16)
    pad_rows = (AUG_ONE - AUG_SEL) - not_chosen.shape[0]
    if pad_rows:
        not_chosen = jnp.concatenate([not_chosen, jnp.zeros((pad_rows, Q_TILE), BF16)], axis=0)

    row = _iota2((LANES - AUG_ONE, Q_TILE), 0)
    consts = []
    for r in range(GQA):
        last = jnp.full((LANES - AUG_ONE, Q_TILE), tbl_ref[N_BUCKETS - 1, g * GQA + r] * LOG2E, F32)
        hi = last.astype(BF16).astype(F32)
        consts.append(jnp.where(row == 0, hi, jnp.where(row == 1, last - hi, 0.0)).astype(BF16))
    q_aug = jnp.concatenate([(q * LOG2E).astype(BF16), jnp.concatenate([not_chosen] * GQA, axis=1),
                             jnp.concatenate(consts, axis=1)], axis=0)
    near0, near1, oldest = near_ref[0, 0], near_ref[0, 1], near_ref[0, 2]
    init = (jnp.full((1, width), NEG, F32), jnp.zeros((VT_ROWS, width), F32))

    def key_rows(ref, tile, n_tiles):
        return ref[0, pl.ds(pl.multiple_of(tile * Q_TILE, Q_TILE), n_tiles * Q_TILE), :]

    def value_cols(ref, tile, n_tiles):
        return jnp.concatenate([ref[0, tile + j, v_rows, :] for j in range(n_tiles)], axis=1)

    def tile_mask(ok):
        return jnp.full((Q_TILE, width), jnp.where(ok, 0.0, NEG), F32)

    extra = jnp.concatenate([near1 + tile_mask(qi >= 1), near0], axis=0)
    st = _attend(init, key_rows(ks_ref, qi, 2), value_cols(vst_ref, qi, 2), q_aug, extra)
    n_far = jnp.maximum(qi - 1, 0)

    def far_chunk(c, st, masked=False):
        t0 = c * FAR_TILES
        extra = None
        if masked:
            extra = jnp.concatenate([tile_mask(t0 + j < n_far) for j in range(FAR_TILES)], axis=0)
        return _attend(st, key_rows(ks_ref, t0 + 1, FAR_TILES), value_cols(vst_ref, t0 + 1, FAR_TILES), q_aug, extra)

    full = n_far // FAR_TILES
    st = lax.fori_loop(0, full, far_chunk, st)
    st = lax.cond(n_far % FAR_TILES != 0, lambda st=st: far_chunk(full, st, masked=True), lambda st=st: st)
    o_sel = _attend_result(st)

    extras = []
    for mt in range(win_tiles):
        ok = tile_mask(qi + mt >= win_tiles)
        extras.append(oldest + ok if mt == 0 else near1 + ok if mt == win_tiles - 1 else ok)
    extra = jnp.concatenate(extras + [near0], axis=0)
    st = _attend(init, key_rows(kw_ref, qi, win_tiles + 1), value_cols(vwt_ref, qi, win_tiles + 1), q_aug, extra)
    o_win = _attend_result(st)

    def gate(branch):
        return jnp.concatenate([gt_ref[0, r * N_BRANCH + branch:r * N_BRANCH + branch + 1, :]
                                for r in range(GQA)], axis=1)

    o = gate(0) * o_cmp + gate(1) * o_sel + gate(2) * o_win
    o_ref[0] = jnp.concatenate([o[:, r * Q_TILE:(r + 1) * Q_TILE] for r in range(GQA)], axis=0).T


def _nsa_prompt(rel_bias, q_t, gates_t, kblk, vblk_t, bcmp, near, ks_aug, vs_t, kw_aug, vw_t):
    n_seq, _, seq = q_t.shape
    nq = seq // Q_TILE
    assert nq % FAR_TILES == 0
    n_cmp = kblk.shape[2]
    gw = GQA * HEAD_DIM
    wt = WINDOW // Q_TILE
    return pl.pallas_call(
        _nsa_prompt_kernel,
        grid=(n_seq, N_KV, nq),
        in_specs=[pl.BlockSpec(memory_space=pltpu.SMEM),
                  pl.BlockSpec((1, gw, Q_TILE), lambda b, g, qi: (b, g, qi)),
                  pl.BlockSpec((1, GATE_ROWS, Q_TILE), lambda b, g, qi: (b, g, qi)),
                  pl.BlockSpec((1, 1, n_cmp, HEAD_DIM), lambda b, g, qi: (b, g, 0, 0)),
                  pl.BlockSpec((1, 1, HEAD_DIM, n_cmp), lambda b, g, qi: (b, g, 0, 0)),
                  pl.BlockSpec((1, 1, n_cmp, GQA * Q_TILE), lambda b, g, qi: (qi, g, 0, 0)),
                  pl.BlockSpec((1, 3, Q_TILE, GQA * Q_TILE), lambda b, g, qi: (g, 0, 0, 0)),
                  pl.BlockSpec((1, Q_TILE + seq, LANES), lambda b, g, qi: (b, 0, g)),
                  pl.BlockSpec((1, 1 + nq, N_KV * VT_ROWS, Q_TILE), lambda b, g, qi: (b, 0, 0, 0)),
                  pl.BlockSpec((1, WINDOW + seq, LANES), lambda b, g, qi: (b, 0, g)),
                  pl.BlockSpec((1, wt + nq, N_KV * VT_ROWS, Q_TILE), lambda b, g, qi: (b, 0, 0, 0))],
        out_specs=pl.BlockSpec((1, Q_TILE, gw), lambda b, g, qi: (b, qi, g)),
        out_shape=jax.ShapeDtypeStruct((n_seq, seq, NSA_WIDTH), F32),
        compiler_params=_params("arbitrary", "arbitrary", "arbitrary"),
    )(rel_bias, q_t, gates_t, kblk, vblk_t, bcmp, near, ks_aug, vs_t, kw_aug, vw_t)


STATE_CHUNK = 512
SSM_KB = SSM_WIDTH // LANES
GROUPS_PER_KB = N_SSM_GROUPS // SSM_KB


def _ssm_disc_kernel(are_ref, aim_ref, ldt_ref, bre_ref, bim_ref, abr_ref, abi_ref, bbr_ref, bbi_ref):
    a_re, a_im = are_ref[...], aim_ref[...]
    dt = jnp.exp(ldt_ref[...])
    mag = jnp.exp(a_re * dt)
    ab_re = mag * jnp.cos(a_im * dt)
    ab_im = mag * jnp.sin(a_im * dt)
    den = a_re * a_re + a_im * a_im
    co_re = ((ab_re - 1.0) * a_re + ab_im * a_im) / den
    co_im = (ab_im * a_re - (ab_re - 1.0) * a_im) / den
    abr_ref[...] = ab_re
    abi_ref[...] = ab_im
    b_re, b_im = bre_ref[...], bim_ref[...]
    bbr_ref[...] = co_re[:, None, :] * b_re - co_im[:, None, :] * b_im
    bbi_ref[...] = co_re[:, None, :] * b_im + co_im[:, None, :] * b_re


def _block_diag(w):
    a, b = w.shape[1:]
    w = w.reshape(SSM_KB, GROUPS_PER_KB, a, b)
    eye = jnp.eye(GROUPS_PER_KB, dtype=w.dtype)
    return jnp.einsum('kgab,gh->kgahb', w, eye).reshape(SSM_KB, GROUPS_PER_KB * a, GROUPS_PER_KB * b)


def _ssm_weights(a_re, a_im, b_re, b_im, c_re, c_im, log_dt):
    ng, p = a_re.shape
    vm = pl.BlockSpec(memory_space=pltpu.VMEM)
    ab_re, ab_im, bb_re, bb_im = pl.pallas_call(
        _ssm_disc_kernel,
        in_specs=[vm] * 5, out_specs=[vm] * 4,
        out_shape=[jax.ShapeDtypeStruct((ng, p), F32)] * 2
        + [jax.ShapeDtypeStruct((ng, SSM_GROUP, p), F32)] * 2,
    )(a_re, a_im, log_dt.reshape(ng, 1), b_re.transpose(0, 2, 1), b_im.transpose(0, 2, 1))
    return (ab_re.reshape(1, N_STATE), ab_im.reshape(1, N_STATE),
            _block_diag(bb_re).astype(BF16), _block_diag(bb_im).astype(BF16),
            _block_diag(c_re.transpose(0, 2, 1)).astype(BF16),
            _block_diag(c_im.transpose(0, 2, 1)).astype(BF16))


def _ssm_kernel(u_ref, h0r_ref, h0i_ref, abr_ref, abi_ref, bbr_ref, bbi_ref, ccr_ref, cci_ref,
                d_ref, gw_ref, gb_ref, o_ref, hr_ref, hi_ref, sre, sim, *, nb, steps):
    @pl.when(pl.program_id(0) == 0)
    def _():
        hr_ref[...] = h0r_ref[...]
        hi_ref[...] = h0i_ref[...]

    u = u_ref[...]
    ub = u.astype(BF16)
    kw = N_STATE // SSM_KB
    for kb in range(SSM_KB):
        uk = ub[:, kb * LANES:(kb + 1) * LANES]
        sre[:, kb * kw:(kb + 1) * kw] = jnp.dot(uk, bbr_ref[kb], preferred_element_type=F32)
        sim[:, kb * kw:(kb + 1) * kw] = jnp.dot(uk, bbi_ref[kb], preferred_element_type=F32)

    if steps == 1:
        ar, ai = abr_ref[...], abi_ref[...]
        hr, hi = hr_ref[...], hi_ref[...]
        nr = ar * hr - ai * hi + sre[...]
        ni = ar * hi + ai * hr + sim[...]
        sre[...] = nr
        sim[...] = ni
        hr_ref[...] = nr
        hi_ref[...] = ni
    else:
        for cb in range(N_STATE // STATE_CHUNK):
            cols = slice(cb * STATE_CHUNK, (cb + 1) * STATE_CHUNK)
            ar = jnp.broadcast_to(abr_ref[:, cols], (nb, STATE_CHUNK))
            ai = jnp.broadcast_to(abi_ref[:, cols], (nb, STATE_CHUNK))

            def step(t, carry, cols=cols, ar=ar, ai=ai):
                hr, hi = carry
                rows = pl.ds(pl.multiple_of(t * nb, nb), nb)
                nr = ar * hr - ai * hi + sre[rows, cols]
                ni = ar * hi + ai * hr + sim[rows, cols]
                sre[rows, cols] = nr
                sim[rows, cols] = ni
                return nr, ni

            hr, hi = lax.fori_loop(0, steps, step, (hr_ref[:, cols], hi_ref[:, cols]))
            hr_ref[:, cols] = hr
            hi_ref[:, cols] = hi

    ys = []
    for kb in range(SSM_KB):
        cols = slice(kb * kw, (kb + 1) * kw)
        ys.append(_bdot(sre[:, cols], ccr_ref[kb]) - _bdot(sim[:, cols], cci_ref[kb]))
    y = jnp.concatenate(ys, axis=1) + d_ref[...] * u
    g = jax.nn.gelu(y)
    o_ref[...] = g * jax.nn.sigmoid(_bdot(g, gw_ref[...]) + gb_ref[...])


def _ssm(u_rows, h0_re, h0_im, wts, d, glu_w, glu_b, nb, steps_per_call):
    ab_re, ab_im, bb_re, bb_im, cc_re, cc_im = wts
    rows = nb * steps_per_call
    n_calls = u_rows.shape[0] // rows
    const2 = lambda i: (0, 0)
    const3 = lambda i: (0, 0, 0)
    kw = N_STATE // SSM_KB
    return pl.pallas_call(
        functools.partial(_ssm_kernel, nb=nb, steps=steps_per_call),
        grid=(n_calls,),
        in_specs=[pl.BlockSpec((rows, SSM_WIDTH), lambda i: (i, 0)),
                  pl.BlockSpec((nb, N_STATE), const2), pl.BlockSpec((nb, N_STATE), const2),
                  pl.BlockSpec((1, N_STATE), const2), pl.BlockSpec((1, N_STATE), const2),
                  pl.BlockSpec((SSM_KB, LANES, kw), const3), pl.BlockSpec((SSM_KB, LANES, kw), const3),
                  pl.BlockSpec((SSM_KB, kw, LANES), const3), pl.BlockSpec((SSM_KB, kw, LANES), const3),
                  pl.BlockSpec((1, SSM_WIDTH), const2),
                  pl.BlockSpec((SSM_WIDTH, SSM_WIDTH), const2),
                  pl.BlockSpec((1, SSM_WIDTH), const2)],
        out_specs=[pl.BlockSpec((rows, SSM_WIDTH), lambda i: (i, 0)),
                   pl.BlockSpec((nb, N_STATE), const2), pl.BlockSpec((nb, N_STATE), const2)],
        out_shape=[jax.ShapeDtypeStruct(u_rows.shape, F32),
                   jax.ShapeDtypeStruct((nb, N_STATE), F32), jax.ShapeDtypeStruct((nb, N_STATE), F32)],
        scratch_shapes=[pltpu.VMEM((rows, N_STATE), F32), pltpu.VMEM((rows, N_STATE), F32)],
        compiler_params=_params("arbitrary"),
    )(u_rows, h0_re, h0_im, ab_re, ab_im, bb_re, bb_im, cc_re, cc_im, d, glu_w, glu_b)


def _merge_kernel(onsa_ref, ossm_ref, x_ref, gg_ref, wout_ref, ng_ref, xq_ref, x1_ref, qx_ref):
    gg = gg_ref[...]
    a = _rms(onsa_ref[...], gg[:, :NSA_WIDTH])
    b = _rms(ossm_ref[...], gg[:, NSA_WIDTH:])
    m = (jnp.dot(a.astype(BF16), wout_ref[:NSA_WIDTH], preferred_element_type=F32)
         + jnp.dot(b.astype(BF16), wout_ref[NSA_WIDTH:], preferred_element_type=F32))
    x1 = x_ref[...] + _rms(m, ng_ref[1:2])
    x1_ref[...] = x1
    qx_ref[...] = _bdot(_rms(x1, ng_ref[2:3]), xq_ref[...])


def _merge(o_nsa, o_ssm, x, gg, w_out, ng, xq, n_seq, tile):
    rows = x.shape[0]
    nt = rows // n_seq // tile
    row_map = lambda b, t: (b * nt + t, 0)
    const = lambda b, t: (0, 0)
    return pl.pallas_call(
        _merge_kernel,
        grid=(n_seq, nt),
        in_specs=[pl.BlockSpec((tile, NSA_WIDTH), row_map),
                  pl.BlockSpec((tile, SSM_WIDTH), lambda b, t: (t, b)),
                  pl.BlockSpec((tile, D_MODEL), row_map),
                  pl.BlockSpec((1, D_MODEL), const),
                  pl.BlockSpec((D_MODEL, D_MODEL), const),
                  pl.BlockSpec((6, D_MODEL), const),
                  pl.BlockSpec((D_MODEL, MEM_WIDTH), const)],
        out_specs=[pl.BlockSpec((tile, D_MODEL), row_map), pl.BlockSpec((tile, MEM_WIDTH), row_map)],
        out_shape=[jax.ShapeDtypeStruct((rows, D_MODEL), F32),
                   jax.ShapeDtypeStruct((rows, MEM_WIDTH), F32)],
        compiler_params=_params("arbitrary", "arbitrary"),
    )(o_nsa, o_ssm, x, gg, w_out, ng, xq)


def _norm_matmul_kernel(x_ref, g_ref, w_ref, o_ref):
    o_ref[...] = _bdot(_rms(x_ref[...], g_ref[...]), w_ref[...])


def _norm_matmul(x, g, w, tile):
    rows, k = x.shape
    n = w.shape[1]
    return pl.pallas_call(
        _norm_matmul_kernel,
        grid=(rows // tile,),
        in_specs=[pl.BlockSpec((tile, k), lambda i: (i, 0)),
                  pl.BlockSpec((1, k), lambda i: (0, 0)),
                  pl.BlockSpec((k, n), lambda i: (0, 0))],
        out_specs=pl.BlockSpec((tile, n), lambda i: (i, 0)),
        out_shape=jax.ShapeDtypeStruct((rows, n), F32),
        compiler_params=_params("arbitrary"),
    )(x, g, w)


def _cross_prompt_kernel(qx_ref, mkv_ref, o_ref):
    qx = qx_ref[...]
    outs = []
    for h in range(MEM_HEADS):
        cols = slice(h * MEM_HEAD_DIM, (h + 1) * MEM_HEAD_DIM)
        k = mkv_ref[0, :, cols]
        v = mkv_ref[0, :, MEM_WIDTH + h * MEM_HEAD_DIM: MEM_WIDTH + (h + 1) * MEM_HEAD_DIM]
        s = _bdot_nt(qx[:, cols], k) * MEM_SCALE
        e = jnp.exp(s - jnp.max(s, axis=-1, keepdims=True))
        p = e * (1.0 / jnp.sum(e, axis=-1, keepdims=True))
        outs.append(_bdot(p, v))
    o_ref[...] = jnp.concatenate(outs, axis=1)


def _cross_prompt(qx, mkv, n_seq, tile):
    rows = qx.shape[0]
    nt = rows // n_seq // tile
    n_mem = mkv.shape[1]
    return pl.pallas_call(
        _cross_prompt_kernel,
        grid=(n_seq, nt),
        in_specs=[pl.BlockSpec((tile, MEM_WIDTH), lambda b, t: (b * nt + t, 0)),
                  pl.BlockSpec((1, n_mem, 2 * MEM_WIDTH), lambda b, t: (b, 0, 0))],
        out_specs=pl.BlockSpec((tile, MEM_WIDTH), lambda b, t: (b * nt + t, 0)),
        out_shape=jax.ShapeDtypeStruct((rows, MEM_WIDTH), F32),
        compiler_params=_params("arbitrary", "arbitrary"),
    )(qx, mkv)


SAMPLE_GROUP = 8


def _cross_sample_kernel(qx_ref, mkv_ref, o_ref, *, n_mem):
    per_tok = 2 * MEM_HEADS
    sub = _iota2((SAMPLE_GROUP, MEM_HEAD_DIM), 0)
    qx = qx_ref[...]
    outs = []
    for i in range(SAMPLE_GROUP):
        s = jnp.zeros((SAMPLE_GROUP, n_mem), F32)
        for h in range(MEM_HEADS):
            qh = jnp.broadcast_to(qx[i:i + 1, h * MEM_HEAD_DIM:(h + 1) * MEM_HEAD_DIM], sub.shape)
            k = mkv_ref[i, pl.ds(h, n_mem, stride=per_tok), :]
            s = s + _bdot_nt(jnp.where(sub == h, qh, 0.0), k)
        s = s * MEM_SCALE
        e = jnp.exp(s - jnp.max(s, axis=-1, keepdims=True))
        p = e * (1.0 / jnp.sum(e, axis=-1, keepdims=True))
        heads = []
        for h in range(MEM_HEADS):
            v = mkv_ref[i, pl.ds(MEM_HEADS + h, n_mem, stride=per_tok), :]
            heads.append(_bdot(p, v)[h:h + 1])
        outs.append(jnp.concatenate(heads, axis=1))
    o_ref[...] = jnp.concatenate(outs, axis=0)


def _cross_sample(qx, mkv, n_mem):
    rows = qx.shape[0]
    return pl.pallas_call(
        functools.partial(_cross_sample_kernel, n_mem=n_mem),
        grid=(rows // SAMPLE_GROUP,),
        in_specs=[pl.BlockSpec((SAMPLE_GROUP, MEM_WIDTH), lambda i: (i, 0)),
                  pl.BlockSpec((SAMPLE_GROUP,) + mkv.shape[1:], lambda i: (i, 0, 0))],
        out_specs=pl.BlockSpec((SAMPLE_GROUP, MEM_WIDTH), lambda i: (i, 0)),
        out_shape=jax.ShapeDtypeStruct((rows, MEM_WIDTH), F32),
        compiler_params=_params("arbitrary"),
    )(qx, mkv)


def _ffn_kernel(x1_ref, oa_ref, ng_ref, xo_ref, wi_ref, wo_ref, o_ref, *, d_ff):
    x2 = x1_ref[...] + _rms(_bdot(oa_ref[...], xo_ref[...]), ng_ref[3:4])
    h = _rms(x2, ng_ref[4:5]).astype(BF16)
    z1 = jnp.dot(h, wi_ref[:, :d_ff], preferred_element_type=F32)
    z2 = jnp.dot(h, wi_ref[:, d_ff:], preferred_element_type=F32)
    y = _bdot(z1 * jax.nn.sigmoid(z1) * z2, wo_ref[...])
    o_ref[...] = x2 + _rms(y, ng_ref[5:6])


def _ffn(x1, oa, ng, xo, wi, wo, tile):
    rows = x1.shape[0]
    d_ff = wo.shape[0]
    const = lambda i: (0, 0)
    once = dict(pipeline_mode=pl.Buffered(1))
    return pl.pallas_call(
        functools.partial(_ffn_kernel, d_ff=d_ff),
        grid=(rows // tile,),
        in_specs=[pl.BlockSpec((tile, D_MODEL), lambda i: (i, 0)),
                  pl.BlockSpec((tile, MEM_WIDTH), lambda i: (i, 0)),
                  pl.BlockSpec((6, D_MODEL), const),
                  pl.BlockSpec((MEM_WIDTH, D_MODEL), const, **once),
                  pl.BlockSpec((D_MODEL, 2 * d_ff), const, **once),
                  pl.BlockSpec((d_ff, D_MODEL), const, **once)],
        out_specs=pl.BlockSpec((tile, D_MODEL), lambda i: (i, 0)),
        out_shape=jax.ShapeDtypeStruct((rows, D_MODEL), F32),
        compiler_params=_params("arbitrary"),
    )(x1, oa, ng, xo, wi, wo)


PAGES_PER_STEP = 32
NEW_TILE = 8
CMP_GROUP = 8


def _bias_rows(dist, tblt_ref):
    v = jnp.broadcast_to(tblt_ref[:, 0:1], dist.shape)
    for k in range(1, N_BUCKETS):
        v = jnp.where(dist >= _BUCKET_START[k], tblt_ref[:, k:k + 1], v)
    return v


def _sample_bias_kernel(tblt_ref, bw_ref, bc_ref, bn_ref, bs_ref, b0_ref, *, past_len):
    dist = (WINDOW - 1) - _iota2(bw_ref.shape, 1)
    bw_ref[...] = jnp.where((dist >= 0) & (dist < WINDOW), _bias_rows(dist, tblt_ref), NEG)
    dist = past_len - (_iota2(bc_ref.shape, 1) * CMP_BLOCK + (CMP_BLOCK - 1))
    bc_ref[...] = jnp.where(dist >= 0, _bias_rows(dist, tblt_ref), NEG)
    lane = _iota2(bn_ref.shape, 1)
    dist = past_len - ((past_len // CMP_BLOCK + lane) * CMP_BLOCK + (CMP_BLOCK - 1))
    bn_ref[...] = jnp.where((dist >= 0) & (lane < SEL_BLOCK // CMP_BLOCK),
                            _bias_rows(jnp.maximum(dist, 0), tblt_ref), NEG)
    page = lax.broadcasted_iota(I32, bs_ref.shape, 0)
    dist = past_len - page * PAGE_SIZE - lax.broadcasted_iota(I32, bs_ref.shape, 2)
    bs_ref[...] = _bias_rows(dist, tblt_ref)
    b0_ref[...] = _bias_rows(jnp.zeros(b0_ref.shape, I32), tblt_ref)


def _sample_bias(rel_bias, past_len):
    vm = pl.BlockSpec(memory_space=pltpu.VMEM)
    shapes = [(N_HEADS, WINDOW), (N_HEADS, past_len // CMP_BLOCK), (N_HEADS, LANES),
              (past_len // PAGE_SIZE, N_HEADS, PAGE_SIZE), (N_HEADS, LANES)]
    return pl.pallas_call(
        functools.partial(_sample_bias_kernel, past_len=past_len),
        in_specs=[vm], out_specs=[vm] * len(shapes),
        out_shape=[jax.ShapeDtypeStruct(s, F32) for s in shapes],
    )(rel_bias.T)


def _compress_sample_kernel(pt_ref, *refs):
    pages = refs[:PAGES_PER_STEP]
    post_ref, fold_ref, projt_ref, o_ref = refs[PAGES_PER_STEP:]
    post = post_ref[...]
    weighted = jnp.concatenate([(pg[0] * post).astype(BF16) for pg in pages], axis=1)
    sums_t = jnp.dot(weighted, fold_ref[...], preferred_element_type=F32)
    o_ref[0] = _dot3(projt_ref[...], sums_t)


def _compress_sample(pool_t, page_table_flat, n_seq, n_pages, pos, proj):
    steps = n_pages // PAGES_PER_STEP
    blocks_per_page = PAGE_SIZE // CMP_BLOCK
    blocks_per_step = PAGES_PER_STEP * blocks_per_page
    assert blocks_per_step == LANES
    pos_t = jnp.tile(pos.T, (1, blocks_per_page))
    row = np.arange(PAGES_PER_STEP * PAGE_SIZE)
    fold = jnp.asarray(row[:, None] // CMP_BLOCK == np.arange(blocks_per_step)[None, :], dtype=BF16)

    def page_map(i):
        return lambda b, s, pt: (pt[b * n_pages + s * PAGES_PER_STEP + i], 0, 0)

    const = lambda b, s, pt: (0, 0)
    grid_spec = pltpu.PrefetchScalarGridSpec(
        num_scalar_prefetch=1,
        grid=(n_seq, steps),
        in_specs=[pl.BlockSpec((1, KV_COLS, PAGE_SIZE), page_map(i)) for i in range(PAGES_PER_STEP)]
        + [pl.BlockSpec((KV_COLS, PAGE_SIZE), const),
           pl.BlockSpec(fold.shape, const),
           pl.BlockSpec((KV_COLS, KV_COLS), const)],
        out_specs=pl.BlockSpec((1, KV_COLS, blocks_per_step), lambda b, s, pt: (b, 0, s)),
    )
    return pl.pallas_call(
        _compress_sample_kernel,
        grid_spec=grid_spec,
        out_shape=jax.ShapeDtypeStruct((n_seq, KV_COLS, steps * blocks_per_step), F32),
        compiler_params=_params("arbitrary", "arbitrary"),
    )(page_table_flat, *([pool_t] * PAGES_PER_STEP), pos_t, fold, proj.T)


def _cmp_sample_kernel(qm_ref, blk_ref, kvc_ref, pos_ref, proj_ref, bc_ref, bn_ref, oc_ref, idx_ref,
                       *, past_len):
    n_cmp = blk_ref.shape[2]
    bias = bc_ref[...]
    mask = bias > MASKED_BELOW
    bias_n = bn_ref[:, :NEW_TILE]
    mask_n = bias_n > MASKED_BELOW
    group_ps = []
    for i in range(CMP_GROUP):
        qm = qm_ref[i]
        s = jnp.where(mask, _dot3(qm, blk_ref[i, :KV_HALF, :]) * SCALE + bias, NEG)
        new_sum = jnp.concatenate([kvc_ref[i] * pos_ref[0:1], jnp.zeros((NEW_TILE - 1, KV_COLS), F32)], axis=0)
        new_blk = _dot3(new_sum, proj_ref[...])
        s_n = jnp.where(mask_n, _dot3_nt(qm, new_blk[:, :KV_HALF]) * SCALE + bias_n, NEG)
        m = jnp.maximum(jnp.max(s, axis=-1, keepdims=True), jnp.max(s_n, axis=-1, keepdims=True))
        e = jnp.where(mask, jnp.exp(s - m), 0.0)
        e_n = jnp.where(mask_n, jnp.exp(s_n - m), 0.0)
        inv = 1.0 / jnp.maximum(jnp.sum(e, axis=-1, keepdims=True) + jnp.sum(e_n, axis=-1, keepdims=True), TINY)
        p = e * inv
        oc_ref[i] = _bdot_nt(p, blk_ref[i, KV_HALF:, :]) + _bdot(e_n * inv, new_blk[:, KV_HALF:])
        group_ps += [jnp.sum(p[g * GQA:(g + 1) * GQA], axis=0, keepdims=True) for g in range(N_KV)]

    ps = jnp.concatenate(group_ps, axis=0)
    lane = _iota2(ps.shape, 1)
    pair = ps + pltpu.roll(ps, n_cmp - 1, 1)
    cur = past_len // SEL_BLOCK
    j = lane // (SEL_BLOCK // CMP_BLOCK)
    forced = (j == 0) | (j == cur) | (j == cur - 1)
    score = jnp.where(lane % 2 == 0, jnp.where(forced, pair + FORCE, pair), -jnp.inf)
    slot = _iota2((ps.shape[0], LANES), 1)
    ids = jnp.zeros((ps.shape[0], LANES), I32)
    for it in range(N_SEL - 1):
        top = jnp.max(score, axis=-1, keepdims=True)
        first = jnp.min(jnp.where(score == top, lane, n_cmp), axis=-1, keepdims=True)
        ids = jnp.where(slot == it, first // (SEL_BLOCK // CMP_BLOCK), ids)
        score = jnp.where(lane == first, -jnp.inf, score)
    idx_ref[...] = jnp.where(slot == N_SEL - 1, cur, ids)


def _cmp_sample(qm, blocks_t, kvc_new, pos, proj, bc, bn, past_len):
    n_seq, _, n_cmp = blocks_t.shape
    assert past_len // SEL_BLOCK > N_SEL and past_len % SEL_BLOCK == 0
    const = lambda b: (0, 0)
    per_seq = lambda b: (b, 0, 0)
    return pl.pallas_call(
        functools.partial(_cmp_sample_kernel, past_len=past_len),
        grid=(n_seq // CMP_GROUP,),
        in_specs=[pl.BlockSpec((CMP_GROUP, N_HEADS, LANES), per_seq),
                  pl.BlockSpec((CMP_GROUP, KV_COLS, n_cmp), per_seq),
                  pl.BlockSpec((CMP_GROUP, 1, KV_COLS), per_seq),
                  pl.BlockSpec((CMP_BLOCK, KV_COLS), const),
                  pl.BlockSpec((KV_COLS, KV_COLS), const),
                  pl.BlockSpec((N_HEADS, n_cmp), const),
                  pl.BlockSpec((N_HEADS, LANES), const)],
        out_specs=[pl.BlockSpec((CMP_GROUP, N_HEADS, LANES), per_seq),
                   pl.BlockSpec((CMP_GROUP * N_KV, LANES), lambda b: (b, 0))],
        out_shape=[jax.ShapeDtypeStruct((n_seq, N_HEADS, LANES), F32),
                   jax.ShapeDtypeStruct((n_seq * N_KV, LANES), I32)],
        compiler_params=_params("arbitrary"),
    )(qm, blocks_t, kvc_new, pos, proj, bc, bn)


def _sel_win_sample_kernel(idx_ref, pt_ref, *refs, n_past_blocks):
    n_sel_pages = N_KV * N_SEL
    pages = refs[:n_sel_pages]
    (buf_ref, qm_ref, kvs_ref, kvw_ref, gate_ref, oc_ref, bw_ref, bs_ref, b0_ref,
     o_ref, nbuf_ref) = refs[n_sel_pages:]
    b = pl.program_id(0)
    bpp = PAGE_SIZE // SEL_BLOCK
    qm = qm_ref[0]

    buf = buf_ref[0]
    shifted = pltpu.roll(buf, WINDOW - 1, 1)
    nbuf = jnp.where(_iota2(buf.shape, 1) == WINDOW - 1, jnp.broadcast_to(kvw_ref[0], buf.shape), shifted)
    nbuf_ref[0] = nbuf
    bias = bw_ref[...]
    p = _masked_softmax(_bdot(qm, nbuf[:KV_HALF]) * SCALE + bias, bias > MASKED_BELOW)
    o_win = _bdot_nt(p, nbuf[KV_HALF:])

    lane_half = _iota2((GQA, PAGE_SIZE), 1) // SEL_BLOCK
    o_sel = []
    for g in range(N_KV):
        cols = slice(g * HEAD_DIM, (g + 1) * HEAD_DIM)
        rows = slice(g * GQA, (g + 1) * GQA)
        qg = qm[rows, cols]
        ks, vs, biases = [], [], []
        has_new = False
        for k in range(N_SEL):
            i = idx_ref[(b * N_KV + g) * N_SEL + k]
            is_past = i < n_past_blocks
            has_new = jnp.logical_or(has_new, jnp.logical_not(is_past))
            i = jnp.minimum(i, n_past_blocks - 1)
            blk = pages[g * N_SEL + k]
            ks.append(blk[0, 0, 0])
            vs.append(blk[0, 1, 0])
            tile = bs_ref[i // bpp]
            biases.append(jnp.where(is_past & (lane_half == i % bpp), tile[rows], NEG))
        bias = jnp.concatenate(biases, axis=1)
        mask = bias > MASKED_BELOW
        s = jnp.where(mask, _bdot(qg, jnp.concatenate(ks, axis=1)) * SCALE + bias, NEG)
        k_new = kvs_ref[0][:, cols]
        v_new = kvs_ref[0][:, KV_HALF + g * HEAD_DIM:KV_HALF + (g + 1) * HEAD_DIM]
        s_new = jnp.where(has_new,
                          jnp.sum(qg * k_new, axis=-1, keepdims=True) * SCALE + b0_ref[rows, 0:1], NEG)
        m = jnp.maximum(jnp.max(s, axis=-1, keepdims=True), s_new)
        e = jnp.where(mask, jnp.exp(s - m), 0.0)
        e_new = jnp.where(has_new, jnp.exp(s_new - m), 0.0)
        inv = 1.0 / jnp.maximum(jnp.sum(e, axis=-1, keepdims=True) + e_new, TINY)
        o_sel.append(_bdot_nt(e * inv, jnp.concatenate(vs, axis=1)) + (e_new * inv) * v_new)
    o_sel = jnp.concatenate(o_sel, axis=0)

    first_group = _iota2((N_HEADS, HEAD_DIM), 0) < GQA
    oc = oc_ref[0]
    o_cmp = jnp.where(first_group, oc[:, :HEAD_DIM], oc[:, HEAD_DIM:])
    o_win = jnp.where(first_group, o_win[:, :HEAD_DIM], o_win[:, HEAD_DIM:])
    gate = gate_ref[0]
    o_ref[0] = gate[:, 0:1] * o_cmp + gate[:, 1:2] * o_sel + gate[:, 2:3] * o_win


def _sel_win_sample(idx_flat, pt_flat, pool_t, buf_t, qm, kvs_new, kvw_new_col, gates, o_cmp, bw, bs, b0,
                    n_pages):
    n_seq = buf_t.shape[0]
    bpp = PAGE_SIZE // SEL_BLOCK
    n_past_blocks = n_pages * bpp

    def page_map(g, k):
        return lambda b, idx, pt: (pt[(b * N_KV + g) * N_SEL + k], 0, g, 0, 0)

    per_seq3 = lambda b, idx, pt: (b, 0, 0)
    const2 = lambda b, idx, pt: (0, 0)
    head_tile = pl.BlockSpec((1, N_HEADS, LANES), per_seq3)
    grid_spec = pltpu.PrefetchScalarGridSpec(
        num_scalar_prefetch=2,
        grid=(n_seq,),
        in_specs=[pl.BlockSpec((1, 2, 1, HEAD_DIM, PAGE_SIZE), page_map(g, k))
                  for g in range(N_KV) for k in range(N_SEL)]
        + [pl.BlockSpec((1, KV_COLS, WINDOW), per_seq3), head_tile,
           pl.BlockSpec((1, 1, KV_COLS), per_seq3), pl.BlockSpec((1, KV_COLS, 1), per_seq3), head_tile,
           head_tile,
           pl.BlockSpec((N_HEADS, WINDOW), const2),
           pl.BlockSpec((n_pages, N_HEADS, PAGE_SIZE), lambda b, idx, pt: (0, 0, 0)),
           pl.BlockSpec((N_HEADS, LANES), const2)],
        out_specs=[pl.BlockSpec((1, N_HEADS, HEAD_DIM), per_seq3),
                   pl.BlockSpec((1, KV_COLS, WINDOW), per_seq3)],
    )
    return pl.pallas_call(
        functools.partial(_sel_win_sample_kernel, n_past_blocks=n_past_blocks),
        grid_spec=grid_spec,
        out_shape=[jax.ShapeDtypeStruct((n_seq, N_HEADS, HEAD_DIM), F32),
                   jax.ShapeDtypeStruct(buf_t.shape, F32)],
        compiler_params=_params("arbitrary"),
    )(idx_flat, pt_flat, *([pool_t] * (N_KV * N_SEL)), buf_t, qm, kvs_new, kvw_new_col, gates, o_cmp, bw, bs, b0)


PROJ_TILE = 512
ROW_TILE = 256
SCAN_STEPS = 128


def kernel(x_prompt, x_sample, cache_kv_cmp, cache_kv_sel, cache_kv_win, state_ssm_re, state_ssm_im,
           cache_mem_kv, page_table, mem_prompt, w_in, w_out, norm_g, grp_norm_g, cmp_pos, cmp_proj,
           rel_bias, ssm_a_re, ssm_a_im, ssm_b_re, ssm_b_im, ssm_c_re, ssm_c_im, ssm_d, ssm_log_dt,
           glu_w, glu_b, mem_norm_g, xq, xkv, xo, ffn_wi, ffn_wo):
    depth = w_in.shape[0]
    n_seq, seq, _ = x_prompt.shape
    n_dec, dec_seq, _ = x_sample.shape
    assert dec_seq == 1, "the sample kernels handle one new token per sequence"
    n_pages = page_table.shape[1]
    past_len = n_pages * PAGE_SIZE
    n_mem = mem_prompt.shape[1]
    assert cache_kv_win.shape[2] == WINDOW and seq >= WINDOW and WINDOW % Q_TILE == 0
    win_tiles = WINDOW // Q_TILE

    bcmp, near = _prompt_bias(rel_bias, seq)
    bw_s, bc_s, bn_s, bs_s, b0_s = _sample_bias(rel_bias, past_len)
    pt_flat = page_table.reshape(-1)
    kv5 = (2, N_KV, HEAD_DIM)

    xp = x_prompt.reshape(n_seq * seq, D_MODEL)
    xs = x_sample.reshape(n_dec, D_MODEL)
    outs = [[] for _ in range(11)]
    for l in range(depth):
        ng = norm_g[l]
        w_row, w_t = _pack_w_prompt(w_in[l])
        w_sample = _pack_w_sample(w_in[l])
        w_out_b, xq_b, xkv_b, xo_b = (w[l].astype(BF16) for w in (w_out, xq, xkv, xo))
        wi_b, wo_b, glu_w_b = ffn_wi[l].astype(BF16), ffn_wo[l].astype(BF16), glu_w[l].astype(BF16)
        gg = grp_norm_g[l][None]
        ssm_w = _ssm_weights(ssm_a_re[l], ssm_a_im[l], ssm_b_re[l], ssm_b_im[l], ssm_c_re[l], ssm_c_im[l],
                             ssm_log_dt[l])
        ssm_tail = (ssm_d[l][None], glu_w_b, glu_b[l][None])
        pos, proj_g = _pack_compress(cmp_pos[l], cmp_proj[l], group_major=True)
        _, proj_c = _pack_compress(cmp_pos[l], cmp_proj[l], group_major=False)

        (kvc, ks_aug, kw_aug, u, q_t, kvc_t, kvs_t, kvw_t, vs_t, vw_t, gates_t) = _inproj_prompt(
            xp, ng[0:1], w_row, w_t, n_seq, PROJ_TILE)
        kblk, vblk_t = _compress_prompt(kvc.reshape(n_seq, seq, KV_COLS), pos, proj_g)
        kw_aug = jnp.pad(kw_aug.reshape(n_seq, seq, N_KV * LANES), ((0, 0), (WINDOW, 0), (0, 0)))
        vw_t = jnp.pad(vw_t, ((0, 0), (win_tiles, 0), (0, 0), (0, 0)))
        ks_aug = jnp.pad(ks_aug.reshape(n_seq, seq, N_KV * LANES), ((0, 0), (Q_TILE, 0), (0, 0)))
        vs_t = jnp.pad(vs_t, ((0, 0), (1, 0), (0, 0), (0, 0)))
        o_nsa = _nsa_prompt(rel_bias, q_t, gates_t, kblk, vblk_t, bcmp, near, ks_aug, vs_t, kw_aug, vw_t)
        zeros = jnp.zeros((n_seq, N_STATE), F32)
        o_ssm, h_re, h_im = _ssm(u.reshape(seq * n_seq, SSM_WIDTH), zeros, zeros, ssm_w, *ssm_tail,
                                 n_seq, SCAN_STEPS)
        x1, qx = _merge(o_nsa.reshape(n_seq * seq, NSA_WIDTH), o_ssm.reshape(seq, n_seq * SSM_WIDTH), xp,
                        gg, w_out_b, ng, xq_b, n_seq, ROW_TILE)
        mkv = _norm_matmul(mem_prompt.reshape(n_seq * n_mem, D_MODEL), mem_norm_g[l][None], xkv_b, ROW_TILE)
        oa = _cross_prompt(qx, mkv.reshape(n_seq, n_mem, 2 * MEM_WIDTH), n_seq, ROW_TILE)
        xp = _ffn(x1, oa, ng, xo_b, wi_b, wo_b, ROW_TILE)
        outs[0].append(_rows_minor(kvc_t))
        outs[1].append(_rows_minor(kvs_t))
        outs[2].append(_rows_minor(kvw_t[:, :, seq - WINDOW:]))
        outs[3].append(h_re.reshape(n_seq, N_SSM_GROUPS, SSM_STATE))
        outs[4].append(h_im.reshape(n_seq, N_SSM_GROUPS, SSM_STATE))
        outs[5].append(mkv.reshape(n_seq, n_mem, 2, MEM_HEADS, MEM_HEAD_DIM))

        q, kv, gates, u = _inproj_sample(xs, ng[0:1], w_sample)
        kvc, kvs, kvw = (kv[:, i * KV_COLS:(i + 1) * KV_COLS] for i in range(3))
        head_group = (jnp.arange(N_HEADS)[:, None] // GQA == jnp.arange(N_KV)[None, :]).astype(F32)
        qm = (q.reshape(n_dec, N_HEADS, 1, HEAD_DIM) * head_group[None, :, :, None]).reshape(n_dec, N_HEADS, LANES)
        gates_h = gates[:, :N_HEADS * N_BRANCH].reshape(n_dec, N_HEADS, N_BRANCH)
        gates_h = jnp.pad(gates_h, ((0, 0), (0, 0), (0, LANES - N_BRANCH)))
        blocks_t = _compress_sample(_rows_minor_view(cache_kv_cmp[l]), pt_flat, n_dec, n_pages, pos, proj_c)
        o_cmp, ids = _cmp_sample(qm, blocks_t, kvc.reshape(n_dec, 1, KV_COLS), pos, proj_c, bc_s, bn_s, past_len)
        ids = ids[:, :N_SEL]
        past_page = jnp.minimum(ids, past_len // SEL_BLOCK - 1) // (PAGE_SIZE // SEL_BLOCK)
        sel_pages = jnp.take_along_axis(jnp.repeat(page_table, N_KV, axis=0), past_page, axis=1)
        pool_sel = cache_kv_sel[l].transpose(0, 2, 3, 4, 1)
        o_nsa, new_buf_t = _sel_win_sample(
            ids.reshape(-1), sel_pages.reshape(-1), pool_sel, _rows_minor_view(cache_kv_win[l]), qm, kvs.reshape(n_dec, 1, KV_COLS),
            kvw.reshape(n_dec, KV_COLS, 1), gates_h, o_cmp, bw_s, bs_s, b0_s, n_pages)
        o_ssm, h_re, h_im = _ssm(u, state_ssm_re[l].reshape(n_dec, N_STATE), state_ssm_im[l].reshape(n_dec, N_STATE),
                                 ssm_w, *ssm_tail, n_dec, 1)
        x1, qx = _merge(o_nsa.reshape(n_dec, NSA_WIDTH), o_ssm, xs, gg, w_out_b, ng, xq_b, 1, n_dec)
        oa = _cross_sample(qx, cache_mem_kv[l].reshape(n_dec, n_mem * 2 * MEM_HEADS, MEM_HEAD_DIM), n_mem)
        xs = _ffn(x1, oa, ng, xo_b, wi_b, wo_b, n_dec)
        outs[6].append(kvc.reshape(n_dec, 1, *kv5))
        outs[7].append(kvs.reshape(n_dec, 1, *kv5))
        outs[8].append(_rows_minor(new_buf_t))
        outs[9].append(h_re.reshape(n_dec, N_SSM_GROUPS, SSM_STATE))
        outs[10].append(h_im.reshape(n_dec, N_SSM_GROUPS, SSM_STATE))

    stacked = [jnp.stack(o, axis=0) for o in outs]
    return (xp.reshape(x_prompt.shape), xs.reshape(x_sample.shape), *stacked)
```

```python
import functools
import math

import numpy as np
import jax
import jax.numpy as jnp
from jax import lax
from jax.experimental import pallas as pl
from jax.experimental.pallas import tpu as pltpu

F32 = jnp.float32
BF16 = jnp.bfloat16
I32 = jnp.int32

D_MODEL = 1024
HEAD_DIM = 64
N_HEADS = 8
N_KV = 2
GQA = N_HEADS // N_KV
NSA_WIDTH = N_HEADS * HEAD_DIM
SSM_WIDTH = 512
KV_COLS = 2 * N_KV * HEAD_DIM
KV_HALF = N_KV * HEAD_DIM
N_BRANCH = 3
CMP_BLOCK = 32
SEL_BLOCK = 64
N_SEL = 16
WINDOW = 512
PAGE_SIZE = 128
SSM_GROUP = 16
N_SSM_GROUPS = 32
SSM_STATE = 64
N_STATE = N_SSM_GROUPS * SSM_STATE
N_BUCKETS = 32
MAX_DISTANCE = 128
MEM_HEADS = 4
MEM_HEAD_DIM = 128
MEM_WIDTH = MEM_HEADS * MEM_HEAD_DIM
EPS = 1e-6
NEG = -1e30
MASKED_BELOW = -1e29
TINY = 1e-30
FORCE = 1e4
SCALE = HEAD_DIM ** -0.5
MEM_SCALE = MEM_HEAD_DIM ** -0.5

LANES = 128
Q_TILE = 256
FAR_TILES = 2
FFN_CHUNKS = 2
VMEM_LIMIT = 56 * 1024 * 1024

AUG_SEL = HEAD_DIM
AUG_ONE = AUG_SEL + 32
GATE_ROWS = 16
VT_ROWS = HEAD_DIM + 16
LOG2E = math.log2(math.e)


def _bucket_starts():
    n = np.arange(0, 4 * MAX_DISTANCE)
    exact = N_BUCKETS // 2
    nf = np.maximum(n, exact).astype(np.float32)
    big = exact + (np.log(nf / exact) / np.float32(math.log(MAX_DISTANCE / exact))
                   * (N_BUCKETS - exact)).astype(np.int32)
    bucket = np.where(n < exact, n, np.minimum(big, N_BUCKETS - 1))
    return [int(np.argmax(bucket >= k)) for k in range(N_BUCKETS)]


_BUCKET_START = _bucket_starts()
assert _BUCKET_START[-1] <= Q_TILE


def _params(*sem):
    return pltpu.CompilerParams(dimension_semantics=sem, vmem_limit_bytes=VMEM_LIMIT)


def _bdot(a, b):
    return jnp.dot(a.astype(BF16), b.astype(BF16), preferred_element_type=F32)


def _bdot_nt(a, b):
    return lax.dot_general(a.astype(BF16), b.astype(BF16), (((1,), (1,)), ((), ())),
                           preferred_element_type=F32)


def _split(a):
    hi = a.astype(BF16)
    return hi, (a - hi.astype(F32)).astype(BF16)


def _dot3(a, b):
    ah, al = _split(a)
    bh, bl = _split(b)
    d = functools.partial(jnp.dot, preferred_element_type=F32)
    return d(ah, bh) + (d(ah, bl) + d(al, bh))


def _dot3_nt(a, b):
    ah, al = _split(a)
    bh, bl = _split(b)
    d = functools.partial(lax.dot_general, dimension_numbers=(((1,), (1,)), ((), ())),
                          preferred_element_type=F32)
    return d(ah, bh) + (d(ah, bl) + d(al, bh))


def _rms(x, g):
    return x * lax.rsqrt(jnp.mean(x * x, axis=-1, keepdims=True) + EPS) * g


def _masked_softmax(s, mask, axis=-1):
    s = jnp.where(mask, s, NEG)
    m = jnp.max(s, axis=axis, keepdims=True)
    e = jnp.where(mask, jnp.exp(s - m), 0.0)
    return e * (1.0 / jnp.maximum(jnp.sum(e, axis=axis, keepdims=True), TINY))


def _bias_of_dist(dist, tbl_ref, h):
    v = jnp.full(dist.shape, tbl_ref[0, h], F32)
    for k in range(1, N_BUCKETS):
        v = jnp.where(dist >= _BUCKET_START[k], tbl_ref[k, h], v)
    return v


def _iota2(shape, dim):
    return lax.broadcasted_iota(I32, shape, dim)


def _rows_minor(a):
    n, _, rows = a.shape
    return a.reshape(n, 2, N_KV, HEAD_DIM, rows).transpose(0, 4, 1, 2, 3)


def _rows_minor_view(a):
    n, rows = a.shape[:2]
    return a.transpose(0, 2, 3, 4, 1).reshape(n, KV_COLS, rows)


def _cmp_block_of_row(row, n_cmp):
    return jnp.where(row < n_cmp // 2, 2 * row, 2 * (row - n_cmp // 2) + 1)


def _prompt_bias_kernel(tbl_ref, bcmp_ref, near_ref, *, n_cmp, nq):
    h = pl.program_id(0)
    last = tbl_ref[N_BUCKETS - 1, h]

    blk = _cmp_block_of_row(_iota2((n_cmp, Q_TILE), 0), n_cmp)
    for qi in range(nq):
        dist = qi * Q_TILE + _iota2((n_cmp, Q_TILE), 1) - (blk * CMP_BLOCK + (CMP_BLOCK - 1))
        bcmp_ref[qi, 0] = jnp.where(dist >= 0, _bias_of_dist(dist, tbl_ref, h), NEG)

    key = _iota2((Q_TILE, Q_TILE), 0)
    qry = _iota2((Q_TILE, Q_TILE), 1)
    d0 = qry - key
    near_ref[0, 0] = jnp.where(d0 >= 0, (_bias_of_dist(d0, tbl_ref, h) - last) * LOG2E, NEG)
    d1 = Q_TILE + qry - key
    near_ref[0, 1] = (_bias_of_dist(d1, tbl_ref, h) - last) * LOG2E
    near_ref[0, 2] = jnp.where(key > qry, 0.0, NEG)


def _prompt_bias(rel_bias, seq):
    nq = seq // Q_TILE
    n_cmp = seq // CMP_BLOCK
    return pl.pallas_call(
        functools.partial(_prompt_bias_kernel, n_cmp=n_cmp, nq=nq),
        grid=(N_HEADS,),
        in_specs=[pl.BlockSpec(memory_space=pltpu.SMEM)],
        out_specs=[pl.BlockSpec((nq, 1, n_cmp, Q_TILE), lambda h: (0, h // GQA, 0, h % GQA)),
                   pl.BlockSpec((1, 3, Q_TILE, Q_TILE), lambda h: (h // GQA, 0, 0, h % GQA))],
        out_shape=[jax.ShapeDtypeStruct((nq, N_KV, n_cmp, GQA * Q_TILE), F32),
                   jax.ShapeDtypeStruct((N_KV, 3, Q_TILE, GQA * Q_TILE), F32)],
        compiler_params=_params("arbitrary"),
    )(rel_bias)


OFF_KV = NSA_WIDTH
OFF_GATE = OFF_KV + 3 * KV_COLS
OFF_U = OFF_GATE + N_HEADS * N_BRANCH


def _k_cols(w_in, which):
    base = OFF_KV + which * KV_COLS
    return [w_in[:, base + g * HEAD_DIM: base + (g + 1) * HEAD_DIM] for g in range(N_KV)]


def _pack_w_prompt(w_in):
    w_row = jnp.concatenate([w_in[:, OFF_KV:OFF_KV + KV_COLS]] + _k_cols(w_in, 1) + _k_cols(w_in, 2)
                            + [w_in[:, OFF_U:]], axis=1)
    gates = w_in[:, OFF_GATE:OFF_U].reshape(-1, N_KV, GQA * N_BRANCH)
    gates = jnp.pad(gates, ((0, 0), (0, 0), (0, GATE_ROWS - GQA * N_BRANCH))).reshape(-1, N_KV * GATE_ROWS)
    w_t = jnp.concatenate([w_in[:, :OFF_GATE], gates], axis=1).T
    return w_row.astype(BF16), w_t.astype(BF16)


def _inproj_prompt_kernel(x_ref, g_ref, wr_ref, wt_ref, kvc_ref, ksa_ref, kwa_ref, u_ref, qt_ref, kvct_ref,
                          kvst_ref, kvwt_ref, vst_ref, vwt_ref, gt_ref, *, tile):
    hb = _rms(x_ref[...], g_ref[...]).astype(BF16)
    z = jnp.dot(hb, wr_ref[...], preferred_element_type=F32)
    kvc_ref[...] = z[:, :KV_COLS]
    u_ref[...] = z[:, KV_COLS + 4 * HEAD_DIM:]

    pos = pl.program_id(1) * tile + _iota2((tile, LANES - HEAD_DIM), 0)
    col = _iota2((tile, LANES - HEAD_DIM), 1)
    ones = ((col >= AUG_ONE - HEAD_DIM) & (col < AUG_ONE - HEAD_DIM + 2)).astype(F32)
    aug_sel = jnp.where(col == pos // SEL_BLOCK, -NEG, ones)
    k0 = KV_COLS
    ksa_ref[...] = jnp.concatenate(
        [z[:, k0:k0 + HEAD_DIM], aug_sel, z[:, k0 + HEAD_DIM:k0 + 2 * HEAD_DIM], aug_sel], axis=1).astype(BF16)
    k0 = KV_COLS + 2 * HEAD_DIM
    kwa_ref[...] = jnp.concatenate(
        [z[:, k0:k0 + HEAD_DIM], ones, z[:, k0 + HEAD_DIM:k0 + 2 * HEAD_DIM], ones], axis=1).astype(BF16)

    zt = lax.dot_general(wt_ref[...], hb, (((1,), (1,)), ((), ())), preferred_element_type=F32)
    qt_ref[0] = zt[:NSA_WIDTH]
    kvct_ref[0] = zt[OFF_KV:OFF_KV + KV_COLS]
    kvst_ref[0] = zt[OFF_KV + KV_COLS:OFF_KV + 2 * KV_COLS]
    kvwt_ref[0] = zt[OFF_KV + 2 * KV_COLS:OFF_GATE]
    gt_ref[0] = jax.nn.sigmoid(zt[OFF_GATE:])
    v_sel = OFF_KV + KV_COLS + KV_HALF
    v_win = OFF_KV + 2 * KV_COLS + KV_HALF
    sum_rows = (_iota2((VT_ROWS - HEAD_DIM, Q_TILE), 0) == 0).astype(F32)
    for j in range(tile // Q_TILE):
        cols = slice(j * Q_TILE, (j + 1) * Q_TILE)
        for ref, v0 in ((vst_ref, v_sel), (vwt_ref, v_win)):
            parts = []
            for g in range(N_KV):
                parts += [zt[v0 + g * HEAD_DIM:v0 + (g + 1) * HEAD_DIM, cols], sum_rows]
            ref[0, j] = jnp.concatenate(parts, axis=0).astype(BF16)


def _inproj_prompt(x, g, w_row, w_t, n_seq, tile):
    rows = x.shape[0]
    seq = rows // n_seq
    assert seq // SEL_BLOCK <= AUG_ONE - AUG_SEL
    nt = seq // tile
    row_map = lambda b, t: (b * nt + t, 0)
    t_map = lambda b, t: (b, 0, t)
    tiles = tile // Q_TILE
    n_gate = N_KV * GATE_ROWS
    out = [
        (pl.BlockSpec((tile, KV_COLS), row_map), jax.ShapeDtypeStruct((rows, KV_COLS), F32)),
        (pl.BlockSpec((tile, N_KV * LANES), row_map), jax.ShapeDtypeStruct((rows, N_KV * LANES), BF16)),
        (pl.BlockSpec((tile, N_KV * LANES), row_map), jax.ShapeDtypeStruct((rows, N_KV * LANES), BF16)),
        (pl.BlockSpec((tile, SSM_WIDTH), lambda b, t: (t, b)), jax.ShapeDtypeStruct((seq, n_seq * SSM_WIDTH), F32)),
        (pl.BlockSpec((1, NSA_WIDTH, tile), t_map), jax.ShapeDtypeStruct((n_seq, NSA_WIDTH, seq), F32)),
        (pl.BlockSpec((1, KV_COLS, tile), t_map), jax.ShapeDtypeStruct((n_seq, KV_COLS, seq), F32)),
        (pl.BlockSpec((1, KV_COLS, tile), t_map), jax.ShapeDtypeStruct((n_seq, KV_COLS, seq), F32)),
        (pl.BlockSpec((1, KV_COLS, tile), t_map), jax.ShapeDtypeStruct((n_seq, KV_COLS, seq), F32)),
        (pl.BlockSpec((1, tiles, N_KV * VT_ROWS, Q_TILE), lambda b, t: (b, t, 0, 0)),
         jax.ShapeDtypeStruct((n_seq, seq // Q_TILE, N_KV * VT_ROWS, Q_TILE), BF16)),
        (pl.BlockSpec((1, tiles, N_KV * VT_ROWS, Q_TILE), lambda b, t: (b, t, 0, 0)),
         jax.ShapeDtypeStruct((n_seq, seq // Q_TILE, N_KV * VT_ROWS, Q_TILE), BF16)),
        (pl.BlockSpec((1, n_gate, tile), t_map), jax.ShapeDtypeStruct((n_seq, n_gate, seq), F32)),
    ]
    return pl.pallas_call(
        functools.partial(_inproj_prompt_kernel, tile=tile),
        grid=(n_seq, nt),
        in_specs=[pl.BlockSpec((tile, D_MODEL), row_map),
                  pl.BlockSpec((1, D_MODEL), lambda b, t: (0, 0)),
                  pl.BlockSpec(w_row.shape, lambda b, t: (0, 0)),
                  pl.BlockSpec(w_t.shape, lambda b, t: (0, 0))],
        out_specs=[o[0] for o in out],
        out_shape=[o[1] for o in out],
        compiler_params=_params("arbitrary", "arbitrary"),
    )(x, g, w_row, w_t)


S_Q = 0
S_KV = S_Q + NSA_WIDTH
S_GATE = S_KV + 3 * KV_COLS
S_U = S_GATE + LANES
S_END = S_U + SSM_WIDTH


def _pack_w_sample(w_in):
    gates = jnp.pad(w_in[:, OFF_GATE:OFF_U], ((0, 0), (0, LANES - N_HEADS * N_BRANCH)))
    return jnp.concatenate([w_in[:, :OFF_GATE], gates, w_in[:, OFF_U:]], axis=1).astype(BF16)


def _inproj_sample_kernel(x_ref, g_ref, w_ref, q_ref, kv_ref, gate_ref, u_ref):
    z = jnp.dot(_rms(x_ref[...], g_ref[...]).astype(BF16), w_ref[...], preferred_element_type=F32)
    q_ref[...] = z[:, S_Q:S_KV]
    kv_ref[...] = z[:, S_KV:S_GATE]
    gate_ref[...] = jax.nn.sigmoid(z[:, S_GATE:S_U])
    u_ref[...] = z[:, S_U:S_END]


def _inproj_sample(x, g, w):
    rows = x.shape[0]
    vm = pl.BlockSpec(memory_space=pltpu.VMEM)
    widths = [NSA_WIDTH, 3 * KV_COLS, LANES, SSM_WIDTH]
    return pl.pallas_call(
        _inproj_sample_kernel,
        in_specs=[vm] * 3, out_specs=[vm] * 4,
        out_shape=[jax.ShapeDtypeStruct((rows, w_), F32) for w_ in widths],
        compiler_params=pltpu.CompilerParams(vmem_limit_bytes=VMEM_LIMIT),
    )(x, g, w)


def _pack_compress(cmp_pos, cmp_proj, group_major):
    pos = jnp.broadcast_to(cmp_pos.transpose(1, 0, 2)[:, :, None, :],
                           (CMP_BLOCK, 2, N_KV, HEAD_DIM)).reshape(CMP_BLOCK, KV_COLS)
    eye = jnp.eye(2 * N_KV, dtype=F32).reshape(2, N_KV, 2, N_KV)
    order = 'cgdGCe' if group_major else 'cgdCGe'
    proj = jnp.einsum('cde,cgCG->' + order, cmp_proj, eye).reshape(KV_COLS, KV_COLS)
    return pos, proj


def _compress_prompt_kernel(kv_ref, pos_ref, proj_ref, projt_ref, kblk_ref, vblkt_ref):
    x = kv_ref[0]
    n_sel = x.shape[0] // SEL_BLOCK
    x = x.reshape(n_sel, SEL_BLOCK // CMP_BLOCK, CMP_BLOCK, KV_COLS)
    pos = pos_ref[...][None]
    sums = jnp.concatenate([jnp.sum(x[:, i] * pos, axis=1) for i in range(SEL_BLOCK // CMP_BLOCK)], axis=0)
    blk = _dot3(sums, proj_ref[...])
    blk_t = _dot3_nt(projt_ref[...], sums)
    for g in range(N_KV):
        kblk_ref[0, g] = blk[:, g * KV_HALF:g * KV_HALF + HEAD_DIM]
        vblkt_ref[0, g] = blk_t[g * KV_HALF + HEAD_DIM:(g + 1) * KV_HALF]


def _compress_prompt(kvc, pos, proj):
    n_seq, seq, _ = kvc.shape
    n_cmp = seq // CMP_BLOCK
    return pl.pallas_call(
        _compress_prompt_kernel,
        grid=(n_seq,),
        in_specs=[pl.BlockSpec((1, seq, KV_COLS), lambda b: (b, 0, 0)),
                  pl.BlockSpec((CMP_BLOCK, KV_COLS), lambda b: (0, 0)),
                  pl.BlockSpec((KV_COLS, KV_COLS), lambda b: (0, 0)),
                  pl.BlockSpec((KV_COLS, KV_COLS), lambda b: (0, 0))],
        out_specs=[pl.BlockSpec((1, N_KV, n_cmp, HEAD_DIM), lambda b: (b, 0, 0, 0)),
                   pl.BlockSpec((1, N_KV, HEAD_DIM, n_cmp), lambda b: (b, 0, 0, 0))],
        out_shape=[jax.ShapeDtypeStruct((n_seq, N_KV, n_cmp, HEAD_DIM), F32),
                   jax.ShapeDtypeStruct((n_seq, N_KV, HEAD_DIM, n_cmp), F32)],
        compiler_params=_params("arbitrary"),
    )(kvc, pos, proj, proj.T)


def _select_blocks_t(pair, q0):
    row = _iota2(pair.shape, 0)
    n_sel = pair.shape[0]
    cur = (q0 + _iota2(pair.shape, 1)) // SEL_BLOCK
    forced = (row == 0) | (row == cur) | (row == cur - 1)
    score = jnp.where(row <= cur, jnp.where(forced, pair + FORCE, pair), -jnp.inf)
    chosen = jnp.zeros(pair.shape, F32)
    for _ in range(N_SEL):
        m = jnp.max(score, axis=0, keepdims=True)
        hit = (score == m) & (m > -jnp.inf)
        first = jnp.min(jnp.where(hit, row, n_sel), axis=0, keepdims=True)
        pick = row == first
        chosen = jnp.where(pick, 1.0, chosen)
        score = jnp.where(pick, -jnp.inf, score)
    return chosen


def _attend(state, k_aug, v_t, q_aug, extras):
    m, acc = state
    s = jnp.dot(k_aug, q_aug, preferred_element_type=F32)
    if any(t is not None or c is not None for t, c in extras):
        rows = []
        for t, c in extras:
            if t is None:
                rows.append(jnp.full((Q_TILE, s.shape[1]), 0.0 if c is None else c, F32))
            else:
                rows.append(t if c is None else t + c)
        s = s + jnp.concatenate(rows, axis=0)
    m_new = jnp.maximum(m, jnp.max(s, axis=0, keepdims=True))
    e = jnp.exp2(s - m_new)
    acc = jnp.exp2(m - m_new) * acc + jnp.dot(v_t, e.astype(BF16), preferred_element_type=F32)
    return m_new, acc


def _attend_result(state):
    acc = state[1]
    return acc[:HEAD_DIM] * (1.0 / jnp.maximum(acc[HEAD_DIM:HEAD_DIM + 1], TINY))


def _nsa_prompt_kernel(tbl_ref, qt_ref, gt_ref, kblk_ref, vblkt_ref, bcmp_ref, near_ref, ks_ref, vst_ref,
                       kw_ref, vwt_ref, o_ref):
    g = pl.program_id(1)
    qi = pl.program_id(2)
    q0 = qi * Q_TILE
    win_tiles = WINDOW // Q_TILE
    v_rows = pl.ds(pl.multiple_of(g * VT_ROWS, VT_ROWS), VT_ROWS)

    width = GQA * Q_TILE
    q = jnp.concatenate([qt_ref[0, r * HEAD_DIM:(r + 1) * HEAD_DIM, :] for r in range(GQA)], axis=1) * SCALE

    k_cmp, v_cmp_t = kblk_ref[0, 0], vblkt_ref[0, 0]
    n_cmp = k_cmp.shape[0]
    bias = bcmp_ref[0, 0]
    p = _masked_softmax(_dot3(k_cmp, q) + bias, bias > MASKED_BELOW, axis=0)
    o_cmp = _bdot(v_cmp_t, p)
    ps = p[:, :Q_TILE]
    for r in range(1, GQA):
        ps = ps + p[:, r * Q_TILE:(r + 1) * Q_TILE]
    chosen = _select_blocks_t(ps[:n_cmp // 2] + ps[n_cmp // 2:], q0)
    not_chosen = (chosen - 1.0).astype(BF16)
    pad_rows = (AUG_ONE - AUG_SEL) - not_chosen.shape[0]
    if pad_rows:
        not_chosen = jnp.concatenate([not_chosen, jnp.zeros((pad_rows, Q_TILE), BF16)], axis=0)

    row = _iota2((LANES - AUG_ONE, Q_TILE), 0)
    consts = []
    for r in range(GQA):
        last = jnp.full((LANES - AUG_ONE, Q_TILE), tbl_ref[N_BUCKETS - 1, g * GQA + r] * LOG2E, F32)
        hi = last.astype(BF16).astype(F32)
        consts.append(jnp.where(row == 0, hi, jnp.where(row == 1, last - hi, 0.0)).astype(BF16))
    q_aug = jnp.concatenate([(q * LOG2E).astype(BF16), jnp.concatenate([not_chosen] * GQA, axis=1),
                             jnp.concatenate(consts, axis=1)], axis=0)
    near0, near1, oldest = near_ref[0, 0], near_ref[0, 1], near_ref[0, 2]
    init = (jnp.full((1, width), NEG, F32), jnp.zeros((VT_ROWS, width), F32))

    def key_rows(ref, tile, n_tiles):
        return ref[0, pl.ds(pl.multiple_of(tile * Q_TILE, Q_TILE), n_tiles * Q_TILE), :]

    def value_cols(ref, tile, n_tiles):
        return jnp.concatenate([ref[0, tile + j, v_rows, :] for j in range(n_tiles)], axis=1)

    def tile_mask(ok):
        return jnp.where(ok, 0.0, NEG)

    extras = [(near1, tile_mask(qi >= 1)), (near0, None)]
    st = _attend(init, key_rows(ks_ref, qi, 2), value_cols(vst_ref, qi, 2), q_aug, extras)
    n_far = jnp.maximum(qi - 1, 0)

    def far_chunk(c, st, masked=False):
        t0 = c * FAR_TILES
        extras = [(None, tile_mask(t0 + j < n_far) if masked else None) for j in range(FAR_TILES)]
        return _attend(st, key_rows(ks_ref, t0 + 1, FAR_TILES), value_cols(vst_ref, t0 + 1, FAR_TILES), q_aug, extras)

    full = n_far // FAR_TILES
    st = lax.fori_loop(0, full, far_chunk, st)
    st = lax.cond(n_far % FAR_TILES != 0, lambda st=st: far_chunk(full, st, masked=True), lambda st=st: st)
    o_sel = _attend_result(st)

    extras = []
    for mt in range(win_tiles):
        table = oldest if mt == 0 else near1 if mt == win_tiles - 1 else None
        extras.append((table, tile_mask(qi + mt >= win_tiles)))
    extras.append((near0, None))
    st = _attend(init, key_rows(kw_ref, qi, win_tiles + 1), value_cols(vwt_ref, qi, win_tiles + 1), q_aug, extras)
    o_win = _attend_result(st)

    def gate(branch):
        return jnp.concatenate([gt_ref[0, r * N_BRANCH + branch:r * N_BRANCH + branch + 1, :]
                                for r in range(GQA)], axis=1)

    o = gate(0) * o_cmp + gate(1) * o_sel + gate(2) * o_win
    o_ref[0] = jnp.concatenate([o[:, r * Q_TILE:(r + 1) * Q_TILE] for r in range(GQA)], axis=0).T


def _nsa_prompt(rel_bias, q_t, gates_t, kblk, vblk_t, bcmp, near, ks_aug, vs_t, kw_aug, vw_t):
    n_seq, _, seq = q_t.shape
    nq = seq // Q_TILE
    assert nq % FAR_TILES == 0
    n_cmp = kblk.shape[2]
    gw = GQA * HEAD_DIM
    wt = WINDOW // Q_TILE
    return pl.pallas_call(
        _nsa_prompt_kernel,
        grid=(n_seq, N_KV, nq),
        in_specs=[pl.BlockSpec(memory_space=pltpu.SMEM),
                  pl.BlockSpec((1, gw, Q_TILE), lambda b, g, qi: (b, g, qi)),
                  pl.BlockSpec((1, GATE_ROWS, Q_TILE), lambda b, g, qi: (b, g, qi)),
                  pl.BlockSpec((1, 1, n_cmp, HEAD_DIM), lambda b, g, qi: (b, g, 0, 0)),
                  pl.BlockSpec((1, 1, HEAD_DIM, n_cmp), lambda b, g, qi: (b, g, 0, 0)),
                  pl.BlockSpec((1, 1, n_cmp, GQA * Q_TILE), lambda b, g, qi: (qi, g, 0, 0)),
                  pl.BlockSpec((1, 3, Q_TILE, GQA * Q_TILE), lambda b, g, qi: (g, 0, 0, 0)),
                  pl.BlockSpec((1, Q_TILE + seq, LANES), lambda b, g, qi: (b, 0, g)),
                  pl.BlockSpec((1, 1 + nq, N_KV * VT_ROWS, Q_TILE), lambda b, g, qi: (b, 0, 0, 0)),
                  pl.BlockSpec((1, WINDOW + seq, LANES), lambda b, g, qi: (b, 0, g)),
                  pl.BlockSpec((1, wt + nq, N_KV * VT_ROWS, Q_TILE), lambda b, g, qi: (b, 0, 0, 0))],
        out_specs=pl.BlockSpec((1, Q_TILE, gw), lambda b, g, qi: (b, qi, g)),
        out_shape=jax.ShapeDtypeStruct((n_seq, seq, NSA_WIDTH), F32),
        compiler_params=_params("arbitrary", "arbitrary", "arbitrary"),
    )(rel_bias, q_t, gates_t, kblk, vblk_t, bcmp, near, ks_aug, vs_t, kw_aug, vw_t)


STATE_CHUNK = 512
SSM_KB = SSM_WIDTH // LANES
GROUPS_PER_KB = N_SSM_GROUPS // SSM_KB


def _ssm_disc_kernel(are_ref, aim_ref, ldt_ref, bre_ref, bim_ref, abr_ref, abi_ref, bbr_ref, bbi_ref):
    a_re, a_im = are_ref[...], aim_ref[...]
    dt = jnp.exp(ldt_ref[...])
    mag = jnp.exp(a_re * dt)
    ab_re = mag * jnp.cos(a_im * dt)
    ab_im = mag * jnp.sin(a_im * dt)
    den = a_re * a_re + a_im * a_im
    co_re = ((ab_re - 1.0) * a_re + ab_im * a_im) / den
    co_im = (ab_im * a_re - (ab_re - 1.0) * a_im) / den
    abr_ref[...] = ab_re
    abi_ref[...] = ab_im
    b_re, b_im = bre_ref[...], bim_ref[...]
    bbr_ref[...] = co_re[:, None, :] * b_re - co_im[:, None, :] * b_im
    bbi_ref[...] = co_re[:, None, :] * b_im + co_im[:, None, :] * b_re


def _block_diag(w):
    a, b = w.shape[1:]
    w = w.reshape(SSM_KB, GROUPS_PER_KB, a, b)
    eye = jnp.eye(GROUPS_PER_KB, dtype=w.dtype)
    return jnp.einsum('kgab,gh->kgahb', w, eye).reshape(SSM_KB, GROUPS_PER_KB * a, GROUPS_PER_KB * b)


def _ssm_weights(a_re, a_im, b_re, b_im, c_re, c_im, log_dt):
    ng, p = a_re.shape
    vm = pl.BlockSpec(memory_space=pltpu.VMEM)
    ab_re, ab_im, bb_re, bb_im = pl.pallas_call(
        _ssm_disc_kernel,
        in_specs=[vm] * 5, out_specs=[vm] * 4,
        out_shape=[jax.ShapeDtypeStruct((ng, p), F32)] * 2
        + [jax.ShapeDtypeStruct((ng, SSM_GROUP, p), F32)] * 2,
    )(a_re, a_im, log_dt.reshape(ng, 1), b_re.transpose(0, 2, 1), b_im.transpose(0, 2, 1))
    return (ab_re.reshape(1, N_STATE), ab_im.reshape(1, N_STATE),
            _block_diag(bb_re).astype(BF16), _block_diag(bb_im).astype(BF16),
            _block_diag(c_re.transpose(0, 2, 1)).astype(BF16),
            _block_diag(c_im.transpose(0, 2, 1)).astype(BF16))


def _ssm_kernel(u_ref, h0r_ref, h0i_ref, abr_ref, abi_ref, bbr_ref, bbi_ref, ccr_ref, cci_ref,
                d_ref, gw_ref, gb_ref, o_ref, hr_ref, hi_ref, sre, sim, *, nb, steps):
    @pl.when(pl.program_id(0) == 0)
    def _():
        hr_ref[...] = h0r_ref[...]
        hi_ref[...] = h0i_ref[...]

    u = u_ref[...]
    ub = u.astype(BF16)
    kw = N_STATE // SSM_KB
    for kb in range(SSM_KB):
        uk = ub[:, kb * LANES:(kb + 1) * LANES]
        sre[:, kb * kw:(kb + 1) * kw] = jnp.dot(uk, bbr_ref[kb], preferred_element_type=F32)
        sim[:, kb * kw:(kb + 1) * kw] = jnp.dot(uk, bbi_ref[kb], preferred_element_type=F32)

    if steps == 1:
        ar, ai = abr_ref[...], abi_ref[...]
        hr, hi = hr_ref[...], hi_ref[...]
        nr = ar * hr - ai * hi + sre[...]
        ni = ar * hi + ai * hr + sim[...]
        sre[...] = nr
        sim[...] = ni
        hr_ref[...] = nr
        hi_ref[...] = ni
    else:
        for cb in range(N_STATE // STATE_CHUNK):
            cols = slice(cb * STATE_CHUNK, (cb + 1) * STATE_CHUNK)
            ar = jnp.broadcast_to(abr_ref[:, cols], (nb, STATE_CHUNK))
            ai = jnp.broadcast_to(abi_ref[:, cols], (nb, STATE_CHUNK))

            def step(t, carry, cols=cols, ar=ar, ai=ai):
                hr, hi = carry
                rows = pl.ds(pl.multiple_of(t * nb, nb), nb)
                nr = ar * hr - ai * hi + sre[rows, cols]
                ni = ar * hi + ai * hr + sim[rows, cols]
                sre[rows, cols] = nr
                sim[rows, cols] = ni
                return nr, ni

            hr, hi = lax.fori_loop(0, steps, step, (hr_ref[:, cols], hi_ref[:, cols]))
            hr_ref[:, cols] = hr
            hi_ref[:, cols] = hi

    ys = []
    for kb in range(SSM_KB):
        cols = slice(kb * kw, (kb + 1) * kw)
        ys.append(_bdot(sre[:, cols], ccr_ref[kb]) - _bdot(sim[:, cols], cci_ref[kb]))
    y = jnp.concatenate(ys, axis=1) + d_ref[...] * u
    g = jax.nn.gelu(y)
    o_ref[...] = g * jax.nn.sigmoid(_bdot(g, gw_ref[...]) + gb_ref[...])


def _ssm(u_rows, h0_re, h0_im, wts, d, glu_w, glu_b, nb, steps_per_call):
    ab_re, ab_im, bb_re, bb_im, cc_re, cc_im = wts
    rows = nb * steps_per_call
    n_calls = u_rows.shape[0] // rows
    const2 = lambda i: (0, 0)
    const3 = lambda i: (0, 0, 0)
    kw = N_STATE // SSM_KB
    return pl.pallas_call(
        functools.partial(_ssm_kernel, nb=nb, steps=steps_per_call),
        grid=(n_calls,),
        in_specs=[pl.BlockSpec((rows, SSM_WIDTH), lambda i: (i, 0)),
                  pl.BlockSpec((nb, N_STATE), const2), pl.BlockSpec((nb, N_STATE), const2),
                  pl.BlockSpec((1, N_STATE), const2), pl.BlockSpec((1, N_STATE), const2),
                  pl.BlockSpec((SSM_KB, LANES, kw), const3), pl.BlockSpec((SSM_KB, LANES, kw), const3),
                  pl.BlockSpec((SSM_KB, kw, LANES), const3), pl.BlockSpec((SSM_KB, kw, LANES), const3),
                  pl.BlockSpec((1, SSM_WIDTH), const2),
                  pl.BlockSpec((SSM_WIDTH, SSM_WIDTH), const2),
                  pl.BlockSpec((1, SSM_WIDTH), const2)],
        out_specs=[pl.BlockSpec((rows, SSM_WIDTH), lambda i: (i, 0)),
                   pl.BlockSpec((nb, N_STATE), const2), pl.BlockSpec((nb, N_STATE), const2)],
        out_shape=[jax.ShapeDtypeStruct(u_rows.shape, F32),
                   jax.ShapeDtypeStruct((nb, N_STATE), F32), jax.ShapeDtypeStruct((nb, N_STATE), F32)],
        scratch_shapes=[pltpu.VMEM((rows, N_STATE), F32), pltpu.VMEM((rows, N_STATE), F32)],
        compiler_params=_params("arbitrary"),
    )(u_rows, h0_re, h0_im, ab_re, ab_im, bb_re, bb_im, cc_re, cc_im, d, glu_w, glu_b)


def _merge_rows(o_nsa, o_ssm, x, gg_ref, wout_ref, ng_ref, xq_ref):
    gg = gg_ref[...]
    a = _rms(o_nsa, gg[:, :NSA_WIDTH])
    b = _rms(o_ssm, gg[:, NSA_WIDTH:])
    m = (jnp.dot(a.astype(BF16), wout_ref[:NSA_WIDTH], preferred_element_type=F32)
         + jnp.dot(b.astype(BF16), wout_ref[NSA_WIDTH:], preferred_element_type=F32))
    x1 = x + _rms(m, ng_ref[1:2])
    return x1, _bdot(_rms(x1, ng_ref[2:3]), xq_ref[...])


def _cross_rows(qx, mkv_ref):
    outs = []
    for h in range(MEM_HEADS):
        cols = slice(h * MEM_HEAD_DIM, (h + 1) * MEM_HEAD_DIM)
        k = mkv_ref[0, :, cols]
        v = mkv_ref[0, :, MEM_WIDTH + h * MEM_HEAD_DIM: MEM_WIDTH + (h + 1) * MEM_HEAD_DIM]
        s = _bdot_nt(qx[:, cols], k) * MEM_SCALE
        e = jnp.exp(s - jnp.max(s, axis=-1, keepdims=True))
        p = e * (1.0 / jnp.sum(e, axis=-1, keepdims=True))
        outs.append(_bdot(p, v))
    return jnp.concatenate(outs, axis=1)


def _ffn_rows(x1, oa, ng_ref, xo_ref, wi_ref, wo_ref, d_ff):
    x2 = x1 + _rms(_bdot(oa, xo_ref[...]), ng_ref[3:4])
    h = _rms(x2, ng_ref[4:5]).astype(BF16)
    chunk = d_ff // FFN_CHUNKS
    y = None
    for c in range(0, d_ff, chunk):
        z1 = jnp.dot(h, wi_ref[:, c:c + chunk], preferred_element_type=F32)
        z2 = jnp.dot(h, wi_ref[:, d_ff + c:d_ff + c + chunk], preferred_element_type=F32)
        part = _bdot(z1 * jax.nn.sigmoid(z1) * z2, wo_ref[c:c + chunk, :])
        y = part if y is None else y + part
    return x2 + _rms(y, ng_ref[5:6])


def _tail_prompt_kernel(onsa_ref, ossm_ref, x_ref, mkv_ref, gg_ref, wout_ref, ng_ref, xq_ref, xo_ref, wi_ref,
                        wo_ref, o_ref, *, d_ff):
    x1, qx = _merge_rows(onsa_ref[...], ossm_ref[...], x_ref[...], gg_ref, wout_ref, ng_ref, xq_ref)
    o_ref[...] = _ffn_rows(x1, _cross_rows(qx, mkv_ref), ng_ref, xo_ref, wi_ref, wo_ref, d_ff)


def _tail_prompt(o_nsa, o_ssm, x, mkv, gg, w_out, ng, xq, xo, wi, wo, n_seq, tile):
    rows = x.shape[0]
    nt = rows // n_seq // tile
    d_ff = wo.shape[0]
    assert d_ff % (FFN_CHUNKS * LANES) == 0
    row_map = lambda b, t: (b * nt + t, 0)
    const = lambda b, t: (0, 0)
    once = dict(pipeline_mode=pl.Buffered(1))
    return pl.pallas_call(
        functools.partial(_tail_prompt_kernel, d_ff=d_ff),
        grid=(n_seq, nt),
        in_specs=[pl.BlockSpec((tile, NSA_WIDTH), row_map),
                  pl.BlockSpec((tile, SSM_WIDTH), lambda b, t: (t, b)),
                  pl.BlockSpec((tile, D_MODEL), row_map),
                  pl.BlockSpec((1,) + mkv.shape[1:], lambda b, t: (b, 0, 0)),
                  pl.BlockSpec((1, D_MODEL), const),
                  pl.BlockSpec((D_MODEL, D_MODEL), const, **once),
                  pl.BlockSpec((6, D_MODEL), const),
                  pl.BlockSpec((D_MODEL, MEM_WIDTH), const, **once),
                  pl.BlockSpec((MEM_WIDTH, D_MODEL), const, **once),
                  pl.BlockSpec((D_MODEL, 2 * d_ff), const, **once),
                  pl.BlockSpec((d_ff, D_MODEL), const, **once)],
        out_specs=pl.BlockSpec((tile, D_MODEL), row_map),
        out_shape=jax.ShapeDtypeStruct((rows, D_MODEL), F32),
        compiler_params=_params("arbitrary", "arbitrary"),
    )(o_nsa, o_ssm, x, mkv, gg, w_out, ng, xq, xo, wi, wo)


def _merge_kernel(onsa_ref, ossm_ref, x_ref, gg_ref, wout_ref, ng_ref, xq_ref, x1_ref, qx_ref):
    x1_ref[...], qx_ref[...] = _merge_rows(onsa_ref[...], ossm_ref[...], x_ref[...], gg_ref, wout_ref, ng_ref,
                                           xq_ref)


def _merge(o_nsa, o_ssm, x, gg, w_out, ng, xq, n_seq, tile):
    rows = x.shape[0]
    nt = rows // n_seq // tile
    row_map = lambda b, t: (b * nt + t, 0)
    const = lambda b, t: (0, 0)
    return pl.pallas_call(
        _merge_kernel,
        grid=(n_seq, nt),
        in_specs=[pl.BlockSpec((tile, NSA_WIDTH), row_map),
                  pl.BlockSpec((tile, SSM_WIDTH), lambda b, t: (t, b)),
                  pl.BlockSpec((tile, D_MODEL), row_map),
                  pl.BlockSpec((1, D_MODEL), const),
                  pl.BlockSpec((D_MODEL, D_MODEL), const),
                  pl.BlockSpec((6, D_MODEL), const),
                  pl.BlockSpec((D_MODEL, MEM_WIDTH), const)],
        out_specs=[pl.BlockSpec((tile, D_MODEL), row_map), pl.BlockSpec((tile, MEM_WIDTH), row_map)],
        out_shape=[jax.ShapeDtypeStruct((rows, D_MODEL), F32),
                   jax.ShapeDtypeStruct((rows, MEM_WIDTH), F32)],
        compiler_params=_params("arbitrary", "arbitrary"),
    )(o_nsa, o_ssm, x, gg, w_out, ng, xq)


def _norm_matmul_kernel(x_ref, g_ref, w_ref, o_ref):
    o_ref[...] = _bdot(_rms(x_ref[...], g_ref[...]), w_ref[...])


def _norm_matmul(x, g, w, tile):
    rows, k = x.shape
    n = w.shape[1]
    return pl.pallas_call(
        _norm_matmul_kernel,
        grid=(rows // tile,),
        in_specs=[pl.BlockSpec((tile, k), lambda i: (i, 0)),
                  pl.BlockSpec((1, k), lambda i: (0, 0)),
                  pl.BlockSpec((k, n), lambda i: (0, 0))],
        out_specs=pl.BlockSpec((tile, n), lambda i: (i, 0)),
        out_shape=jax.ShapeDtypeStruct((rows, n), F32),
        compiler_params=_params("arbitrary"),
    )(x, g, w)


SAMPLE_GROUP = 8


def _cross_sample_kernel(qx_ref, mkv_ref, o_ref, *, n_mem):
    per_tok = 2 * MEM_HEADS
    sub = _iota2((SAMPLE_GROUP, MEM_HEAD_DIM), 0)
    qx = qx_ref[...]
    outs = []
    for i in range(SAMPLE_GROUP):
        s = jnp.zeros((SAMPLE_GROUP, n_mem), F32)
        for h in range(MEM_HEADS):
            qh = jnp.broadcast_to(qx[i:i + 1, h * MEM_HEAD_DIM:(h + 1) * MEM_HEAD_DIM], sub.shape)
            k = mkv_ref[i, pl.ds(h, n_mem, stride=per_tok), :]
            s = s + _bdot_nt(jnp.where(sub == h, qh, 0.0), k)
        s = s * MEM_SCALE
        e = jnp.exp(s - jnp.max(s, axis=-1, keepdims=True))
        p = e * (1.0 / jnp.sum(e, axis=-1, keepdims=True))
        heads = []
        for h in range(MEM_HEADS):
            v = mkv_ref[i, pl.ds(MEM_HEADS + h, n_mem, stride=per_tok), :]
            heads.append(_bdot(p, v)[h:h + 1])
        outs.append(jnp.concatenate(heads, axis=1))
    o_ref[...] = jnp.concatenate(outs, axis=0)


def _cross_sample(qx, mkv, n_mem):
    rows = qx.shape[0]
    return pl.pallas_call(
        functools.partial(_cross_sample_kernel, n_mem=n_mem),
        grid=(rows // SAMPLE_GROUP,),
        in_specs=[pl.BlockSpec((SAMPLE_GROUP, MEM_WIDTH), lambda i: (i, 0)),
                  pl.BlockSpec((SAMPLE_GROUP,) + mkv.shape[1:], lambda i: (i, 0, 0))],
        out_specs=pl.BlockSpec((SAMPLE_GROUP, MEM_WIDTH), lambda i: (i, 0)),
        out_shape=jax.ShapeDtypeStruct((rows, MEM_WIDTH), F32),
        compiler_params=_params("arbitrary"),
    )(qx, mkv)


def _ffn_kernel(x1_ref, oa_ref, ng_ref, xo_ref, wi_ref, wo_ref, o_ref, *, d_ff):
    o_ref[...] = _ffn_rows(x1_ref[...], oa_ref[...], ng_ref, xo_ref, wi_ref, wo_ref, d_ff)


def _ffn(x1, oa, ng, xo, wi, wo, tile):
    rows = x1.shape[0]
    d_ff = wo.shape[0]
    assert d_ff % (FFN_CHUNKS * LANES) == 0
    const = lambda i: (0, 0)
    once = dict(pipeline_mode=pl.Buffered(1))
    return pl.pallas_call(
        functools.partial(_ffn_kernel, d_ff=d_ff),
        grid=(rows // tile,),
        in_specs=[pl.BlockSpec((tile, D_MODEL), lambda i: (i, 0)),
                  pl.BlockSpec((tile, MEM_WIDTH), lambda i: (i, 0)),
                  pl.BlockSpec((6, D_MODEL), const),
                  pl.BlockSpec((MEM_WIDTH, D_MODEL), const, **once),
                  pl.BlockSpec((D_MODEL, 2 * d_ff), const, **once),
                  pl.BlockSpec((d_ff, D_MODEL), const, **once)],
        out_specs=pl.BlockSpec((tile, D_MODEL), lambda i: (i, 0)),
        out_shape=jax.ShapeDtypeStruct((rows, D_MODEL), F32),
        compiler_params=_params("arbitrary"),
    )(x1, oa, ng, xo, wi, wo)


PAGES_PER_STEP = 32
FOLD_PAGES = 8
NEW_TILE = 8
CMP_GROUP = 8


def _bias_rows(dist, tblt_ref):
    v = jnp.broadcast_to(tblt_ref[:, 0:1], dist.shape)
    for k in range(1, N_BUCKETS):
        v = jnp.where(dist >= _BUCKET_START[k], tblt_ref[:, k:k + 1], v)
    return v


def _sample_bias_kernel(tblt_ref, bw_ref, bc_ref, bn_ref, bs_ref, b0_ref, *, past_len):
    dist = (WINDOW - 1) - _iota2(bw_ref.shape, 1)
    bw_ref[...] = jnp.where((dist >= 0) & (dist < WINDOW), _bias_rows(dist, tblt_ref), NEG)
    dist = past_len - (_iota2(bc_ref.shape, 1) * CMP_BLOCK + (CMP_BLOCK - 1))
    bc_ref[...] = jnp.where(dist >= 0, _bias_rows(dist, tblt_ref), NEG)
    lane = _iota2(bn_ref.shape, 1)
    dist = past_len - ((past_len // CMP_BLOCK + lane) * CMP_BLOCK + (CMP_BLOCK - 1))
    bn_ref[...] = jnp.where((dist >= 0) & (lane < SEL_BLOCK // CMP_BLOCK),
                            _bias_rows(jnp.maximum(dist, 0), tblt_ref), NEG)
    page = lax.broadcasted_iota(I32, bs_ref.shape, 0)
    dist = past_len - page * PAGE_SIZE - lax.broadcasted_iota(I32, bs_ref.shape, 2)
    bs_ref[...] = _bias_rows(dist, tblt_ref)
    b0_ref[...] = _bias_rows(jnp.zeros(b0_ref.shape, I32), tblt_ref)


def _sample_bias(rel_bias, past_len):
    vm = pl.BlockSpec(memory_space=pltpu.VMEM)
    shapes = [(N_HEADS, WINDOW), (N_HEADS, past_len // CMP_BLOCK), (N_HEADS, LANES),
              (past_len // PAGE_SIZE, N_HEADS, PAGE_SIZE), (N_HEADS, LANES)]
    return pl.pallas_call(
        functools.partial(_sample_bias_kernel, past_len=past_len),
        in_specs=[vm], out_specs=[vm] * len(shapes),
        out_shape=[jax.ShapeDtypeStruct(s, F32) for s in shapes],
    )(rel_bias.T)


def _compress_sample_kernel(pt_ref, pool_ref, post_ref, fold_ref, projt_ref, o_ref, buf, sem, *, n_steps):
    i = pl.program_id(0)
    slot = i % 2

    def page_copies(step, slot):
        return [pltpu.make_async_copy(pool_ref.at[pt_ref[step * PAGES_PER_STEP + j]], buf.at[slot, j], sem.at[slot])
                for j in range(PAGES_PER_STEP)]

    @pl.when(i == 0)
    def _():
        for c in page_copies(0, 0):
            c.start()

    @pl.when(i + 1 < n_steps)
    def _():
        for c in page_copies(i + 1, 1 - slot):
            c.start()

    for c in page_copies(i, slot):
        c.wait()

    post = post_ref[...]
    sums_t = None
    for c in range(0, PAGES_PER_STEP, FOLD_PAGES):
        weighted = jnp.concatenate([(buf[slot, j] * post).astype(BF16) for j in range(c, c + FOLD_PAGES)], axis=1)
        part = jnp.dot(weighted, fold_ref[c * PAGE_SIZE:(c + FOLD_PAGES) * PAGE_SIZE, :],
                       preferred_element_type=F32)
        sums_t = part if sums_t is None else sums_t + part
    o_ref[0] = _dot3(projt_ref[...], sums_t)


def _compress_sample(pool_t, page_table_flat, n_seq, n_pages, pos, proj):
    steps = n_pages // PAGES_PER_STEP
    blocks_per_page = PAGE_SIZE // CMP_BLOCK
    blocks_per_step = PAGES_PER_STEP * blocks_per_page
    assert blocks_per_step == LANES
    pos_t = jnp.tile(pos.T, (1, blocks_per_page))
    row = np.arange(PAGES_PER_STEP * PAGE_SIZE)
    fold = jnp.asarray(row[:, None] // CMP_BLOCK == np.arange(blocks_per_step)[None, :], dtype=BF16)

    const = lambda i, pt: (0, 0)
    grid_spec = pltpu.PrefetchScalarGridSpec(
        num_scalar_prefetch=1,
        grid=(n_seq * steps,),
        in_specs=[pl.BlockSpec(memory_space=pl.ANY),
                  pl.BlockSpec((KV_COLS, PAGE_SIZE), const),
                  pl.BlockSpec(fold.shape, const),
                  pl.BlockSpec((KV_COLS, KV_COLS), const)],
        out_specs=pl.BlockSpec((1, KV_COLS, blocks_per_step), lambda i, pt: (i // steps, 0, i % steps)),
        scratch_shapes=[pltpu.VMEM((2, PAGES_PER_STEP, KV_COLS, PAGE_SIZE), F32),
                        pltpu.SemaphoreType.DMA((2,))],
    )
    return pl.pallas_call(
        functools.partial(_compress_sample_kernel, n_steps=n_seq * steps),
        grid_spec=grid_spec,
        out_shape=jax.ShapeDtypeStruct((n_seq, KV_COLS, steps * blocks_per_step), F32),
        compiler_params=_params("arbitrary"),
    )(page_table_flat, pool_t, pos_t, fold, proj.T)


def _cmp_sample_kernel(qm_ref, blk_ref, kvc_ref, pos_ref, proj_ref, bc_ref, bn_ref, oc_ref, idx_ref,
                       *, past_len):
    n_cmp = blk_ref.shape[2]
    bias = jnp.concatenate([bc_ref[...]] * CMP_GROUP, axis=0)
    mask = bias > MASKED_BELOW
    bias_n = jnp.concatenate([bn_ref[:, :NEW_TILE]] * CMP_GROUP, axis=0)
    mask_n = bias_n > MASKED_BELOW
    new_blk = _dot3(jnp.concatenate([kvc_ref[i] for i in range(CMP_GROUP)], axis=0) * pos_ref[0:1], proj_ref[...])
    first_row = _iota2((NEW_TILE, KV_COLS), 0) == 0
    new_blks = [jnp.where(first_row, jnp.broadcast_to(new_blk[i:i + 1], first_row.shape), 0.0)
                for i in range(CMP_GROUP)]
    s = jnp.concatenate([_dot3(qm_ref[i], blk_ref[i, :KV_HALF, :]) for i in range(CMP_GROUP)], axis=0)
    s_n = jnp.concatenate([_dot3_nt(qm_ref[i], new_blks[i][:, :KV_HALF]) for i in range(CMP_GROUP)], axis=0)
    s = jnp.where(mask, s * SCALE + bias, NEG)
    s_n = jnp.where(mask_n, s_n * SCALE + bias_n, NEG)
    m = jnp.maximum(jnp.max(s, axis=-1, keepdims=True), jnp.max(s_n, axis=-1, keepdims=True))
    e = jnp.where(mask, jnp.exp(s - m), 0.0)
    e_n = jnp.where(mask_n, jnp.exp(s_n - m), 0.0)
    inv = 1.0 / jnp.maximum(jnp.sum(e, axis=-1, keepdims=True) + jnp.sum(e_n, axis=-1, keepdims=True), TINY)
    p = e * inv
    p_n = e_n * inv
    group_ps = []
    for i in range(CMP_GROUP):
        rows = slice(i * N_HEADS, (i + 1) * N_HEADS)
        oc_ref[i] = _bdot_nt(p[rows], blk_ref[i, KV_HALF:, :]) + _bdot(p_n[rows], new_blks[i][:, KV_HALF:])
        group_ps += [jnp.sum(p[i * N_HEADS + g * GQA:i * N_HEADS + (g + 1) * GQA], axis=0, keepdims=True)
                     for g in range(N_KV)]

    ps = jnp.concatenate(group_ps, axis=0)
    lane = _iota2(ps.shape, 1)
    pair = ps + pltpu.roll(ps, n_cmp - 1, 1)
    cur = past_len // SEL_BLOCK
    j = lane // (SEL_BLOCK // CMP_BLOCK)
    forced = (j == 0) | (j == cur) | (j == cur - 1)
    score = jnp.where(lane % 2 == 0, jnp.where(forced, pair + FORCE, pair), -jnp.inf)
    slot = _iota2((ps.shape[0], LANES), 1)
    ids = jnp.zeros((ps.shape[0], LANES), I32)
    for it in range(N_SEL - 1):
        top = jnp.max(score, axis=-1, keepdims=True)
        first = jnp.min(jnp.where(score == top, lane, n_cmp), axis=-1, keepdims=True)
        ids = jnp.where(slot == it, first // (SEL_BLOCK // CMP_BLOCK), ids)
        score = jnp.where(lane == first, -jnp.inf, score)
    idx_ref[...] = jnp.where(slot == N_SEL - 1, cur, ids)


def _cmp_sample(qm, blocks_t, kvc_new, pos, proj, bc, bn, past_len):
    n_seq, _, n_cmp = blocks_t.shape
    assert past_len // SEL_BLOCK > N_SEL and past_len % SEL_BLOCK == 0
    const = lambda b: (0, 0)
    per_seq = lambda b: (b, 0, 0)
    return pl.pallas_call(
        functools.partial(_cmp_sample_kernel, past_len=past_len),
        grid=(n_seq // CMP_GROUP,),
        in_specs=[pl.BlockSpec((CMP_GROUP, N_HEADS, LANES), per_seq),
                  pl.BlockSpec((CMP_GROUP, KV_COLS, n_cmp), per_seq),
                  pl.BlockSpec((CMP_GROUP, 1, KV_COLS), per_seq),
                  pl.BlockSpec((CMP_BLOCK, KV_COLS), const),
                  pl.BlockSpec((KV_COLS, KV_COLS), const),
                  pl.BlockSpec((N_HEADS, n_cmp), const),
                  pl.BlockSpec((N_HEADS, LANES), const)],
        out_specs=[pl.BlockSpec((CMP_GROUP, N_HEADS, LANES), per_seq),
                   pl.BlockSpec((CMP_GROUP * N_KV, LANES), lambda b: (b, 0))],
        out_shape=[jax.ShapeDtypeStruct((n_seq, N_HEADS, LANES), F32),
                   jax.ShapeDtypeStruct((n_seq * N_KV, LANES), I32)],
        compiler_params=_params("arbitrary"),
    )(qm, blocks_t, kvc_new, pos, proj, bc, bn)


def _sel_win_sample_kernel(idx_ref, pt_ref, pool_ref, buf_ref, qm_ref, kvs_ref, kvw_ref, gate_ref, oc_ref, bw_ref,
                           bs_ref, b0_ref, o_ref, nbuf_ref, pages, sem, *, n_past_blocks, n_seq):
    b = pl.program_id(0)
    slot = b % 2
    bpp = PAGE_SIZE // SEL_BLOCK

    def page_copies(seq, slot):
        return [pltpu.make_async_copy(pool_ref.at[pt_ref[(seq * N_KV + g) * N_SEL + k], :, g],
                                      pages.at[slot, g * N_SEL + k], sem.at[slot])
                for g in range(N_KV) for k in range(N_SEL)]

    @pl.when(b == 0)
    def _():
        for c in page_copies(0, 0):
            c.start()

    @pl.when(b + 1 < n_seq)
    def _():
        for c in page_copies(b + 1, 1 - slot):
            c.start()

    qm = qm_ref[0]

    buf = buf_ref[0]
    shifted = pltpu.roll(buf, WINDOW - 1, 1)
    nbuf = jnp.where(_iota2(buf.shape, 1) == WINDOW - 1, jnp.broadcast_to(kvw_ref[0], buf.shape), shifted)
    nbuf_ref[0] = nbuf
    bias = bw_ref[...]
    p = _masked_softmax(_bdot(qm, nbuf[:KV_HALF]) * SCALE + bias, bias > MASKED_BELOW)
    o_win = _bdot_nt(p, nbuf[KV_HALF:])

    for c in page_copies(b, slot):
        c.wait()
    lane_half = _iota2((GQA, PAGE_SIZE), 1) // SEL_BLOCK
    o_sel = []
    for g in range(N_KV):
        cols = slice(g * HEAD_DIM, (g + 1) * HEAD_DIM)
        rows = slice(g * GQA, (g + 1) * GQA)
        qg = qm[rows, cols]
        ks, vs, biases = [], [], []
        has_new = False
        for k in range(N_SEL):
            i = idx_ref[(b * N_KV + g) * N_SEL + k]
            is_past = i < n_past_blocks
            has_new = jnp.logical_or(has_new, jnp.logical_not(is_past))
            i = jnp.minimum(i, n_past_blocks - 1)
            ks.append(pages[slot, g * N_SEL + k, 0])
            vs.append(pages[slot, g * N_SEL + k, 1])
            tile = bs_ref[i // bpp]
            biases.append(jnp.where(is_past & (lane_half == i % bpp), tile[rows], NEG))
        bias = jnp.concatenate(biases, axis=1)
        mask = bias > MASKED_BELOW
        s = jnp.where(mask, _bdot(qg, jnp.concatenate(ks, axis=1)) * SCALE + bias, NEG)
        k_new = kvs_ref[0][:, cols]
        v_new = kvs_ref[0][:, KV_HALF + g * HEAD_DIM:KV_HALF + (g + 1) * HEAD_DIM]
        s_new = jnp.where(has_new,
                          jnp.sum(qg * k_new, axis=-1, keepdims=True) * SCALE + b0_ref[rows, 0:1], NEG)
        m = jnp.maximum(jnp.max(s, axis=-1, keepdims=True), s_new)
        e = jnp.where(mask, jnp.exp(s - m), 0.0)
        e_new = jnp.where(has_new, jnp.exp(s_new - m), 0.0)
        inv = 1.0 / jnp.maximum(jnp.sum(e, axis=-1, keepdims=True) + e_new, TINY)
        o_sel.append(_bdot_nt(e * inv, jnp.concatenate(vs, axis=1)) + (e_new * inv) * v_new)
    o_sel = jnp.concatenate(o_sel, axis=0)

    first_group = _iota2((N_HEADS, HEAD_DIM), 0) < GQA
    oc = oc_ref[0]
    o_cmp = jnp.where(first_group, oc[:, :HEAD_DIM], oc[:, HEAD_DIM:])
    o_win = jnp.where(first_group, o_win[:, :HEAD_DIM], o_win[:, HEAD_DIM:])
    gate = gate_ref[0]
    o_ref[0] = gate[:, 0:1] * o_cmp + gate[:, 1:2] * o_sel + gate[:, 2:3] * o_win


def _sel_win_sample(idx_flat, pt_flat, pool_t, buf_t, qm, kvs_new, kvw_new_col, gates, o_cmp, bw, bs, b0,
                    n_pages):
    n_seq = buf_t.shape[0]
    bpp = PAGE_SIZE // SEL_BLOCK
    n_past_blocks = n_pages * bpp

    per_seq3 = lambda b, idx, pt: (b, 0, 0)
    const2 = lambda b, idx, pt: (0, 0)
    head_tile = pl.BlockSpec((1, N_HEADS, LANES), per_seq3)
    grid_spec = pltpu.PrefetchScalarGridSpec(
        num_scalar_prefetch=2,
        grid=(n_seq,),
        in_specs=[pl.BlockSpec(memory_space=pl.ANY),
                  pl.BlockSpec((1, KV_COLS, WINDOW), per_seq3), head_tile,
                  pl.BlockSpec((1, 1, KV_COLS), per_seq3), pl.BlockSpec((1, KV_COLS, 1), per_seq3), head_tile,
                  head_tile,
                  pl.BlockSpec((N_HEADS, WINDOW), const2),
                  pl.BlockSpec((n_pages, N_HEADS, PAGE_SIZE), lambda b, idx, pt: (0, 0, 0)),
                  pl.BlockSpec((N_HEADS, LANES), const2)],
        out_specs=[pl.BlockSpec((1, N_HEADS, HEAD_DIM), per_seq3),
                   pl.BlockSpec((1, KV_COLS, WINDOW), per_seq3)],
        scratch_shapes=[pltpu.VMEM((2, N_KV * N_SEL, 2, HEAD_DIM, PAGE_SIZE), F32),
                        pltpu.SemaphoreType.DMA((2,))],
    )
    return pl.pallas_call(
        functools.partial(_sel_win_sample_kernel, n_past_blocks=n_past_blocks, n_seq=n_seq),
        grid_spec=grid_spec,
        out_shape=[jax.ShapeDtypeStruct((n_seq, N_HEADS, HEAD_DIM), F32),
                   jax.ShapeDtypeStruct(buf_t.shape, F32)],
        compiler_params=_params("arbitrary"),
    )(idx_flat, pt_flat, pool_t, buf_t, qm, kvs_new, kvw_new_col, gates, o_cmp, bw, bs, b0)


PROJ_TILE = 512
ROW_TILE = 256
TAIL_TILE = 512
SCAN_STEPS = 128


def kernel(x_prompt, x_sample, cache_kv_cmp, cache_kv_sel, cache_kv_win, state_ssm_re, state_ssm_im,
           cache_mem_kv, page_table, mem_prompt, w_in, w_out, norm_g, grp_norm_g, cmp_pos, cmp_proj,
           rel_bias, ssm_a_re, ssm_a_im, ssm_b_re, ssm_b_im, ssm_c_re, ssm_c_im, ssm_d, ssm_log_dt,
           glu_w, glu_b, mem_norm_g, xq, xkv, xo, ffn_wi, ffn_wo):
    depth = w_in.shape[0]
    n_seq, seq, _ = x_prompt.shape
    n_dec, dec_seq, _ = x_sample.shape
    assert dec_seq == 1, "the sample kernels handle one new token per sequence"
    n_pages = page_table.shape[1]
    past_len = n_pages * PAGE_SIZE
    n_mem = mem_prompt.shape[1]
    assert cache_kv_win.shape[2] == WINDOW and seq >= WINDOW and WINDOW % Q_TILE == 0
    win_tiles = WINDOW // Q_TILE

    bcmp, near = _prompt_bias(rel_bias, seq)
    bw_s, bc_s, bn_s, bs_s, b0_s = _sample_bias(rel_bias, past_len)
    pt_flat = page_table.reshape(-1)
    kv5 = (2, N_KV, HEAD_DIM)

    xp = x_prompt.reshape(n_seq * seq, D_MODEL)
    xs = x_sample.reshape(n_dec, D_MODEL)
    outs = [[] for _ in range(11)]
    for l in range(depth):
        ng = norm_g[l]
        w_row, w_t = _pack_w_prompt(w_in[l])
        w_sample = _pack_w_sample(w_in[l])
        w_out_b, xq_b, xkv_b, xo_b = (w[l].astype(BF16) for w in (w_out, xq, xkv, xo))
        wi_b, wo_b, glu_w_b = ffn_wi[l].astype(BF16), ffn_wo[l].astype(BF16), glu_w[l].astype(BF16)
        gg = grp_norm_g[l][None]
        ssm_w = _ssm_weights(ssm_a_re[l], ssm_a_im[l], ssm_b_re[l], ssm_b_im[l], ssm_c_re[l], ssm_c_im[l],
                             ssm_log_dt[l])
        ssm_tail = (ssm_d[l][None], glu_w_b, glu_b[l][None])
        pos, proj_g = _pack_compress(cmp_pos[l], cmp_proj[l], group_major=True)
        _, proj_c = _pack_compress(cmp_pos[l], cmp_proj[l], group_major=False)

        (kvc, ks_aug, kw_aug, u, q_t, kvc_t, kvs_t, kvw_t, vs_t, vw_t, gates_t) = _inproj_prompt(
            xp, ng[0:1], w_row, w_t, n_seq, PROJ_TILE)
        kblk, vblk_t = _compress_prompt(kvc.reshape(n_seq, seq, KV_COLS), pos, proj_g)
        kw_aug = jnp.pad(kw_aug.reshape(n_seq, seq, N_KV * LANES), ((0, 0), (WINDOW, 0), (0, 0)))
        vw_t = jnp.pad(vw_t, ((0, 0), (win_tiles, 0), (0, 0), (0, 0)))
        ks_aug = jnp.pad(ks_aug.reshape(n_seq, seq, N_KV * LANES), ((0, 0), (Q_TILE, 0), (0, 0)))
        vs_t = jnp.pad(vs_t, ((0, 0), (1, 0), (0, 0), (0, 0)))
        o_nsa = _nsa_prompt(rel_bias, q_t, gates_t, kblk, vblk_t, bcmp, near, ks_aug, vs_t, kw_aug, vw_t)
        zeros = jnp.zeros((n_seq, N_STATE), F32)
        o_ssm, h_re, h_im = _ssm(u.reshape(seq * n_seq, SSM_WIDTH), zeros, zeros, ssm_w, *ssm_tail,
                                 n_seq, SCAN_STEPS)
        mkv = _norm_matmul(mem_prompt.reshape(n_seq * n_mem, D_MODEL), mem_norm_g[l][None], xkv_b, ROW_TILE)
        xp = _tail_prompt(o_nsa.reshape(n_seq * seq, NSA_WIDTH), o_ssm.reshape(seq, n_seq * SSM_WIDTH), xp,
                          mkv.reshape(n_seq, n_mem, 2 * MEM_WIDTH), gg, w_out_b, ng, xq_b, xo_b, wi_b, wo_b,
                          n_seq, TAIL_TILE)
        outs[0].append(_rows_minor(kvc_t))
        outs[1].append(_rows_minor(kvs_t))
        outs[2].append(_rows_minor(kvw_t[:, :, seq - WINDOW:]))
        outs[3].append(h_re.reshape(n_seq, N_SSM_GROUPS, SSM_STATE))
        outs[4].append(h_im.reshape(n_seq, N_SSM_GROUPS, SSM_STATE))
        outs[5].append(mkv.reshape(n_seq, n_mem, 2, MEM_HEADS, MEM_HEAD_DIM))

        q, kv, gates, u = _inproj_sample(xs, ng[0:1], w_sample)
        kvc, kvs, kvw = (kv[:, i * KV_COLS:(i + 1) * KV_COLS] for i in range(3))
        head_group = (jnp.arange(N_HEADS)[:, None] // GQA == jnp.arange(N_KV)[None, :]).astype(F32)
        qm = (q.reshape(n_dec, N_HEADS, 1, HEAD_DIM) * head_group[None, :, :, None]).reshape(n_dec, N_HEADS, LANES)
        gates_h = gates[:, :N_HEADS * N_BRANCH].reshape(n_dec, N_HEADS, N_BRANCH)
        gates_h = jnp.pad(gates_h, ((0, 0), (0, 0), (0, LANES - N_BRANCH)))
        blocks_t = _compress_sample(_rows_minor_view(cache_kv_cmp[l]), pt_flat, n_dec, n_pages, pos, proj_c)
        o_cmp, ids = _cmp_sample(qm, blocks_t, kvc.reshape(n_dec, 1, KV_COLS), pos, proj_c, bc_s, bn_s, past_len)
        ids = ids[:, :N_SEL]
        past_page = jnp.minimum(ids, past_len // SEL_BLOCK - 1) // (PAGE_SIZE // SEL_BLOCK)
        sel_pages = jnp.take_along_axis(jnp.repeat(page_table, N_KV, axis=0), past_page, axis=1)
        pool_sel = cache_kv_sel[l].transpose(0, 2, 3, 4, 1)
        o_nsa, new_buf_t = _sel_win_sample(
            ids.reshape(-1), sel_pages.reshape(-1), pool_sel, _rows_minor_view(cache_kv_win[l]), qm, kvs.reshape(n_dec, 1, KV_COLS),
            kvw.reshape(n_dec, KV_COLS, 1), gates_h, o_cmp, bw_s, bs_s, b0_s, n_pages)
        o_ssm, h_re, h_im = _ssm(u, state_ssm_re[l].reshape(n_dec, N_STATE), state_ssm_im[l].reshape(n_dec, N_STATE),
                                 ssm_w, *ssm_tail, n_dec, 1)
        x1, qx = _merge(o_nsa.reshape(n_dec, NSA_WIDTH), o_ssm, xs, gg, w_out_b, ng, xq_b, 1, n_dec)
        oa = _cross_sample(qx, cache_mem_kv[l].reshape(n_dec, n_mem * 2 * MEM_HEADS, MEM_HEAD_DIM), n_mem)
        xs = _ffn(x1, oa, ng, xo_b, wi_b, wo_b, n_dec)
        outs[6].append(kvc.reshape(n_dec, 1, *kv5))
        outs[7].append(kvs.reshape(n_dec, 1, *kv5))
        outs[8].append(_rows_minor(new_buf_t))
        outs[9].append(h_re.reshape(n_dec, N_SSM_GROUPS, SSM_STATE))
        outs[10].append(h_im.reshape(n_dec, N_SSM_GROUPS, SSM_STATE))

    stacked = [jnp.stack(o, axis=0) for o in outs]
    return (xp.reshape(x_prompt.shape), xs.reshape(x_sample.shape), *stacked)
```

```python
import functools
import math

import numpy as np
import jax
import jax.numpy as jnp
from jax import lax
from jax.experimental import pallas as pl
from jax.experimental.pallas import tpu as pltpu

F32 = jnp.float32
BF16 = jnp.bfloat16
I32 = jnp.int32

D_MODEL = 1024
HEAD_DIM = 64
N_HEADS = 8
N_KV = 2
GQA = N_HEADS // N_KV
NSA_WIDTH = N_HEADS * HEAD_DIM
SSM_WIDTH = 512
KV_COLS = 2 * N_KV * HEAD_DIM
KV_HALF = N_KV * HEAD_DIM
N_BRANCH = 3
CMP_BLOCK = 32
SEL_BLOCK = 64
N_SEL = 16
WINDOW = 512
PAGE_SIZE = 128
SSM_GROUP = 16
N_SSM_GROUPS = 32
SSM_STATE = 64
N_STATE = N_SSM_GROUPS * SSM_STATE
N_BUCKETS = 32
MAX_DISTANCE = 128
MEM_HEADS = 4
MEM_HEAD_DIM = 128
MEM_WIDTH = MEM_HEADS * MEM_HEAD_DIM
EPS = 1e-6
NEG = -1e30
MASKED_BELOW = -1e29
TINY = 1e-30
FORCE = 1e4
SCALE = HEAD_DIM ** -0.5
MEM_SCALE = MEM_HEAD_DIM ** -0.5

LANES = 128
Q_TILE = 256
FAR_TILES = 2
FFN_CHUNKS = 2
VMEM_LIMIT = 56 * 1024 * 1024

AUG_SEL = HEAD_DIM
AUG_ONE = AUG_SEL + 32
GATE_ROWS = 16
VT_ROWS = HEAD_DIM + 16
LOG2E = math.log2(math.e)


def _bucket_starts():
    n = np.arange(0, 4 * MAX_DISTANCE)
    exact = N_BUCKETS // 2
    nf = np.maximum(n, exact).astype(np.float32)
    big = exact + (np.log(nf / exact) / np.float32(math.log(MAX_DISTANCE / exact))
                   * (N_BUCKETS - exact)).astype(np.int32)
    bucket = np.where(n < exact, n, np.minimum(big, N_BUCKETS - 1))
    return [int(np.argmax(bucket >= k)) for k in range(N_BUCKETS)]


_BUCKET_START = _bucket_starts()
assert _BUCKET_START[-1] <= Q_TILE


def _params(*sem):
    return pltpu.CompilerParams(dimension_semantics=sem, vmem_limit_bytes=VMEM_LIMIT)


def _bdot(a, b):
    return jnp.dot(a.astype(BF16), b.astype(BF16), preferred_element_type=F32)


def _bdot_nt(a, b):
    return lax.dot_general(a.astype(BF16), b.astype(BF16), (((1,), (1,)), ((), ())),
                           preferred_element_type=F32)


def _split(a):
    hi = a.astype(BF16)
    return hi, (a - hi.astype(F32)).astype(BF16)


def _dot3(a, b):
    ah, al = _split(a)
    bh, bl = _split(b)
    d = functools.partial(jnp.dot, preferred_element_type=F32)
    return d(ah, bh) + (d(ah, bl) + d(al, bh))


def _dot3_nt(a, b):
    ah, al = _split(a)
    bh, bl = _split(b)
    d = functools.partial(lax.dot_general, dimension_numbers=(((1,), (1,)), ((), ())),
                          preferred_element_type=F32)
    return d(ah, bh) + (d(ah, bl) + d(al, bh))


def _rms(x, g):
    return x * lax.rsqrt(jnp.mean(x * x, axis=-1, keepdims=True) + EPS) * g


def _masked_softmax(s, mask, axis=-1):
    s = jnp.where(mask, s, NEG)
    m = jnp.max(s, axis=axis, keepdims=True)
    e = jnp.where(mask, jnp.exp(s - m), 0.0)
    return e * (1.0 / jnp.maximum(jnp.sum(e, axis=axis, keepdims=True), TINY))


def _bias_of_dist(dist, tbl_ref, h):
    v = jnp.full(dist.shape, tbl_ref[0, h], F32)
    for k in range(1, N_BUCKETS):
        v = jnp.where(dist >= _BUCKET_START[k], tbl_ref[k, h], v)
    return v


def _iota2(shape, dim):
    return lax.broadcasted_iota(I32, shape, dim)


def _rows_minor(a):
    n, _, rows = a.shape
    return a.reshape(n, 2, N_KV, HEAD_DIM, rows).transpose(0, 4, 1, 2, 3)


def _rows_minor_view(a):
    n, rows = a.shape[:2]
    return a.transpose(0, 2, 3, 4, 1).reshape(n, KV_COLS, rows)


def _cmp_block_of_row(row, n_cmp):
    return jnp.where(row < n_cmp // 2, 2 * row, 2 * (row - n_cmp // 2) + 1)


def _prompt_bias_kernel(tbl_ref, bcmp_ref, near_ref, *, n_cmp, nq):
    h = pl.program_id(0)
    last = tbl_ref[N_BUCKETS - 1, h]

    blk = _cmp_block_of_row(_iota2((n_cmp, Q_TILE), 0), n_cmp)
    for qi in range(nq):
        dist = qi * Q_TILE + _iota2((n_cmp, Q_TILE), 1) - (blk * CMP_BLOCK + (CMP_BLOCK - 1))
        bcmp_ref[qi, 0] = jnp.where(dist >= 0, _bias_of_dist(dist, tbl_ref, h), NEG)

    key = _iota2((Q_TILE, Q_TILE), 0)
    qry = _iota2((Q_TILE, Q_TILE), 1)
    d0 = qry - key
    near_ref[0, 0] = jnp.where(d0 >= 0, (_bias_of_dist(d0, tbl_ref, h) - last) * LOG2E, NEG)
    d1 = Q_TILE + qry - key
    near_ref[0, 1] = (_bias_of_dist(d1, tbl_ref, h) - last) * LOG2E
    near_ref[0, 2] = jnp.where(key > qry, 0.0, NEG)


def _prompt_bias(rel_bias, seq):
    nq = seq // Q_TILE
    n_cmp = seq // CMP_BLOCK
    return pl.pallas_call(
        functools.partial(_prompt_bias_kernel, n_cmp=n_cmp, nq=nq),
        grid=(N_HEADS,),
        in_specs=[pl.BlockSpec(memory_space=pltpu.SMEM)],
        out_specs=[pl.BlockSpec((nq, 1, n_cmp, Q_TILE), lambda h: (0, h // GQA, 0, h % GQA)),
                   pl.BlockSpec((1, 3, Q_TILE, Q_TILE), lambda h: (h // GQA, 0, 0, h % GQA))],
        out_shape=[jax.ShapeDtypeStruct((nq, N_KV, n_cmp, GQA * Q_TILE), F32),
                   jax.ShapeDtypeStruct((N_KV, 3, Q_TILE, GQA * Q_TILE), F32)],
        compiler_params=_params("arbitrary"),
    )(rel_bias)


OFF_KV = NSA_WIDTH
OFF_GATE = OFF_KV + 3 * KV_COLS
OFF_U = OFF_GATE + N_HEADS * N_BRANCH


def _k_cols(w_in, which):
    base = OFF_KV + which * KV_COLS
    return [w_in[:, base + g * HEAD_DIM: base + (g + 1) * HEAD_DIM] for g in range(N_KV)]


def _pack_w_prompt(w_in):
    w_row = jnp.concatenate([w_in[:, OFF_KV:OFF_KV + KV_COLS]] + _k_cols(w_in, 1) + _k_cols(w_in, 2)
                            + [w_in[:, OFF_U:]], axis=1)
    gates = w_in[:, OFF_GATE:OFF_U].reshape(-1, N_KV, GQA * N_BRANCH)
    gates = jnp.pad(gates, ((0, 0), (0, 0), (0, GATE_ROWS - GQA * N_BRANCH))).reshape(-1, N_KV * GATE_ROWS)
    w_t = jnp.concatenate([w_in[:, :OFF_GATE], gates], axis=1).T
    return w_row.astype(BF16), w_t.astype(BF16)


def _inproj_prompt_kernel(x_ref, g_ref, wr_ref, wt_ref, kvc_ref, ksa_ref, kwa_ref, u_ref, qt_ref, kvct_ref,
                          kvst_ref, kvwt_ref, vst_ref, vwt_ref, gt_ref, *, tile):
    hb = _rms(x_ref[...], g_ref[...]).astype(BF16)
    z = jnp.dot(hb, wr_ref[...], preferred_element_type=F32)
    kvc_ref[...] = z[:, :KV_COLS]
    u_ref[...] = z[:, KV_COLS + 4 * HEAD_DIM:]

    pos = pl.program_id(1) * tile + _iota2((tile, LANES - HEAD_DIM), 0)
    col = _iota2((tile, LANES - HEAD_DIM), 1)
    ones = ((col >= AUG_ONE - HEAD_DIM) & (col < AUG_ONE - HEAD_DIM + 2)).astype(F32)
    aug_sel = jnp.where(col == pos // SEL_BLOCK, -NEG, ones)
    k0 = KV_COLS
    ksa_ref[...] = jnp.concatenate(
        [z[:, k0:k0 + HEAD_DIM], aug_sel, z[:, k0 + HEAD_DIM:k0 + 2 * HEAD_DIM], aug_sel], axis=1).astype(BF16)
    k0 = KV_COLS + 2 * HEAD_DIM
    kwa_ref[...] = jnp.concatenate(
        [z[:, k0:k0 + HEAD_DIM], ones, z[:, k0 + HEAD_DIM:k0 + 2 * HEAD_DIM], ones], axis=1).astype(BF16)

    zt = lax.dot_general(wt_ref[...], hb, (((1,), (1,)), ((), ())), preferred_element_type=F32)
    qt_ref[0] = zt[:NSA_WIDTH]
    kvct_ref[0] = zt[OFF_KV:OFF_KV + KV_COLS]
    kvst_ref[0] = zt[OFF_KV + KV_COLS:OFF_KV + 2 * KV_COLS]
    kvwt_ref[0] = zt[OFF_KV + 2 * KV_COLS:OFF_GATE]
    gt_ref[0] = jax.nn.sigmoid(zt[OFF_GATE:])
    v_sel = OFF_KV + KV_COLS + KV_HALF
    v_win = OFF_KV + 2 * KV_COLS + KV_HALF
    sum_rows = (_iota2((VT_ROWS - HEAD_DIM, Q_TILE), 0) == 0).astype(F32)
    for j in range(tile // Q_TILE):
        cols = slice(j * Q_TILE, (j + 1) * Q_TILE)
        for ref, v0 in ((vst_ref, v_sel), (vwt_ref, v_win)):
            parts = []
            for g in range(N_KV):
                parts += [zt[v0 + g * HEAD_DIM:v0 + (g + 1) * HEAD_DIM, cols], sum_rows]
            ref[0, j] = jnp.concatenate(parts, axis=0).astype(BF16)


def _inproj_prompt(x, g, w_row, w_t, n_seq, tile):
    rows = x.shape[0]
    seq = rows // n_seq
    assert seq // SEL_BLOCK <= AUG_ONE - AUG_SEL
    nt = seq // tile
    row_map = lambda b, t: (b * nt + t, 0)
    t_map = lambda b, t: (b, 0, t)
    tiles = tile // Q_TILE
    n_gate = N_KV * GATE_ROWS
    out = [
        (pl.BlockSpec((tile, KV_COLS), row_map), jax.ShapeDtypeStruct((rows, KV_COLS), F32)),
        (pl.BlockSpec((tile, N_KV * LANES), row_map), jax.ShapeDtypeStruct((rows, N_KV * LANES), BF16)),
        (pl.BlockSpec((tile, N_KV * LANES), row_map), jax.ShapeDtypeStruct((rows, N_KV * LANES), BF16)),
        (pl.BlockSpec((tile, SSM_WIDTH), lambda b, t: (t, b)), jax.ShapeDtypeStruct((seq, n_seq * SSM_WIDTH), F32)),
        (pl.BlockSpec((1, NSA_WIDTH, tile), t_map), jax.ShapeDtypeStruct((n_seq, NSA_WIDTH, seq), F32)),
        (pl.BlockSpec((1, KV_COLS, tile), t_map), jax.ShapeDtypeStruct((n_seq, KV_COLS, seq), F32)),
        (pl.BlockSpec((1, KV_COLS, tile), t_map), jax.ShapeDtypeStruct((n_seq, KV_COLS, seq), F32)),
        (pl.BlockSpec((1, KV_COLS, tile), t_map), jax.ShapeDtypeStruct((n_seq, KV_COLS, seq), F32)),
        (pl.BlockSpec((1, tiles, N_KV * VT_ROWS, Q_TILE), lambda b, t: (b, t, 0, 0)),
         jax.ShapeDtypeStruct((n_seq, seq // Q_TILE, N_KV * VT_ROWS, Q_TILE), BF16)),
        (pl.BlockSpec((1, tiles, N_KV * VT_ROWS, Q_TILE), lambda b, t: (b, t, 0, 0)),
         jax.ShapeDtypeStruct((n_seq, seq // Q_TILE, N_KV * VT_ROWS, Q_TILE), BF16)),
        (pl.BlockSpec((1, n_gate, tile), t_map), jax.ShapeDtypeStruct((n_seq, n_gate, seq), F32)),
    ]
    return pl.pallas_call(
        functools.partial(_inproj_prompt_kernel, tile=tile),
        grid=(n_seq, nt),
        in_specs=[pl.BlockSpec((tile, D_MODEL), row_map),
                  pl.BlockSpec((1, D_MODEL), lambda b, t: (0, 0)),
                  pl.BlockSpec(w_row.shape, lambda b, t: (0, 0)),
                  pl.BlockSpec(w_t.shape, lambda b, t: (0, 0))],
        out_specs=[o[0] for o in out],
        out_shape=[o[1] for o in out],
        compiler_params=_params("arbitrary", "arbitrary"),
    )(x, g, w_row, w_t)


S_Q = 0
S_KV = S_Q + NSA_WIDTH
S_GATE = S_KV + 3 * KV_COLS
S_U = S_GATE + LANES
S_END = S_U + SSM_WIDTH


def _pack_w_sample(w_in):
    gates = jnp.pad(w_in[:, OFF_GATE:OFF_U], ((0, 0), (0, LANES - N_HEADS * N_BRANCH)))
    return jnp.concatenate([w_in[:, :OFF_GATE], gates, w_in[:, OFF_U:]], axis=1).astype(BF16)


def _inproj_sample_kernel(x_ref, g_ref, w_ref, q_ref, kv_ref, gate_ref, u_ref):
    z = jnp.dot(_rms(x_ref[...], g_ref[...]).astype(BF16), w_ref[...], preferred_element_type=F32)
    q_ref[...] = z[:, S_Q:S_KV]
    kv_ref[...] = z[:, S_KV:S_GATE]
    gate_ref[...] = jax.nn.sigmoid(z[:, S_GATE:S_U])
    u_ref[...] = z[:, S_U:S_END]


def _inproj_sample(x, g, w):
    rows = x.shape[0]
    vm = pl.BlockSpec(memory_space=pltpu.VMEM)
    widths = [NSA_WIDTH, 3 * KV_COLS, LANES, SSM_WIDTH]
    return pl.pallas_call(
        _inproj_sample_kernel,
        in_specs=[vm] * 3, out_specs=[vm] * 4,
        out_shape=[jax.ShapeDtypeStruct((rows, w_), F32) for w_ in widths],
        compiler_params=pltpu.CompilerParams(vmem_limit_bytes=VMEM_LIMIT),
    )(x, g, w)


def _pack_compress(cmp_pos, cmp_proj, group_major):
    pos = jnp.broadcast_to(cmp_pos.transpose(1, 0, 2)[:, :, None, :],
                           (CMP_BLOCK, 2, N_KV, HEAD_DIM)).reshape(CMP_BLOCK, KV_COLS)
    eye = jnp.eye(2 * N_KV, dtype=F32).reshape(2, N_KV, 2, N_KV)
    order = 'cgdGCe' if group_major else 'cgdCGe'
    proj = jnp.einsum('cde,cgCG->' + order, cmp_proj, eye).reshape(KV_COLS, KV_COLS)
    return pos, proj


def _compress_prompt_kernel(kv_ref, pos_ref, proj_ref, projt_ref, kblk_ref, vblkt_ref):
    x = kv_ref[0]
    n_sel = x.shape[0] // SEL_BLOCK
    x = x.reshape(n_sel, SEL_BLOCK // CMP_BLOCK, CMP_BLOCK, KV_COLS)
    pos = pos_ref[...][None]
    sums = jnp.concatenate([jnp.sum(x[:, i] * pos, axis=1) for i in range(SEL_BLOCK // CMP_BLOCK)], axis=0)
    blk = _dot3(sums, proj_ref[...])
    blk_t = _dot3_nt(projt_ref[...], sums)
    for g in range(N_KV):
        kblk_ref[0, g] = blk[:, g * KV_HALF:g * KV_HALF + HEAD_DIM]
        vblkt_ref[0, g] = blk_t[g * KV_HALF + HEAD_DIM:(g + 1) * KV_HALF]


def _compress_prompt(kvc, pos, proj):
    n_seq, seq, _ = kvc.shape
    n_cmp = seq // CMP_BLOCK
    return pl.pallas_call(
        _compress_prompt_kernel,
        grid=(n_seq,),
        in_specs=[pl.BlockSpec((1, seq, KV_COLS), lambda b: (b, 0, 0)),
                  pl.BlockSpec((CMP_BLOCK, KV_COLS), lambda b: (0, 0)),
                  pl.BlockSpec((KV_COLS, KV_COLS), lambda b: (0, 0)),
                  pl.BlockSpec((KV_COLS, KV_COLS), lambda b: (0, 0))],
        out_specs=[pl.BlockSpec((1, N_KV, n_cmp, HEAD_DIM), lambda b: (b, 0, 0, 0)),
                   pl.BlockSpec((1, N_KV, HEAD_DIM, n_cmp), lambda b: (b, 0, 0, 0))],
        out_shape=[jax.ShapeDtypeStruct((n_seq, N_KV, n_cmp, HEAD_DIM), F32),
                   jax.ShapeDtypeStruct((n_seq, N_KV, HEAD_DIM, n_cmp), F32)],
        compiler_params=_params("arbitrary"),
    )(kvc, pos, proj, proj.T)


def _select_blocks_t(pair, q0):
    row = _iota2(pair.shape, 0)
    n_sel = pair.shape[0]
    cur = (q0 + _iota2(pair.shape, 1)) // SEL_BLOCK
    forced = (row == 0) | (row == cur) | (row == cur - 1)
    score = jnp.where(row <= cur, jnp.where(forced, pair + FORCE, pair), -jnp.inf)
    chosen = jnp.zeros(pair.shape, F32)
    for _ in range(N_SEL):
        m = jnp.max(score, axis=0, keepdims=True)
        hit = (score == m) & (m > -jnp.inf)
        first = jnp.min(jnp.where(hit, row, n_sel), axis=0, keepdims=True)
        pick = row == first
        chosen = jnp.where(pick, 1.0, chosen)
        score = jnp.where(pick, -jnp.inf, score)
    return chosen


def _attend(state, k_aug, v_t, q_aug, extras):
    m, acc = state
    s = jnp.dot(k_aug, q_aug, preferred_element_type=F32)
    if any(t is not None or c is not None for t, c in extras):
        rows = []
        for t, c in extras:
            if t is None:
                rows.append(jnp.full((Q_TILE, s.shape[1]), 0.0 if c is None else c, F32))
            else:
                rows.append(t if c is None else t + c)
        s = s + jnp.concatenate(rows, axis=0)
    m_new = jnp.maximum(m, jnp.max(s, axis=0, keepdims=True))
    e = jnp.exp2(s - m_new)
    acc = jnp.exp2(m - m_new) * acc + jnp.dot(v_t, e.astype(BF16), preferred_element_type=F32)
    return m_new, acc


def _attend_result(state):
    acc = state[1]
    return acc[:HEAD_DIM] * (1.0 / jnp.maximum(acc[HEAD_DIM:HEAD_DIM + 1], TINY))


def _nsa_prompt_kernel(tbl_ref, qt_ref, gt_ref, kblk_ref, vblkt_ref, bcmp_ref, near_ref, ks_ref, vst_ref,
                       kw_ref, vwt_ref, o_ref):
    g = pl.program_id(1)
    qi = pl.program_id(2)
    q0 = qi * Q_TILE
    win_tiles = WINDOW // Q_TILE
    v_rows = pl.ds(pl.multiple_of(g * VT_ROWS, VT_ROWS), VT_ROWS)

    width = GQA * Q_TILE
    q = jnp.concatenate([qt_ref[0, r * HEAD_DIM:(r + 1) * HEAD_DIM, :] for r in range(GQA)], axis=1) * SCALE

    k_cmp, v_cmp_t = kblk_ref[0, 0], vblkt_ref[0, 0]
    n_cmp = k_cmp.shape[0]
    bias = bcmp_ref[0, 0]
    p = _masked_softmax(_dot3(k_cmp, q) + bias, bias > MASKED_BELOW, axis=0)
    o_cmp = _bdot(v_cmp_t, p)
    ps = p[:, :Q_TILE]
    for r in range(1, GQA):
        ps = ps + p[:, r * Q_TILE:(r + 1) * Q_TILE]
    chosen = _select_blocks_t(ps[:n_cmp // 2] + ps[n_cmp // 2:], q0)
    not_chosen = (chosen - 1.0).astype(BF16)
    pad_rows = (AUG_ONE - AUG_SEL) - not_chosen.shape[0]
    if pad_rows:
        not_chosen = jnp.concatenate([not_chosen, jnp.zeros((pad_rows, Q_TILE), BF16)], axis=0)

    row = _iota2((LANES - AUG_ONE, Q_TILE), 0)
    consts = []
    for r in range(GQA):
        last = jnp.full((LANES - AUG_ONE, Q_TILE), tbl_ref[N_BUCKETS - 1, g * GQA + r] * LOG2E, F32)
        hi = last.astype(BF16).astype(F32)
        consts.append(jnp.where(row == 0, hi, jnp.where(row == 1, last - hi, 0.0)).astype(BF16))
    q_aug = jnp.concatenate([(q * LOG2E).astype(BF16), jnp.concatenate([not_chosen] * GQA, axis=1),
                             jnp.concatenate(consts, axis=1)], axis=0)
    near0, near1, oldest = near_ref[0, 0], near_ref[0, 1], near_ref[0, 2]
    init = (jnp.full((1, width), NEG, F32), jnp.zeros((VT_ROWS, width), F32))

    def key_rows(ref, tile, n_tiles):
        return ref[0, pl.ds(pl.multiple_of(tile * Q_TILE, Q_TILE), n_tiles * Q_TILE), :]

    def value_cols(ref, tile, n_tiles):
        return jnp.concatenate([ref[0, tile + j, v_rows, :] for j in range(n_tiles)], axis=1)

    def tile_mask(ok):
        return jnp.where(ok, 0.0, NEG)

    extras = [(near1, tile_mask(qi >= 1)), (near0, None)]
    st = _attend(init, key_rows(ks_ref, qi, 2), value_cols(vst_ref, qi, 2), q_aug, extras)
    n_far = jnp.maximum(qi - 1, 0)

    def far_chunk(c, st, masked=False):
        t0 = c * FAR_TILES
        extras = [(None, tile_mask(t0 + j < n_far) if masked else None) for j in range(FAR_TILES)]
        return _attend(st, key_rows(ks_ref, t0 + 1, FAR_TILES), value_cols(vst_ref, t0 + 1, FAR_TILES), q_aug, extras)

    full = n_far // FAR_TILES
    st = lax.fori_loop(0, full, far_chunk, st)
    st = lax.cond(n_far % FAR_TILES != 0, lambda st=st: far_chunk(full, st, masked=True), lambda st=st: st)
    o_sel = _attend_result(st)

    extras = []
    for mt in range(win_tiles):
        table = oldest if mt == 0 else near1 if mt == win_tiles - 1 else None
        extras.append((table, tile_mask(qi + mt >= win_tiles)))
    extras.append((near0, None))
    st = _attend(init, key_rows(kw_ref, qi, win_tiles + 1), value_cols(vwt_ref, qi, win_tiles + 1), q_aug, extras)
    o_win = _attend_result(st)

    def gate(branch):
        return jnp.concatenate([gt_ref[0, r * N_BRANCH + branch:r * N_BRANCH + branch + 1, :]
                                for r in range(GQA)], axis=1)

    o = gate(0) * o_cmp + gate(1) * o_sel + gate(2) * o_win
    o_ref[0] = jnp.concatenate([o[:, r * Q_TILE:(r + 1) * Q_TILE] for r in range(GQA)], axis=0).T


def _nsa_prompt(rel_bias, q_t, gates_t, kblk, vblk_t, bcmp, near, ks_aug, vs_t, kw_aug, vw_t):
    n_seq, _, seq = q_t.shape
    nq = seq // Q_TILE
    assert nq % FAR_TILES == 0
    n_cmp = kblk.shape[2]
    gw = GQA * HEAD_DIM
    wt = WINDOW // Q_TILE
    return pl.pallas_call(
        _nsa_prompt_kernel,
        grid=(n_seq, N_KV, nq),
        in_specs=[pl.BlockSpec(memory_space=pltpu.SMEM),
                  pl.BlockSpec((1, gw, Q_TILE), lambda b, g, qi: (b, g, qi)),
                  pl.BlockSpec((1, GATE_ROWS, Q_TILE), lambda b, g, qi: (b, g, qi)),
                  pl.BlockSpec((1, 1, n_cmp, HEAD_DIM), lambda b, g, qi: (b, g, 0, 0)),
                  pl.BlockSpec((1, 1, HEAD_DIM, n_cmp), lambda b, g, qi: (b, g, 0, 0)),
                  pl.BlockSpec((1, 1, n_cmp, GQA * Q_TILE), lambda b, g, qi: (qi, g, 0, 0)),
                  pl.BlockSpec((1, 3, Q_TILE, GQA * Q_TILE), lambda b, g, qi: (g, 0, 0, 0)),
                  pl.BlockSpec((1, Q_TILE + seq, LANES), lambda b, g, qi: (b, 0, g)),
                  pl.BlockSpec((1, 1 + nq, N_KV * VT_ROWS, Q_TILE), lambda b, g, qi: (b, 0, 0, 0)),
                  pl.BlockSpec((1, WINDOW + seq, LANES), lambda b, g, qi: (b, 0, g)),
                  pl.BlockSpec((1, wt + nq, N_KV * VT_ROWS, Q_TILE), lambda b, g, qi: (b, 0, 0, 0))],
        out_specs=pl.BlockSpec((1, Q_TILE, gw), lambda b, g, qi: (b, qi, g)),
        out_shape=jax.ShapeDtypeStruct((n_seq, seq, NSA_WIDTH), F32),
        compiler_params=_params("arbitrary", "arbitrary", "arbitrary"),
    )(rel_bias, q_t, gates_t, kblk, vblk_t, bcmp, near, ks_aug, vs_t, kw_aug, vw_t)


STATE_CHUNK = 512
SSM_KB = SSM_WIDTH // LANES
GROUPS_PER_KB = N_SSM_GROUPS // SSM_KB


def _ssm_disc_kernel(are_ref, aim_ref, ldt_ref, bre_ref, bim_ref, abr_ref, abi_ref, bbr_ref, bbi_ref):
    a_re, a_im = are_ref[...], aim_ref[...]
    dt = jnp.exp(ldt_ref[...])
    mag = jnp.exp(a_re * dt)
    ab_re = mag * jnp.cos(a_im * dt)
    ab_im = mag * jnp.sin(a_im * dt)
    den = a_re * a_re + a_im * a_im
    co_re = ((ab_re - 1.0) * a_re + ab_im * a_im) / den
    co_im = (ab_im * a_re - (ab_re - 1.0) * a_im) / den
    abr_ref[...] = ab_re
    abi_ref[...] = ab_im
    b_re, b_im = bre_ref[...], bim_ref[...]
    bbr_ref[...] = co_re[:, None, :] * b_re - co_im[:, None, :] * b_im
    bbi_ref[...] = co_re[:, None, :] * b_im + co_im[:, None, :] * b_re


def _block_diag(w):
    a, b = w.shape[1:]
    w = w.reshape(SSM_KB, GROUPS_PER_KB, a, b)
    eye = jnp.eye(GROUPS_PER_KB, dtype=w.dtype)
    return jnp.einsum('kgab,gh->kgahb', w, eye).reshape(SSM_KB, GROUPS_PER_KB * a, GROUPS_PER_KB * b)


def _ssm_weights(a_re, a_im, b_re, b_im, c_re, c_im, log_dt):
    ng, p = a_re.shape
    vm = pl.BlockSpec(memory_space=pltpu.VMEM)
    ab_re, ab_im, bb_re, bb_im = pl.pallas_call(
        _ssm_disc_kernel,
        in_specs=[vm] * 5, out_specs=[vm] * 4,
        out_shape=[jax.ShapeDtypeStruct((ng, p), F32)] * 2
        + [jax.ShapeDtypeStruct((ng, SSM_GROUP, p), F32)] * 2,
    )(a_re, a_im, log_dt.reshape(ng, 1), b_re.transpose(0, 2, 1), b_im.transpose(0, 2, 1))
    return (ab_re.reshape(1, N_STATE), ab_im.reshape(1, N_STATE),
            _block_diag(bb_re).astype(BF16), _block_diag(bb_im).astype(BF16),
            _block_diag(c_re.transpose(0, 2, 1)).astype(BF16),
            _block_diag(c_im.transpose(0, 2, 1)).astype(BF16))


def _ssm_kernel(u_ref, h0r_ref, h0i_ref, abr_ref, abi_ref, bbr_ref, bbi_ref, ccr_ref, cci_ref,
                d_ref, gw_ref, gb_ref, o_ref, hr_ref, hi_ref, sre, sim, *rows_scratch, nb, steps):
    @pl.when(pl.program_id(0) == 0)
    def _():
        hr_ref[...] = h0r_ref[...]
        hi_ref[...] = h0i_ref[...]

    if rows_scratch:
        rows_ref, = rows_scratch
        for b in range(nb):
            for c in range(SSM_KB):
                col = b * SSM_WIDTH + c * LANES
                rows_ref[c, pl.ds(b, steps, stride=nb), :] = u_ref[:, col:col + LANES]
        u = jnp.concatenate([rows_ref[c] for c in range(SSM_KB)], axis=1)
    else:
        u = u_ref[...]
    ub = u.astype(BF16)
    kw = N_STATE // SSM_KB
    for kb in range(SSM_KB):
        uk = ub[:, kb * LANES:(kb + 1) * LANES]
        sre[:, kb * kw:(kb + 1) * kw] = jnp.dot(uk, bbr_ref[kb], preferred_element_type=F32)
        sim[:, kb * kw:(kb + 1) * kw] = jnp.dot(uk, bbi_ref[kb], preferred_element_type=F32)

    if steps == 1:
        ar, ai = abr_ref[...], abi_ref[...]
        hr, hi = hr_ref[...], hi_ref[...]
        nr = ar * hr - ai * hi + sre[...]
        ni = ar * hi + ai * hr + sim[...]
        sre[...] = nr
        sim[...] = ni
        hr_ref[...] = nr
        hi_ref[...] = ni
    else:
        for cb in range(N_STATE // STATE_CHUNK):
            cols = slice(cb * STATE_CHUNK, (cb + 1) * STATE_CHUNK)
            ar = jnp.broadcast_to(abr_ref[:, cols], (nb, STATE_CHUNK))
            ai = jnp.broadcast_to(abi_ref[:, cols], (nb, STATE_CHUNK))

            def step(t, carry, cols=cols, ar=ar, ai=ai):
                hr, hi = carry
                rows = pl.ds(pl.multiple_of(t * nb, nb), nb)
                nr = ar * hr - ai * hi + sre[rows, cols]
                ni = ar * hi + ai * hr + sim[rows, cols]
                sre[rows, cols] = nr
                sim[rows, cols] = ni
                return nr, ni

            hr, hi = lax.fori_loop(0, steps, step, (hr_ref[:, cols], hi_ref[:, cols]))
            hr_ref[:, cols] = hr
            hi_ref[:, cols] = hi

    ys = []
    for kb in range(SSM_KB):
        cols = slice(kb * kw, (kb + 1) * kw)
        ys.append(_bdot(sre[:, cols], ccr_ref[kb]) - _bdot(sim[:, cols], cci_ref[kb]))
    y = jnp.concatenate(ys, axis=1) + d_ref[...] * u
    g = jax.nn.gelu(y)
    out = g * jax.nn.sigmoid(_bdot(g, gw_ref[...]) + gb_ref[...])
    if rows_scratch:
        for c in range(SSM_KB):
            rows_ref[c] = out[:, c * LANES:(c + 1) * LANES]
        for b in range(nb):
            for c in range(SSM_KB):
                col = b * SSM_WIDTH + c * LANES
                o_ref[:, col:col + LANES] = rows_ref[c, pl.ds(b, steps, stride=nb), :]
    else:
        o_ref[...] = out


def _ssm(u, h0_re, h0_im, wts, d, glu_w, glu_b, nb, steps_per_call):
    ab_re, ab_im, bb_re, bb_im, cc_re, cc_im = wts
    rows = nb * steps_per_call
    wide = steps_per_call > 1
    n_calls = u.shape[0] // steps_per_call if wide else 1
    block = (steps_per_call, nb * SSM_WIDTH) if wide else (rows, SSM_WIDTH)
    const2 = lambda i: (0, 0)
    const3 = lambda i: (0, 0, 0)
    kw = N_STATE // SSM_KB
    scratch = [pltpu.VMEM((rows, N_STATE), F32), pltpu.VMEM((rows, N_STATE), F32)]
    if wide:
        scratch.append(pltpu.VMEM((SSM_KB, rows, LANES), F32))
    return pl.pallas_call(
        functools.partial(_ssm_kernel, nb=nb, steps=steps_per_call),
        grid=(n_calls,),
        in_specs=[pl.BlockSpec(block, lambda i: (i, 0)),
                  pl.BlockSpec((nb, N_STATE), const2), pl.BlockSpec((nb, N_STATE), const2),
                  pl.BlockSpec((1, N_STATE), const2), pl.BlockSpec((1, N_STATE), const2),
                  pl.BlockSpec((SSM_KB, LANES, kw), const3), pl.BlockSpec((SSM_KB, LANES, kw), const3),
                  pl.BlockSpec((SSM_KB, kw, LANES), const3), pl.BlockSpec((SSM_KB, kw, LANES), const3),
                  pl.BlockSpec((1, SSM_WIDTH), const2),
                  pl.BlockSpec((SSM_WIDTH, SSM_WIDTH), const2),
                  pl.BlockSpec((1, SSM_WIDTH), const2)],
        out_specs=[pl.BlockSpec(block, lambda i: (i, 0)),
                   pl.BlockSpec((nb, N_STATE), const2), pl.BlockSpec((nb, N_STATE), const2)],
        out_shape=[jax.ShapeDtypeStruct(u.shape, F32),
                   jax.ShapeDtypeStruct((nb, N_STATE), F32), jax.ShapeDtypeStruct((nb, N_STATE), F32)],
        scratch_shapes=scratch,
        compiler_params=_params("arbitrary"),
    )(u, h0_re, h0_im, ab_re, ab_im, bb_re, bb_im, cc_re, cc_im, d, glu_w, glu_b)


def _merge_rows(o_nsa, o_ssm, x, gg_ref, wout_ref, ng_ref, xq_ref):
    gg = gg_ref[...]
    a = _rms(o_nsa, gg[:, :NSA_WIDTH])
    b = _rms(o_ssm, gg[:, NSA_WIDTH:])
    m = (jnp.dot(a.astype(BF16), wout_ref[:NSA_WIDTH], preferred_element_type=F32)
         + jnp.dot(b.astype(BF16), wout_ref[NSA_WIDTH:], preferred_element_type=F32))
    x1 = x + _rms(m, ng_ref[1:2])
    return x1, _bdot(_rms(x1, ng_ref[2:3]), xq_ref[...])


def _cross_rows(qx, mkv_ref):
    outs = []
    for h in range(MEM_HEADS):
        cols = slice(h * MEM_HEAD_DIM, (h + 1) * MEM_HEAD_DIM)
        k = mkv_ref[0, :, cols]
        v = mkv_ref[0, :, MEM_WIDTH + h * MEM_HEAD_DIM: MEM_WIDTH + (h + 1) * MEM_HEAD_DIM]
        s = _bdot_nt(qx[:, cols], k) * MEM_SCALE
        e = jnp.exp(s - jnp.max(s, axis=-1, keepdims=True))
        p = e * (1.0 / jnp.sum(e, axis=-1, keepdims=True))
        outs.append(_bdot(p, v))
    return jnp.concatenate(outs, axis=1)


def _ffn_rows(x1, oa, ng_ref, xo_ref, wi_ref, wo_ref, d_ff):
    x2 = x1 + _rms(_bdot(oa, xo_ref[...]), ng_ref[3:4])
    h = _rms(x2, ng_ref[4:5]).astype(BF16)
    chunk = d_ff // FFN_CHUNKS
    y = None
    for c in range(0, d_ff, chunk):
        z1 = jnp.dot(h, wi_ref[:, c:c + chunk], preferred_element_type=F32)
        z2 = jnp.dot(h, wi_ref[:, d_ff + c:d_ff + c + chunk], preferred_element_type=F32)
        part = _bdot(z1 * jax.nn.sigmoid(z1) * z2, wo_ref[c:c + chunk, :])
        y = part if y is None else y + part
    return x2 + _rms(y, ng_ref[5:6])


def _tail_prompt_kernel(onsa_ref, ossm_ref, x_ref, mkv_ref, gg_ref, wout_ref, ng_ref, xq_ref, xo_ref, wi_ref,
                        wo_ref, o_ref, *, d_ff):
    x1, qx = _merge_rows(onsa_ref[...], ossm_ref[...], x_ref[...], gg_ref, wout_ref, ng_ref, xq_ref)
    o_ref[...] = _ffn_rows(x1, _cross_rows(qx, mkv_ref), ng_ref, xo_ref, wi_ref, wo_ref, d_ff)


def _tail_prompt(o_nsa, o_ssm, x, mkv, gg, w_out, ng, xq, xo, wi, wo, n_seq, tile):
    rows = x.shape[0]
    nt = rows // n_seq // tile
    d_ff = wo.shape[0]
    assert d_ff % (FFN_CHUNKS * LANES) == 0
    row_map = lambda b, t: (b * nt + t, 0)
    const = lambda b, t: (0, 0)
    once = dict(pipeline_mode=pl.Buffered(1))
    return pl.pallas_call(
        functools.partial(_tail_prompt_kernel, d_ff=d_ff),
        grid=(n_seq, nt),
        in_specs=[pl.BlockSpec((tile, NSA_WIDTH), row_map),
                  pl.BlockSpec((tile, SSM_WIDTH), lambda b, t: (t, b)),
                  pl.BlockSpec((tile, D_MODEL), row_map),
                  pl.BlockSpec((1,) + mkv.shape[1:], lambda b, t: (b, 0, 0)),
                  pl.BlockSpec((1, D_MODEL), const),
                  pl.BlockSpec((D_MODEL, D_MODEL), const, **once),
                  pl.BlockSpec((6, D_MODEL), const),
                  pl.BlockSpec((D_MODEL, MEM_WIDTH), const, **once),
                  pl.BlockSpec((MEM_WIDTH, D_MODEL), const, **once),
                  pl.BlockSpec((D_MODEL, 2 * d_ff), const, **once),
                  pl.BlockSpec((d_ff, D_MODEL), const, **once)],
        out_specs=pl.BlockSpec((tile, D_MODEL), row_map),
        out_shape=jax.ShapeDtypeStruct((rows, D_MODEL), F32),
        compiler_params=_params("arbitrary", "arbitrary"),
    )(o_nsa, o_ssm, x, mkv, gg, w_out, ng, xq, xo, wi, wo)


def _merge_kernel(onsa_ref, ossm_ref, x_ref, gg_ref, wout_ref, ng_ref, xq_ref, x1_ref, qx_ref):
    x1_ref[...], qx_ref[...] = _merge_rows(onsa_ref[...], ossm_ref[...], x_ref[...], gg_ref, wout_ref, ng_ref,
                                           xq_ref)


def _merge(o_nsa, o_ssm, x, gg, w_out, ng, xq, n_seq, tile):
    rows = x.shape[0]
    nt = rows // n_seq // tile
    row_map = lambda b, t: (b * nt + t, 0)
    const = lambda b, t: (0, 0)
    return pl.pallas_call(
        _merge_kernel,
        grid=(n_seq, nt),
        in_specs=[pl.BlockSpec((tile, NSA_WIDTH), row_map),
                  pl.BlockSpec((tile, SSM_WIDTH), lambda b, t: (t, b)),
                  pl.BlockSpec((tile, D_MODEL), row_map),
                  pl.BlockSpec((1, D_MODEL), const),
                  pl.BlockSpec((D_MODEL, D_MODEL), const),
                  pl.BlockSpec((6, D_MODEL), const),
                  pl.BlockSpec((D_MODEL, MEM_WIDTH), const)],
        out_specs=[pl.BlockSpec((tile, D_MODEL), row_map), pl.BlockSpec((tile, MEM_WIDTH), row_map)],
        out_shape=[jax.ShapeDtypeStruct((rows, D_MODEL), F32),
                   jax.ShapeDtypeStruct((rows, MEM_WIDTH), F32)],
        compiler_params=_params("arbitrary", "arbitrary"),
    )(o_nsa, o_ssm, x, gg, w_out, ng, xq)


def _norm_matmul_kernel(x_ref, g_ref, w_ref, o_ref):
    o_ref[...] = _bdot(_rms(x_ref[...], g_ref[...]), w_ref[...])


def _norm_matmul(x, g, w, tile):
    rows, k = x.shape
    n = w.shape[1]
    return pl.pallas_call(
        _norm_matmul_kernel,
        grid=(rows // tile,),
        in_specs=[pl.BlockSpec((tile, k), lambda i: (i, 0)),
                  pl.BlockSpec((1, k), lambda i: (0, 0)),
                  pl.BlockSpec((k, n), lambda i: (0, 0))],
        out_specs=pl.BlockSpec((tile, n), lambda i: (i, 0)),
        out_shape=jax.ShapeDtypeStruct((rows, n), F32),
        compiler_params=_params("arbitrary"),
    )(x, g, w)


SAMPLE_GROUP = 8


def _cross_sample_kernel(qx_ref, mkv_ref, o_ref, *, n_mem):
    per_tok = 2 * MEM_HEADS
    sub = _iota2((SAMPLE_GROUP, MEM_HEAD_DIM), 0)
    qx = qx_ref[...]
    outs = []
    for i in range(SAMPLE_GROUP):
        s = jnp.zeros((SAMPLE_GROUP, n_mem), F32)
        for h in range(MEM_HEADS):
            qh = jnp.broadcast_to(qx[i:i + 1, h * MEM_HEAD_DIM:(h + 1) * MEM_HEAD_DIM], sub.shape)
            k = mkv_ref[i, pl.ds(h, n_mem, stride=per_tok), :]
            s = s + _bdot_nt(jnp.where(sub == h, qh, 0.0), k)
        s = s * MEM_SCALE
        e = jnp.exp(s - jnp.max(s, axis=-1, keepdims=True))
        p = e * (1.0 / jnp.sum(e, axis=-1, keepdims=True))
        heads = []
        for h in range(MEM_HEADS):
            v = mkv_ref[i, pl.ds(MEM_HEADS + h, n_mem, stride=per_tok), :]
            heads.append(_bdot(p, v)[h:h + 1])
        outs.append(jnp.concatenate(heads, axis=1))
    o_ref[...] = jnp.concatenate(outs, axis=0)


def _cross_sample(qx, mkv, n_mem):
    rows = qx.shape[0]
    return pl.pallas_call(
        functools.partial(_cross_sample_kernel, n_mem=n_mem),
        grid=(rows // SAMPLE_GROUP,),
        in_specs=[pl.BlockSpec((SAMPLE_GROUP, MEM_WIDTH), lambda i: (i, 0)),
                  pl.BlockSpec((SAMPLE_GROUP,) + mkv.shape[1:], lambda i: (i, 0, 0))],
        out_specs=pl.BlockSpec((SAMPLE_GROUP, MEM_WIDTH), lambda i: (i, 0)),
        out_shape=jax.ShapeDtypeStruct((rows, MEM_WIDTH), F32),
        compiler_params=_params("arbitrary"),
    )(qx, mkv)


def _ffn_kernel(x1_ref, oa_ref, ng_ref, xo_ref, wi_ref, wo_ref, o_ref, *, d_ff):
    o_ref[...] = _ffn_rows(x1_ref[...], oa_ref[...], ng_ref, xo_ref, wi_ref, wo_ref, d_ff)


def _ffn(x1, oa, ng, xo, wi, wo, tile):
    rows = x1.shape[0]
    d_ff = wo.shape[0]
    assert d_ff % (FFN_CHUNKS * LANES) == 0
    const = lambda i: (0, 0)
    once = dict(pipeline_mode=pl.Buffered(1))
    return pl.pallas_call(
        functools.partial(_ffn_kernel, d_ff=d_ff),
        grid=(rows // tile,),
        in_specs=[pl.BlockSpec((tile, D_MODEL), lambda i: (i, 0)),
                  pl.BlockSpec((tile, MEM_WIDTH), lambda i: (i, 0)),
                  pl.BlockSpec((6, D_MODEL), const),
                  pl.BlockSpec((MEM_WIDTH, D_MODEL), const, **once),
                  pl.BlockSpec((D_MODEL, 2 * d_ff), const, **once),
                  pl.BlockSpec((d_ff, D_MODEL), const, **once)],
        out_specs=pl.BlockSpec((tile, D_MODEL), lambda i: (i, 0)),
        out_shape=jax.ShapeDtypeStruct((rows, D_MODEL), F32),
        compiler_params=_params("arbitrary"),
    )(x1, oa, ng, xo, wi, wo)


PAGES_PER_STEP = 32
FOLD_PAGES = 8
NEW_TILE = 8
CMP_GROUP = 8


def _bias_rows(dist, tblt_ref):
    v = jnp.broadcast_to(tblt_ref[:, 0:1], dist.shape)
    for k in range(1, N_BUCKETS):
        v = jnp.where(dist >= _BUCKET_START[k], tblt_ref[:, k:k + 1], v)
    return v


def _sample_bias_kernel(tblt_ref, bw_ref, bc_ref, bn_ref, bs_ref, b0_ref, *, past_len):
    dist = (WINDOW - 1) - _iota2(bw_ref.shape, 1)
    bw_ref[...] = jnp.where((dist >= 0) & (dist < WINDOW), _bias_rows(dist, tblt_ref), NEG)
    dist = past_len - (_iota2(bc_ref.shape, 1) * CMP_BLOCK + (CMP_BLOCK - 1))
    bc_ref[...] = jnp.where(dist >= 0, _bias_rows(dist, tblt_ref), NEG)
    lane = _iota2(bn_ref.shape, 1)
    dist = past_len - ((past_len // CMP_BLOCK + lane) * CMP_BLOCK + (CMP_BLOCK - 1))
    bn_ref[...] = jnp.where((dist >= 0) & (lane < SEL_BLOCK // CMP_BLOCK),
                            _bias_rows(jnp.maximum(dist, 0), tblt_ref), NEG)
    page = lax.broadcasted_iota(I32, bs_ref.shape, 0)
    dist = past_len - page * PAGE_SIZE - lax.broadcasted_iota(I32, bs_ref.shape, 2)
    bs_ref[...] = _bias_rows(dist, tblt_ref)
    b0_ref[...] = _bias_rows(jnp.zeros(b0_ref.shape, I32), tblt_ref)


def _sample_bias(rel_bias, past_len):
    vm = pl.BlockSpec(memory_space=pltpu.VMEM)
    shapes = [(N_HEADS, WINDOW), (N_HEADS, past_len // CMP_BLOCK), (N_HEADS, LANES),
              (past_len // PAGE_SIZE, N_HEADS, PAGE_SIZE), (N_HEADS, LANES)]
    return pl.pallas_call(
        functools.partial(_sample_bias_kernel, past_len=past_len),
        in_specs=[vm], out_specs=[vm] * len(shapes),
        out_shape=[jax.ShapeDtypeStruct(s, F32) for s in shapes],
    )(rel_bias.T)


def _compress_sample_kernel(pt_ref, pool_ref, post_ref, fold_ref, projt_ref, o_ref, buf, sem, *, n_steps):
    i = pl.program_id(0)
    slot = i % 2

    def page_copies(step, slot):
        return [pltpu.make_async_copy(pool_ref.at[pt_ref[step * PAGES_PER_STEP + j]], buf.at[slot, j], sem.at[slot])
                for j in range(PAGES_PER_STEP)]

    @pl.when(i == 0)
    def _():
        for c in page_copies(0, 0):
            c.start()

    @pl.when(i + 1 < n_steps)
    def _():
        for c in page_copies(i + 1, 1 - slot):
            c.start()

    for c in page_copies(i, slot):
        c.wait()

    post = post_ref[...]
    sums_t = None
    for c in range(0, PAGES_PER_STEP, FOLD_PAGES):
        weighted = jnp.concatenate([(buf[slot, j] * post).astype(BF16) for j in range(c, c + FOLD_PAGES)], axis=1)
        part = jnp.dot(weighted, fold_ref[c * PAGE_SIZE:(c + FOLD_PAGES) * PAGE_SIZE, :],
                       preferred_element_type=F32)
        sums_t = part if sums_t is None else sums_t + part
    o_ref[0] = _dot3(projt_ref[...], sums_t)


def _compress_sample(pool_t, page_table_flat, n_seq, n_pages, pos, proj):
    steps = n_pages // PAGES_PER_STEP
    blocks_per_page = PAGE_SIZE // CMP_BLOCK
    blocks_per_step = PAGES_PER_STEP * blocks_per_page
    assert blocks_per_step == LANES
    pos_t = jnp.tile(pos.T, (1, blocks_per_page))
    row = np.arange(PAGES_PER_STEP * PAGE_SIZE)
    fold = jnp.asarray(row[:, None] // CMP_BLOCK == np.arange(blocks_per_step)[None, :], dtype=BF16)

    const = lambda i, pt: (0, 0)
    grid_spec = pltpu.PrefetchScalarGridSpec(
        num_scalar_prefetch=1,
        grid=(n_seq * steps,),
        in_specs=[pl.BlockSpec(memory_space=pl.ANY),
                  pl.BlockSpec((KV_COLS, PAGE_SIZE), const),
                  pl.BlockSpec(fold.shape, const),
                  pl.BlockSpec((KV_COLS, KV_COLS), const)],
        out_specs=pl.BlockSpec((1, KV_COLS, blocks_per_step), lambda i, pt: (i // steps, 0, i % steps)),
        scratch_shapes=[pltpu.VMEM((2, PAGES_PER_STEP, KV_COLS, PAGE_SIZE), F32),
                        pltpu.SemaphoreType.DMA((2,))],
    )
    return pl.pallas_call(
        functools.partial(_compress_sample_kernel, n_steps=n_seq * steps),
        grid_spec=grid_spec,
        out_shape=jax.ShapeDtypeStruct((n_seq, KV_COLS, steps * blocks_per_step), F32),
        compiler_params=_params("arbitrary"),
    )(page_table_flat, pool_t, pos_t, fold, proj.T)


def _cmp_sample_kernel(qm_ref, blk_ref, kvc_ref, pos_ref, proj_ref, bc_ref, bn_ref, oc_ref, idx_ref,
                       *, past_len):
    n_cmp = blk_ref.shape[2]
    bias = jnp.concatenate([bc_ref[...]] * CMP_GROUP, axis=0)
    mask = bias > MASKED_BELOW
    bias_n = jnp.concatenate([bn_ref[:, :NEW_TILE]] * CMP_GROUP, axis=0)
    mask_n = bias_n > MASKED_BELOW
    new_blk = _dot3(jnp.concatenate([kvc_ref[i] for i in range(CMP_GROUP)], axis=0) * pos_ref[0:1], proj_ref[...])
    first_row = _iota2((NEW_TILE, KV_COLS), 0) == 0
    new_blks = [jnp.where(first_row, jnp.broadcast_to(new_blk[i:i + 1], first_row.shape), 0.0)
                for i in range(CMP_GROUP)]
    s = jnp.concatenate([_dot3(qm_ref[i], blk_ref[i, :KV_HALF, :]) for i in range(CMP_GROUP)], axis=0)
    s_n = jnp.concatenate([_dot3_nt(qm_ref[i], new_blks[i][:, :KV_HALF]) for i in range(CMP_GROUP)], axis=0)
    s = jnp.where(mask, s * SCALE + bias, NEG)
    s_n = jnp.where(mask_n, s_n * SCALE + bias_n, NEG)
    m = jnp.maximum(jnp.max(s, axis=-1, keepdims=True), jnp.max(s_n, axis=-1, keepdims=True))
    e = jnp.where(mask, jnp.exp(s - m), 0.0)
    e_n = jnp.where(mask_n, jnp.exp(s_n - m), 0.0)
    inv = 1.0 / jnp.maximum(jnp.sum(e, axis=-1, keepdims=True) + jnp.sum(e_n, axis=-1, keepdims=True), TINY)
    p = e * inv
    p_n = e_n * inv
    group_ps = []
    for i in range(CMP_GROUP):
        rows = slice(i * N_HEADS, (i + 1) * N_HEADS)
        oc_ref[i] = _bdot_nt(p[rows], blk_ref[i, KV_HALF:, :]) + _bdot(p_n[rows], new_blks[i][:, KV_HALF:])
        group_ps += [jnp.sum(p[i * N_HEADS + g * GQA:i * N_HEADS + (g + 1) * GQA], axis=0, keepdims=True)
                     for g in range(N_KV)]

    ps = jnp.concatenate(group_ps, axis=0)
    lane = _iota2(ps.shape, 1)
    pair = ps + pltpu.roll(ps, n_cmp - 1, 1)
    cur = past_len // SEL_BLOCK
    j = lane // (SEL_BLOCK // CMP_BLOCK)
    forced = (j == 0) | (j == cur) | (j == cur - 1)
    score = jnp.where(lane % 2 == 0, jnp.where(forced, pair + FORCE, pair), -jnp.inf)
    slot = _iota2((ps.shape[0], LANES), 1)
    ids = jnp.zeros((ps.shape[0], LANES), I32)
    for it in range(N_SEL - 1):
        top = jnp.max(score, axis=-1, keepdims=True)
        first = jnp.min(jnp.where(score == top, lane, n_cmp), axis=-1, keepdims=True)
        ids = jnp.where(slot == it, first // (SEL_BLOCK // CMP_BLOCK), ids)
        score = jnp.where(lane == first, -jnp.inf, score)
    idx_ref[...] = jnp.where(slot == N_SEL - 1, cur, ids)


def _cmp_sample(qm, blocks_t, kvc_new, pos, proj, bc, bn, past_len):
    n_seq, _, n_cmp = blocks_t.shape
    assert past_len // SEL_BLOCK > N_SEL and past_len % SEL_BLOCK == 0
    const = lambda b: (0, 0)
    per_seq = lambda b: (b, 0, 0)
    return pl.pallas_call(
        functools.partial(_cmp_sample_kernel, past_len=past_len),
        grid=(n_seq // CMP_GROUP,),
        in_specs=[pl.BlockSpec((CMP_GROUP, N_HEADS, LANES), per_seq),
                  pl.BlockSpec((CMP_GROUP, KV_COLS, n_cmp), per_seq),
                  pl.BlockSpec((CMP_GROUP, 1, KV_COLS), per_seq),
                  pl.BlockSpec((CMP_BLOCK, KV_COLS), const),
                  pl.BlockSpec((KV_COLS, KV_COLS), const),
                  pl.BlockSpec((N_HEADS, n_cmp), const),
                  pl.BlockSpec((N_HEADS, LANES), const)],
        out_specs=[pl.BlockSpec((CMP_GROUP, N_HEADS, LANES), per_seq),
                   pl.BlockSpec((CMP_GROUP * N_KV, LANES), lambda b: (b, 0))],
        out_shape=[jax.ShapeDtypeStruct((n_seq, N_HEADS, LANES), F32),
                   jax.ShapeDtypeStruct((n_seq * N_KV, LANES), I32)],
        compiler_params=_params("arbitrary"),
    )(qm, blocks_t, kvc_new, pos, proj, bc, bn)


def _sel_win_sample_kernel(idx_ref, pt_ref, pool_ref, buf_ref, qm_ref, kvs_ref, kvw_ref, gate_ref, oc_ref, bw_ref,
                           bs_ref, b0_ref, o_ref, nbuf_ref, pages, sem, *, n_past_blocks, n_seq):
    b = pl.program_id(0)
    slot = b % 2
    bpp = PAGE_SIZE // SEL_BLOCK

    def page_copies(seq, slot):
        return [pltpu.make_async_copy(pool_ref.at[pt_ref[(seq * N_KV + g) * N_SEL + k], :, g],
                                      pages.at[slot, g * N_SEL + k], sem.at[slot])
                for g in range(N_KV) for k in range(N_SEL)]

    @pl.when(b == 0)
    def _():
        for c in page_copies(0, 0):
            c.start()

    @pl.when(b + 1 < n_seq)
    def _():
        for c in page_copies(b + 1, 1 - slot):
            c.start()

    qm = qm_ref[0]

    buf = buf_ref[0]
    shifted = pltpu.roll(buf, WINDOW - 1, 1)
    nbuf = jnp.where(_iota2(buf.shape, 1) == WINDOW - 1, jnp.broadcast_to(kvw_ref[0], buf.shape), shifted)
    nbuf_ref[0] = nbuf
    bias = bw_ref[...]
    p = _masked_softmax(_bdot(qm, nbuf[:KV_HALF]) * SCALE + bias, bias > MASKED_BELOW)
    o_win = _bdot_nt(p, nbuf[KV_HALF:])

    for c in page_copies(b, slot):
        c.wait()
    lane_half = _iota2((GQA, PAGE_SIZE), 1) // SEL_BLOCK
    o_sel = []
    for g in range(N_KV):
        cols = slice(g * HEAD_DIM, (g + 1) * HEAD_DIM)
        rows = slice(g * GQA, (g + 1) * GQA)
        qg = qm[rows, cols]
        ks, vs, biases = [], [], []
        has_new = False
        for k in range(N_SEL):
            i = idx_ref[(b * N_KV + g) * N_SEL + k]
            is_past = i < n_past_blocks
            has_new = jnp.logical_or(has_new, jnp.logical_not(is_past))
            i = jnp.minimum(i, n_past_blocks - 1)
            ks.append(pages[slot, g * N_SEL + k, 0])
            vs.append(pages[slot, g * N_SEL + k, 1])
            tile = bs_ref[i // bpp]
            biases.append(jnp.where(is_past & (lane_half == i % bpp), tile[rows], NEG))
        bias = jnp.concatenate(biases, axis=1)
        mask = bias > MASKED_BELOW
        s = jnp.where(mask, _bdot(qg, jnp.concatenate(ks, axis=1)) * SCALE + bias, NEG)
        k_new = kvs_ref[0][:, cols]
        v_new = kvs_ref[0][:, KV_HALF + g * HEAD_DIM:KV_HALF + (g + 1) * HEAD_DIM]
        s_new = jnp.where(has_new,
                          jnp.sum(qg * k_new, axis=-1, keepdims=True) * SCALE + b0_ref[rows, 0:1], NEG)
        m = jnp.maximum(jnp.max(s, axis=-1, keepdims=True), s_new)
        e = jnp.where(mask, jnp.exp(s - m), 0.0)
        e_new = jnp.where(has_new, jnp.exp(s_new - m), 0.0)
        inv = 1.0 / jnp.maximum(jnp.sum(e, axis=-1, keepdims=True) + e_new, TINY)
        o_sel.append(_bdot_nt(e * inv, jnp.concatenate(vs, axis=1)) + (e_new * inv) * v_new)
    o_sel = jnp.concatenate(o_sel, axis=0)

    first_group = _iota2((N_HEADS, HEAD_DIM), 0) < GQA
    oc = oc_ref[0]
    o_cmp = jnp.where(first_group, oc[:, :HEAD_DIM], oc[:, HEAD_DIM:])
    o_win = jnp.where(first_group, o_win[:, :HEAD_DIM], o_win[:, HEAD_DIM:])
    gate = gate_ref[0]
    o_ref[0] = gate[:, 0:1] * o_cmp + gate[:, 1:2] * o_sel + gate[:, 2:3] * o_win


def _sel_win_sample(idx_flat, pt_flat, pool_t, buf_t, qm, kvs_new, kvw_new_col, gates, o_cmp, bw, bs, b0,
                    n_pages):
    n_seq = buf_t.shape[0]
    bpp = PAGE_SIZE // SEL_BLOCK
    n_past_blocks = n_pages * bpp

    per_seq3 = lambda b, idx, pt: (b, 0, 0)
    const2 = lambda b, idx, pt: (0, 0)
    head_tile = pl.BlockSpec((1, N_HEADS, LANES), per_seq3)
    grid_spec = pltpu.PrefetchScalarGridSpec(
        num_scalar_prefetch=2,
        grid=(n_seq,),
        in_specs=[pl.BlockSpec(memory_space=pl.ANY),
                  pl.BlockSpec((1, KV_COLS, WINDOW), per_seq3), head_tile,
                  pl.BlockSpec((1, 1, KV_COLS), per_seq3), pl.BlockSpec((1, KV_COLS, 1), per_seq3), head_tile,
                  head_tile,
                  pl.BlockSpec((N_HEADS, WINDOW), const2),
                  pl.BlockSpec((n_pages, N_HEADS, PAGE_SIZE), lambda b, idx, pt: (0, 0, 0)),
                  pl.BlockSpec((N_HEADS, LANES), const2)],
        out_specs=[pl.BlockSpec((1, N_HEADS, HEAD_DIM), per_seq3),
                   pl.BlockSpec((1, KV_COLS, WINDOW), per_seq3)],
        scratch_shapes=[pltpu.VMEM((2, N_KV * N_SEL, 2, HEAD_DIM, PAGE_SIZE), F32),
                        pltpu.SemaphoreType.DMA((2,))],
    )
    return pl.pallas_call(
        functools.partial(_sel_win_sample_kernel, n_past_blocks=n_past_blocks, n_seq=n_seq),
        grid_spec=grid_spec,
        out_shape=[jax.ShapeDtypeStruct((n_seq, N_HEADS, HEAD_DIM), F32),
                   jax.ShapeDtypeStruct(buf_t.shape, F32)],
        compiler_params=_params("arbitrary"),
    )(idx_flat, pt_flat, pool_t, buf_t, qm, kvs_new, kvw_new_col, gates, o_cmp, bw, bs, b0)


PROJ_TILE = 512
ROW_TILE = 256
TAIL_TILE = 512
SCAN_STEPS = 128


def kernel(x_prompt, x_sample, cache_kv_cmp, cache_kv_sel, cache_kv_win, state_ssm_re, state_ssm_im,
           cache_mem_kv, page_table, mem_prompt, w_in, w_out, norm_g, grp_norm_g, cmp_pos, cmp_proj,
           rel_bias, ssm_a_re, ssm_a_im, ssm_b_re, ssm_b_im, ssm_c_re, ssm_c_im, ssm_d, ssm_log_dt,
           glu_w, glu_b, mem_norm_g, xq, xkv, xo, ffn_wi, ffn_wo):
    depth = w_in.shape[0]
    n_seq, seq, _ = x_prompt.shape
    n_dec, dec_seq, _ = x_sample.shape
    assert dec_seq == 1, "the sample kernels handle one new token per sequence"
    n_pages = page_table.shape[1]
    past_len = n_pages * PAGE_SIZE
    n_mem = mem_prompt.shape[1]
    assert cache_kv_win.shape[2] == WINDOW and seq >= WINDOW and WINDOW % Q_TILE == 0
    win_tiles = WINDOW // Q_TILE

    bcmp, near = _prompt_bias(rel_bias, seq)
    bw_s, bc_s, bn_s, bs_s, b0_s = _sample_bias(rel_bias, past_len)
    pt_flat = page_table.reshape(-1)
    kv5 = (2, N_KV, HEAD_DIM)

    xp = x_prompt.reshape(n_seq * seq, D_MODEL)
    xs = x_sample.reshape(n_dec, D_MODEL)
    outs = [[] for _ in range(11)]
    for l in range(depth):
        ng = norm_g[l]
        w_row, w_t = _pack_w_prompt(w_in[l])
        w_sample = _pack_w_sample(w_in[l])
        w_out_b, xq_b, xkv_b, xo_b = (w[l].astype(BF16) for w in (w_out, xq, xkv, xo))
        wi_b, wo_b, glu_w_b = ffn_wi[l].astype(BF16), ffn_wo[l].astype(BF16), glu_w[l].astype(BF16)
        gg = grp_norm_g[l][None]
        ssm_w = _ssm_weights(ssm_a_re[l], ssm_a_im[l], ssm_b_re[l], ssm_b_im[l], ssm_c_re[l], ssm_c_im[l],
                             ssm_log_dt[l])
        ssm_tail = (ssm_d[l][None], glu_w_b, glu_b[l][None])
        pos, proj_g = _pack_compress(cmp_pos[l], cmp_proj[l], group_major=True)
        _, proj_c = _pack_compress(cmp_pos[l], cmp_proj[l], group_major=False)

        (kvc, ks_aug, kw_aug, u, q_t, kvc_t, kvs_t, kvw_t, vs_t, vw_t, gates_t) = _inproj_prompt(
            xp, ng[0:1], w_row, w_t, n_seq, PROJ_TILE)
        kblk, vblk_t = _compress_prompt(kvc.reshape(n_seq, seq, KV_COLS), pos, proj_g)
        kw_aug = jnp.pad(kw_aug.reshape(n_seq, seq, N_KV * LANES), ((0, 0), (WINDOW, 0), (0, 0)))
        vw_t = jnp.pad(vw_t, ((0, 0), (win_tiles, 0), (0, 0), (0, 0)))
        ks_aug = jnp.pad(ks_aug.reshape(n_seq, seq, N_KV * LANES), ((0, 0), (Q_TILE, 0), (0, 0)))
        vs_t = jnp.pad(vs_t, ((0, 0), (1, 0), (0, 0), (0, 0)))
        o_nsa = _nsa_prompt(rel_bias, q_t, gates_t, kblk, vblk_t, bcmp, near, ks_aug, vs_t, kw_aug, vw_t)
        zeros = jnp.zeros((n_seq, N_STATE), F32)
        o_ssm, h_re, h_im = _ssm(u, zeros, zeros, ssm_w, *ssm_tail, n_seq, SCAN_STEPS)
        mkv = _norm_matmul(mem_prompt.reshape(n_seq * n_mem, D_MODEL), mem_norm_g[l][None], xkv_b, ROW_TILE)
        xp = _tail_prompt(o_nsa.reshape(n_seq * seq, NSA_WIDTH), o_ssm, xp,
                          mkv.reshape(n_seq, n_mem, 2 * MEM_WIDTH), gg, w_out_b, ng, xq_b, xo_b, wi_b, wo_b,
                          n_seq, TAIL_TILE)
        outs[0].append(_rows_minor(kvc_t))
        outs[1].append(_rows_minor(kvs_t))
        outs[2].append(_rows_minor(kvw_t[:, :, seq - WINDOW:]))
        outs[3].append(h_re.reshape(n_seq, N_SSM_GROUPS, SSM_STATE))
        outs[4].append(h_im.reshape(n_seq, N_SSM_GROUPS, SSM_STATE))
        outs[5].append(mkv.reshape(n_seq, n_mem, 2, MEM_HEADS, MEM_HEAD_DIM))

        q, kv, gates, u = _inproj_sample(xs, ng[0:1], w_sample)
        kvc, kvs, kvw = (kv[:, i * KV_COLS:(i + 1) * KV_COLS] for i in range(3))
        head_group = (jnp.arange(N_HEADS)[:, None] // GQA == jnp.arange(N_KV)[None, :]).astype(F32)
        qm = (q.reshape(n_dec, N_HEADS, 1, HEAD_DIM) * head_group[None, :, :, None]).reshape(n_dec, N_HEADS, LANES)
        gates_h = gates[:, :N_HEADS * N_BRANCH].reshape(n_dec, N_HEADS, N_BRANCH)
        gates_h = jnp.pad(gates_h, ((0, 0), (0, 0), (0, LANES - N_BRANCH)))
        blocks_t = _compress_sample(_rows_minor_view(cache_kv_cmp[l]), pt_flat, n_dec, n_pages, pos, proj_c)
        o_cmp, ids = _cmp_sample(qm, blocks_t, kvc.reshape(n_dec, 1, KV_COLS), pos, proj_c, bc_s, bn_s, past_len)
        ids = ids[:, :N_SEL]
        past_page = jnp.minimum(ids, past_len // SEL_BLOCK - 1) // (PAGE_SIZE // SEL_BLOCK)
        sel_pages = jnp.take_along_axis(jnp.repeat(page_table, N_KV, axis=0), past_page, axis=1)
        pool_sel = cache_kv_sel[l].transpose(0, 2, 3, 4, 1)
        o_nsa, new_buf_t = _sel_win_sample(
            ids.reshape(-1), sel_pages.reshape(-1), pool_sel, _rows_minor_view(cache_kv_win[l]), qm, kvs.reshape(n_dec, 1, KV_COLS),
            kvw.reshape(n_dec, KV_COLS, 1), gates_h, o_cmp, bw_s, bs_s, b0_s, n_pages)
        o_ssm, h_re, h_im = _ssm(u, state_ssm_re[l].reshape(n_dec, N_STATE), state_ssm_im[l].reshape(n_dec, N_STATE),
                                 ssm_w, *ssm_tail, n_dec, 1)
        x1, qx = _merge(o_nsa.reshape(n_dec, NSA_WIDTH), o_ssm, xs, gg, w_out_b, ng, xq_b, 1, n_dec)
        oa = _cross_sample(qx, cache_mem_kv[l].reshape(n_dec, n_mem * 2 * MEM_HEADS, MEM_HEAD_DIM), n_mem)
        xs = _ffn(x1, oa, ng, xo_b, wi_b, wo_b, n_dec)
        outs[6].append(kvc.reshape(n_dec, 1, *kv5))
        outs[7].append(kvs.reshape(n_dec, 1, *kv5))
        outs[8].append(_rows_minor(new_buf_t))
        outs[9].append(h_re.reshape(n_dec, N_SSM_GROUPS, SSM_STATE))
        outs[10].append(h_im.reshape(n_dec, N_SSM_GROUPS, SSM_STATE))

    stacked = [jnp.stack(o, axis=0) for o in outs]
    return (xp.reshape(x_prompt.shape), xs.reshape(x_sample.shape), *stacked)
```

```python
import functools
import math

import numpy as np
import jax
import jax.numpy as jnp
from jax import lax
from jax.experimental import pallas as pl
from jax.experimental.pallas import tpu as pltpu

F32 = jnp.float32
BF16 = jnp.bfloat16
I32 = jnp.int32

D_MODEL = 1024
HEAD_DIM = 64
N_HEADS = 8
N_KV = 2
GQA = N_HEADS // N_KV
NSA_WIDTH = N_HEADS * HEAD_DIM
SSM_WIDTH = 512
KV_COLS = 2 * N_KV * HEAD_DIM
KV_HALF = N_KV * HEAD_DIM
N_BRANCH = 3
CMP_BLOCK = 32
SEL_BLOCK = 64
N_SEL = 16
WINDOW = 512
PAGE_SIZE = 128
SSM_GROUP = 16
N_SSM_GROUPS = 32
SSM_STATE = 64
N_STATE = N_SSM_GROUPS * SSM_STATE
N_BUCKETS = 32
MAX_DISTANCE = 128
MEM_HEADS = 4
MEM_HEAD_DIM = 128
MEM_WIDTH = MEM_HEADS * MEM_HEAD_DIM
EPS = 1e-6
NEG = -1e30
MASKED_BELOW = -1e29
TINY = 1e-30
FORCE = 1e4
SCALE = HEAD_DIM ** -0.5
MEM_SCALE = MEM_HEAD_DIM ** -0.5

LANES = 128
Q_TILE = 256
FAR_TILES = 2
FFN_CHUNKS = 2
VMEM_LIMIT = 56 * 1024 * 1024

AUG_SEL = HEAD_DIM
AUG_ONE = AUG_SEL + 32
GATE_ROWS = 16
VT_ROWS = HEAD_DIM + 16
LOG2E = math.log2(math.e)


def _bucket_starts():
    n = np.arange(0, 4 * MAX_DISTANCE)
    exact = N_BUCKETS // 2
    nf = np.maximum(n, exact).astype(np.float32)
    big = exact + (np.log(nf / exact) / np.float32(math.log(MAX_DISTANCE / exact))
                   * (N_BUCKETS - exact)).astype(np.int32)
    bucket = np.where(n < exact, n, np.minimum(big, N_BUCKETS - 1))
    return [int(np.argmax(bucket >= k)) for k in range(N_BUCKETS)]


_BUCKET_START = _bucket_starts()
assert _BUCKET_START[-1] <= Q_TILE


def _params(*sem):
    return pltpu.CompilerParams(dimension_semantics=sem, vmem_limit_bytes=VMEM_LIMIT)


def _bdot(a, b):
    return jnp.dot(a.astype(BF16), b.astype(BF16), preferred_element_type=F32)


def _bdot_nt(a, b):
    return lax.dot_general(a.astype(BF16), b.astype(BF16), (((1,), (1,)), ((), ())),
                           preferred_element_type=F32)


def _split(a):
    hi = a.astype(BF16)
    return hi, (a - hi.astype(F32)).astype(BF16)


def _dot3(a, b):
    ah, al = _split(a)
    bh, bl = _split(b)
    d = functools.partial(jnp.dot, preferred_element_type=F32)
    return d(ah, bh) + (d(ah, bl) + d(al, bh))


def _dot3_nt(a, b):
    ah, al = _split(a)
    bh, bl = _split(b)
    d = functools.partial(lax.dot_general, dimension_numbers=(((1,), (1,)), ((), ())),
                          preferred_element_type=F32)
    return d(ah, bh) + (d(ah, bl) + d(al, bh))


def _rms(x, g):
    return x * lax.rsqrt(jnp.mean(x * x, axis=-1, keepdims=True) + EPS) * g


def _masked_softmax(s, mask, axis=-1):
    s = jnp.where(mask, s, NEG)
    m = jnp.max(s, axis=axis, keepdims=True)
    e = jnp.where(mask, jnp.exp(s - m), 0.0)
    return e * (1.0 / jnp.maximum(jnp.sum(e, axis=axis, keepdims=True), TINY))


def _bias_of_dist(dist, tbl_ref, h):
    v = jnp.full(dist.shape, tbl_ref[0, h], F32)
    for k in range(1, N_BUCKETS):
        v = jnp.where(dist >= _BUCKET_START[k], tbl_ref[k, h], v)
    return v


def _iota2(shape, dim):
    return lax.broadcasted_iota(I32, shape, dim)


def _rows_minor(a):
    n, _, rows = a.shape
    return a.reshape(n, 2, N_KV, HEAD_DIM, rows).transpose(0, 4, 1, 2, 3)


def _rows_minor_view(a):
    n, rows = a.shape[:2]
    return a.transpose(0, 2, 3, 4, 1).reshape(n, KV_COLS, rows)


def _cmp_block_of_row(row, n_cmp):
    return jnp.where(row < n_cmp // 2, 2 * row, 2 * (row - n_cmp // 2) + 1)


def _prompt_bias_kernel(tbl_ref, bcmp_ref, near_ref, *, n_cmp, nq):
    h = pl.program_id(0)
    last = tbl_ref[N_BUCKETS - 1, h]

    blk = _cmp_block_of_row(_iota2((n_cmp, Q_TILE), 0), n_cmp)
    for qi in range(nq):
        dist = qi * Q_TILE + _iota2((n_cmp, Q_TILE), 1) - (blk * CMP_BLOCK + (CMP_BLOCK - 1))
        bcmp_ref[qi, 0] = jnp.where(dist >= 0, _bias_of_dist(dist, tbl_ref, h), NEG)

    key = _iota2((Q_TILE, Q_TILE), 0)
    qry = _iota2((Q_TILE, Q_TILE), 1)
    d0 = qry - key
    near_ref[0, 0] = jnp.where(d0 >= 0, (_bias_of_dist(d0, tbl_ref, h) - last) * LOG2E, NEG)
    d1 = Q_TILE + qry - key
    near_ref[0, 1] = (_bias_of_dist(d1, tbl_ref, h) - last) * LOG2E
    near_ref[0, 2] = jnp.where(key > qry, 0.0, NEG)


def _prompt_bias(rel_bias, seq):
    nq = seq // Q_TILE
    n_cmp = seq // CMP_BLOCK
    return pl.pallas_call(
        functools.partial(_prompt_bias_kernel, n_cmp=n_cmp, nq=nq),
        grid=(N_HEADS,),
        in_specs=[pl.BlockSpec(memory_space=pltpu.SMEM)],
        out_specs=[pl.BlockSpec((nq, 1, n_cmp, Q_TILE), lambda h: (0, h // GQA, 0, h % GQA)),
                   pl.BlockSpec((1, 3, Q_TILE, Q_TILE), lambda h: (h // GQA, 0, 0, h % GQA))],
        out_shape=[jax.ShapeDtypeStruct((nq, N_KV, n_cmp, GQA * Q_TILE), F32),
                   jax.ShapeDtypeStruct((N_KV, 3, Q_TILE, GQA * Q_TILE), F32)],
        compiler_params=_params("arbitrary"),
    )(rel_bias)


OFF_KV = NSA_WIDTH
OFF_GATE = OFF_KV + 3 * KV_COLS
OFF_U = OFF_GATE + N_HEADS * N_BRANCH


def _k_cols(w_in, which):
    base = OFF_KV + which * KV_COLS
    return [w_in[:, base + g * HEAD_DIM: base + (g + 1) * HEAD_DIM] for g in range(N_KV)]


def _pack_w_prompt(w_in):
    w_row = jnp.concatenate([w_in[:, OFF_KV:OFF_KV + KV_COLS]] + _k_cols(w_in, 1) + _k_cols(w_in, 2)
                            + [w_in[:, OFF_U:]], axis=1)
    gates = w_in[:, OFF_GATE:OFF_U].reshape(-1, N_KV, GQA * N_BRANCH)
    gates = jnp.pad(gates, ((0, 0), (0, 0), (0, GATE_ROWS - GQA * N_BRANCH))).reshape(-1, N_KV * GATE_ROWS)
    w_t = jnp.concatenate([w_in[:, :OFF_GATE], gates], axis=1).T
    return w_row.astype(BF16), w_t.astype(BF16)


def _inproj_prompt_kernel(x_ref, g_ref, wr_ref, wt_ref, kvc_ref, ksa_ref, kwa_ref, u_ref, qt_ref, kvct_ref,
                          kvst_ref, kvwt_ref, vst_ref, vwt_ref, gt_ref, *, tile):
    hb = _rms(x_ref[...], g_ref[...]).astype(BF16)
    z = jnp.dot(hb, wr_ref[...], preferred_element_type=F32)
    kvc_ref[...] = z[:, :KV_COLS]
    u_ref[...] = z[:, KV_COLS + 4 * HEAD_DIM:]

    pos = pl.program_id(1) * tile + _iota2((tile, LANES - HEAD_DIM), 0)
    col = _iota2((tile, LANES - HEAD_DIM), 1)
    ones = ((col >= AUG_ONE - HEAD_DIM) & (col < AUG_ONE - HEAD_DIM + 2)).astype(F32)
    aug_sel = jnp.where(col == pos // SEL_BLOCK, -NEG, ones)
    k0 = KV_COLS
    ksa_ref[...] = jnp.concatenate(
        [z[:, k0:k0 + HEAD_DIM], aug_sel, z[:, k0 + HEAD_DIM:k0 + 2 * HEAD_DIM], aug_sel], axis=1).astype(BF16)
    k0 = KV_COLS + 2 * HEAD_DIM
    kwa_ref[...] = jnp.concatenate(
        [z[:, k0:k0 + HEAD_DIM], ones, z[:, k0 + HEAD_DIM:k0 + 2 * HEAD_DIM], ones], axis=1).astype(BF16)

    zt = lax.dot_general(wt_ref[...], hb, (((1,), (1,)), ((), ())), preferred_element_type=F32)
    qt_ref[0] = zt[:NSA_WIDTH]
    kvct_ref[0] = zt[OFF_KV:OFF_KV + KV_COLS]
    kvst_ref[0] = zt[OFF_KV + KV_COLS:OFF_KV + 2 * KV_COLS]
    kvwt_ref[0] = zt[OFF_KV + 2 * KV_COLS:OFF_GATE]
    gt_ref[0] = jax.nn.sigmoid(zt[OFF_GATE:])
    v_sel = OFF_KV + KV_COLS + KV_HALF
    v_win = OFF_KV + 2 * KV_COLS + KV_HALF
    sum_rows = (_iota2((VT_ROWS - HEAD_DIM, Q_TILE), 0) == 0).astype(F32)
    for j in range(tile // Q_TILE):
        cols = slice(j * Q_TILE, (j + 1) * Q_TILE)
        for ref, v0 in ((vst_ref, v_sel), (vwt_ref, v_win)):
            parts = []
            for g in range(N_KV):
                parts += [zt[v0 + g * HEAD_DIM:v0 + (g + 1) * HEAD_DIM, cols], sum_rows]
            ref[0, j] = jnp.concatenate(parts, axis=0).astype(BF16)


def _inproj_prompt(x, g, w_row, w_t, n_seq, tile):
    rows = x.shape[0]
    seq = rows // n_seq
    assert seq // SEL_BLOCK <= AUG_ONE - AUG_SEL
    nt = seq // tile
    row_map = lambda b, t: (b * nt + t, 0)
    t_map = lambda b, t: (b, 0, t)
    tiles = tile // Q_TILE
    n_gate = N_KV * GATE_ROWS
    out = [
        (pl.BlockSpec((tile, KV_COLS), row_map), jax.ShapeDtypeStruct((rows, KV_COLS), F32)),
        (pl.BlockSpec((tile, N_KV * LANES), row_map), jax.ShapeDtypeStruct((rows, N_KV * LANES), BF16)),
        (pl.BlockSpec((tile, N_KV * LANES), row_map), jax.ShapeDtypeStruct((rows, N_KV * LANES), BF16)),
        (pl.BlockSpec((tile, SSM_WIDTH), lambda b, t: (t, b)), jax.ShapeDtypeStruct((seq, n_seq * SSM_WIDTH), F32)),
        (pl.BlockSpec((1, NSA_WIDTH, tile), t_map), jax.ShapeDtypeStruct((n_seq, NSA_WIDTH, seq), F32)),
        (pl.BlockSpec((1, KV_COLS, tile), t_map), jax.ShapeDtypeStruct((n_seq, KV_COLS, seq), F32)),
        (pl.BlockSpec((1, KV_COLS, tile), t_map), jax.ShapeDtypeStruct((n_seq, KV_COLS, seq), F32)),
        (pl.BlockSpec((1, KV_COLS, tile), t_map), jax.ShapeDtypeStruct((n_seq, KV_COLS, seq), F32)),
        (pl.BlockSpec((1, tiles, N_KV * VT_ROWS, Q_TILE), lambda b, t: (b, t, 0, 0)),
         jax.ShapeDtypeStruct((n_seq, seq // Q_TILE, N_KV * VT_ROWS, Q_TILE), BF16)),
        (pl.BlockSpec((1, tiles, N_KV * VT_ROWS, Q_TILE), lambda b, t: (b, t, 0, 0)),
         jax.ShapeDtypeStruct((n_seq, seq // Q_TILE, N_KV * VT_ROWS, Q_TILE), BF16)),
        (pl.BlockSpec((1, n_gate, tile), t_map), jax.ShapeDtypeStruct((n_seq, n_gate, seq), F32)),
    ]
    return pl.pallas_call(
        functools.partial(_inproj_prompt_kernel, tile=tile),
        grid=(n_seq, nt),
        in_specs=[pl.BlockSpec((tile, D_MODEL), row_map),
                  pl.BlockSpec((1, D_MODEL), lambda b, t: (0, 0)),
                  pl.BlockSpec(w_row.shape, lambda b, t: (0, 0)),
                  pl.BlockSpec(w_t.shape, lambda b, t: (0, 0))],
        out_specs=[o[0] for o in out],
        out_shape=[o[1] for o in out],
        compiler_params=_params("arbitrary", "arbitrary"),
    )(x, g, w_row, w_t)


S_Q = 0
S_KV = S_Q + NSA_WIDTH
S_GATE = S_KV + 3 * KV_COLS
S_U = S_GATE + LANES
S_END = S_U + SSM_WIDTH


def _pack_w_sample(w_in):
    gates = jnp.pad(w_in[:, OFF_GATE:OFF_U], ((0, 0), (0, LANES - N_HEADS * N_BRANCH)))
    return jnp.concatenate([w_in[:, :OFF_GATE], gates, w_in[:, OFF_U:]], axis=1).astype(BF16)


def _inproj_sample_kernel(x_ref, g_ref, w_ref, q_ref, kv_ref, gate_ref, u_ref):
    z = jnp.dot(_rms(x_ref[...], g_ref[...]).astype(BF16), w_ref[...], preferred_element_type=F32)
    q_ref[...] = z[:, S_Q:S_KV]
    kv_ref[...] = z[:, S_KV:S_GATE]
    gate_ref[...] = jax.nn.sigmoid(z[:, S_GATE:S_U])
    u_ref[...] = z[:, S_U:S_END]


def _inproj_sample(x, g, w):
    rows = x.shape[0]
    vm = pl.BlockSpec(memory_space=pltpu.VMEM)
    widths = [NSA_WIDTH, 3 * KV_COLS, LANES, SSM_WIDTH]
    return pl.pallas_call(
        _inproj_sample_kernel,
        in_specs=[vm] * 3, out_specs=[vm] * 4,
        out_shape=[jax.ShapeDtypeStruct((rows, w_), F32) for w_ in widths],
        compiler_params=pltpu.CompilerParams(vmem_limit_bytes=VMEM_LIMIT),
    )(x, g, w)


def _pack_compress(cmp_pos, cmp_proj, group_major):
    pos = jnp.broadcast_to(cmp_pos.transpose(1, 0, 2)[:, :, None, :],
                           (CMP_BLOCK, 2, N_KV, HEAD_DIM)).reshape(CMP_BLOCK, KV_COLS)
    eye = jnp.eye(2 * N_KV, dtype=F32).reshape(2, N_KV, 2, N_KV)
    order = 'cgdGCe' if group_major else 'cgdCGe'
    proj = jnp.einsum('cde,cgCG->' + order, cmp_proj, eye).reshape(KV_COLS, KV_COLS)
    return pos, proj


def _compress_prompt_kernel(kv_ref, pos_ref, proj_ref, projt_ref, kblk_ref, vblkt_ref):
    x = kv_ref[0]
    n_sel = x.shape[0] // SEL_BLOCK
    x = x.reshape(n_sel, SEL_BLOCK // CMP_BLOCK, CMP_BLOCK, KV_COLS)
    pos = pos_ref[...][None]
    sums = jnp.concatenate([jnp.sum(x[:, i] * pos, axis=1) for i in range(SEL_BLOCK // CMP_BLOCK)], axis=0)
    blk = _dot3(sums, proj_ref[...])
    blk_t = _dot3_nt(projt_ref[...], sums)
    for g in range(N_KV):
        kblk_ref[0, g] = blk[:, g * KV_HALF:g * KV_HALF + HEAD_DIM]
        vblkt_ref[0, g] = blk_t[g * KV_HALF + HEAD_DIM:(g + 1) * KV_HALF]


def _compress_prompt(kvc, pos, proj):
    n_seq, seq, _ = kvc.shape
    n_cmp = seq // CMP_BLOCK
    return pl.pallas_call(
        _compress_prompt_kernel,
        grid=(n_seq,),
        in_specs=[pl.BlockSpec((1, seq, KV_COLS), lambda b: (b, 0, 0)),
                  pl.BlockSpec((CMP_BLOCK, KV_COLS), lambda b: (0, 0)),
                  pl.BlockSpec((KV_COLS, KV_COLS), lambda b: (0, 0)),
                  pl.BlockSpec((KV_COLS, KV_COLS), lambda b: (0, 0))],
        out_specs=[pl.BlockSpec((1, N_KV, n_cmp, HEAD_DIM), lambda b: (b, 0, 0, 0)),
                   pl.BlockSpec((1, N_KV, HEAD_DIM, n_cmp), lambda b: (b, 0, 0, 0))],
        out_shape=[jax.ShapeDtypeStruct((n_seq, N_KV, n_cmp, HEAD_DIM), F32),
                   jax.ShapeDtypeStruct((n_seq, N_KV, HEAD_DIM, n_cmp), F32)],
        compiler_params=_params("arbitrary"),
    )(kvc, pos, proj, proj.T)


def _select_blocks_t(pair, q0):
    row = _iota2(pair.shape, 0)
    n_sel = pair.shape[0]
    cur = (q0 + _iota2(pair.shape, 1)) // SEL_BLOCK
    forced = (row == 0) | (row == cur) | (row == cur - 1)
    score = jnp.where(row <= cur, jnp.where(forced, pair + FORCE, pair), -jnp.inf)
    chosen = jnp.zeros(pair.shape, F32)
    for _ in range(N_SEL):
        m = jnp.max(score, axis=0, keepdims=True)
        hit = (score == m) & (m > -jnp.inf)
        first = jnp.min(jnp.where(hit, row, n_sel), axis=0, keepdims=True)
        pick = row == first
        chosen = jnp.where(pick, 1.0, chosen)
        score = jnp.where(pick, -jnp.inf, score)
    return chosen


def _attend(state, k_aug, v_t, q_aug, extras):
    m, acc = state
    s = jnp.dot(k_aug, q_aug, preferred_element_type=F32)
    if any(t is not None or c is not None for t, c in extras):
        rows = []
        for t, c in extras:
            if t is None:
                rows.append(jnp.full((Q_TILE, s.shape[1]), 0.0 if c is None else c, F32))
            else:
                rows.append(t if c is None else t + c)
        s = s + jnp.concatenate(rows, axis=0)
    m_new = jnp.maximum(m, jnp.max(s, axis=0, keepdims=True))
    e = jnp.exp2(s - m_new)
    acc = jnp.exp2(m - m_new) * acc + jnp.dot(v_t, e.astype(BF16), preferred_element_type=F32)
    return m_new, acc


def _attend_result(state):
    acc = state[1]
    return acc[:HEAD_DIM] * (1.0 / jnp.maximum(acc[HEAD_DIM:HEAD_DIM + 1], TINY))


def _nsa_prompt_kernel(tbl_ref, qt_ref, gt_ref, kblk_ref, vblkt_ref, bcmp_ref, near_ref, ks_ref, vst_ref,
                       kw_ref, vwt_ref, o_ref):
    g = pl.program_id(1)
    qi = pl.program_id(2)
    q0 = qi * Q_TILE
    win_tiles = WINDOW // Q_TILE
    v_rows = pl.ds(pl.multiple_of(g * VT_ROWS, VT_ROWS), VT_ROWS)

    width = GQA * Q_TILE
    q = jnp.concatenate([qt_ref[0, r * HEAD_DIM:(r + 1) * HEAD_DIM, :] for r in range(GQA)], axis=1) * SCALE

    row = _iota2((LANES - AUG_ONE, Q_TILE), 0)
    consts = []
    for r in range(GQA):
        last = jnp.full((LANES - AUG_ONE, Q_TILE), tbl_ref[N_BUCKETS - 1, g * GQA + r] * LOG2E, F32)
        hi = last.astype(BF16).astype(F32)
        consts.append(jnp.where(row == 0, hi, jnp.where(row == 1, last - hi, 0.0)).astype(BF16))
    consts = jnp.concatenate(consts, axis=1)
    q_log2 = (q * LOG2E).astype(BF16)
    near0, near1, oldest = near_ref[0, 0], near_ref[0, 1], near_ref[0, 2]
    init = (jnp.full((1, width), NEG, F32), jnp.zeros((VT_ROWS, width), F32))

    def key_rows(ref, tile, n_tiles):
        return ref[0, pl.ds(pl.multiple_of(tile * Q_TILE, Q_TILE), n_tiles * Q_TILE), :]

    def value_cols(ref, tile, n_tiles):
        return jnp.concatenate([ref[0, tile + j, v_rows, :] for j in range(n_tiles)], axis=1)

    def tile_mask(ok):
        return jnp.where(ok, 0.0, NEG)

    q_aug = jnp.concatenate([q_log2, jnp.zeros((AUG_ONE - AUG_SEL, width), BF16), consts], axis=0)
    extras = []
    for mt in range(win_tiles):
        table = oldest if mt == 0 else near1 if mt == win_tiles - 1 else None
        extras.append((table, tile_mask(qi + mt >= win_tiles)))
    extras.append((near0, None))
    st = _attend(init, key_rows(kw_ref, qi, win_tiles + 1), value_cols(vwt_ref, qi, win_tiles + 1), q_aug, extras)
    o_win = _attend_result(st)

    k_cmp, v_cmp_t = kblk_ref[0, 0], vblkt_ref[0, 0]
    n_cmp = k_cmp.shape[0]
    bias = bcmp_ref[0, 0]
    p = _masked_softmax(_dot3(k_cmp, q) + bias, bias > MASKED_BELOW, axis=0)
    o_cmp = _bdot(v_cmp_t, p)
    ps = p[:, :Q_TILE]
    for r in range(1, GQA):
        ps = ps + p[:, r * Q_TILE:(r + 1) * Q_TILE]
    chosen = _select_blocks_t(ps[:n_cmp // 2] + ps[n_cmp // 2:], q0)
    not_chosen = (chosen - 1.0).astype(BF16)
    pad_rows = (AUG_ONE - AUG_SEL) - not_chosen.shape[0]
    if pad_rows:
        not_chosen = jnp.concatenate([not_chosen, jnp.zeros((pad_rows, Q_TILE), BF16)], axis=0)
    q_aug = jnp.concatenate([q_log2, jnp.concatenate([not_chosen] * GQA, axis=1), consts], axis=0)

    extras = [(near1, tile_mask(qi >= 1)), (near0, None)]
    st = _attend(init, key_rows(ks_ref, qi, 2), value_cols(vst_ref, qi, 2), q_aug, extras)
    n_far = jnp.maximum(qi - 1, 0)

    def far_chunk(c, st, masked=False):
        t0 = c * FAR_TILES
        extras = [(None, tile_mask(t0 + j < n_far) if masked else None) for j in range(FAR_TILES)]
        return _attend(st, key_rows(ks_ref, t0 + 1, FAR_TILES), value_cols(vst_ref, t0 + 1, FAR_TILES), q_aug, extras)

    full = n_far // FAR_TILES
    st = lax.fori_loop(0, full, far_chunk, st)
    st = lax.cond(n_far % FAR_TILES != 0, lambda st=st: far_chunk(full, st, masked=True), lambda st=st: st)
    o_sel = _attend_result(st)

    def gate(branch):
        return jnp.concatenate([gt_ref[0, r * N_BRANCH + branch:r * N_BRANCH + branch + 1, :]
                                for r in range(GQA)], axis=1)

    o = gate(0) * o_cmp + gate(1) * o_sel + gate(2) * o_win
    o_ref[0] = jnp.concatenate([o[:, r * Q_TILE:(r + 1) * Q_TILE] for r in range(GQA)], axis=0).T


def _nsa_prompt(rel_bias, q_t, gates_t, kblk, vblk_t, bcmp, near, ks_aug, vs_t, kw_aug, vw_t):
    n_seq, _, seq = q_t.shape
    nq = seq // Q_TILE
    assert nq % FAR_TILES == 0
    n_cmp = kblk.shape[2]
    gw = GQA * HEAD_DIM
    wt = WINDOW // Q_TILE
    return pl.pallas_call(
        _nsa_prompt_kernel,
        grid=(n_seq, N_KV, nq),
        in_specs=[pl.BlockSpec(memory_space=pltpu.SMEM),
                  pl.BlockSpec((1, gw, Q_TILE), lambda b, g, qi: (b, g, qi)),
                  pl.BlockSpec((1, GATE_ROWS, Q_TILE), lambda b, g, qi: (b, g, qi)),
                  pl.BlockSpec((1, 1, n_cmp, HEAD_DIM), lambda b, g, qi: (b, g, 0, 0)),
                  pl.BlockSpec((1, 1, HEAD_DIM, n_cmp), lambda b, g, qi: (b, g, 0, 0)),
                  pl.BlockSpec((1, 1, n_cmp, GQA * Q_TILE), lambda b, g, qi: (qi, g, 0, 0)),
                  pl.BlockSpec((1, 3, Q_TILE, GQA * Q_TILE), lambda b, g, qi: (g, 0, 0, 0)),
                  pl.BlockSpec((1, Q_TILE + seq, LANES), lambda b, g, qi: (b, 0, g)),
                  pl.BlockSpec((1, 1 + nq, N_KV * VT_ROWS, Q_TILE), lambda b, g, qi: (b, 0, 0, 0)),
                  pl.BlockSpec((1, WINDOW + seq, LANES), lambda b, g, qi: (b, 0, g)),
                  pl.BlockSpec((1, wt + nq, N_KV * VT_ROWS, Q_TILE), lambda b, g, qi: (b, 0, 0, 0))],
        out_specs=pl.BlockSpec((1, Q_TILE, gw), lambda b, g, qi: (b, qi, g)),
        out_shape=jax.ShapeDtypeStruct((n_seq, seq, NSA_WIDTH), F32),
        compiler_params=_params("arbitrary", "arbitrary", "arbitrary"),
    )(rel_bias, q_t, gates_t, kblk, vblk_t, bcmp, near, ks_aug, vs_t, kw_aug, vw_t)


STATE_CHUNK = 512
SSM_KB = SSM_WIDTH // LANES
GROUPS_PER_KB = N_SSM_GROUPS // SSM_KB


def _ssm_disc_kernel(are_ref, aim_ref, ldt_ref, bre_ref, bim_ref, abr_ref, abi_ref, bbr_ref, bbi_ref):
    a_re, a_im = are_ref[...], aim_ref[...]
    dt = jnp.exp(ldt_ref[...])
    mag = jnp.exp(a_re * dt)
    ab_re = mag * jnp.cos(a_im * dt)
    ab_im = mag * jnp.sin(a_im * dt)
    den = a_re * a_re + a_im * a_im
    co_re = ((ab_re - 1.0) * a_re + ab_im * a_im) / den
    co_im = (ab_im * a_re - (ab_re - 1.0) * a_im) / den
    abr_ref[...] = ab_re
    abi_ref[...] = ab_im
    b_re, b_im = bre_ref[...], bim_ref[...]
    bbr_ref[...] = co_re[:, None, :] * b_re - co_im[:, None, :] * b_im
    bbi_ref[...] = co_re[:, None, :] * b_im + co_im[:, None, :] * b_re


def _block_diag(w):
    a, b = w.shape[1:]
    w = w.reshape(SSM_KB, GROUPS_PER_KB, a, b)
    eye = jnp.eye(GROUPS_PER_KB, dtype=w.dtype)
    return jnp.einsum('kgab,gh->kgahb', w, eye).reshape(SSM_KB, GROUPS_PER_KB * a, GROUPS_PER_KB * b)


def _ssm_weights(a_re, a_im, b_re, b_im, c_re, c_im, log_dt):
    ng, p = a_re.shape
    vm = pl.BlockSpec(memory_space=pltpu.VMEM)
    ab_re, ab_im, bb_re, bb_im = pl.pallas_call(
        _ssm_disc_kernel,
        in_specs=[vm] * 5, out_specs=[vm] * 4,
        out_shape=[jax.ShapeDtypeStruct((ng, p), F32)] * 2
        + [jax.ShapeDtypeStruct((ng, SSM_GROUP, p), F32)] * 2,
    )(a_re, a_im, log_dt.reshape(ng, 1), b_re.transpose(0, 2, 1), b_im.transpose(0, 2, 1))
    return (ab_re.reshape(1, N_STATE), ab_im.reshape(1, N_STATE),
            _block_diag(bb_re).astype(BF16), _block_diag(bb_im).astype(BF16),
            _block_diag(c_re.transpose(0, 2, 1)).astype(BF16),
            _block_diag(c_im.transpose(0, 2, 1)).astype(BF16))


def _ssm_kernel(u_ref, h0r_ref, h0i_ref, abr_ref, abi_ref, bbr_ref, bbi_ref, ccr_ref, cci_ref,
                d_ref, gw_ref, gb_ref, o_ref, hr_ref, hi_ref, sre, sim, *rows_scratch, nb, steps):
    @pl.when(pl.program_id(0) == 0)
    def _():
        hr_ref[...] = h0r_ref[...]
        hi_ref[...] = h0i_ref[...]

    if rows_scratch:
        rows_ref, = rows_scratch
        for b in range(nb):
            for c in range(SSM_KB):
                col = b * SSM_WIDTH + c * LANES
                rows_ref[c, pl.ds(b, steps, stride=nb), :] = u_ref[:, col:col + LANES]
        u = jnp.concatenate([rows_ref[c] for c in range(SSM_KB)], axis=1)
    else:
        u = u_ref[...]
    ub = u.astype(BF16)
    kw = N_STATE // SSM_KB
    for kb in range(SSM_KB):
        uk = ub[:, kb * LANES:(kb + 1) * LANES]
        sre[:, kb * kw:(kb + 1) * kw] = jnp.dot(uk, bbr_ref[kb], preferred_element_type=F32)
        sim[:, kb * kw:(kb + 1) * kw] = jnp.dot(uk, bbi_ref[kb], preferred_element_type=F32)

    if steps == 1:
        ar, ai = abr_ref[...], abi_ref[...]
        hr, hi = hr_ref[...], hi_ref[...]
        nr = ar * hr - ai * hi + sre[...]
        ni = ar * hi + ai * hr + sim[...]
        sre[...] = nr
        sim[...] = ni
        hr_ref[...] = nr
        hi_ref[...] = ni
    else:
        for cb in range(N_STATE // STATE_CHUNK):
            cols = slice(cb * STATE_CHUNK, (cb + 1) * STATE_CHUNK)
            ar = jnp.broadcast_to(abr_ref[:, cols], (nb, STATE_CHUNK))
            ai = jnp.broadcast_to(abi_ref[:, cols], (nb, STATE_CHUNK))

            def step(t, carry, cols=cols, ar=ar, ai=ai):
                hr, hi = carry
                rows = pl.ds(pl.multiple_of(t * nb, nb), nb)
                nr = ar * hr - ai * hi + sre[rows, cols]
                ni = ar * hi + ai * hr + sim[rows, cols]
                sre[rows, cols] = nr
                sim[rows, cols] = ni
                return nr, ni

            hr, hi = lax.fori_loop(0, steps, step, (hr_ref[:, cols], hi_ref[:, cols]))
            hr_ref[:, cols] = hr
            hi_ref[:, cols] = hi

    ys = []
    for kb in range(SSM_KB):
        cols = slice(kb * kw, (kb + 1) * kw)
        ys.append(_bdot(sre[:, cols], ccr_ref[kb]) - _bdot(sim[:, cols], cci_ref[kb]))
    y = jnp.concatenate(ys, axis=1) + d_ref[...] * u
    g = jax.nn.gelu(y)
    out = g * jax.nn.sigmoid(_bdot(g, gw_ref[...]) + gb_ref[...])
    if rows_scratch:
        for c in range(SSM_KB):
            rows_ref[c] = out[:, c * LANES:(c + 1) * LANES]
        for b in range(nb):
            for c in range(SSM_KB):
                col = b * SSM_WIDTH + c * LANES
                o_ref[:, col:col + LANES] = rows_ref[c, pl.ds(b, steps, stride=nb), :]
    else:
        o_ref[...] = out


def _ssm(u, h0_re, h0_im, wts, d, glu_w, glu_b, nb, steps_per_call):
    ab_re, ab_im, bb_re, bb_im, cc_re, cc_im = wts
    rows = nb * steps_per_call
    wide = steps_per_call > 1
    n_calls = u.shape[0] // steps_per_call if wide else 1
    block = (steps_per_call, nb * SSM_WIDTH) if wide else (rows, SSM_WIDTH)
    const2 = lambda i: (0, 0)
    const3 = lambda i: (0, 0, 0)
    kw = N_STATE // SSM_KB
    scratch = [pltpu.VMEM((rows, N_STATE), F32), pltpu.VMEM((rows, N_STATE), F32)]
    if wide:
        scratch.append(pltpu.VMEM((SSM_KB, rows, LANES), F32))
    return pl.pallas_call(
        functools.partial(_ssm_kernel, nb=nb, steps=steps_per_call),
        grid=(n_calls,),
        in_specs=[pl.BlockSpec(block, lambda i: (i, 0)),
                  pl.BlockSpec((nb, N_STATE), const2), pl.BlockSpec((nb, N_STATE), const2),
                  pl.BlockSpec((1, N_STATE), const2), pl.BlockSpec((1, N_STATE), const2),
                  pl.BlockSpec((SSM_KB, LANES, kw), const3), pl.BlockSpec((SSM_KB, LANES, kw), const3),
                  pl.BlockSpec((SSM_KB, kw, LANES), const3), pl.BlockSpec((SSM_KB, kw, LANES), const3),
                  pl.BlockSpec((1, SSM_WIDTH), const2),
                  pl.BlockSpec((SSM_WIDTH, SSM_WIDTH), const2),
                  pl.BlockSpec((1, SSM_WIDTH), const2)],
        out_specs=[pl.BlockSpec(block, lambda i: (i, 0)),
                   pl.BlockSpec((nb, N_STATE), const2), pl.BlockSpec((nb, N_STATE), const2)],
        out_shape=[jax.ShapeDtypeStruct(u.shape, F32),
                   jax.ShapeDtypeStruct((nb, N_STATE), F32), jax.ShapeDtypeStruct((nb, N_STATE), F32)],
        scratch_shapes=scratch,
        compiler_params=_params("arbitrary"),
    )(u, h0_re, h0_im, ab_re, ab_im, bb_re, bb_im, cc_re, cc_im, d, glu_w, glu_b)


def _merge_rows(o_nsa, o_ssm, x, gg_ref, wout_ref, ng_ref, xq_ref):
    gg = gg_ref[...]
    a = _rms(o_nsa, gg[:, :NSA_WIDTH])
    b = _rms(o_ssm, gg[:, NSA_WIDTH:])
    m = (jnp.dot(a.astype(BF16), wout_ref[:NSA_WIDTH], preferred_element_type=F32)
         + jnp.dot(b.astype(BF16), wout_ref[NSA_WIDTH:], preferred_element_type=F32))
    x1 = x + _rms(m, ng_ref[1:2])
    return x1, _bdot(_rms(x1, ng_ref[2:3]), xq_ref[...])


def _cross_rows(qx, mkv_ref):
    outs = []
    for h in range(MEM_HEADS):
        cols = slice(h * MEM_HEAD_DIM, (h + 1) * MEM_HEAD_DIM)
        k = mkv_ref[0, :, cols]
        v = mkv_ref[0, :, MEM_WIDTH + h * MEM_HEAD_DIM: MEM_WIDTH + (h + 1) * MEM_HEAD_DIM]
        s = _bdot_nt(qx[:, cols], k) * MEM_SCALE
        e = jnp.exp(s - jnp.max(s, axis=-1, keepdims=True))
        p = e * (1.0 / jnp.sum(e, axis=-1, keepdims=True))
        outs.append(_bdot(p, v))
    return jnp.concatenate(outs, axis=1)


def _ffn_rows(x1, oa, ng_ref, xo_ref, wi_ref, wo_ref, d_ff):
    x2 = x1 + _rms(_bdot(oa, xo_ref[...]), ng_ref[3:4])
    h = _rms(x2, ng_ref[4:5]).astype(BF16)
    chunk = d_ff // FFN_CHUNKS
    y = None
    for c in range(0, d_ff, chunk):
        z1 = jnp.dot(h, wi_ref[:, c:c + chunk], preferred_element_type=F32)
        z2 = jnp.dot(h, wi_ref[:, d_ff + c:d_ff + c + chunk], preferred_element_type=F32)
        part = _bdot(z1 * jax.nn.sigmoid(z1) * z2, wo_ref[c:c + chunk, :])
        y = part if y is None else y + part
    return x2 + _rms(y, ng_ref[5:6])


def _tail_prompt_kernel(onsa_ref, ossm_ref, x_ref, mkv_ref, gg_ref, wout_ref, ng_ref, xq_ref, xo_ref, wi_ref,
                        wo_ref, o_ref, *, d_ff):
    x1, qx = _merge_rows(onsa_ref[...], ossm_ref[...], x_ref[...], gg_ref, wout_ref, ng_ref, xq_ref)
    o_ref[...] = _ffn_rows(x1, _cross_rows(qx, mkv_ref), ng_ref, xo_ref, wi_ref, wo_ref, d_ff)


def _tail_prompt(o_nsa, o_ssm, x, mkv, gg, w_out, ng, xq, xo, wi, wo, n_seq, tile):
    rows = x.shape[0]
    nt = rows // n_seq // tile
    d_ff = wo.shape[0]
    assert d_ff % (FFN_CHUNKS * LANES) == 0
    row_map = lambda b, t: (b * nt + t, 0)
    const = lambda b, t: (0, 0)
    once = dict(pipeline_mode=pl.Buffered(1))
    return pl.pallas_call(
        functools.partial(_tail_prompt_kernel, d_ff=d_ff),
        grid=(n_seq, nt),
        in_specs=[pl.BlockSpec((tile, NSA_WIDTH), row_map),
                  pl.BlockSpec((tile, SSM_WIDTH), lambda b, t: (t, b)),
                  pl.BlockSpec((tile, D_MODEL), row_map),
                  pl.BlockSpec((1,) + mkv.shape[1:], lambda b, t: (b, 0, 0)),
                  pl.BlockSpec((1, D_MODEL), const),
                  pl.BlockSpec((D_MODEL, D_MODEL), const, **once),
                  pl.BlockSpec((6, D_MODEL), const),
                  pl.BlockSpec((D_MODEL, MEM_WIDTH), const, **once),
                  pl.BlockSpec((MEM_WIDTH, D_MODEL), const, **once),
                  pl.BlockSpec((D_MODEL, 2 * d_ff), const, **once),
                  pl.BlockSpec((d_ff, D_MODEL), const, **once)],
        out_specs=pl.BlockSpec((tile, D_MODEL), row_map),
        out_shape=jax.ShapeDtypeStruct((rows, D_MODEL), F32),
        compiler_params=_params("arbitrary", "arbitrary"),
    )(o_nsa, o_ssm, x, mkv, gg, w_out, ng, xq, xo, wi, wo)


def _merge_kernel(onsa_ref, ossm_ref, x_ref, gg_ref, wout_ref, ng_ref, xq_ref, x1_ref, qx_ref):
    x1_ref[...], qx_ref[...] = _merge_rows(onsa_ref[...], ossm_ref[...], x_ref[...], gg_ref, wout_ref, ng_ref,
                                           xq_ref)


def _merge(o_nsa, o_ssm, x, gg, w_out, ng, xq, n_seq, tile):
    rows = x.shape[0]
    nt = rows // n_seq // tile
    row_map = lambda b, t: (b * nt + t, 0)
    const = lambda b, t: (0, 0)
    return pl.pallas_call(
        _merge_kernel,
        grid=(n_seq, nt),
        in_specs=[pl.BlockSpec((tile, NSA_WIDTH), row_map),
                  pl.BlockSpec((tile, SSM_WIDTH), lambda b, t: (t, b)),
                  pl.BlockSpec((tile, D_MODEL), row_map),
                  pl.BlockSpec((1, D_MODEL), const),
                  pl.BlockSpec((D_MODEL, D_MODEL), const),
                  pl.BlockSpec((6, D_MODEL), const),
                  pl.BlockSpec((D_MODEL, MEM_WIDTH), const)],
        out_specs=[pl.BlockSpec((tile, D_MODEL), row_map), pl.BlockSpec((tile, MEM_WIDTH), row_map)],
        out_shape=[jax.ShapeDtypeStruct((rows, D_MODEL), F32),
                   jax.ShapeDtypeStruct((rows, MEM_WIDTH), F32)],
        compiler_params=_params("arbitrary", "arbitrary"),
    )(o_nsa, o_ssm, x, gg, w_out, ng, xq)


def _norm_matmul_kernel(x_ref, g_ref, w_ref, o_ref):
    o_ref[...] = _bdot(_rms(x_ref[...], g_ref[...]), w_ref[...])


def _norm_matmul(x, g, w, tile):
    rows, k = x.shape
    n = w.shape[1]
    return pl.pallas_call(
        _norm_matmul_kernel,
        grid=(rows // tile,),
        in_specs=[pl.BlockSpec((tile, k), lambda i: (i, 0)),
                  pl.BlockSpec((1, k), lambda i: (0, 0)),
                  pl.BlockSpec((k, n), lambda i: (0, 0))],
        out_specs=pl.BlockSpec((tile, n), lambda i: (i, 0)),
        out_shape=jax.ShapeDtypeStruct((rows, n), F32),
        compiler_params=_params("arbitrary"),
    )(x, g, w)


SAMPLE_GROUP = 8


def _cross_sample_kernel(qx_ref, mkv_ref, o_ref, *, n_mem):
    per_tok = 2 * MEM_HEADS
    sub = _iota2((SAMPLE_GROUP, MEM_HEAD_DIM), 0)
    qx = qx_ref[...]
    outs = []
    for i in range(SAMPLE_GROUP):
        s = jnp.zeros((SAMPLE_GROUP, n_mem), F32)
        for h in range(MEM_HEADS):
            qh = jnp.broadcast_to(qx[i:i + 1, h * MEM_HEAD_DIM:(h + 1) * MEM_HEAD_DIM], sub.shape)
            k = mkv_ref[i, pl.ds(h, n_mem, stride=per_tok), :]
            s = s + _bdot_nt(jnp.where(sub == h, qh, 0.0), k)
        s = s * MEM_SCALE
        e = jnp.exp(s - jnp.max(s, axis=-1, keepdims=True))
        p = e * (1.0 / jnp.sum(e, axis=-1, keepdims=True))
        heads = []
        for h in range(MEM_HEADS):
            v = mkv_ref[i, pl.ds(MEM_HEADS + h, n_mem, stride=per_tok), :]
            heads.append(_bdot(p, v)[h:h + 1])
        outs.append(jnp.concatenate(heads, axis=1))
    o_ref[...] = jnp.concatenate(outs, axis=0)


def _cross_sample(qx, mkv, n_mem):
    rows = qx.shape[0]
    return pl.pallas_call(
        functools.partial(_cross_sample_kernel, n_mem=n_mem),
        grid=(rows // SAMPLE_GROUP,),
        in_specs=[pl.BlockSpec((SAMPLE_GROUP, MEM_WIDTH), lambda i: (i, 0)),
                  pl.BlockSpec((SAMPLE_GROUP,) + mkv.shape[1:], lambda i: (i, 0, 0))],
        out_specs=pl.BlockSpec((SAMPLE_GROUP, MEM_WIDTH), lambda i: (i, 0)),
        out_shape=jax.ShapeDtypeStruct((rows, MEM_WIDTH), F32),
        compiler_params=_params("arbitrary"),
    )(qx, mkv)


def _ffn_kernel(x1_ref, oa_ref, ng_ref, xo_ref, wi_ref, wo_ref, o_ref, *, d_ff):
    o_ref[...] = _ffn_rows(x1_ref[...], oa_ref[...], ng_ref, xo_ref, wi_ref, wo_ref, d_ff)


def _ffn(x1, oa, ng, xo, wi, wo, tile):
    rows = x1.shape[0]
    d_ff = wo.shape[0]
    assert d_ff % (FFN_CHUNKS * LANES) == 0
    const = lambda i: (0, 0)
    once = dict(pipeline_mode=pl.Buffered(1))
    return pl.pallas_call(
        functools.partial(_ffn_kernel, d_ff=d_ff),
        grid=(rows // tile,),
        in_specs=[pl.BlockSpec((tile, D_MODEL), lambda i: (i, 0)),
                  pl.BlockSpec((tile, MEM_WIDTH), lambda i: (i, 0)),
                  pl.BlockSpec((6, D_MODEL), const),
                  pl.BlockSpec((MEM_WIDTH, D_MODEL), const, **once),
                  pl.BlockSpec((D_MODEL, 2 * d_ff), const, **once),
                  pl.BlockSpec((d_ff, D_MODEL), const, **once)],
        out_specs=pl.BlockSpec((tile, D_MODEL), lambda i: (i, 0)),
        out_shape=jax.ShapeDtypeStruct((rows, D_MODEL), F32),
        compiler_params=_params("arbitrary"),
    )(x1, oa, ng, xo, wi, wo)


PAGES_PER_STEP = 32
FOLD_PAGES = 8
NEW_TILE = 8
CMP_GROUP = 16


def _bias_rows(dist, tblt_ref):
    v = jnp.broadcast_to(tblt_ref[:, 0:1], dist.shape)
    for k in range(1, N_BUCKETS):
        v = jnp.where(dist >= _BUCKET_START[k], tblt_ref[:, k:k + 1], v)
    return v


def _sample_bias_kernel(tblt_ref, bw_ref, bc_ref, bn_ref, bs_ref, b0_ref, *, past_len):
    dist = (WINDOW - 1) - _iota2(bw_ref.shape, 1)
    bw_ref[...] = jnp.where((dist >= 0) & (dist < WINDOW), _bias_rows(dist, tblt_ref), NEG)
    dist = past_len - (_iota2(bc_ref.shape, 1) * CMP_BLOCK + (CMP_BLOCK - 1))
    bc_ref[...] = jnp.where(dist >= 0, _bias_rows(dist, tblt_ref), NEG)
    lane = _iota2(bn_ref.shape, 1)
    dist = past_len - ((past_len // CMP_BLOCK + lane) * CMP_BLOCK + (CMP_BLOCK - 1))
    bn_ref[...] = jnp.where((dist >= 0) & (lane < SEL_BLOCK // CMP_BLOCK),
                            _bias_rows(jnp.maximum(dist, 0), tblt_ref), NEG)
    page = lax.broadcasted_iota(I32, bs_ref.shape, 0)
    dist = past_len - page * PAGE_SIZE - lax.broadcasted_iota(I32, bs_ref.shape, 2)
    bs_ref[...] = _bias_rows(dist, tblt_ref)
    b0_ref[...] = _bias_rows(jnp.zeros(b0_ref.shape, I32), tblt_ref)


def _sample_bias(rel_bias, past_len):
    vm = pl.BlockSpec(memory_space=pltpu.VMEM)
    shapes = [(N_HEADS, WINDOW), (N_HEADS, past_len // CMP_BLOCK), (N_HEADS, LANES),
              (past_len // PAGE_SIZE, N_HEADS, PAGE_SIZE), (N_HEADS, LANES)]
    return pl.pallas_call(
        functools.partial(_sample_bias_kernel, past_len=past_len),
        in_specs=[vm], out_specs=[vm] * len(shapes),
        out_shape=[jax.ShapeDtypeStruct(s, F32) for s in shapes],
    )(rel_bias.T)


def _compress_sample_kernel(pt_ref, pool_ref, post_ref, fold_ref, projt_ref, o_ref, buf, sem, *, n_steps):
    i = pl.program_id(0)
    slot = i % 2

    def page_copies(step, slot):
        return [pltpu.make_async_copy(pool_ref.at[pt_ref[step * PAGES_PER_STEP + j]], buf.at[slot, j], sem.at[slot])
                for j in range(PAGES_PER_STEP)]

    @pl.when(i == 0)
    def _():
        for c in page_copies(0, 0):
            c.start()

    @pl.when(i + 1 < n_steps)
    def _():
        for c in page_copies(i + 1, 1 - slot):
            c.start()

    for c in page_copies(i, slot):
        c.wait()

    post = post_ref[...]
    sums_t = None
    for c in range(0, PAGES_PER_STEP, FOLD_PAGES):
        weighted = jnp.concatenate([(buf[slot, j] * post).astype(BF16) for j in range(c, c + FOLD_PAGES)], axis=1)
        part = jnp.dot(weighted, fold_ref[c * PAGE_SIZE:(c + FOLD_PAGES) * PAGE_SIZE, :],
                       preferred_element_type=F32)
        sums_t = part if sums_t is None else sums_t + part
    o_ref[0] = _dot3(projt_ref[...], sums_t)


def _compress_sample(pool_t, page_table_flat, n_seq, n_pages, pos, proj):
    steps = n_pages // PAGES_PER_STEP
    blocks_per_page = PAGE_SIZE // CMP_BLOCK
    blocks_per_step = PAGES_PER_STEP * blocks_per_page
    assert blocks_per_step == LANES
    pos_t = jnp.tile(pos.T, (1, blocks_per_page))
    row = np.arange(PAGES_PER_STEP * PAGE_SIZE)
    fold = jnp.asarray(row[:, None] // CMP_BLOCK == np.arange(blocks_per_step)[None, :], dtype=BF16)

    const = lambda i, pt: (0, 0)
    grid_spec = pltpu.PrefetchScalarGridSpec(
        num_scalar_prefetch=1,
        grid=(n_seq * steps,),
        in_specs=[pl.BlockSpec(memory_space=pl.ANY),
                  pl.BlockSpec((KV_COLS, PAGE_SIZE), const),
                  pl.BlockSpec(fold.shape, const),
                  pl.BlockSpec((KV_COLS, KV_COLS), const)],
        out_specs=pl.BlockSpec((1, KV_COLS, blocks_per_step), lambda i, pt: (i // steps, 0, i % steps)),
        scratch_shapes=[pltpu.VMEM((2, PAGES_PER_STEP, KV_COLS, PAGE_SIZE), F32),
                        pltpu.SemaphoreType.DMA((2,))],
    )
    return pl.pallas_call(
        functools.partial(_compress_sample_kernel, n_steps=n_seq * steps),
        grid_spec=grid_spec,
        out_shape=jax.ShapeDtypeStruct((n_seq, KV_COLS, steps * blocks_per_step), F32),
        compiler_params=_params("arbitrary"),
    )(page_table_flat, pool_t, pos_t, fold, proj.T)


def _cmp_sample_kernel(qm_ref, blk_ref, kvc_ref, pos_ref, proj_ref, bc_ref, bn_ref, oc_ref, idx_ref,
                       *, past_len):
    n_cmp = blk_ref.shape[2]
    bias = jnp.concatenate([bc_ref[...]] * CMP_GROUP, axis=0)
    mask = bias > MASKED_BELOW
    bias_n = jnp.concatenate([bn_ref[:, :NEW_TILE]] * CMP_GROUP, axis=0)
    mask_n = bias_n > MASKED_BELOW
    new_blk = _dot3(jnp.concatenate([kvc_ref[i] for i in range(CMP_GROUP)], axis=0) * pos_ref[0:1], proj_ref[...])
    first_row = _iota2((NEW_TILE, KV_COLS), 0) == 0
    new_blks = [jnp.where(first_row, jnp.broadcast_to(new_blk[i:i + 1], first_row.shape), 0.0)
                for i in range(CMP_GROUP)]
    s = jnp.concatenate([_dot3(qm_ref[i], blk_ref[i, :KV_HALF, :]) for i in range(CMP_GROUP)], axis=0)
    s_n = jnp.concatenate([_dot3_nt(qm_ref[i], new_blks[i][:, :KV_HALF]) for i in range(CMP_GROUP)], axis=0)
    s = jnp.where(mask, s * SCALE + bias, NEG)
    s_n = jnp.where(mask_n, s_n * SCALE + bias_n, NEG)
    m = jnp.maximum(jnp.max(s, axis=-1, keepdims=True), jnp.max(s_n, axis=-1, keepdims=True))
    e = jnp.where(mask, jnp.exp(s - m), 0.0)
    e_n = jnp.where(mask_n, jnp.exp(s_n - m), 0.0)
    inv = 1.0 / jnp.maximum(jnp.sum(e, axis=-1, keepdims=True) + jnp.sum(e_n, axis=-1, keepdims=True), TINY)
    p = e * inv
    p_n = e_n * inv
    group_ps = []
    for i in range(CMP_GROUP):
        rows = slice(i * N_HEADS, (i + 1) * N_HEADS)
        oc_ref[i] = _bdot_nt(p[rows], blk_ref[i, KV_HALF:, :]) + _bdot(p_n[rows], new_blks[i][:, KV_HALF:])
        group_ps += [jnp.sum(p[i * N_HEADS + g * GQA:i * N_HEADS + (g + 1) * GQA], axis=0, keepdims=True)
                     for g in range(N_KV)]

    ps = jnp.concatenate(group_ps, axis=0)
    lane = _iota2(ps.shape, 1)
    pair = ps + pltpu.roll(ps, n_cmp - 1, 1)
    cur = past_len // SEL_BLOCK
    j = lane // (SEL_BLOCK // CMP_BLOCK)
    forced = (j == 0) | (j == cur) | (j == cur - 1)
    score = jnp.where(lane % 2 == 0, jnp.where(forced, pair + FORCE, pair), -jnp.inf)
    slot = _iota2((ps.shape[0], LANES), 1)
    ids = jnp.zeros((ps.shape[0], LANES), I32)
    for it in range(N_SEL - 1):
        top = jnp.max(score, axis=-1, keepdims=True)
        first = jnp.min(jnp.where(score == top, lane, n_cmp), axis=-1, keepdims=True)
        ids = jnp.where(slot == it, first // (SEL_BLOCK // CMP_BLOCK), ids)
        score = jnp.where(lane == first, -jnp.inf, score)
    idx_ref[...] = jnp.where(slot == N_SEL - 1, cur, ids)


def _cmp_sample(qm, blocks_t, kvc_new, pos, proj, bc, bn, past_len):
    n_seq, _, n_cmp = blocks_t.shape
    assert past_len // SEL_BLOCK > N_SEL and past_len % SEL_BLOCK == 0
    const = lambda b: (0, 0)
    per_seq = lambda b: (b, 0, 0)
    return pl.pallas_call(
        functools.partial(_cmp_sample_kernel, past_len=past_len),
        grid=(n_seq // CMP_GROUP,),
        in_specs=[pl.BlockSpec((CMP_GROUP, N_HEADS, LANES), per_seq),
                  pl.BlockSpec((CMP_GROUP, KV_COLS, n_cmp), per_seq),
                  pl.BlockSpec((CMP_GROUP, 1, KV_COLS), per_seq),
                  pl.BlockSpec((CMP_BLOCK, KV_COLS), const),
                  pl.BlockSpec((KV_COLS, KV_COLS), const),
                  pl.BlockSpec((N_HEADS, n_cmp), const),
                  pl.BlockSpec((N_HEADS, LANES), const)],
        out_specs=[pl.BlockSpec((CMP_GROUP, N_HEADS, LANES), per_seq),
                   pl.BlockSpec((CMP_GROUP * N_KV, LANES), lambda b: (b, 0))],
        out_shape=[jax.ShapeDtypeStruct((n_seq, N_HEADS, LANES), F32),
                   jax.ShapeDtypeStruct((n_seq * N_KV, LANES), I32)],
        compiler_params=_params("arbitrary"),
    )(qm, blocks_t, kvc_new, pos, proj, bc, bn)


def _sel_win_sample_kernel(idx_ref, pt_ref, pool_ref, buf_ref, qm_ref, kvs_ref, kvw_ref, gate_ref, oc_ref, bw_ref,
                           bs_ref, b0_ref, o_ref, nbuf_ref, pages, sem, *, n_past_blocks, n_seq):
    b = pl.program_id(0)
    slot = b % 2
    bpp = PAGE_SIZE // SEL_BLOCK

    def page_copies(seq, slot):
        return [pltpu.make_async_copy(pool_ref.at[pt_ref[(seq * N_KV + g) * N_SEL + k], :, g],
                                      pages.at[slot, g * N_SEL + k], sem.at[slot])
                for g in range(N_KV) for k in range(N_SEL)]

    @pl.when(b == 0)
    def _():
        for c in page_copies(0, 0):
            c.start()

    @pl.when(b + 1 < n_seq)
    def _():
        for c in page_copies(b + 1, 1 - slot):
            c.start()

    qm = qm_ref[0]

    buf = buf_ref[0]
    shifted = pltpu.roll(buf, WINDOW - 1, 1)
    nbuf = jnp.where(_iota2(buf.shape, 1) == WINDOW - 1, jnp.broadcast_to(kvw_ref[0], buf.shape), shifted)
    nbuf_ref[0] = nbuf
    bias = bw_ref[...]
    p = _masked_softmax(_bdot(qm, nbuf[:KV_HALF]) * SCALE + bias, bias > MASKED_BELOW)
    o_win = _bdot_nt(p, nbuf[KV_HALF:])

    for c in page_copies(b, slot):
        c.wait()
    lane_half = _iota2((GQA, PAGE_SIZE), 1) // SEL_BLOCK
    o_sel = []
    for g in range(N_KV):
        cols = slice(g * HEAD_DIM, (g + 1) * HEAD_DIM)
        rows = slice(g * GQA, (g + 1) * GQA)
        qg = qm[rows, cols]
        ks, vs, biases = [], [], []
        has_new = False
        for k in range(N_SEL):
            i = idx_ref[(b * N_KV + g) * N_SEL + k]
            is_past = i < n_past_blocks
            has_new = jnp.logical_or(has_new, jnp.logical_not(is_past))
            i = jnp.minimum(i, n_past_blocks - 1)
            ks.append(pages[slot, g * N_SEL + k, 0])
            vs.append(pages[slot, g * N_SEL + k, 1])
            tile = bs_ref[i // bpp]
            biases.append(jnp.where(is_past & (lane_half == i % bpp), tile[rows], NEG))
        bias = jnp.concatenate(biases, axis=1)
        mask = bias > MASKED_BELOW
        s = jnp.where(mask, _bdot(qg, jnp.concatenate(ks, axis=1)) * SCALE + bias, NEG)
        k_new = kvs_ref[0][:, cols]
        v_new = kvs_ref[0][:, KV_HALF + g * HEAD_DIM:KV_HALF + (g + 1) * HEAD_DIM]
        s_new = jnp.where(has_new,
                          jnp.sum(qg * k_new, axis=-1, keepdims=True) * SCALE + b0_ref[rows, 0:1], NEG)
        m = jnp.maximum(jnp.max(s, axis=-1, keepdims=True), s_new)
        e = jnp.where(mask, jnp.exp(s - m), 0.0)
        e_new = jnp.where(has_new, jnp.exp(s_new - m), 0.0)
        inv = 1.0 / jnp.maximum(jnp.sum(e, axis=-1, keepdims=True) + e_new, TINY)
        o_sel.append(_bdot_nt(e * inv, jnp.concatenate(vs, axis=1)) + (e_new * inv) * v_new)
    o_sel = jnp.concatenate(o_sel, axis=0)

    first_group = _iota2((N_HEADS, HEAD_DIM), 0) < GQA
    oc = oc_ref[0]
    o_cmp = jnp.where(first_group, oc[:, :HEAD_DIM], oc[:, HEAD_DIM:])
    o_win = jnp.where(first_group, o_win[:, :HEAD_DIM], o_win[:, HEAD_DIM:])
    gate = gate_ref[0]
    o_ref[0] = gate[:, 0:1] * o_cmp + gate[:, 1:2] * o_sel + gate[:, 2:3] * o_win


def _sel_win_sample(idx_flat, pt_flat, pool_t, buf_t, qm, kvs_new, kvw_new_col, gates, o_cmp, bw, bs, b0,
                    n_pages):
    n_seq = buf_t.shape[0]
    bpp = PAGE_SIZE // SEL_BLOCK
    n_past_blocks = n_pages * bpp

    per_seq3 = lambda b, idx, pt: (b, 0, 0)
    const2 = lambda b, idx, pt: (0, 0)
    head_tile = pl.BlockSpec((1, N_HEADS, LANES), per_seq3)
    grid_spec = pltpu.PrefetchScalarGridSpec(
        num_scalar_prefetch=2,
        grid=(n_seq,),
        in_specs=[pl.BlockSpec(memory_space=pl.ANY),
                  pl.BlockSpec((1, KV_COLS, WINDOW), per_seq3), head_tile,
                  pl.BlockSpec((1, 1, KV_COLS), per_seq3), pl.BlockSpec((1, KV_COLS, 1), per_seq3), head_tile,
                  head_tile,
                  pl.BlockSpec((N_HEADS, WINDOW), const2),
                  pl.BlockSpec((n_pages, N_HEADS, PAGE_SIZE), lambda b, idx, pt: (0, 0, 0)),
                  pl.BlockSpec((N_HEADS, LANES), const2)],
        out_specs=[pl.BlockSpec((1, N_HEADS, HEAD_DIM), per_seq3),
                   pl.BlockSpec((1, KV_COLS, WINDOW), per_seq3)],
        scratch_shapes=[pltpu.VMEM((2, N_KV * N_SEL, 2, HEAD_DIM, PAGE_SIZE), F32),
                        pltpu.SemaphoreType.DMA((2,))],
    )
    return pl.pallas_call(
        functools.partial(_sel_win_sample_kernel, n_past_blocks=n_past_blocks, n_seq=n_seq),
        grid_spec=grid_spec,
        out_shape=[jax.ShapeDtypeStruct((n_seq, N_HEADS, HEAD_DIM), F32),
                   jax.ShapeDtypeStruct(buf_t.shape, F32)],
        compiler_params=_params("arbitrary"),
    )(idx_flat, pt_flat, pool_t, buf_t, qm, kvs_new, kvw_new_col, gates, o_cmp, bw, bs, b0)


PROJ_TILE = 512
ROW_TILE = 256
TAIL_TILE = 512
SCAN_STEPS = 128


def kernel(x_prompt, x_sample, cache_kv_cmp, cache_kv_sel, cache_kv_win, state_ssm_re, state_ssm_im,
           cache_mem_kv, page_table, mem_prompt, w_in, w_out, norm_g, grp_norm_g, cmp_pos, cmp_proj,
           rel_bias, ssm_a_re, ssm_a_im, ssm_b_re, ssm_b_im, ssm_c_re, ssm_c_im, ssm_d, ssm_log_dt,
           glu_w, glu_b, mem_norm_g, xq, xkv, xo, ffn_wi, ffn_wo):
    depth = w_in.shape[0]
    n_seq, seq, _ = x_prompt.shape
    n_dec, dec_seq, _ = x_sample.shape
    assert dec_seq == 1, "the sample kernels handle one new token per sequence"
    n_pages = page_table.shape[1]
    past_len = n_pages * PAGE_SIZE
    n_mem = mem_prompt.shape[1]
    assert cache_kv_win.shape[2] == WINDOW and seq >= WINDOW and WINDOW % Q_TILE == 0
    win_tiles = WINDOW // Q_TILE

    bcmp, near = _prompt_bias(rel_bias, seq)
    bw_s, bc_s, bn_s, bs_s, b0_s = _sample_bias(rel_bias, past_len)
    pt_flat = page_table.reshape(-1)
    kv5 = (2, N_KV, HEAD_DIM)

    xp = x_prompt.reshape(n_seq * seq, D_MODEL)
    xs = x_sample.reshape(n_dec, D_MODEL)
    outs = [[] for _ in range(11)]
    for l in range(depth):
        ng = norm_g[l]
        w_row, w_t = _pack_w_prompt(w_in[l])
        w_sample = _pack_w_sample(w_in[l])
        w_out_b, xq_b, xkv_b, xo_b = (w[l].astype(BF16) for w in (w_out, xq, xkv, xo))
        wi_b, wo_b, glu_w_b = ffn_wi[l].astype(BF16), ffn_wo[l].astype(BF16), glu_w[l].astype(BF16)
        gg = grp_norm_g[l][None]
        ssm_w = _ssm_weights(ssm_a_re[l], ssm_a_im[l], ssm_b_re[l], ssm_b_im[l], ssm_c_re[l], ssm_c_im[l],
                             ssm_log_dt[l])
        ssm_tail = (ssm_d[l][None], glu_w_b, glu_b[l][None])
        pos, proj_g = _pack_compress(cmp_pos[l], cmp_proj[l], group_major=True)
        _, proj_c = _pack_compress(cmp_pos[l], cmp_proj[l], group_major=False)

        (kvc, ks_aug, kw_aug, u, q_t, kvc_t, kvs_t, kvw_t, vs_t, vw_t, gates_t) = _inproj_prompt(
            xp, ng[0:1], w_row, w_t, n_seq, PROJ_TILE)
        kblk, vblk_t = _compress_prompt(kvc.reshape(n_seq, seq, KV_COLS), pos, proj_g)
        kw_aug = jnp.pad(kw_aug.reshape(n_seq, seq, N_KV * LANES), ((0, 0), (WINDOW, 0), (0, 0)))
        vw_t = jnp.pad(vw_t, ((0, 0), (win_tiles, 0), (0, 0), (0, 0)))
        ks_aug = jnp.pad(ks_aug.reshape(n_seq, seq, N_KV * LANES), ((0, 0), (Q_TILE, 0), (0, 0)))
        vs_t = jnp.pad(vs_t, ((0, 0), (1, 0), (0, 0), (0, 0)))
        o_nsa = _nsa_prompt(rel_bias, q_t, gates_t, kblk, vblk_t, bcmp, near, ks_aug, vs_t, kw_aug, vw_t)
        zeros = jnp.zeros((n_seq, N_STATE), F32)
        o_ssm, h_re, h_im = _ssm(u, zeros, zeros, ssm_w, *ssm_tail, n_seq, SCAN_STEPS)
        mkv = _norm_matmul(mem_prompt.reshape(n_seq * n_mem, D_MODEL), mem_norm_g[l][None], xkv_b, ROW_TILE)
        xp = _tail_prompt(o_nsa.reshape(n_seq * seq, NSA_WIDTH), o_ssm, xp,
                          mkv.reshape(n_seq, n_mem, 2 * MEM_WIDTH), gg, w_out_b, ng, xq_b, xo_b, wi_b, wo_b,
                          n_seq, TAIL_TILE)
        outs[0].append(_rows_minor(kvc_t))
        outs[1].append(_rows_minor(kvs_t))
        outs[2].append(_rows_minor(kvw_t[:, :, seq - WINDOW:]))
        outs[3].append(h_re.reshape(n_seq, N_SSM_GROUPS, SSM_STATE))
        outs[4].append(h_im.reshape(n_seq, N_SSM_GROUPS, SSM_STATE))
        outs[5].append(mkv.reshape(n_seq, n_mem, 2, MEM_HEADS, MEM_HEAD_DIM))

        q, kv, gates, u = _inproj_sample(xs, ng[0:1], w_sample)
        kvc, kvs, kvw = (kv[:, i * KV_COLS:(i + 1) * KV_COLS] for i in range(3))
        head_group = (jnp.arange(N_HEADS)[:, None] // GQA == jnp.arange(N_KV)[None, :]).astype(F32)
        qm = (q.reshape(n_dec, N_HEADS, 1, HEAD_DIM) * head_group[None, :, :, None]).reshape(n_dec, N_HEADS, LANES)
        gates_h = gates[:, :N_HEADS * N_BRANCH].reshape(n_dec, N_HEADS, N_BRANCH)
        gates_h = jnp.pad(gates_h, ((0, 0), (0, 0), (0, LANES - N_BRANCH)))
        blocks_t = _compress_sample(_rows_minor_view(cache_kv_cmp[l]), pt_flat, n_dec, n_pages, pos, proj_c)
        o_cmp, ids = _cmp_sample(qm, blocks_t, kvc.reshape(n_dec, 1, KV_COLS), pos, proj_c, bc_s, bn_s, past_len)
        ids = ids[:, :N_SEL]
        past_page = jnp.minimum(ids, past_len // SEL_BLOCK - 1) // (PAGE_SIZE // SEL_BLOCK)
        sel_pages = jnp.take_along_axis(jnp.repeat(page_table, N_KV, axis=0), past_page, axis=1)
        pool_sel = cache_kv_sel[l].transpose(0, 2, 3, 4, 1)
        o_nsa, new_buf_t = _sel_win_sample(
            ids.reshape(-1), sel_pages.reshape(-1), pool_sel, _rows_minor_view(cache_kv_win[l]), qm, kvs.reshape(n_dec, 1, KV_COLS),
            kvw.reshape(n_dec, KV_COLS, 1), gates_h, o_cmp, bw_s, bs_s, b0_s, n_pages)
        o_ssm, h_re, h_im = _ssm(u, state_ssm_re[l].reshape(n_dec, N_STATE), state_ssm_im[l].reshape(n_dec, N_STATE),
                                 ssm_w, *ssm_tail, n_dec, 1)
        x1, qx = _merge(o_nsa.reshape(n_dec, NSA_WIDTH), o_ssm, xs, gg, w_out_b, ng, xq_b, 1, n_dec)
        oa = _cross_sample(qx, cache_mem_kv[l].reshape(n_dec, n_mem * 2 * MEM_HEADS, MEM_HEAD_DIM), n_mem)
        xs = _ffn(x1, oa, ng, xo_b, wi_b, wo_b, n_dec)
        outs[6].append(kvc.reshape(n_dec, 1, *kv5))
        outs[7].append(kvs.reshape(n_dec, 1, *kv5))
        outs[8].append(_rows_minor(new_buf_t))
        outs[9].append(h_re.reshape(n_dec, N_SSM_GROUPS, SSM_STATE))
        outs[10].append(h_im.reshape(n_dec, N_SSM_GROUPS, SSM_STATE))

    stacked = [jnp.stack(o, axis=0) for o in outs]
    return (xp.reshape(x_prompt.shape), xs.reshape(x_sample.shape), *stacked)
```

```python
import functools
import math

import numpy as np
import jax
import jax.numpy as jnp
from jax import lax
from jax.experimental import pallas as pl
from jax.experimental.pallas import tpu as pltpu

F32 = jnp.float32
BF16 = jnp.bfloat16
I32 = jnp.int32

D_MODEL = 1024
HEAD_DIM = 64
N_HEADS = 8
N_KV = 2
GQA = N_HEADS // N_KV
NSA_WIDTH = N_HEADS * HEAD_DIM
SSM_WIDTH = 512
KV_COLS = 2 * N_KV * HEAD_DIM
KV_HALF = N_KV * HEAD_DIM
N_BRANCH = 3
CMP_BLOCK = 32
SEL_BLOCK = 64
N_SEL = 16
WINDOW = 512
PAGE_SIZE = 128
SSM_GROUP = 16
N_SSM_GROUPS = 32
SSM_STATE = 64
N_STATE = N_SSM_GROUPS * SSM_STATE
N_BUCKETS = 32
MAX_DISTANCE = 128
MEM_HEADS = 4
MEM_HEAD_DIM = 128
MEM_WIDTH = MEM_HEADS * MEM_HEAD_DIM
EPS = 1e-6
NEG = -1e30
MASKED_BELOW = -1e29
TINY = 1e-30
FORCE = 1e4
SCALE = HEAD_DIM ** -0.5
MEM_SCALE = MEM_HEAD_DIM ** -0.5

LANES = 128
Q_TILE = 256
FAR_TILES = 2
FFN_CHUNKS = 2
VMEM_LIMIT = 56 * 1024 * 1024

AUG_SEL = HEAD_DIM
AUG_ONE = AUG_SEL + 32
GATE_ROWS = 16
VT_ROWS = HEAD_DIM + 16
LOG2E = math.log2(math.e)


def _bucket_starts():
    n = np.arange(0, 4 * MAX_DISTANCE)
    exact = N_BUCKETS // 2
    nf = np.maximum(n, exact).astype(np.float32)
    big = exact + (np.log(nf / exact) / np.float32(math.log(MAX_DISTANCE / exact))
                   * (N_BUCKETS - exact)).astype(np.int32)
    bucket = np.where(n < exact, n, np.minimum(big, N_BUCKETS - 1))
    return [int(np.argmax(bucket >= k)) for k in range(N_BUCKETS)]


_BUCKET_START = _bucket_starts()
assert _BUCKET_START[-1] <= Q_TILE


def _params(*sem):
    return pltpu.CompilerParams(dimension_semantics=sem, vmem_limit_bytes=VMEM_LIMIT)


def _bdot(a, b):
    return jnp.dot(a.astype(BF16), b.astype(BF16), preferred_element_type=F32)


def _bdot_nt(a, b):
    return lax.dot_general(a.astype(BF16), b.astype(BF16), (((1,), (1,)), ((), ())),
                           preferred_element_type=F32)


def _split(a):
    hi = a.astype(BF16)
    return hi, (a - hi.astype(F32)).astype(BF16)


def _dot3(a, b):
    ah, al = _split(a)
    bh, bl = _split(b)
    d = functools.partial(jnp.dot, preferred_element_type=F32)
    return d(ah, bh) + (d(ah, bl) + d(al, bh))


def _dot3_nt(a, b):
    ah, al = _split(a)
    bh, bl = _split(b)
    d = functools.partial(lax.dot_general, dimension_numbers=(((1,), (1,)), ((), ())),
                          preferred_element_type=F32)
    return d(ah, bh) + (d(ah, bl) + d(al, bh))


def _rms(x, g):
    return x * lax.rsqrt(jnp.mean(x * x, axis=-1, keepdims=True) + EPS) * g


def _masked_softmax(s, mask, axis=-1):
    s = jnp.where(mask, s, NEG)
    m = jnp.max(s, axis=axis, keepdims=True)
    e = jnp.where(mask, jnp.exp(s - m), 0.0)
    return e * (1.0 / jnp.maximum(jnp.sum(e, axis=axis, keepdims=True), TINY))


def _bias_of_dist(dist, tbl_ref, h):
    v = jnp.full(dist.shape, tbl_ref[0, h], F32)
    for k in range(1, N_BUCKETS):
        v = jnp.where(dist >= _BUCKET_START[k], tbl_ref[k, h], v)
    return v


def _iota2(shape, dim):
    return lax.broadcasted_iota(I32, shape, dim)


def _rows_minor(a):
    n, _, rows = a.shape
    return a.reshape(n, 2, N_KV, HEAD_DIM, rows).transpose(0, 4, 1, 2, 3)


def _rows_minor_view(a):
    n, rows = a.shape[:2]
    return a.transpose(0, 2, 3, 4, 1).reshape(n, KV_COLS, rows)


def _cmp_block_of_row(row, n_cmp):
    return jnp.where(row < n_cmp // 2, 2 * row, 2 * (row - n_cmp // 2) + 1)


def _prompt_bias_kernel(tbl_ref, bcmp_ref, near_ref, *, n_cmp, nq):
    h = pl.program_id(0)
    last = tbl_ref[N_BUCKETS - 1, h]

    blk = _cmp_block_of_row(_iota2((n_cmp, Q_TILE), 0), n_cmp)
    for qi in range(nq):
        dist = qi * Q_TILE + _iota2((n_cmp, Q_TILE), 1) - (blk * CMP_BLOCK + (CMP_BLOCK - 1))
        bcmp_ref[qi, 0] = jnp.where(dist >= 0, _bias_of_dist(dist, tbl_ref, h), NEG)

    key = _iota2((Q_TILE, Q_TILE), 0)
    qry = _iota2((Q_TILE, Q_TILE), 1)
    d0 = qry - key
    near_ref[0, 0] = jnp.where(d0 >= 0, (_bias_of_dist(d0, tbl_ref, h) - last) * LOG2E, NEG)
    d1 = Q_TILE + qry - key
    near_ref[0, 1] = (_bias_of_dist(d1, tbl_ref, h) - last) * LOG2E
    near_ref[0, 2] = jnp.where(key > qry, 0.0, NEG)


def _prompt_bias(rel_bias, seq):
    nq = seq // Q_TILE
    n_cmp = seq // CMP_BLOCK
    return pl.pallas_call(
        functools.partial(_prompt_bias_kernel, n_cmp=n_cmp, nq=nq),
        grid=(N_HEADS,),
        in_specs=[pl.BlockSpec(memory_space=pltpu.SMEM)],
        out_specs=[pl.BlockSpec((nq, 1, n_cmp, Q_TILE), lambda h: (0, h // GQA, 0, h % GQA)),
                   pl.BlockSpec((1, 3, Q_TILE, Q_TILE), lambda h: (h // GQA, 0, 0, h % GQA))],
        out_shape=[jax.ShapeDtypeStruct((nq, N_KV, n_cmp, GQA * Q_TILE), F32),
                   jax.ShapeDtypeStruct((N_KV, 3, Q_TILE, GQA * Q_TILE), F32)],
        compiler_params=_params("arbitrary"),
    )(rel_bias)


OFF_KV = NSA_WIDTH
OFF_GATE = OFF_KV + 3 * KV_COLS
OFF_U = OFF_GATE + N_HEADS * N_BRANCH


def _k_cols(w_in, which):
    base = OFF_KV + which * KV_COLS
    return [w_in[:, base + g * HEAD_DIM: base + (g + 1) * HEAD_DIM] for g in range(N_KV)]


def _pack_w_prompt(w_in):
    w_row = jnp.concatenate([w_in[:, OFF_KV:OFF_KV + KV_COLS]] + _k_cols(w_in, 1) + _k_cols(w_in, 2)
                            + [w_in[:, OFF_U:]], axis=1)
    gates = w_in[:, OFF_GATE:OFF_U].reshape(-1, N_KV, GQA * N_BRANCH)
    gates = jnp.pad(gates, ((0, 0), (0, 0), (0, GATE_ROWS - GQA * N_BRANCH))).reshape(-1, N_KV * GATE_ROWS)
    w_t = jnp.concatenate([w_in[:, :OFF_GATE], gates], axis=1).T
    return w_row.astype(BF16), w_t.astype(BF16)


def _inproj_prompt_kernel(x_ref, g_ref, wr_ref, wt_ref, kvc_ref, ksa_ref, kwa_ref, u_ref, qt_ref, kvct_ref,
                          kvst_ref, kvwt_ref, vst_ref, vwt_ref, gt_ref, *, tile):
    hb = _rms(x_ref[...], g_ref[...]).astype(BF16)
    z = jnp.dot(hb, wr_ref[...], preferred_element_type=F32)
    kvc_ref[...] = z[:, :KV_COLS]
    u_ref[...] = z[:, KV_COLS + 4 * HEAD_DIM:]

    pos = pl.program_id(1) * tile + _iota2((tile, LANES - HEAD_DIM), 0)
    col = _iota2((tile, LANES - HEAD_DIM), 1)
    ones = ((col >= AUG_ONE - HEAD_DIM) & (col < AUG_ONE - HEAD_DIM + 2)).astype(F32)
    aug_sel = jnp.where(col == pos // SEL_BLOCK, -NEG, ones)
    k0 = KV_COLS
    ksa_ref[...] = jnp.concatenate(
        [z[:, k0:k0 + HEAD_DIM], aug_sel, z[:, k0 + HEAD_DIM:k0 + 2 * HEAD_DIM], aug_sel], axis=1).astype(BF16)
    k0 = KV_COLS + 2 * HEAD_DIM
    kwa_ref[...] = jnp.concatenate(
        [z[:, k0:k0 + HEAD_DIM], ones, z[:, k0 + HEAD_DIM:k0 + 2 * HEAD_DIM], ones], axis=1).astype(BF16)

    zt = lax.dot_general(wt_ref[...], hb, (((1,), (1,)), ((), ())), preferred_element_type=F32)
    qt_ref[0] = zt[:NSA_WIDTH]
    kvct_ref[0] = zt[OFF_KV:OFF_KV + KV_COLS]
    kvst_ref[0] = zt[OFF_KV + KV_COLS:OFF_KV + 2 * KV_COLS]
    kvwt_ref[0] = zt[OFF_KV + 2 * KV_COLS:OFF_GATE]
    gt_ref[0] = jax.nn.sigmoid(zt[OFF_GATE:])
    v_sel = OFF_KV + KV_COLS + KV_HALF
    v_win = OFF_KV + 2 * KV_COLS + KV_HALF
    sum_rows = (_iota2((VT_ROWS - HEAD_DIM, Q_TILE), 0) == 0).astype(F32)
    for j in range(tile // Q_TILE):
        cols = slice(j * Q_TILE, (j + 1) * Q_TILE)
        for ref, v0 in ((vst_ref, v_sel), (vwt_ref, v_win)):
            parts = []
            for g in range(N_KV):
                parts += [zt[v0 + g * HEAD_DIM:v0 + (g + 1) * HEAD_DIM, cols], sum_rows]
            ref[0, j] = jnp.concatenate(parts, axis=0).astype(BF16)


def _inproj_prompt(x, g, w_row, w_t, n_seq, tile):
    rows = x.shape[0]
    seq = rows // n_seq
    assert seq // SEL_BLOCK <= AUG_ONE - AUG_SEL
    nt = seq // tile
    row_map = lambda b, t: (b * nt + t, 0)
    t_map = lambda b, t: (b, 0, t)
    tiles = tile // Q_TILE
    n_gate = N_KV * GATE_ROWS
    out = [
        (pl.BlockSpec((tile, KV_COLS), row_map), jax.ShapeDtypeStruct((rows, KV_COLS), F32)),
        (pl.BlockSpec((tile, N_KV * LANES), row_map), jax.ShapeDtypeStruct((rows, N_KV * LANES), BF16)),
        (pl.BlockSpec((tile, N_KV * LANES), row_map), jax.ShapeDtypeStruct((rows, N_KV * LANES), BF16)),
        (pl.BlockSpec((tile, SSM_WIDTH), lambda b, t: (t, b)), jax.ShapeDtypeStruct((seq, n_seq * SSM_WIDTH), F32)),
        (pl.BlockSpec((1, NSA_WIDTH, tile), t_map), jax.ShapeDtypeStruct((n_seq, NSA_WIDTH, seq), F32)),
        (pl.BlockSpec((1, KV_COLS, tile), t_map), jax.ShapeDtypeStruct((n_seq, KV_COLS, seq), F32)),
        (pl.BlockSpec((1, KV_COLS, tile), t_map), jax.ShapeDtypeStruct((n_seq, KV_COLS, seq), F32)),
        (pl.BlockSpec((1, KV_COLS, tile), t_map), jax.ShapeDtypeStruct((n_seq, KV_COLS, seq), F32)),
        (pl.BlockSpec((1, tiles, N_KV * VT_ROWS, Q_TILE), lambda b, t: (b, t, 0, 0)),
         jax.ShapeDtypeStruct((n_seq, seq // Q_TILE, N_KV * VT_ROWS, Q_TILE), BF16)),
        (pl.BlockSpec((1, tiles, N_KV * VT_ROWS, Q_TILE), lambda b, t: (b, t, 0, 0)),
         jax.ShapeDtypeStruct((n_seq, seq // Q_TILE, N_KV * VT_ROWS, Q_TILE), BF16)),
        (pl.BlockSpec((1, n_gate, tile), t_map), jax.ShapeDtypeStruct((n_seq, n_gate, seq), F32)),
    ]
    return pl.pallas_call(
        functools.partial(_inproj_prompt_kernel, tile=tile),
        grid=(n_seq, nt),
        in_specs=[pl.BlockSpec((tile, D_MODEL), row_map),
                  pl.BlockSpec((1, D_MODEL), lambda b, t: (0, 0)),
                  pl.BlockSpec(w_row.shape, lambda b, t: (0, 0)),
                  pl.BlockSpec(w_t.shape, lambda b, t: (0, 0))],
        out_specs=[o[0] for o in out],
        out_shape=[o[1] for o in out],
        compiler_params=_params("arbitrary", "arbitrary"),
    )(x, g, w_row, w_t)


S_Q = 0
S_KV = S_Q + NSA_WIDTH
S_GATE = S_KV + 3 * KV_COLS
S_U = S_GATE + LANES
S_END = S_U + SSM_WIDTH


def _pack_w_sample(w_in):
    gates = jnp.pad(w_in[:, OFF_GATE:OFF_U], ((0, 0), (0, LANES - N_HEADS * N_BRANCH)))
    return jnp.concatenate([w_in[:, :OFF_GATE], gates, w_in[:, OFF_U:]], axis=1).astype(BF16)


def _inproj_sample_kernel(x_ref, g_ref, w_ref, q_ref, kv_ref, gate_ref, u_ref):
    z = jnp.dot(_rms(x_ref[...], g_ref[...]).astype(BF16), w_ref[...], preferred_element_type=F32)
    q_ref[...] = z[:, S_Q:S_KV]
    kv_ref[...] = z[:, S_KV:S_GATE]
    gate_ref[...] = jax.nn.sigmoid(z[:, S_GATE:S_U])
    u_ref[...] = z[:, S_U:S_END]


def _inproj_sample(x, g, w):
    rows = x.shape[0]
    vm = pl.BlockSpec(memory_space=pltpu.VMEM)
    widths = [NSA_WIDTH, 3 * KV_COLS, LANES, SSM_WIDTH]
    return pl.pallas_call(
        _inproj_sample_kernel,
        in_specs=[vm] * 3, out_specs=[vm] * 4,
        out_shape=[jax.ShapeDtypeStruct((rows, w_), F32) for w_ in widths],
        compiler_params=pltpu.CompilerParams(vmem_limit_bytes=VMEM_LIMIT),
    )(x, g, w)


def _pack_compress(cmp_pos, cmp_proj, group_major):
    pos = jnp.broadcast_to(cmp_pos.transpose(1, 0, 2)[:, :, None, :],
                           (CMP_BLOCK, 2, N_KV, HEAD_DIM)).reshape(CMP_BLOCK, KV_COLS)
    eye = jnp.eye(2 * N_KV, dtype=F32).reshape(2, N_KV, 2, N_KV)
    order = 'cgdGCe' if group_major else 'cgdCGe'
    proj = jnp.einsum('cde,cgCG->' + order, cmp_proj, eye).reshape(KV_COLS, KV_COLS)
    return pos, proj


def _compress_prompt_kernel(kv_ref, pos_ref, proj_ref, projt_ref, kblk_ref, vblkt_ref):
    x = kv_ref[0]
    n_sel = x.shape[0] // SEL_BLOCK
    x = x.reshape(n_sel, SEL_BLOCK // CMP_BLOCK, CMP_BLOCK, KV_COLS)
    pos = pos_ref[...][None]
    sums = jnp.concatenate([jnp.sum(x[:, i] * pos, axis=1) for i in range(SEL_BLOCK // CMP_BLOCK)], axis=0)
    blk = _dot3(sums, proj_ref[...])
    blk_t = _dot3_nt(projt_ref[...], sums)
    for g in range(N_KV):
        kblk_ref[0, g] = blk[:, g * KV_HALF:g * KV_HALF + HEAD_DIM]
        vblkt_ref[0, g] = blk_t[g * KV_HALF + HEAD_DIM:(g + 1) * KV_HALF]


def _compress_prompt(kvc, pos, proj):
    n_seq, seq, _ = kvc.shape
    n_cmp = seq // CMP_BLOCK
    return pl.pallas_call(
        _compress_prompt_kernel,
        grid=(n_seq,),
        in_specs=[pl.BlockSpec((1, seq, KV_COLS), lambda b: (b, 0, 0)),
                  pl.BlockSpec((CMP_BLOCK, KV_COLS), lambda b: (0, 0)),
                  pl.BlockSpec((KV_COLS, KV_COLS), lambda b: (0, 0)),
                  pl.BlockSpec((KV_COLS, KV_COLS), lambda b: (0, 0))],
        out_specs=[pl.BlockSpec((1, N_KV, n_cmp, HEAD_DIM), lambda b: (b, 0, 0, 0)),
                   pl.BlockSpec((1, N_KV, HEAD_DIM, n_cmp), lambda b: (b, 0, 0, 0))],
        out_shape=[jax.ShapeDtypeStruct((n_seq, N_KV, n_cmp, HEAD_DIM), F32),
                   jax.ShapeDtypeStruct((n_seq, N_KV, HEAD_DIM, n_cmp), F32)],
        compiler_params=_params("arbitrary"),
    )(kvc, pos, proj, proj.T)


def _select_blocks_t(pair, q0):
    row = _iota2(pair.shape, 0)
    n_sel = pair.shape[0]
    cur = (q0 + _iota2(pair.shape, 1)) // SEL_BLOCK
    forced = (row == 0) | (row == cur) | (row == cur - 1)
    score = jnp.where(row <= cur, jnp.where(forced, pair + FORCE, pair), -jnp.inf)
    chosen = jnp.zeros(pair.shape, F32)
    for _ in range(N_SEL):
        m = jnp.max(score, axis=0, keepdims=True)
        hit = (score == m) & (m > -jnp.inf)
        first = jnp.min(jnp.where(hit, row, n_sel), axis=0, keepdims=True)
        pick = row == first
        chosen = jnp.where(pick, 1.0, chosen)
        score = jnp.where(pick, -jnp.inf, score)
    return chosen


def _attend(state, k_aug, v_t, q_aug, extras):
    m, acc = state
    s = jnp.dot(k_aug, q_aug, preferred_element_type=F32)
    if any(t is not None or c is not None for t, c in extras):
        rows = []
        for t, c in extras:
            if t is None:
                rows.append(jnp.full((Q_TILE, s.shape[1]), 0.0 if c is None else c, F32))
            else:
                rows.append(t if c is None else t + c)
        s = s + jnp.concatenate(rows, axis=0)
    m_new = jnp.maximum(m, jnp.max(s, axis=0, keepdims=True))
    e = jnp.exp2(s - m_new)
    acc = jnp.exp2(m - m_new) * acc + jnp.dot(v_t, e.astype(BF16), preferred_element_type=F32)
    return m_new, acc


def _attend_result(state):
    acc = state[1]
    return acc[:HEAD_DIM] * (1.0 / jnp.maximum(acc[HEAD_DIM:HEAD_DIM + 1], TINY))


def _nsa_prompt_kernel(tbl_ref, qt_ref, gt_ref, kblk_ref, vblkt_ref, bcmp_ref, near_ref, ks_ref, vst_ref,
                       kw_ref, vwt_ref, o_ref):
    g = pl.program_id(1)
    qi = pl.program_id(2)
    q0 = qi * Q_TILE
    win_tiles = WINDOW // Q_TILE
    v_rows = pl.ds(pl.multiple_of(g * VT_ROWS, VT_ROWS), VT_ROWS)

    width = GQA * Q_TILE
    q = jnp.concatenate([qt_ref[0, r * HEAD_DIM:(r + 1) * HEAD_DIM, :] for r in range(GQA)], axis=1) * SCALE

    row = _iota2((LANES - AUG_ONE, Q_TILE), 0)
    consts = []
    for r in range(GQA):
        last = jnp.full((LANES - AUG_ONE, Q_TILE), tbl_ref[N_BUCKETS - 1, g * GQA + r] * LOG2E, F32)
        hi = last.astype(BF16).astype(F32)
        consts.append(jnp.where(row == 0, hi, jnp.where(row == 1, last - hi, 0.0)).astype(BF16))
    consts = jnp.concatenate(consts, axis=1)
    q_log2 = (q * LOG2E).astype(BF16)
    near0, near1, oldest = near_ref[0, 0], near_ref[0, 1], near_ref[0, 2]
    init = (jnp.full((1, width), NEG, F32), jnp.zeros((VT_ROWS, width), F32))

    def key_rows(ref, tile, n_tiles):
        return ref[0, pl.ds(pl.multiple_of(tile * Q_TILE, Q_TILE), n_tiles * Q_TILE), :]

    def value_cols(ref, tile, n_tiles):
        return jnp.concatenate([ref[0, tile + j, v_rows, :] for j in range(n_tiles)], axis=1)

    def tile_mask(ok):
        return jnp.where(ok, 0.0, NEG)

    q_aug = jnp.concatenate([q_log2, jnp.zeros((AUG_ONE - AUG_SEL, width), BF16), consts], axis=0)
    extras = []
    for mt in range(win_tiles):
        table = oldest if mt == 0 else near1 if mt == win_tiles - 1 else None
        extras.append((table, tile_mask(qi + mt >= win_tiles)))
    extras.append((near0, None))
    st = _attend(init, key_rows(kw_ref, qi, win_tiles + 1), value_cols(vwt_ref, qi, win_tiles + 1), q_aug, extras)
    o_win = _attend_result(st)

    k_cmp, v_cmp_t = kblk_ref[0, 0], vblkt_ref[0, 0]
    n_cmp = k_cmp.shape[0]
    bias = bcmp_ref[0, 0]
    p = _masked_softmax(_dot3(k_cmp, q) + bias, bias > MASKED_BELOW, axis=0)
    o_cmp = _bdot(v_cmp_t, p)
    ps = p[:, :Q_TILE]
    for r in range(1, GQA):
        ps = ps + p[:, r * Q_TILE:(r + 1) * Q_TILE]
    chosen = _select_blocks_t(ps[:n_cmp // 2] + ps[n_cmp // 2:], q0)
    not_chosen = (chosen - 1.0).astype(BF16)
    pad_rows = (AUG_ONE - AUG_SEL) - not_chosen.shape[0]
    if pad_rows:
        not_chosen = jnp.concatenate([not_chosen, jnp.zeros((pad_rows, Q_TILE), BF16)], axis=0)
    q_aug = jnp.concatenate([q_log2, jnp.concatenate([not_chosen] * GQA, axis=1), consts], axis=0)

    extras = [(near1, tile_mask(qi >= 1)), (near0, None)]
    st = _attend(init, key_rows(ks_ref, qi, 2), value_cols(vst_ref, qi, 2), q_aug, extras)
    n_far = jnp.maximum(qi - 1, 0)

    def far_chunk(c, st, masked=False):
        t0 = c * FAR_TILES
        extras = [(None, tile_mask(t0 + j < n_far) if masked else None) for j in range(FAR_TILES)]
        return _attend(st, key_rows(ks_ref, t0 + 1, FAR_TILES), value_cols(vst_ref, t0 + 1, FAR_TILES), q_aug, extras)

    full = n_far // FAR_TILES
    st = lax.fori_loop(0, full, far_chunk, st)
    st = lax.cond(n_far % FAR_TILES != 0, lambda st=st: far_chunk(full, st, masked=True), lambda st=st: st)
    o_sel = _attend_result(st)

    def gate(branch):
        return jnp.concatenate([gt_ref[0, r * N_BRANCH + branch:r * N_BRANCH + branch + 1, :]
                                for r in range(GQA)], axis=1)

    o = gate(0) * o_cmp + gate(1) * o_sel + gate(2) * o_win
    o_ref[0] = jnp.concatenate([o[:, r * Q_TILE:(r + 1) * Q_TILE] for r in range(GQA)], axis=0).T


def _nsa_prompt(rel_bias, q_t, gates_t, kblk, vblk_t, bcmp, near, ks_aug, vs_t, kw_aug, vw_t):
    n_seq, _, seq = q_t.shape
    nq = seq // Q_TILE
    assert nq % FAR_TILES == 0
    n_cmp = kblk.shape[2]
    gw = GQA * HEAD_DIM
    wt = WINDOW // Q_TILE
    return pl.pallas_call(
        _nsa_prompt_kernel,
        grid=(n_seq, N_KV, nq),
        in_specs=[pl.BlockSpec(memory_space=pltpu.SMEM),
                  pl.BlockSpec((1, gw, Q_TILE), lambda b, g, qi: (b, g, qi)),
                  pl.BlockSpec((1, GATE_ROWS, Q_TILE), lambda b, g, qi: (b, g, qi)),
                  pl.BlockSpec((1, 1, n_cmp, HEAD_DIM), lambda b, g, qi: (b, g, 0, 0)),
                  pl.BlockSpec((1, 1, HEAD_DIM, n_cmp), lambda b, g, qi: (b, g, 0, 0)),
                  pl.BlockSpec((1, 1, n_cmp, GQA * Q_TILE), lambda b, g, qi: (qi, g, 0, 0)),
                  pl.BlockSpec((1, 3, Q_TILE, GQA * Q_TILE), lambda b, g, qi: (g, 0, 0, 0)),
                  pl.BlockSpec((1, Q_TILE + seq, LANES), lambda b, g, qi: (b, 0, g)),
                  pl.BlockSpec((1, 1 + nq, N_KV * VT_ROWS, Q_TILE), lambda b, g, qi: (b, 0, 0, 0)),
                  pl.BlockSpec((1, WINDOW + seq, LANES), lambda b, g, qi: (b, 0, g)),
                  pl.BlockSpec((1, wt + nq, N_KV * VT_ROWS, Q_TILE), lambda b, g, qi: (b, 0, 0, 0))],
        out_specs=pl.BlockSpec((1, Q_TILE, gw), lambda b, g, qi: (b, qi, g)),
        out_shape=jax.ShapeDtypeStruct((n_seq, seq, NSA_WIDTH), F32),
        compiler_params=_params("arbitrary", "arbitrary", "arbitrary"),
    )(rel_bias, q_t, gates_t, kblk, vblk_t, bcmp, near, ks_aug, vs_t, kw_aug, vw_t)


STATE_CHUNK = 512
SCAN_UNROLL = 4
SSM_KB = SSM_WIDTH // LANES
GROUPS_PER_KB = N_SSM_GROUPS // SSM_KB


def _ssm_disc_kernel(are_ref, aim_ref, ldt_ref, bre_ref, bim_ref, abr_ref, abi_ref, bbr_ref, bbi_ref):
    a_re, a_im = are_ref[...], aim_ref[...]
    dt = jnp.exp(ldt_ref[...])
    mag = jnp.exp(a_re * dt)
    ab_re = mag * jnp.cos(a_im * dt)
    ab_im = mag * jnp.sin(a_im * dt)
    den = a_re * a_re + a_im * a_im
    co_re = ((ab_re - 1.0) * a_re + ab_im * a_im) / den
    co_im = (ab_im * a_re - (ab_re - 1.0) * a_im) / den
    abr_ref[...] = ab_re
    abi_ref[...] = ab_im
    b_re, b_im = bre_ref[...], bim_ref[...]
    bbr_ref[...] = co_re[:, None, :] * b_re - co_im[:, None, :] * b_im
    bbi_ref[...] = co_re[:, None, :] * b_im + co_im[:, None, :] * b_re


def _block_diag(w):
    a, b = w.shape[1:]
    w = w.reshape(SSM_KB, GROUPS_PER_KB, a, b)
    eye = jnp.eye(GROUPS_PER_KB, dtype=w.dtype)
    return jnp.einsum('kgab,gh->kgahb', w, eye).reshape(SSM_KB, GROUPS_PER_KB * a, GROUPS_PER_KB * b)


def _ssm_weights(a_re, a_im, b_re, b_im, c_re, c_im, log_dt):
    ng, p = a_re.shape
    vm = pl.BlockSpec(memory_space=pltpu.VMEM)
    ab_re, ab_im, bb_re, bb_im = pl.pallas_call(
        _ssm_disc_kernel,
        in_specs=[vm] * 5, out_specs=[vm] * 4,
        out_shape=[jax.ShapeDtypeStruct((ng, p), F32)] * 2
        + [jax.ShapeDtypeStruct((ng, SSM_GROUP, p), F32)] * 2,
    )(a_re, a_im, log_dt.reshape(ng, 1), b_re.transpose(0, 2, 1), b_im.transpose(0, 2, 1))
    return (ab_re.reshape(1, N_STATE), ab_im.reshape(1, N_STATE),
            _block_diag(bb_re).astype(BF16), _block_diag(bb_im).astype(BF16),
            _block_diag(c_re.transpose(0, 2, 1)).astype(BF16),
            _block_diag(c_im.transpose(0, 2, 1)).astype(BF16))


def _ssm_kernel(u_ref, h0r_ref, h0i_ref, abr_ref, abi_ref, bbr_ref, bbi_ref, ccr_ref, cci_ref,
                d_ref, gw_ref, gb_ref, o_ref, hr_ref, hi_ref, sre, sim, *rows_scratch, nb, steps):
    @pl.when(pl.program_id(0) == 0)
    def _():
        hr_ref[...] = h0r_ref[...]
        hi_ref[...] = h0i_ref[...]

    if rows_scratch:
        rows_ref, = rows_scratch
        for b in range(nb):
            for c in range(SSM_KB):
                col = b * SSM_WIDTH + c * LANES
                rows_ref[c, pl.ds(b, steps, stride=nb), :] = u_ref[:, col:col + LANES]
        u = jnp.concatenate([rows_ref[c] for c in range(SSM_KB)], axis=1)
    else:
        u = u_ref[...]
    ub = u.astype(BF16)
    kw = N_STATE // SSM_KB
    for kb in range(SSM_KB):
        uk = ub[:, kb * LANES:(kb + 1) * LANES]
        sre[:, kb * kw:(kb + 1) * kw] = jnp.dot(uk, bbr_ref[kb], preferred_element_type=F32)
        sim[:, kb * kw:(kb + 1) * kw] = jnp.dot(uk, bbi_ref[kb], preferred_element_type=F32)

    if steps == 1:
        ar, ai = abr_ref[...], abi_ref[...]
        hr, hi = hr_ref[...], hi_ref[...]
        nr = ar * hr - ai * hi + sre[...]
        ni = ar * hi + ai * hr + sim[...]
        sre[...] = nr
        sim[...] = ni
        hr_ref[...] = nr
        hi_ref[...] = ni
    else:
        for cb in range(N_STATE // STATE_CHUNK):
            cols = slice(cb * STATE_CHUNK, (cb + 1) * STATE_CHUNK)
            ar = jnp.broadcast_to(abr_ref[:, cols], (nb, STATE_CHUNK))
            ai = jnp.broadcast_to(abi_ref[:, cols], (nb, STATE_CHUNK))

            def step(t, carry, cols=cols, ar=ar, ai=ai):
                hr, hi = carry
                rows = pl.ds(pl.multiple_of(t * nb, nb), nb)
                nr = ar * hr - ai * hi + sre[rows, cols]
                ni = ar * hi + ai * hr + sim[rows, cols]
                sre[rows, cols] = nr
                sim[rows, cols] = ni
                return nr, ni

            hr, hi = lax.fori_loop(0, steps, step, (hr_ref[:, cols], hi_ref[:, cols]), unroll=SCAN_UNROLL)
            hr_ref[:, cols] = hr
            hi_ref[:, cols] = hi

    ys = []
    for kb in range(SSM_KB):
        cols = slice(kb * kw, (kb + 1) * kw)
        ys.append(_bdot(sre[:, cols], ccr_ref[kb]) - _bdot(sim[:, cols], cci_ref[kb]))
    y = jnp.concatenate(ys, axis=1) + d_ref[...] * u
    g = jax.nn.gelu(y)
    out = g * jax.nn.sigmoid(_bdot(g, gw_ref[...]) + gb_ref[...])
    if rows_scratch:
        for c in range(SSM_KB):
            rows_ref[c] = out[:, c * LANES:(c + 1) * LANES]
        for b in range(nb):
            for c in range(SSM_KB):
                col = b * SSM_WIDTH + c * LANES
                o_ref[:, col:col + LANES] = rows_ref[c, pl.ds(b, steps, stride=nb), :]
    else:
        o_ref[...] = out


def _ssm(u, h0_re, h0_im, wts, d, glu_w, glu_b, nb, steps_per_call):
    ab_re, ab_im, bb_re, bb_im, cc_re, cc_im = wts
    rows = nb * steps_per_call
    wide = steps_per_call > 1
    n_calls = u.shape[0] // steps_per_call if wide else 1
    block = (steps_per_call, nb * SSM_WIDTH) if wide else (rows, SSM_WIDTH)
    const2 = lambda i: (0, 0)
    const3 = lambda i: (0, 0, 0)
    kw = N_STATE // SSM_KB
    scratch = [pltpu.VMEM((rows, N_STATE), F32), pltpu.VMEM((rows, N_STATE), F32)]
    if wide:
        scratch.append(pltpu.VMEM((SSM_KB, rows, LANES), F32))
    return pl.pallas_call(
        functools.partial(_ssm_kernel, nb=nb, steps=steps_per_call),
        grid=(n_calls,),
        in_specs=[pl.BlockSpec(block, lambda i: (i, 0)),
                  pl.BlockSpec((nb, N_STATE), const2), pl.BlockSpec((nb, N_STATE), const2),
                  pl.BlockSpec((1, N_STATE), const2), pl.BlockSpec((1, N_STATE), const2),
                  pl.BlockSpec((SSM_KB, LANES, kw), const3), pl.BlockSpec((SSM_KB, LANES, kw), const3),
                  pl.BlockSpec((SSM_KB, kw, LANES), const3), pl.BlockSpec((SSM_KB, kw, LANES), const3),
                  pl.BlockSpec((1, SSM_WIDTH), const2),
                  pl.BlockSpec((SSM_WIDTH, SSM_WIDTH), const2),
                  pl.BlockSpec((1, SSM_WIDTH), const2)],
        out_specs=[pl.BlockSpec(block, lambda i: (i, 0)),
                   pl.BlockSpec((nb, N_STATE), const2), pl.BlockSpec((nb, N_STATE), const2)],
        out_shape=[jax.ShapeDtypeStruct(u.shape, F32),
                   jax.ShapeDtypeStruct((nb, N_STATE), F32), jax.ShapeDtypeStruct((nb, N_STATE), F32)],
        scratch_shapes=scratch,
        compiler_params=_params("arbitrary"),
    )(u, h0_re, h0_im, ab_re, ab_im, bb_re, bb_im, cc_re, cc_im, d, glu_w, glu_b)


def _merge_rows(o_nsa, o_ssm, x, gg_ref, wout_ref, ng_ref, xq_ref):
    gg = gg_ref[...]
    a = _rms(o_nsa, gg[:, :NSA_WIDTH])
    b = _rms(o_ssm, gg[:, NSA_WIDTH:])
    m = (jnp.dot(a.astype(BF16), wout_ref[:NSA_WIDTH], preferred_element_type=F32)
         + jnp.dot(b.astype(BF16), wout_ref[NSA_WIDTH:], preferred_element_type=F32))
    x1 = x + _rms(m, ng_ref[1:2])
    return x1, _bdot(_rms(x1, ng_ref[2:3]), xq_ref[...])


def _cross_rows(qx, mkv_ref):
    outs = []
    for h in range(MEM_HEADS):
        cols = slice(h * MEM_HEAD_DIM, (h + 1) * MEM_HEAD_DIM)
        k = mkv_ref[0, :, cols]
        v = mkv_ref[0, :, MEM_WIDTH + h * MEM_HEAD_DIM: MEM_WIDTH + (h + 1) * MEM_HEAD_DIM]
        s = _bdot_nt(qx[:, cols], k) * MEM_SCALE
        e = jnp.exp(s - jnp.max(s, axis=-1, keepdims=True))
        p = e * (1.0 / jnp.sum(e, axis=-1, keepdims=True))
        outs.append(_bdot(p, v))
    return jnp.concatenate(outs, axis=1)


def _ffn_rows(x1, oa, ng_ref, xo_ref, wi_ref, wo_ref, d_ff):
    x2 = x1 + _rms(_bdot(oa, xo_ref[...]), ng_ref[3:4])
    h = _rms(x2, ng_ref[4:5]).astype(BF16)
    chunk = d_ff // FFN_CHUNKS
    y = None
    for c in range(0, d_ff, chunk):
        z1 = jnp.dot(h, wi_ref[:, c:c + chunk], preferred_element_type=F32)
        z2 = jnp.dot(h, wi_ref[:, d_ff + c:d_ff + c + chunk], preferred_element_type=F32)
        part = _bdot(z1 * jax.nn.sigmoid(z1) * z2, wo_ref[c:c + chunk, :])
        y = part if y is None else y + part
    return x2 + _rms(y, ng_ref[5:6])


def _tail_prompt_kernel(onsa_ref, ossm_ref, x_ref, mkv_ref, gg_ref, wout_ref, ng_ref, xq_ref, xo_ref, wi_ref,
                        wo_ref, o_ref, *, d_ff):
    x1, qx = _merge_rows(onsa_ref[...], ossm_ref[...], x_ref[...], gg_ref, wout_ref, ng_ref, xq_ref)
    o_ref[...] = _ffn_rows(x1, _cross_rows(qx, mkv_ref), ng_ref, xo_ref, wi_ref, wo_ref, d_ff)


def _tail_prompt(o_nsa, o_ssm, x, mkv, gg, w_out, ng, xq, xo, wi, wo, n_seq, tile):
    rows = x.shape[0]
    nt = rows // n_seq // tile
    d_ff = wo.shape[0]
    assert d_ff % (FFN_CHUNKS * LANES) == 0
    row_map = lambda b, t: (b * nt + t, 0)
    const = lambda b, t: (0, 0)
    once = dict(pipeline_mode=pl.Buffered(1))
    return pl.pallas_call(
        functools.partial(_tail_prompt_kernel, d_ff=d_ff),
        grid=(n_seq, nt),
        in_specs=[pl.BlockSpec((tile, NSA_WIDTH), row_map),
                  pl.BlockSpec((tile, SSM_WIDTH), lambda b, t: (t, b)),
                  pl.BlockSpec((tile, D_MODEL), row_map),
                  pl.BlockSpec((1,) + mkv.shape[1:], lambda b, t: (b, 0, 0)),
                  pl.BlockSpec((1, D_MODEL), const),
                  pl.BlockSpec((D_MODEL, D_MODEL), const, **once),
                  pl.BlockSpec((6, D_MODEL), const),
                  pl.BlockSpec((D_MODEL, MEM_WIDTH), const, **once),
                  pl.BlockSpec((MEM_WIDTH, D_MODEL), const, **once),
                  pl.BlockSpec((D_MODEL, 2 * d_ff), const, **once),
                  pl.BlockSpec((d_ff, D_MODEL), const, **once)],
        out_specs=pl.BlockSpec((tile, D_MODEL), row_map),
        out_shape=jax.ShapeDtypeStruct((rows, D_MODEL), F32),
        compiler_params=_params("arbitrary", "arbitrary"),
    )(o_nsa, o_ssm, x, mkv, gg, w_out, ng, xq, xo, wi, wo)


def _merge_kernel(onsa_ref, ossm_ref, x_ref, gg_ref, wout_ref, ng_ref, xq_ref, x1_ref, qx_ref):
    x1_ref[...], qx_ref[...] = _merge_rows(onsa_ref[...], ossm_ref[...], x_ref[...], gg_ref, wout_ref, ng_ref,
                                           xq_ref)


def _merge(o_nsa, o_ssm, x, gg, w_out, ng, xq, n_seq, tile):
    rows = x.shape[0]
    nt = rows // n_seq // tile
    row_map = lambda b, t: (b * nt + t, 0)
    const = lambda b, t: (0, 0)
    return pl.pallas_call(
        _merge_kernel,
        grid=(n_seq, nt),
        in_specs=[pl.BlockSpec((tile, NSA_WIDTH), row_map),
                  pl.BlockSpec((tile, SSM_WIDTH), lambda b, t: (t, b)),
                  pl.BlockSpec((tile, D_MODEL), row_map),
                  pl.BlockSpec((1, D_MODEL), const),
                  pl.BlockSpec((D_MODEL, D_MODEL), const),
                  pl.BlockSpec((6, D_MODEL), const),
                  pl.BlockSpec((D_MODEL, MEM_WIDTH), const)],
        out_specs=[pl.BlockSpec((tile, D_MODEL), row_map), pl.BlockSpec((tile, MEM_WIDTH), row_map)],
        out_shape=[jax.ShapeDtypeStruct((rows, D_MODEL), F32),
                   jax.ShapeDtypeStruct((rows, MEM_WIDTH), F32)],
        compiler_params=_params("arbitrary", "arbitrary"),
    )(o_nsa, o_ssm, x, gg, w_out, ng, xq)


def _norm_matmul_kernel(x_ref, g_ref, w_ref, o_ref):
    o_ref[...] = _bdot(_rms(x_ref[...], g_ref[...]), w_ref[...])


def _norm_matmul(x, g, w, tile):
    rows, k = x.shape
    n = w.shape[1]
    return pl.pallas_call(
        _norm_matmul_kernel,
        grid=(rows // tile,),
        in_specs=[pl.BlockSpec((tile, k), lambda i: (i, 0)),
                  pl.BlockSpec((1, k), lambda i: (0, 0)),
                  pl.BlockSpec((k, n), lambda i: (0, 0))],
        out_specs=pl.BlockSpec((tile, n), lambda i: (i, 0)),
        out_shape=jax.ShapeDtypeStruct((rows, n), F32),
        compiler_params=_params("arbitrary"),
    )(x, g, w)


SAMPLE_GROUP = 8


def _cross_sample_kernel(qx_ref, mkv_ref, o_ref, *, n_mem):
    per_tok = 2 * MEM_HEADS
    sub = _iota2((SAMPLE_GROUP, MEM_HEAD_DIM), 0)
    qx = qx_ref[...]
    outs = []
    for i in range(SAMPLE_GROUP):
        s = jnp.zeros((SAMPLE_GROUP, n_mem), F32)
        for h in range(MEM_HEADS):
            qh = jnp.broadcast_to(qx[i:i + 1, h * MEM_HEAD_DIM:(h + 1) * MEM_HEAD_DIM], sub.shape)
            k = mkv_ref[i, pl.ds(h, n_mem, stride=per_tok), :]
            s = s + _bdot_nt(jnp.where(sub == h, qh, 0.0), k)
        s = s * MEM_SCALE
        e = jnp.exp(s - jnp.max(s, axis=-1, keepdims=True))
        p = e * (1.0 / jnp.sum(e, axis=-1, keepdims=True))
        heads = []
        for h in range(MEM_HEADS):
            v = mkv_ref[i, pl.ds(MEM_HEADS + h, n_mem, stride=per_tok), :]
            heads.append(_bdot(p, v)[h:h + 1])
        outs.append(jnp.concatenate(heads, axis=1))
    o_ref[...] = jnp.concatenate(outs, axis=0)


def _cross_sample(qx, mkv, n_mem):
    rows = qx.shape[0]
    return pl.pallas_call(
        functools.partial(_cross_sample_kernel, n_mem=n_mem),
        grid=(rows // SAMPLE_GROUP,),
        in_specs=[pl.BlockSpec((SAMPLE_GROUP, MEM_WIDTH), lambda i: (i, 0)),
                  pl.BlockSpec((SAMPLE_GROUP,) + mkv.shape[1:], lambda i: (i, 0, 0))],
        out_specs=pl.BlockSpec((SAMPLE_GROUP, MEM_WIDTH), lambda i: (i, 0)),
        out_shape=jax.ShapeDtypeStruct((rows, MEM_WIDTH), F32),
        compiler_params=_params("arbitrary"),
    )(qx, mkv)


def _ffn_kernel(x1_ref, oa_ref, ng_ref, xo_ref, wi_ref, wo_ref, o_ref, *, d_ff):
    o_ref[...] = _ffn_rows(x1_ref[...], oa_ref[...], ng_ref, xo_ref, wi_ref, wo_ref, d_ff)


def _ffn(x1, oa, ng, xo, wi, wo, tile):
    rows = x1.shape[0]
    d_ff = wo.shape[0]
    assert d_ff % (FFN_CHUNKS * LANES) == 0
    const = lambda i: (0, 0)
    once = dict(pipeline_mode=pl.Buffered(1))
    return pl.pallas_call(
        functools.partial(_ffn_kernel, d_ff=d_ff),
        grid=(rows // tile,),
        in_specs=[pl.BlockSpec((tile, D_MODEL), lambda i: (i, 0)),
                  pl.BlockSpec((tile, MEM_WIDTH), lambda i: (i, 0)),
                  pl.BlockSpec((6, D_MODEL), const),
                  pl.BlockSpec((MEM_WIDTH, D_MODEL), const, **once),
                  pl.BlockSpec((D_MODEL, 2 * d_ff), const, **once),
                  pl.BlockSpec((d_ff, D_MODEL), const, **once)],
        out_specs=pl.BlockSpec((tile, D_MODEL), lambda i: (i, 0)),
        out_shape=jax.ShapeDtypeStruct((rows, D_MODEL), F32),
        compiler_params=_params("arbitrary"),
    )(x1, oa, ng, xo, wi, wo)


PAGES_PER_STEP = 32
FOLD_PAGES = 8
NEW_TILE = 8
CMP_GROUP = 16


def _bias_rows(dist, tblt_ref):
    v = jnp.broadcast_to(tblt_ref[:, 0:1], dist.shape)
    for k in range(1, N_BUCKETS):
        v = jnp.where(dist >= _BUCKET_START[k], tblt_ref[:, k:k + 1], v)
    return v


def _sample_bias_kernel(tblt_ref, bw_ref, bc_ref, bn_ref, bs_ref, b0_ref, *, past_len):
    dist = (WINDOW - 1) - _iota2(bw_ref.shape, 1)
    bw_ref[...] = jnp.where((dist >= 0) & (dist < WINDOW), _bias_rows(dist, tblt_ref), NEG)
    dist = past_len - (_iota2(bc_ref.shape, 1) * CMP_BLOCK + (CMP_BLOCK - 1))
    bc_ref[...] = jnp.where(dist >= 0, _bias_rows(dist, tblt_ref), NEG)
    lane = _iota2(bn_ref.shape, 1)
    dist = past_len - ((past_len // CMP_BLOCK + lane) * CMP_BLOCK + (CMP_BLOCK - 1))
    bn_ref[...] = jnp.where((dist >= 0) & (lane < SEL_BLOCK // CMP_BLOCK),
                            _bias_rows(jnp.maximum(dist, 0), tblt_ref), NEG)
    page = lax.broadcasted_iota(I32, bs_ref.shape, 0)
    dist = past_len - page * PAGE_SIZE - lax.broadcasted_iota(I32, bs_ref.shape, 2)
    bs_ref[...] = _bias_rows(dist, tblt_ref)
    b0_ref[...] = _bias_rows(jnp.zeros(b0_ref.shape, I32), tblt_ref)


def _sample_bias(rel_bias, past_len):
    vm = pl.BlockSpec(memory_space=pltpu.VMEM)
    shapes = [(N_HEADS, WINDOW), (N_HEADS, past_len // CMP_BLOCK), (N_HEADS, LANES),
              (past_len // PAGE_SIZE, N_HEADS, PAGE_SIZE), (N_HEADS, LANES)]
    return pl.pallas_call(
        functools.partial(_sample_bias_kernel, past_len=past_len),
        in_specs=[vm], out_specs=[vm] * len(shapes),
        out_shape=[jax.ShapeDtypeStruct(s, F32) for s in shapes],
    )(rel_bias.T)


def _compress_sample_kernel(pt_ref, pool_ref, post_ref, fold_ref, projt_ref, o_ref, buf, sem, *, n_steps):
    i = pl.program_id(0)
    slot = i % 2

    def page_copies(step, slot):
        return [pltpu.make_async_copy(pool_ref.at[pt_ref[step * PAGES_PER_STEP + j]], buf.at[slot, j], sem.at[slot])
                for j in range(PAGES_PER_STEP)]

    @pl.when(i == 0)
    def _():
        for c in page_copies(0, 0):
            c.start()

    @pl.when(i + 1 < n_steps)
    def _():
        for c in page_copies(i + 1, 1 - slot):
            c.start()

    for c in page_copies(i, slot):
        c.wait()

    post = post_ref[...]
    sums_t = None
    for c in range(0, PAGES_PER_STEP, FOLD_PAGES):
        weighted = jnp.concatenate([(buf[slot, j] * post).astype(BF16) for j in range(c, c + FOLD_PAGES)], axis=1)
        part = jnp.dot(weighted, fold_ref[c * PAGE_SIZE:(c + FOLD_PAGES) * PAGE_SIZE, :],
                       preferred_element_type=F32)
        sums_t = part if sums_t is None else sums_t + part
    o_ref[0] = _dot3(projt_ref[...], sums_t)


def _compress_sample(pool_t, page_table_flat, n_seq, n_pages, pos, proj):
    steps = n_pages // PAGES_PER_STEP
    blocks_per_page = PAGE_SIZE // CMP_BLOCK
    blocks_per_step = PAGES_PER_STEP * blocks_per_page
    assert blocks_per_step == LANES
    pos_t = jnp.tile(pos.T, (1, blocks_per_page))
    row = np.arange(PAGES_PER_STEP * PAGE_SIZE)
    fold = jnp.asarray(row[:, None] // CMP_BLOCK == np.arange(blocks_per_step)[None, :], dtype=BF16)

    const = lambda i, pt: (0, 0)
    grid_spec = pltpu.PrefetchScalarGridSpec(
        num_scalar_prefetch=1,
        grid=(n_seq * steps,),
        in_specs=[pl.BlockSpec(memory_space=pl.ANY),
                  pl.BlockSpec((KV_COLS, PAGE_SIZE), const),
                  pl.BlockSpec(fold.shape, const),
                  pl.BlockSpec((KV_COLS, KV_COLS), const)],
        out_specs=pl.BlockSpec((1, KV_COLS, blocks_per_step), lambda i, pt: (i // steps, 0, i % steps)),
        scratch_shapes=[pltpu.VMEM((2, PAGES_PER_STEP, KV_COLS, PAGE_SIZE), F32),
                        pltpu.SemaphoreType.DMA((2,))],
    )
    return pl.pallas_call(
        functools.partial(_compress_sample_kernel, n_steps=n_seq * steps),
        grid_spec=grid_spec,
        out_shape=jax.ShapeDtypeStruct((n_seq, KV_COLS, steps * blocks_per_step), F32),
        compiler_params=_params("arbitrary"),
    )(page_table_flat, pool_t, pos_t, fold, proj.T)


def _cmp_sample_kernel(qm_ref, blk_ref, kvc_ref, pos_ref, proj_ref, bc_ref, bn_ref, oc_ref, idx_ref,
                       *, past_len):
    n_cmp = blk_ref.shape[2]
    bias = jnp.concatenate([bc_ref[...]] * CMP_GROUP, axis=0)
    mask = bias > MASKED_BELOW
    bias_n = jnp.concatenate([bn_ref[:, :NEW_TILE]] * CMP_GROUP, axis=0)
    mask_n = bias_n > MASKED_BELOW
    new_blk = _dot3(jnp.concatenate([kvc_ref[i] for i in range(CMP_GROUP)], axis=0) * pos_ref[0:1], proj_ref[...])
    first_row = _iota2((NEW_TILE, KV_COLS), 0) == 0
    new_blks = [jnp.where(first_row, jnp.broadcast_to(new_blk[i:i + 1], first_row.shape), 0.0)
                for i in range(CMP_GROUP)]
    s = jnp.concatenate([_dot3(qm_ref[i], blk_ref[i, :KV_HALF, :]) for i in range(CMP_GROUP)], axis=0)
    s_n = jnp.concatenate([_dot3_nt(qm_ref[i], new_blks[i][:, :KV_HALF]) for i in range(CMP_GROUP)], axis=0)
    s = jnp.where(mask, s * SCALE + bias, NEG)
    s_n = jnp.where(mask_n, s_n * SCALE + bias_n, NEG)
    m = jnp.maximum(jnp.max(s, axis=-1, keepdims=True), jnp.max(s_n, axis=-1, keepdims=True))
    e = jnp.where(mask, jnp.exp(s - m), 0.0)
    e_n = jnp.where(mask_n, jnp.exp(s_n - m), 0.0)
    inv = 1.0 / jnp.maximum(jnp.sum(e, axis=-1, keepdims=True) + jnp.sum(e_n, axis=-1, keepdims=True), TINY)
    p = e * inv
    p_n = e_n * inv
    group_ps = []
    for i in range(CMP_GROUP):
        rows = slice(i * N_HEADS, (i + 1) * N_HEADS)
        oc_ref[i] = _bdot_nt(p[rows], blk_ref[i, KV_HALF:, :]) + _bdot(p_n[rows], new_blks[i][:, KV_HALF:])
        group_ps += [jnp.sum(p[i * N_HEADS + g * GQA:i * N_HEADS + (g + 1) * GQA], axis=0, keepdims=True)
                     for g in range(N_KV)]

    ps = jnp.concatenate(group_ps, axis=0)
    lane = _iota2(ps.shape, 1)
    pair = ps + pltpu.roll(ps, n_cmp - 1, 1)
    cur = past_len // SEL_BLOCK
    j = lane // (SEL_BLOCK // CMP_BLOCK)
    forced = (j == 0) | (j == cur) | (j == cur - 1)
    score = jnp.where(lane % 2 == 0, jnp.where(forced, pair + FORCE, pair), -jnp.inf)
    slot = _iota2((ps.shape[0], LANES), 1)
    ids = jnp.zeros((ps.shape[0], LANES), I32)
    for it in range(N_SEL - 1):
        top = jnp.max(score, axis=-1, keepdims=True)
        first = jnp.min(jnp.where(score == top, lane, n_cmp), axis=-1, keepdims=True)
        ids = jnp.where(slot == it, first // (SEL_BLOCK // CMP_BLOCK), ids)
        score = jnp.where(lane == first, -jnp.inf, score)
    idx_ref[...] = jnp.where(slot == N_SEL - 1, cur, ids)


def _cmp_sample(qm, blocks_t, kvc_new, pos, proj, bc, bn, past_len):
    n_seq, _, n_cmp = blocks_t.shape
    assert past_len // SEL_BLOCK > N_SEL and past_len % SEL_BLOCK == 0
    const = lambda b: (0, 0)
    per_seq = lambda b: (b, 0, 0)
    return pl.pallas_call(
        functools.partial(_cmp_sample_kernel, past_len=past_len),
        grid=(n_seq // CMP_GROUP,),
        in_specs=[pl.BlockSpec((CMP_GROUP, N_HEADS, LANES), per_seq),
                  pl.BlockSpec((CMP_GROUP, KV_COLS, n_cmp), per_seq),
                  pl.BlockSpec((CMP_GROUP, 1, KV_COLS), per_seq),
                  pl.BlockSpec((CMP_BLOCK, KV_COLS), const),
                  pl.BlockSpec((KV_COLS, KV_COLS), const),
                  pl.BlockSpec((N_HEADS, n_cmp), const),
                  pl.BlockSpec((N_HEADS, LANES), const)],
        out_specs=[pl.BlockSpec((CMP_GROUP, N_HEADS, LANES), per_seq),
                   pl.BlockSpec((CMP_GROUP * N_KV, LANES), lambda b: (b, 0))],
        out_shape=[jax.ShapeDtypeStruct((n_seq, N_HEADS, LANES), F32),
                   jax.ShapeDtypeStruct((n_seq * N_KV, LANES), I32)],
        compiler_params=_params("arbitrary"),
    )(qm, blocks_t, kvc_new, pos, proj, bc, bn)


def _sel_win_sample_kernel(idx_ref, pt_ref, pool_ref, buf_ref, qm_ref, kvs_ref, kvw_ref, gate_ref, oc_ref, bw_ref,
                           bs_ref, b0_ref, o_ref, nbuf_ref, pages, sem, *, n_past_blocks, n_seq):
    b = pl.program_id(0)
    slot = b % 2
    bpp = PAGE_SIZE // SEL_BLOCK

    def page_copies(seq, slot):
        return [pltpu.make_async_copy(pool_ref.at[pt_ref[(seq * N_KV + g) * N_SEL + k], :, g],
                                      pages.at[slot, g * N_SEL + k], sem.at[slot])
                for g in range(N_KV) for k in range(N_SEL)]

    @pl.when(b == 0)
    def _():
        for c in page_copies(0, 0):
            c.start()

    @pl.when(b + 1 < n_seq)
    def _():
        for c in page_copies(b + 1, 1 - slot):
            c.start()

    qm = qm_ref[0]

    buf = buf_ref[0]
    shifted = pltpu.roll(buf, WINDOW - 1, 1)
    nbuf = jnp.where(_iota2(buf.shape, 1) == WINDOW - 1, jnp.broadcast_to(kvw_ref[0], buf.shape), shifted)
    nbuf_ref[0] = nbuf
    bias = bw_ref[...]
    s_win = jnp.where(bias > MASKED_BELOW, _bdot(qm, nbuf[:KV_HALF]) * SCALE + bias, NEG)

    for c in page_copies(b, slot):
        c.wait()
    lane_half = _iota2((GQA, PAGE_SIZE), 1) // SEL_BLOCK
    n_keys = N_SEL * PAGE_SIZE
    scores = [jnp.concatenate([s_win, jnp.full((N_HEADS, n_keys - WINDOW), NEG, F32)], axis=1)]
    new_scores = [jnp.full((N_HEADS, 1), NEG, F32)]
    values, new_values = [], []
    for g in range(N_KV):
        cols = slice(g * HEAD_DIM, (g + 1) * HEAD_DIM)
        rows = slice(g * GQA, (g + 1) * GQA)
        qg = qm[rows, cols]
        ks, vs, biases = [], [], []
        has_new = False
        for k in range(N_SEL):
            i = idx_ref[(b * N_KV + g) * N_SEL + k]
            is_past = i < n_past_blocks
            has_new = jnp.logical_or(has_new, jnp.logical_not(is_past))
            i = jnp.minimum(i, n_past_blocks - 1)
            ks.append(pages[slot, g * N_SEL + k, 0])
            vs.append(pages[slot, g * N_SEL + k, 1])
            tile = bs_ref[i // bpp]
            biases.append(jnp.where(is_past & (lane_half == i % bpp), tile[rows], NEG))
        bias = jnp.concatenate(biases, axis=1)
        scores.append(jnp.where(bias > MASKED_BELOW, _bdot(qg, jnp.concatenate(ks, axis=1)) * SCALE + bias, NEG))
        values.append(jnp.concatenate(vs, axis=1))
        k_new = kvs_ref[0][:, cols]
        new_values.append(kvs_ref[0][:, KV_HALF + g * HEAD_DIM:KV_HALF + (g + 1) * HEAD_DIM])
        new_scores.append(jnp.where(
            has_new, jnp.sum(qg * k_new, axis=-1, keepdims=True) * SCALE + b0_ref[rows, 0:1], NEG))

    s = jnp.concatenate(scores, axis=0)
    s_new = jnp.concatenate(new_scores, axis=0)
    mask, mask_new = s > MASKED_BELOW, s_new > MASKED_BELOW
    m = jnp.maximum(jnp.max(s, axis=-1, keepdims=True), s_new)
    e = jnp.where(mask, jnp.exp(s - m), 0.0)
    e_new = jnp.where(mask_new, jnp.exp(s_new - m), 0.0)
    inv = 1.0 / jnp.maximum(jnp.sum(e, axis=-1, keepdims=True) + e_new, TINY)
    p, p_new = e * inv, e_new * inv
    o_win = _bdot_nt(p[:N_HEADS, :WINDOW], nbuf[KV_HALF:])
    o_sel = jnp.concatenate(
        [_bdot_nt(p[N_HEADS + g * GQA:N_HEADS + (g + 1) * GQA], values[g])
         + p_new[N_HEADS + g * GQA:N_HEADS + (g + 1) * GQA] * new_values[g] for g in range(N_KV)], axis=0)

    first_group = _iota2((N_HEADS, HEAD_DIM), 0) < GQA
    oc = oc_ref[0]
    o_cmp = jnp.where(first_group, oc[:, :HEAD_DIM], oc[:, HEAD_DIM:])
    o_win = jnp.where(first_group, o_win[:, :HEAD_DIM], o_win[:, HEAD_DIM:])
    gate = gate_ref[0]
    o_ref[0] = gate[:, 0:1] * o_cmp + gate[:, 1:2] * o_sel + gate[:, 2:3] * o_win


def _sel_win_sample(idx_flat, pt_flat, pool_t, buf_t, qm, kvs_new, kvw_new_col, gates, o_cmp, bw, bs, b0,
                    n_pages):
    n_seq = buf_t.shape[0]
    bpp = PAGE_SIZE // SEL_BLOCK
    n_past_blocks = n_pages * bpp

    per_seq3 = lambda b, idx, pt: (b, 0, 0)
    const2 = lambda b, idx, pt: (0, 0)
    head_tile = pl.BlockSpec((1, N_HEADS, LANES), per_seq3)
    grid_spec = pltpu.PrefetchScalarGridSpec(
        num_scalar_prefetch=2,
        grid=(n_seq,),
        in_specs=[pl.BlockSpec(memory_space=pl.ANY),
                  pl.BlockSpec((1, KV_COLS, WINDOW), per_seq3), head_tile,
                  pl.BlockSpec((1, 1, KV_COLS), per_seq3), pl.BlockSpec((1, KV_COLS, 1), per_seq3), head_tile,
                  head_tile,
                  pl.BlockSpec((N_HEADS, WINDOW), const2),
                  pl.BlockSpec((n_pages, N_HEADS, PAGE_SIZE), lambda b, idx, pt: (0, 0, 0)),
                  pl.BlockSpec((N_HEADS, LANES), const2)],
        out_specs=[pl.BlockSpec((1, N_HEADS, HEAD_DIM), per_seq3),
                   pl.BlockSpec((1, KV_COLS, WINDOW), per_seq3)],
        scratch_shapes=[pltpu.VMEM((2, N_KV * N_SEL, 2, HEAD_DIM, PAGE_SIZE), F32),
                        pltpu.SemaphoreType.DMA((2,))],
    )
    return pl.pallas_call(
        functools.partial(_sel_win_sample_kernel, n_past_blocks=n_past_blocks, n_seq=n_seq),
        grid_spec=grid_spec,
        out_shape=[jax.ShapeDtypeStruct((n_seq, N_HEADS, HEAD_DIM), F32),
                   jax.ShapeDtypeStruct(buf_t.shape, F32)],
        compiler_params=_params("arbitrary"),
    )(idx_flat, pt_flat, pool_t, buf_t, qm, kvs_new, kvw_new_col, gates, o_cmp, bw, bs, b0)


PROJ_TILE = 512
ROW_TILE = 256
TAIL_TILE = 512
SCAN_STEPS = 128


def kernel(x_prompt, x_sample, cache_kv_cmp, cache_kv_sel, cache_kv_win, state_ssm_re, state_ssm_im,
           cache_mem_kv, page_table, mem_prompt, w_in, w_out, norm_g, grp_norm_g, cmp_pos, cmp_proj,
           rel_bias, ssm_a_re, ssm_a_im, ssm_b_re, ssm_b_im, ssm_c_re, ssm_c_im, ssm_d, ssm_log_dt,
           glu_w, glu_b, mem_norm_g, xq, xkv, xo, ffn_wi, ffn_wo):
    depth = w_in.shape[0]
    n_seq, seq, _ = x_prompt.shape
    n_dec, dec_seq, _ = x_sample.shape
    assert dec_seq == 1, "the sample kernels handle one new token per sequence"
    n_pages = page_table.shape[1]
    past_len = n_pages * PAGE_SIZE
    n_mem = mem_prompt.shape[1]
    assert cache_kv_win.shape[2] == WINDOW and seq >= WINDOW and WINDOW % Q_TILE == 0
    win_tiles = WINDOW // Q_TILE

    bcmp, near = _prompt_bias(rel_bias, seq)
    bw_s, bc_s, bn_s, bs_s, b0_s = _sample_bias(rel_bias, past_len)
    pt_flat = page_table.reshape(-1)
    kv5 = (2, N_KV, HEAD_DIM)

    xp = x_prompt.reshape(n_seq * seq, D_MODEL)
    xs = x_sample.reshape(n_dec, D_MODEL)
    outs = [[] for _ in range(11)]
    for l in range(depth):
        ng = norm_g[l]
        w_row, w_t = _pack_w_prompt(w_in[l])
        w_sample = _pack_w_sample(w_in[l])
        w_out_b, xq_b, xkv_b, xo_b = (w[l].astype(BF16) for w in (w_out, xq, xkv, xo))
        wi_b, wo_b, glu_w_b = ffn_wi[l].astype(BF16), ffn_wo[l].astype(BF16), glu_w[l].astype(BF16)
        gg = grp_norm_g[l][None]
        ssm_w = _ssm_weights(ssm_a_re[l], ssm_a_im[l], ssm_b_re[l], ssm_b_im[l], ssm_c_re[l], ssm_c_im[l],
                             ssm_log_dt[l])
        ssm_tail = (ssm_d[l][None], glu_w_b, glu_b[l][None])
        pos, proj_g = _pack_compress(cmp_pos[l], cmp_proj[l], group_major=True)
        _, proj_c = _pack_compress(cmp_pos[l], cmp_proj[l], group_major=False)

        (kvc, ks_aug, kw_aug, u, q_t, kvc_t, kvs_t, kvw_t, vs_t, vw_t, gates_t) = _inproj_prompt(
            xp, ng[0:1], w_row, w_t, n_seq, PROJ_TILE)
        kblk, vblk_t = _compress_prompt(kvc.reshape(n_seq, seq, KV_COLS), pos, proj_g)
        kw_aug = jnp.pad(kw_aug.reshape(n_seq, seq, N_KV * LANES), ((0, 0), (WINDOW, 0), (0, 0)))
        vw_t = jnp.pad(vw_t, ((0, 0), (win_tiles, 0), (0, 0), (0, 0)))
        ks_aug = jnp.pad(ks_aug.reshape(n_seq, seq, N_KV * LANES), ((0, 0), (Q_TILE, 0), (0, 0)))
        vs_t = jnp.pad(vs_t, ((0, 0), (1, 0), (0, 0), (0, 0)))
        o_nsa = _nsa_prompt(rel_bias, q_t, gates_t, kblk, vblk_t, bcmp, near, ks_aug, vs_t, kw_aug, vw_t)
        zeros = jnp.zeros((n_seq, N_STATE), F32)
        o_ssm, h_re, h_im = _ssm(u, zeros, zeros, ssm_w, *ssm_tail, n_seq, SCAN_STEPS)
        mkv = _norm_matmul(mem_prompt.reshape(n_seq * n_mem, D_MODEL), mem_norm_g[l][None], xkv_b, ROW_TILE)
        xp = _tail_prompt(o_nsa.reshape(n_seq * seq, NSA_WIDTH), o_ssm, xp,
                          mkv.reshape(n_seq, n_mem, 2 * MEM_WIDTH), gg, w_out_b, ng, xq_b, xo_b, wi_b, wo_b,
                          n_seq, TAIL_TILE)
        outs[0].append(_rows_minor(kvc_t))
        outs[1].append(_rows_minor(kvs_t))
        outs[2].append(_rows_minor(kvw_t[:, :, seq - WINDOW:]))
        outs[3].append(h_re.reshape(n_seq, N_SSM_GROUPS, SSM_STATE))
        outs[4].append(h_im.reshape(n_seq, N_SSM_GROUPS, SSM_STATE))
        outs[5].append(mkv.reshape(n_seq, n_mem, 2, MEM_HEADS, MEM_HEAD_DIM))

        q, kv, gates, u = _inproj_sample(xs, ng[0:1], w_sample)
        kvc, kvs, kvw = (kv[:, i * KV_COLS:(i + 1) * KV_COLS] for i in range(3))
        head_group = (jnp.arange(N_HEADS)[:, None] // GQA == jnp.arange(N_KV)[None, :]).astype(F32)
        qm = (q.reshape(n_dec, N_HEADS, 1, HEAD_DIM) * head_group[None, :, :, None]).reshape(n_dec, N_HEADS, LANES)
        gates_h = gates[:, :N_HEADS * N_BRANCH].reshape(n_dec, N_HEADS, N_BRANCH)
        gates_h = jnp.pad(gates_h, ((0, 0), (0, 0), (0, LANES - N_BRANCH)))
        blocks_t = _compress_sample(_rows_minor_view(cache_kv_cmp[l]), pt_flat, n_dec, n_pages, pos, proj_c)
        o_cmp, ids = _cmp_sample(qm, blocks_t, kvc.reshape(n_dec, 1, KV_COLS), pos, proj_c, bc_s, bn_s, past_len)
        ids = ids[:, :N_SEL]
        past_page = jnp.minimum(ids, past_len // SEL_BLOCK - 1) // (PAGE_SIZE // SEL_BLOCK)
        sel_pages = jnp.take_along_axis(jnp.repeat(page_table, N_KV, axis=0), past_page, axis=1)
        pool_sel = cache_kv_sel[l].transpose(0, 2, 3, 4, 1)
        o_nsa, new_buf_t = _sel_win_sample(
            ids.reshape(-1), sel_pages.reshape(-1), pool_sel, _rows_minor_view(cache_kv_win[l]), qm, kvs.reshape(n_dec, 1, KV_COLS),
            kvw.reshape(n_dec, KV_COLS, 1), gates_h, o_cmp, bw_s, bs_s, b0_s, n_pages)
        o_ssm, h_re, h_im = _ssm(u, state_ssm_re[l].reshape(n_dec, N_STATE), state_ssm_im[l].reshape(n_dec, N_STATE),
                                 ssm_w, *ssm_tail, n_dec, 1)
        x1, qx = _merge(o_nsa.reshape(n_dec, NSA_WIDTH), o_ssm, xs, gg, w_out_b, ng, xq_b, 1, n_dec)
        oa = _cross_sample(qx, cache_mem_kv[l].reshape(n_dec, n_mem * 2 * MEM_HEADS, MEM_HEAD_DIM), n_mem)
        xs = _ffn(x1, oa, ng, xo_b, wi_b, wo_b, n_dec)
        outs[6].append(kvc.reshape(n_dec, 1, *kv5))
        outs[7].append(kvs.reshape(n_dec, 1, *kv5))
        outs[8].append(_rows_minor(new_buf_t))
        outs[9].append(h_re.reshape(n_dec, N_SSM_GROUPS, SSM_STATE))
        outs[10].append(h_im.reshape(n_dec, N_SSM_GROUPS, SSM_STATE))

    stacked = [jnp.stack(o, axis=0) for o in outs]
    return (xp.reshape(x_prompt.shape), xs.reshape(x_sample.shape), *stacked)
```

```python
import functools
import math

import numpy as np
import jax
import jax.numpy as jnp
from jax import lax
from jax.experimental import pallas as pl
from jax.experimental.pallas import tpu as pltpu

F32 = jnp.float32
BF16 = jnp.bfloat16
I32 = jnp.int32

D_MODEL = 1024
HEAD_DIM = 64
N_HEADS = 8
N_KV = 2
GQA = N_HEADS // N_KV
NSA_WIDTH = N_HEADS * HEAD_DIM
SSM_WIDTH = 512
KV_COLS = 2 * N_KV * HEAD_DIM
KV_HALF = N_KV * HEAD_DIM
N_BRANCH = 3
CMP_BLOCK = 32
SEL_BLOCK = 64
N_SEL = 16
WINDOW = 512
PAGE_SIZE = 128
SSM_GROUP = 16
N_SSM_GROUPS = 32
SSM_STATE = 64
N_STATE = N_SSM_GROUPS * SSM_STATE
N_BUCKETS = 32
MAX_DISTANCE = 128
MEM_HEADS = 4
MEM_HEAD_DIM = 128
MEM_WIDTH = MEM_HEADS * MEM_HEAD_DIM
EPS = 1e-6
NEG = -1e30
MASKED_BELOW = -1e29
TINY = 1e-30
FORCE = 1e4
SCALE = HEAD_DIM ** -0.5
MEM_SCALE = MEM_HEAD_DIM ** -0.5

LANES = 128
Q_TILE = 256
FAR_TILES = 2
FFN_CHUNKS = 2
VMEM_LIMIT = 56 * 1024 * 1024

AUG_SEL = HEAD_DIM
AUG_ONE = AUG_SEL + 32
GATE_ROWS = 16
VT_ROWS = HEAD_DIM + 16
LOG2E = math.log2(math.e)


def _bucket_starts():
    n = np.arange(0, 4 * MAX_DISTANCE)
    exact = N_BUCKETS // 2
    nf = np.maximum(n, exact).astype(np.float32)
    big = exact + (np.log(nf / exact) / np.float32(math.log(MAX_DISTANCE / exact))
                   * (N_BUCKETS - exact)).astype(np.int32)
    bucket = np.where(n < exact, n, np.minimum(big, N_BUCKETS - 1))
    return [int(np.argmax(bucket >= k)) for k in range(N_BUCKETS)]


_BUCKET_START = _bucket_starts()
assert _BUCKET_START[-1] <= Q_TILE


def _params(*sem):
    return pltpu.CompilerParams(dimension_semantics=sem, vmem_limit_bytes=VMEM_LIMIT)


def _bdot(a, b):
    return jnp.dot(a.astype(BF16), b.astype(BF16), preferred_element_type=F32)


def _bdot_nt(a, b):
    return lax.dot_general(a.astype(BF16), b.astype(BF16), (((1,), (1,)), ((), ())),
                           preferred_element_type=F32)


def _split(a):
    hi = a.astype(BF16)
    return hi, (a - hi.astype(F32)).astype(BF16)


def _dot3(a, b):
    ah, al = _split(a)
    bh, bl = _split(b)
    d = functools.partial(jnp.dot, preferred_element_type=F32)
    return d(ah, bh) + (d(ah, bl) + d(al, bh))


def _dot3_nt(a, b):
    ah, al = _split(a)
    bh, bl = _split(b)
    d = functools.partial(lax.dot_general, dimension_numbers=(((1,), (1,)), ((), ())),
                          preferred_element_type=F32)
    return d(ah, bh) + (d(ah, bl) + d(al, bh))


def _rms(x, g):
    return x * lax.rsqrt(jnp.mean(x * x, axis=-1, keepdims=True) + EPS) * g


def _masked_softmax(s, mask, axis=-1):
    s = jnp.where(mask, s, NEG)
    m = jnp.max(s, axis=axis, keepdims=True)
    e = jnp.where(mask, jnp.exp(s - m), 0.0)
    return e * (1.0 / jnp.maximum(jnp.sum(e, axis=axis, keepdims=True), TINY))


def _bias_of_dist(dist, tbl_ref, h):
    v = jnp.full(dist.shape, tbl_ref[0, h], F32)
    for k in range(1, N_BUCKETS):
        v = jnp.where(dist >= _BUCKET_START[k], tbl_ref[k, h], v)
    return v


def _iota2(shape, dim):
    return lax.broadcasted_iota(I32, shape, dim)


def _rows_minor(a):
    n, _, rows = a.shape
    return a.reshape(n, 2, N_KV, HEAD_DIM, rows).transpose(0, 4, 1, 2, 3)


def _rows_minor_view(a):
    n, rows = a.shape[:2]
    return a.transpose(0, 2, 3, 4, 1).reshape(n, KV_COLS, rows)


def _cmp_block_of_row(row, n_cmp):
    return jnp.where(row < n_cmp // 2, 2 * row, 2 * (row - n_cmp // 2) + 1)


def _prompt_bias_kernel(tbl_ref, bcmp_ref, near_ref, *, n_cmp, nq):
    h = pl.program_id(0)
    last = tbl_ref[N_BUCKETS - 1, h]

    blk = _cmp_block_of_row(_iota2((n_cmp, Q_TILE), 0), n_cmp)
    for qi in range(nq):
        dist = qi * Q_TILE + _iota2((n_cmp, Q_TILE), 1) - (blk * CMP_BLOCK + (CMP_BLOCK - 1))
        bcmp_ref[qi, 0] = jnp.where(dist >= 0, _bias_of_dist(dist, tbl_ref, h), NEG)

    key = _iota2((Q_TILE, Q_TILE), 0)
    qry = _iota2((Q_TILE, Q_TILE), 1)
    d0 = qry - key
    near_ref[0, 0] = jnp.where(d0 >= 0, (_bias_of_dist(d0, tbl_ref, h) - last) * LOG2E, NEG)
    d1 = Q_TILE + qry - key
    near_ref[0, 1] = (_bias_of_dist(d1, tbl_ref, h) - last) * LOG2E
    near_ref[0, 2] = jnp.where(key > qry, 0.0, NEG)


def _prompt_bias(rel_bias, seq):
    nq = seq // Q_TILE
    n_cmp = seq // CMP_BLOCK
    return pl.pallas_call(
        functools.partial(_prompt_bias_kernel, n_cmp=n_cmp, nq=nq),
        grid=(N_HEADS,),
        in_specs=[pl.BlockSpec(memory_space=pltpu.SMEM)],
        out_specs=[pl.BlockSpec((nq, 1, n_cmp, Q_TILE), lambda h: (0, h // GQA, 0, h % GQA)),
                   pl.BlockSpec((1, 3, Q_TILE, Q_TILE), lambda h: (h // GQA, 0, 0, h % GQA))],
        out_shape=[jax.ShapeDtypeStruct((nq, N_KV, n_cmp, GQA * Q_TILE), F32),
                   jax.ShapeDtypeStruct((N_KV, 3, Q_TILE, GQA * Q_TILE), F32)],
        compiler_params=_params("arbitrary"),
    )(rel_bias)


OFF_KV = NSA_WIDTH
OFF_GATE = OFF_KV + 3 * KV_COLS
OFF_U = OFF_GATE + N_HEADS * N_BRANCH


def _k_cols(w_in, which):
    base = OFF_KV + which * KV_COLS
    return [w_in[:, base + g * HEAD_DIM: base + (g + 1) * HEAD_DIM] for g in range(N_KV)]


def _pack_w_prompt(w_in):
    w_row = jnp.concatenate([w_in[:, OFF_KV:OFF_KV + KV_COLS]] + _k_cols(w_in, 1) + _k_cols(w_in, 2)
                            + [w_in[:, OFF_U:]], axis=1)
    gates = w_in[:, OFF_GATE:OFF_U].reshape(-1, N_KV, GQA * N_BRANCH)
    gates = jnp.pad(gates, ((0, 0), (0, 0), (0, GATE_ROWS - GQA * N_BRANCH))).reshape(-1, N_KV * GATE_ROWS)
    w_t = jnp.concatenate([w_in[:, :OFF_GATE], gates], axis=1).T
    return w_row.astype(BF16), w_t.astype(BF16)


def _inproj_prompt_kernel(x_ref, g_ref, wr_ref, wt_ref, kvc_ref, ksa_ref, kwa_ref, u_ref, qt_ref, kvct_ref,
                          kvst_ref, kvwt_ref, vst_ref, vwt_ref, gt_ref, *, tile):
    hb = _rms(x_ref[...], g_ref[...]).astype(BF16)
    z = jnp.dot(hb, wr_ref[...], preferred_element_type=F32)
    kvc_ref[...] = z[:, :KV_COLS]
    u_ref[...] = z[:, KV_COLS + 4 * HEAD_DIM:]

    pos = pl.program_id(1) * tile + _iota2((tile, LANES - HEAD_DIM), 0)
    col = _iota2((tile, LANES - HEAD_DIM), 1)
    ones = ((col >= AUG_ONE - HEAD_DIM) & (col < AUG_ONE - HEAD_DIM + 2)).astype(F32)
    aug_sel = jnp.where(col == pos // SEL_BLOCK, -NEG, ones)
    k0 = KV_COLS
    ksa_ref[...] = jnp.concatenate(
        [z[:, k0:k0 + HEAD_DIM], aug_sel, z[:, k0 + HEAD_DIM:k0 + 2 * HEAD_DIM], aug_sel], axis=1).astype(BF16)
    k0 = KV_COLS + 2 * HEAD_DIM
    kwa_ref[...] = jnp.concatenate(
        [z[:, k0:k0 + HEAD_DIM], ones, z[:, k0 + HEAD_DIM:k0 + 2 * HEAD_DIM], ones], axis=1).astype(BF16)

    zt = lax.dot_general(wt_ref[...], hb, (((1,), (1,)), ((), ())), preferred_element_type=F32)
    qt_ref[0] = zt[:NSA_WIDTH]
    kvct_ref[0] = zt[OFF_KV:OFF_KV + KV_COLS]
    kvst_ref[0] = zt[OFF_KV + KV_COLS:OFF_KV + 2 * KV_COLS]
    kvwt_ref[0] = zt[OFF_KV + 2 * KV_COLS:OFF_GATE]
    gt_ref[0] = jax.nn.sigmoid(zt[OFF_GATE:])
    v_sel = OFF_KV + KV_COLS + KV_HALF
    v_win = OFF_KV + 2 * KV_COLS + KV_HALF
    sum_rows = (_iota2((VT_ROWS - HEAD_DIM, Q_TILE), 0) == 0).astype(F32)
    for j in range(tile // Q_TILE):
        cols = slice(j * Q_TILE, (j + 1) * Q_TILE)
        for ref, v0 in ((vst_ref, v_sel), (vwt_ref, v_win)):
            parts = []
            for g in range(N_KV):
                parts += [zt[v0 + g * HEAD_DIM:v0 + (g + 1) * HEAD_DIM, cols], sum_rows]
            ref[0, j] = jnp.concatenate(parts, axis=0).astype(BF16)


def _inproj_prompt(x, g, w_row, w_t, n_seq, tile):
    rows = x.shape[0]
    seq = rows // n_seq
    assert seq // SEL_BLOCK <= AUG_ONE - AUG_SEL
    nt = seq // tile
    row_map = lambda b, t: (b * nt + t, 0)
    t_map = lambda b, t: (b, 0, t)
    tiles = tile // Q_TILE
    n_gate = N_KV * GATE_ROWS
    out = [
        (pl.BlockSpec((tile, KV_COLS), row_map), jax.ShapeDtypeStruct((rows, KV_COLS), F32)),
        (pl.BlockSpec((tile, N_KV * LANES), row_map), jax.ShapeDtypeStruct((rows, N_KV * LANES), BF16)),
        (pl.BlockSpec((tile, N_KV * LANES), row_map), jax.ShapeDtypeStruct((rows, N_KV * LANES), BF16)),
        (pl.BlockSpec((tile, SSM_WIDTH), lambda b, t: (t, b)), jax.ShapeDtypeStruct((seq, n_seq * SSM_WIDTH), F32)),
        (pl.BlockSpec((1, NSA_WIDTH, tile), t_map), jax.ShapeDtypeStruct((n_seq, NSA_WIDTH, seq), F32)),
        (pl.BlockSpec((1, KV_COLS, tile), t_map), jax.ShapeDtypeStruct((n_seq, KV_COLS, seq), F32)),
        (pl.BlockSpec((1, KV_COLS, tile), t_map), jax.ShapeDtypeStruct((n_seq, KV_COLS, seq), F32)),
        (pl.BlockSpec((1, KV_COLS, tile), t_map), jax.ShapeDtypeStruct((n_seq, KV_COLS, seq), F32)),
        (pl.BlockSpec((1, tiles, N_KV * VT_ROWS, Q_TILE), lambda b, t: (b, t, 0, 0)),
         jax.ShapeDtypeStruct((n_seq, seq // Q_TILE, N_KV * VT_ROWS, Q_TILE), BF16)),
        (pl.BlockSpec((1, tiles, N_KV * VT_ROWS, Q_TILE), lambda b, t: (b, t, 0, 0)),
         jax.ShapeDtypeStruct((n_seq, seq // Q_TILE, N_KV * VT_ROWS, Q_TILE), BF16)),
        (pl.BlockSpec((1, n_gate, tile), t_map), jax.ShapeDtypeStruct((n_seq, n_gate, seq), F32)),
    ]
    return pl.pallas_call(
        functools.partial(_inproj_prompt_kernel, tile=tile),
        grid=(n_seq, nt),
        in_specs=[pl.BlockSpec((tile, D_MODEL), row_map),
                  pl.BlockSpec((1, D_MODEL), lambda b, t: (0, 0)),
                  pl.BlockSpec(w_row.shape, lambda b, t: (0, 0)),
                  pl.BlockSpec(w_t.shape, lambda b, t: (0, 0))],
        out_specs=[o[0] for o in out],
        out_shape=[o[1] for o in out],
        compiler_params=_params("arbitrary", "arbitrary"),
    )(x, g, w_row, w_t)


S_Q = 0
S_KV = S_Q + NSA_WIDTH
S_GATE = S_KV + 3 * KV_COLS
S_U = S_GATE + LANES
S_END = S_U + SSM_WIDTH


def _pack_w_sample(w_in):
    gates = jnp.pad(w_in[:, OFF_GATE:OFF_U], ((0, 0), (0, LANES - N_HEADS * N_BRANCH)))
    return jnp.concatenate([w_in[:, :OFF_GATE], gates, w_in[:, OFF_U:]], axis=1).astype(BF16)


def _inproj_sample_kernel(x_ref, g_ref, w_ref, q_ref, kv_ref, gate_ref, u_ref):
    z = jnp.dot(_rms(x_ref[...], g_ref[...]).astype(BF16), w_ref[...], preferred_element_type=F32)
    q_ref[...] = z[:, S_Q:S_KV]
    kv_ref[...] = z[:, S_KV:S_GATE]
    gate_ref[...] = jax.nn.sigmoid(z[:, S_GATE:S_U])
    u_ref[...] = z[:, S_U:S_END]


def _inproj_sample(x, g, w):
    rows = x.shape[0]
    vm = pl.BlockSpec(memory_space=pltpu.VMEM)
    widths = [NSA_WIDTH, 3 * KV_COLS, LANES, SSM_WIDTH]
    return pl.pallas_call(
        _inproj_sample_kernel,
        in_specs=[vm] * 3, out_specs=[vm] * 4,
        out_shape=[jax.ShapeDtypeStruct((rows, w_), F32) for w_ in widths],
        compiler_params=pltpu.CompilerParams(vmem_limit_bytes=VMEM_LIMIT),
    )(x, g, w)


def _pack_compress(cmp_pos, cmp_proj, group_major):
    pos = jnp.broadcast_to(cmp_pos.transpose(1, 0, 2)[:, :, None, :],
                           (CMP_BLOCK, 2, N_KV, HEAD_DIM)).reshape(CMP_BLOCK, KV_COLS)
    eye = jnp.eye(2 * N_KV, dtype=F32).reshape(2, N_KV, 2, N_KV)
    order = 'cgdGCe' if group_major else 'cgdCGe'
    proj = jnp.einsum('cde,cgCG->' + order, cmp_proj, eye).reshape(KV_COLS, KV_COLS)
    return pos, proj


def _compress_prompt_kernel(kv_ref, pos_ref, proj_ref, projt_ref, kblk_ref, vblkt_ref):
    x = kv_ref[0]
    n_sel = x.shape[0] // SEL_BLOCK
    x = x.reshape(n_sel, SEL_BLOCK // CMP_BLOCK, CMP_BLOCK, KV_COLS)
    pos = pos_ref[...][None]
    sums = jnp.concatenate([jnp.sum(x[:, i] * pos, axis=1) for i in range(SEL_BLOCK // CMP_BLOCK)], axis=0)
    blk = _dot3(sums, proj_ref[...])
    blk_t = _dot3_nt(projt_ref[...], sums)
    for g in range(N_KV):
        kblk_ref[0, g] = blk[:, g * KV_HALF:g * KV_HALF + HEAD_DIM]
        vblkt_ref[0, g] = blk_t[g * KV_HALF + HEAD_DIM:(g + 1) * KV_HALF]


def _compress_prompt(kvc, pos, proj):
    n_seq, seq, _ = kvc.shape
    n_cmp = seq // CMP_BLOCK
    return pl.pallas_call(
        _compress_prompt_kernel,
        grid=(n_seq,),
        in_specs=[pl.BlockSpec((1, seq, KV_COLS), lambda b: (b, 0, 0)),
                  pl.BlockSpec((CMP_BLOCK, KV_COLS), lambda b: (0, 0)),
                  pl.BlockSpec((KV_COLS, KV_COLS), lambda b: (0, 0)),
                  pl.BlockSpec((KV_COLS, KV_COLS), lambda b: (0, 0))],
        out_specs=[pl.BlockSpec((1, N_KV, n_cmp, HEAD_DIM), lambda b: (b, 0, 0, 0)),
                   pl.BlockSpec((1, N_KV, HEAD_DIM, n_cmp), lambda b: (b, 0, 0, 0))],
        out_shape=[jax.ShapeDtypeStruct((n_seq, N_KV, n_cmp, HEAD_DIM), F32),
                   jax.ShapeDtypeStruct((n_seq, N_KV, HEAD_DIM, n_cmp), F32)],
        compiler_params=_params("arbitrary"),
    )(kvc, pos, proj, proj.T)


def _select_blocks_t(pair, q0):
    row = _iota2(pair.shape, 0)
    n_sel = pair.shape[0]
    cur = (q0 + _iota2(pair.shape, 1)) // SEL_BLOCK
    forced = (row == 0) | (row == cur) | (row == cur - 1)
    score = jnp.where(row <= cur, jnp.where(forced, pair + FORCE, pair), -jnp.inf)
    chosen = jnp.zeros(pair.shape, F32)
    for _ in range(N_SEL):
        m = jnp.max(score, axis=0, keepdims=True)
        hit = (score == m) & (m > -jnp.inf)
        first = jnp.min(jnp.where(hit, row, n_sel), axis=0, keepdims=True)
        pick = row == first
        chosen = jnp.where(pick, 1.0, chosen)
        score = jnp.where(pick, -jnp.inf, score)
    return chosen


def _attend(state, k_aug, v_t, q_aug, extras):
    m, acc = state
    s = jnp.dot(k_aug, q_aug, preferred_element_type=F32)
    if any(t is not None or c is not None for t, c in extras):
        rows = []
        for t, c in extras:
            if t is None:
                rows.append(jnp.full((Q_TILE, s.shape[1]), 0.0 if c is None else c, F32))
            else:
                rows.append(t if c is None else t + c)
        s = s + jnp.concatenate(rows, axis=0)
    m_new = jnp.maximum(m, jnp.max(s, axis=0, keepdims=True))
    e = jnp.exp2(s - m_new)
    acc = jnp.exp2(m - m_new) * acc + jnp.dot(v_t, e.astype(BF16), preferred_element_type=F32)
    return m_new, acc


def _attend_result(state):
    acc = state[1]
    return acc[:HEAD_DIM] * (1.0 / jnp.maximum(acc[HEAD_DIM:HEAD_DIM + 1], TINY))


def _nsa_prompt_kernel(tbl_ref, qt_ref, gt_ref, kblk_ref, vblkt_ref, bcmp_ref, near_ref, ks_ref, vst_ref,
                       kw_ref, vwt_ref, o_ref):
    g = pl.program_id(1)
    qi = pl.program_id(2)
    q0 = qi * Q_TILE
    win_tiles = WINDOW // Q_TILE
    v_rows = pl.ds(pl.multiple_of(g * VT_ROWS, VT_ROWS), VT_ROWS)

    width = GQA * Q_TILE
    q = jnp.concatenate([qt_ref[0, r * HEAD_DIM:(r + 1) * HEAD_DIM, :] for r in range(GQA)], axis=1) * SCALE

    row = _iota2((LANES - AUG_ONE, Q_TILE), 0)
    consts = []
    for r in range(GQA):
        last = jnp.full((LANES - AUG_ONE, Q_TILE), tbl_ref[N_BUCKETS - 1, g * GQA + r] * LOG2E, F32)
        hi = last.astype(BF16).astype(F32)
        consts.append(jnp.where(row == 0, hi, jnp.where(row == 1, last - hi, 0.0)).astype(BF16))
    consts = jnp.concatenate(consts, axis=1)
    q_log2 = (q * LOG2E).astype(BF16)
    near0, near1, oldest = near_ref[0, 0], near_ref[0, 1], near_ref[0, 2]
    init = (jnp.full((1, width), NEG, F32), jnp.zeros((VT_ROWS, width), F32))

    def key_rows(ref, tile, n_tiles):
        return ref[0, pl.ds(pl.multiple_of(tile * Q_TILE, Q_TILE), n_tiles * Q_TILE), :]

    def value_cols(ref, tile, n_tiles):
        return jnp.concatenate([ref[0, tile + j, v_rows, :] for j in range(n_tiles)], axis=1)

    def tile_mask(ok):
        return jnp.where(ok, 0.0, NEG)

    q_aug = jnp.concatenate([q_log2, jnp.zeros((AUG_ONE - AUG_SEL, width), BF16), consts], axis=0)
    extras = []
    for mt in range(win_tiles):
        table = oldest if mt == 0 else near1 if mt == win_tiles - 1 else None
        extras.append((table, tile_mask(qi + mt >= win_tiles)))
    extras.append((near0, None))
    st = _attend(init, key_rows(kw_ref, qi, win_tiles + 1), value_cols(vwt_ref, qi, win_tiles + 1), q_aug, extras)
    o_win = _attend_result(st)

    k_cmp, v_cmp_t = kblk_ref[0, 0], vblkt_ref[0, 0]
    n_cmp = k_cmp.shape[0]
    bias = bcmp_ref[0, 0]
    p = _masked_softmax(_dot3(k_cmp, q) + bias, bias > MASKED_BELOW, axis=0)
    o_cmp = _bdot(v_cmp_t, p)
    ps = p[:, :Q_TILE]
    for r in range(1, GQA):
        ps = ps + p[:, r * Q_TILE:(r + 1) * Q_TILE]
    chosen = _select_blocks_t(ps[:n_cmp // 2] + ps[n_cmp // 2:], q0)
    not_chosen = (chosen - 1.0).astype(BF16)
    pad_rows = (AUG_ONE - AUG_SEL) - not_chosen.shape[0]
    if pad_rows:
        not_chosen = jnp.concatenate([not_chosen, jnp.zeros((pad_rows, Q_TILE), BF16)], axis=0)
    q_aug = jnp.concatenate([q_log2, jnp.concatenate([not_chosen] * GQA, axis=1), consts], axis=0)

    extras = [(near1, tile_mask(qi >= 1)), (near0, None)]
    st = _attend(init, key_rows(ks_ref, qi, 2), value_cols(vst_ref, qi, 2), q_aug, extras)
    n_far = jnp.maximum(qi - 1, 0)

    def far_chunk(c, st, masked=False):
        t0 = c * FAR_TILES
        extras = [(None, tile_mask(t0 + j < n_far) if masked else None) for j in range(FAR_TILES)]
        return _attend(st, key_rows(ks_ref, t0 + 1, FAR_TILES), value_cols(vst_ref, t0 + 1, FAR_TILES), q_aug, extras)

    full = n_far // FAR_TILES
    st = lax.fori_loop(0, full, far_chunk, st)
    st = lax.cond(n_far % FAR_TILES != 0, lambda st=st: far_chunk(full, st, masked=True), lambda st=st: st)
    o_sel = _attend_result(st)

    def gate(branch):
        return jnp.concatenate([gt_ref[0, r * N_BRANCH + branch:r * N_BRANCH + branch + 1, :]
                                for r in range(GQA)], axis=1)

    o = gate(0) * o_cmp + gate(1) * o_sel + gate(2) * o_win
    o_ref[0] = jnp.concatenate([o[:, r * Q_TILE:(r + 1) * Q_TILE] for r in range(GQA)], axis=0).T


def _nsa_prompt(rel_bias, q_t, gates_t, kblk, vblk_t, bcmp, near, ks_aug, vs_t, kw_aug, vw_t):
    n_seq, _, seq = q_t.shape
    nq = seq // Q_TILE
    assert nq % FAR_TILES == 0
    n_cmp = kblk.shape[2]
    gw = GQA * HEAD_DIM
    wt = WINDOW // Q_TILE
    return pl.pallas_call(
        _nsa_prompt_kernel,
        grid=(n_seq, N_KV, nq),
        in_specs=[pl.BlockSpec(memory_space=pltpu.SMEM),
                  pl.BlockSpec((1, gw, Q_TILE), lambda b, g, qi: (b, g, qi)),
                  pl.BlockSpec((1, GATE_ROWS, Q_TILE), lambda b, g, qi: (b, g, qi)),
                  pl.BlockSpec((1, 1, n_cmp, HEAD_DIM), lambda b, g, qi: (b, g, 0, 0)),
                  pl.BlockSpec((1, 1, HEAD_DIM, n_cmp), lambda b, g, qi: (b, g, 0, 0)),
                  pl.BlockSpec((1, 1, n_cmp, GQA * Q_TILE), lambda b, g, qi: (qi, g, 0, 0)),
                  pl.BlockSpec((1, 3, Q_TILE, GQA * Q_TILE), lambda b, g, qi: (g, 0, 0, 0)),
                  pl.BlockSpec((1, Q_TILE + seq, LANES), lambda b, g, qi: (b, 0, g)),
                  pl.BlockSpec((1, 1 + nq, N_KV * VT_ROWS, Q_TILE), lambda b, g, qi: (b, 0, 0, 0)),
                  pl.BlockSpec((1, WINDOW + seq, LANES), lambda b, g, qi: (b, 0, g)),
                  pl.BlockSpec((1, wt + nq, N_KV * VT_ROWS, Q_TILE), lambda b, g, qi: (b, 0, 0, 0))],
        out_specs=pl.BlockSpec((1, Q_TILE, gw), lambda b, g, qi: (b, qi, g)),
        out_shape=jax.ShapeDtypeStruct((n_seq, seq, NSA_WIDTH), F32),
        compiler_params=_params("arbitrary", "arbitrary", "arbitrary"),
    )(rel_bias, q_t, gates_t, kblk, vblk_t, bcmp, near, ks_aug, vs_t, kw_aug, vw_t)


STATE_CHUNK = 512
SCAN_UNROLL = 4
SSM_KB = SSM_WIDTH // LANES
GROUPS_PER_KB = N_SSM_GROUPS // SSM_KB


def _ssm_disc_kernel(are_ref, aim_ref, ldt_ref, bre_ref, bim_ref, abr_ref, abi_ref, bbr_ref, bbi_ref):
    a_re, a_im = are_ref[...], aim_ref[...]
    dt = jnp.exp(ldt_ref[...])
    mag = jnp.exp(a_re * dt)
    ab_re = mag * jnp.cos(a_im * dt)
    ab_im = mag * jnp.sin(a_im * dt)
    den = a_re * a_re + a_im * a_im
    co_re = ((ab_re - 1.0) * a_re + ab_im * a_im) / den
    co_im = (ab_im * a_re - (ab_re - 1.0) * a_im) / den
    abr_ref[...] = ab_re
    abi_ref[...] = ab_im
    b_re, b_im = bre_ref[...], bim_ref[...]
    bbr_ref[...] = co_re[:, None, :] * b_re - co_im[:, None, :] * b_im
    bbi_ref[...] = co_re[:, None, :] * b_im + co_im[:, None, :] * b_re


def _block_diag(w):
    a, b = w.shape[1:]
    w = w.reshape(SSM_KB, GROUPS_PER_KB, a, b)
    eye = jnp.eye(GROUPS_PER_KB, dtype=w.dtype)
    return jnp.einsum('kgab,gh->kgahb', w, eye).reshape(SSM_KB, GROUPS_PER_KB * a, GROUPS_PER_KB * b)


def _ssm_weights(a_re, a_im, b_re, b_im, c_re, c_im, log_dt):
    ng, p = a_re.shape
    vm = pl.BlockSpec(memory_space=pltpu.VMEM)
    ab_re, ab_im, bb_re, bb_im = pl.pallas_call(
        _ssm_disc_kernel,
        in_specs=[vm] * 5, out_specs=[vm] * 4,
        out_shape=[jax.ShapeDtypeStruct((ng, p), F32)] * 2
        + [jax.ShapeDtypeStruct((ng, SSM_GROUP, p), F32)] * 2,
    )(a_re, a_im, log_dt.reshape(ng, 1), b_re.transpose(0, 2, 1), b_im.transpose(0, 2, 1))
    return (ab_re.reshape(1, N_STATE), ab_im.reshape(1, N_STATE),
            _block_diag(bb_re).astype(BF16), _block_diag(bb_im).astype(BF16),
            _block_diag(c_re.transpose(0, 2, 1)).astype(BF16),
            _block_diag(c_im.transpose(0, 2, 1)).astype(BF16))


def _ssm_kernel(u_ref, h0r_ref, h0i_ref, abr_ref, abi_ref, bbr_ref, bbi_ref, ccr_ref, cci_ref,
                d_ref, gw_ref, gb_ref, o_ref, hr_ref, hi_ref, sre, sim, *rows_scratch, nb, steps):
    @pl.when(pl.program_id(0) == 0)
    def _():
        hr_ref[...] = h0r_ref[...]
        hi_ref[...] = h0i_ref[...]

    if rows_scratch:
        rows_ref, = rows_scratch
        for b in range(nb):
            for c in range(SSM_KB):
                col = b * SSM_WIDTH + c * LANES
                rows_ref[c, pl.ds(b, steps, stride=nb), :] = u_ref[:, col:col + LANES]
        u = jnp.concatenate([rows_ref[c] for c in range(SSM_KB)], axis=1)
    else:
        u = u_ref[...]
    ub = u.astype(BF16)
    kw = N_STATE // SSM_KB
    for kb in range(SSM_KB):
        uk = ub[:, kb * LANES:(kb + 1) * LANES]
        sre[:, kb * kw:(kb + 1) * kw] = jnp.dot(uk, bbr_ref[kb], preferred_element_type=F32)
        sim[:, kb * kw:(kb + 1) * kw] = jnp.dot(uk, bbi_ref[kb], preferred_element_type=F32)

    if steps == 1:
        ar, ai = abr_ref[...], abi_ref[...]
        hr, hi = hr_ref[...], hi_ref[...]
        nr = ar * hr - ai * hi + sre[...]
        ni = ar * hi + ai * hr + sim[...]
        sre[...] = nr
        sim[...] = ni
        hr_ref[...] = nr
        hi_ref[...] = ni
    else:
        for cb in range(N_STATE // STATE_CHUNK):
            cols = slice(cb * STATE_CHUNK, (cb + 1) * STATE_CHUNK)
            ar = jnp.broadcast_to(abr_ref[:, cols], (nb, STATE_CHUNK))
            ai = jnp.broadcast_to(abi_ref[:, cols], (nb, STATE_CHUNK))

            def step(t, carry, cols=cols, ar=ar, ai=ai):
                hr, hi = carry
                rows = pl.ds(pl.multiple_of(t * nb, nb), nb)
                nr = ar * hr - ai * hi + sre[rows, cols]
                ni = ar * hi + ai * hr + sim[rows, cols]
                sre[rows, cols] = nr
                sim[rows, cols] = ni
                return nr, ni

            hr, hi = lax.fori_loop(0, steps, step, (hr_ref[:, cols], hi_ref[:, cols]), unroll=SCAN_UNROLL)
            hr_ref[:, cols] = hr
            hi_ref[:, cols] = hi

    ys = []
    for kb in range(SSM_KB):
        cols = slice(kb * kw, (kb + 1) * kw)
        ys.append(_bdot(sre[:, cols], ccr_ref[kb]) - _bdot(sim[:, cols], cci_ref[kb]))
    y = jnp.concatenate(ys, axis=1) + d_ref[...] * u
    g = jax.nn.gelu(y)
    out = g * jax.nn.sigmoid(_bdot(g, gw_ref[...]) + gb_ref[...])
    if rows_scratch:
        for c in range(SSM_KB):
            rows_ref[c] = out[:, c * LANES:(c + 1) * LANES]
        for b in range(nb):
            for c in range(SSM_KB):
                col = b * SSM_WIDTH + c * LANES
                o_ref[:, col:col + LANES] = rows_ref[c, pl.ds(b, steps, stride=nb), :]
    else:
        o_ref[...] = out


def _ssm(u, h0_re, h0_im, wts, d, glu_w, glu_b, nb, steps_per_call):
    ab_re, ab_im, bb_re, bb_im, cc_re, cc_im = wts
    rows = nb * steps_per_call
    wide = steps_per_call > 1
    n_calls = u.shape[0] // steps_per_call if wide else 1
    block = (steps_per_call, nb * SSM_WIDTH) if wide else (rows, SSM_WIDTH)
    const2 = lambda i: (0, 0)
    const3 = lambda i: (0, 0, 0)
    kw = N_STATE // SSM_KB
    scratch = [pltpu.VMEM((rows, N_STATE), F32), pltpu.VMEM((rows, N_STATE), F32)]
    if wide:
        scratch.append(pltpu.VMEM((SSM_KB, rows, LANES), F32))
    return pl.pallas_call(
        functools.partial(_ssm_kernel, nb=nb, steps=steps_per_call),
        grid=(n_calls,),
        in_specs=[pl.BlockSpec(block, lambda i: (i, 0)),
                  pl.BlockSpec((nb, N_STATE), const2), pl.BlockSpec((nb, N_STATE), const2),
                  pl.BlockSpec((1, N_STATE), const2), pl.BlockSpec((1, N_STATE), const2),
                  pl.BlockSpec((SSM_KB, LANES, kw), const3), pl.BlockSpec((SSM_KB, LANES, kw), const3),
                  pl.BlockSpec((SSM_KB, kw, LANES), const3), pl.BlockSpec((SSM_KB, kw, LANES), const3),
                  pl.BlockSpec((1, SSM_WIDTH), const2),
                  pl.BlockSpec((SSM_WIDTH, SSM_WIDTH), const2),
                  pl.BlockSpec((1, SSM_WIDTH), const2)],
        out_specs=[pl.BlockSpec(block, lambda i: (i, 0)),
                   pl.BlockSpec((nb, N_STATE), const2), pl.BlockSpec((nb, N_STATE), const2)],
        out_shape=[jax.ShapeDtypeStruct(u.shape, F32),
                   jax.ShapeDtypeStruct((nb, N_STATE), F32), jax.ShapeDtypeStruct((nb, N_STATE), F32)],
        scratch_shapes=scratch,
        compiler_params=_params("arbitrary"),
    )(u, h0_re, h0_im, ab_re, ab_im, bb_re, bb_im, cc_re, cc_im, d, glu_w, glu_b)


def _merge_rows(o_nsa, o_ssm, x, gg_ref, wout_ref, ng_ref, xq_ref):
    gg = gg_ref[...]
    a = _rms(o_nsa, gg[:, :NSA_WIDTH])
    b = _rms(o_ssm, gg[:, NSA_WIDTH:])
    m = (jnp.dot(a.astype(BF16), wout_ref[:NSA_WIDTH], preferred_element_type=F32)
         + jnp.dot(b.astype(BF16), wout_ref[NSA_WIDTH:], preferred_element_type=F32))
    x1 = x + _rms(m, ng_ref[1:2])
    return x1, _bdot(_rms(x1, ng_ref[2:3]), xq_ref[...])


def _cross_rows(qx, mkv_ref):
    outs = []
    for h in range(MEM_HEADS):
        cols = slice(h * MEM_HEAD_DIM, (h + 1) * MEM_HEAD_DIM)
        k = mkv_ref[0, :, cols]
        v = mkv_ref[0, :, MEM_WIDTH + h * MEM_HEAD_DIM: MEM_WIDTH + (h + 1) * MEM_HEAD_DIM]
        s = _bdot_nt(qx[:, cols], k) * MEM_SCALE
        e = jnp.exp(s - jnp.max(s, axis=-1, keepdims=True))
        p = e * (1.0 / jnp.sum(e, axis=-1, keepdims=True))
        outs.append(_bdot(p, v))
    return jnp.concatenate(outs, axis=1)


def _ffn_rows(x1, oa, ng_ref, xo_ref, wi_ref, wo_ref, d_ff):
    x2 = x1 + _rms(_bdot(oa, xo_ref[...]), ng_ref[3:4])
    h = _rms(x2, ng_ref[4:5]).astype(BF16)
    chunk = d_ff // FFN_CHUNKS
    y = None
    for c in range(0, d_ff, chunk):
        z1 = jnp.dot(h, wi_ref[:, c:c + chunk], preferred_element_type=F32)
        z2 = jnp.dot(h, wi_ref[:, d_ff + c:d_ff + c + chunk], preferred_element_type=F32)
        part = _bdot(z1 * jax.nn.sigmoid(z1) * z2, wo_ref[c:c + chunk, :])
        y = part if y is None else y + part
    return x2 + _rms(y, ng_ref[5:6])


def _tail_prompt_kernel(onsa_ref, ossm_ref, x_ref, mkv_ref, gg_ref, wout_ref, ng_ref, xq_ref, xo_ref, wi_ref,
                        wo_ref, o_ref, *, d_ff):
    x1, qx = _merge_rows(onsa_ref[...], ossm_ref[...], x_ref[...], gg_ref, wout_ref, ng_ref, xq_ref)
    o_ref[...] = _ffn_rows(x1, _cross_rows(qx, mkv_ref), ng_ref, xo_ref, wi_ref, wo_ref, d_ff)


def _tail_prompt(o_nsa, o_ssm, x, mkv, gg, w_out, ng, xq, xo, wi, wo, n_seq, tile):
    rows = x.shape[0]
    nt = rows // n_seq // tile
    d_ff = wo.shape[0]
    assert d_ff % (FFN_CHUNKS * LANES) == 0
    row_map = lambda b, t: (b * nt + t, 0)
    const = lambda b, t: (0, 0)
    once = dict(pipeline_mode=pl.Buffered(1))
    return pl.pallas_call(
        functools.partial(_tail_prompt_kernel, d_ff=d_ff),
        grid=(n_seq, nt),
        in_specs=[pl.BlockSpec((tile, NSA_WIDTH), row_map),
                  pl.BlockSpec((tile, SSM_WIDTH), lambda b, t: (t, b)),
                  pl.BlockSpec((tile, D_MODEL), row_map),
                  pl.BlockSpec((1,) + mkv.shape[1:], lambda b, t: (b, 0, 0)),
                  pl.BlockSpec((1, D_MODEL), const),
                  pl.BlockSpec((D_MODEL, D_MODEL), const, **once),
                  pl.BlockSpec((6, D_MODEL), const),
                  pl.BlockSpec((D_MODEL, MEM_WIDTH), const, **once),
                  pl.BlockSpec((MEM_WIDTH, D_MODEL), const, **once),
                  pl.BlockSpec((D_MODEL, 2 * d_ff), const, **once),
                  pl.BlockSpec((d_ff, D_MODEL), const, **once)],
        out_specs=pl.BlockSpec((tile, D_MODEL), row_map),
        out_shape=jax.ShapeDtypeStruct((rows, D_MODEL), F32),
        compiler_params=_params("arbitrary", "arbitrary"),
    )(o_nsa, o_ssm, x, mkv, gg, w_out, ng, xq, xo, wi, wo)


def _merge_kernel(onsa_ref, ossm_ref, x_ref, gg_ref, wout_ref, ng_ref, xq_ref, x1_ref, qx_ref):
    x1_ref[...], qx_ref[...] = _merge_rows(onsa_ref[...], ossm_ref[...], x_ref[...], gg_ref, wout_ref, ng_ref,
                                           xq_ref)


def _merge(o_nsa, o_ssm, x, gg, w_out, ng, xq, n_seq, tile):
    rows = x.shape[0]
    nt = rows // n_seq // tile
    row_map = lambda b, t: (b * nt + t, 0)
    const = lambda b, t: (0, 0)
    return pl.pallas_call(
        _merge_kernel,
        grid=(n_seq, nt),
        in_specs=[pl.BlockSpec((tile, NSA_WIDTH), row_map),
                  pl.BlockSpec((tile, SSM_WIDTH), lambda b, t: (t, b)),
                  pl.BlockSpec((tile, D_MODEL), row_map),
                  pl.BlockSpec((1, D_MODEL), const),
                  pl.BlockSpec((D_MODEL, D_MODEL), const),
                  pl.BlockSpec((6, D_MODEL), const),
                  pl.BlockSpec((D_MODEL, MEM_WIDTH), const)],
        out_specs=[pl.BlockSpec((tile, D_MODEL), row_map), pl.BlockSpec((tile, MEM_WIDTH), row_map)],
        out_shape=[jax.ShapeDtypeStruct((rows, D_MODEL), F32),
                   jax.ShapeDtypeStruct((rows, MEM_WIDTH), F32)],
        compiler_params=_params("arbitrary", "arbitrary"),
    )(o_nsa, o_ssm, x, gg, w_out, ng, xq)


def _norm_matmul_kernel(x_ref, g_ref, w_ref, o_ref):
    o_ref[...] = _bdot(_rms(x_ref[...], g_ref[...]), w_ref[...])


def _norm_matmul(x, g, w, tile):
    rows, k = x.shape
    n = w.shape[1]
    return pl.pallas_call(
        _norm_matmul_kernel,
        grid=(rows // tile,),
        in_specs=[pl.BlockSpec((tile, k), lambda i: (i, 0)),
                  pl.BlockSpec((1, k), lambda i: (0, 0)),
                  pl.BlockSpec((k, n), lambda i: (0, 0))],
        out_specs=pl.BlockSpec((tile, n), lambda i: (i, 0)),
        out_shape=jax.ShapeDtypeStruct((rows, n), F32),
        compiler_params=_params("arbitrary"),
    )(x, g, w)


SAMPLE_GROUP = 8


def _cross_sample_kernel(qx_ref, mkv_ref, o_ref, *, n_mem):
    per_tok = 2 * MEM_HEADS
    sub = _iota2((SAMPLE_GROUP, MEM_HEAD_DIM), 0)
    qx = qx_ref[...]
    outs = []
    for i in range(SAMPLE_GROUP):
        s = jnp.zeros((SAMPLE_GROUP, n_mem), F32)
        for h in range(MEM_HEADS):
            qh = jnp.broadcast_to(qx[i:i + 1, h * MEM_HEAD_DIM:(h + 1) * MEM_HEAD_DIM], sub.shape)
            k = mkv_ref[i, pl.ds(h, n_mem, stride=per_tok), :]
            s = s + _bdot_nt(jnp.where(sub == h, qh, 0.0), k)
        s = s * MEM_SCALE
        e = jnp.exp(s - jnp.max(s, axis=-1, keepdims=True))
        p = e * (1.0 / jnp.sum(e, axis=-1, keepdims=True))
        heads = []
        for h in range(MEM_HEADS):
            v = mkv_ref[i, pl.ds(MEM_HEADS + h, n_mem, stride=per_tok), :]
            heads.append(_bdot(p, v)[h:h + 1])
        outs.append(jnp.concatenate(heads, axis=1))
    o_ref[...] = jnp.concatenate(outs, axis=0)


def _cross_sample(qx, mkv, n_mem):
    rows = qx.shape[0]
    return pl.pallas_call(
        functools.partial(_cross_sample_kernel, n_mem=n_mem),
        grid=(rows // SAMPLE_GROUP,),
        in_specs=[pl.BlockSpec((SAMPLE_GROUP, MEM_WIDTH), lambda i: (i, 0)),
                  pl.BlockSpec((SAMPLE_GROUP,) + mkv.shape[1:], lambda i: (i, 0, 0))],
        out_specs=pl.BlockSpec((SAMPLE_GROUP, MEM_WIDTH), lambda i: (i, 0)),
        out_shape=jax.ShapeDtypeStruct((rows, MEM_WIDTH), F32),
        compiler_params=_params("arbitrary"),
    )(qx, mkv)


def _ffn_kernel(x1_ref, oa_ref, ng_ref, xo_ref, wi_ref, wo_ref, o_ref, *, d_ff):
    o_ref[...] = _ffn_rows(x1_ref[...], oa_ref[...], ng_ref, xo_ref, wi_ref, wo_ref, d_ff)


def _ffn(x1, oa, ng, xo, wi, wo, tile):
    rows = x1.shape[0]
    d_ff = wo.shape[0]
    assert d_ff % (FFN_CHUNKS * LANES) == 0
    const = lambda i: (0, 0)
    once = dict(pipeline_mode=pl.Buffered(1))
    return pl.pallas_call(
        functools.partial(_ffn_kernel, d_ff=d_ff),
        grid=(rows // tile,),
        in_specs=[pl.BlockSpec((tile, D_MODEL), lambda i: (i, 0)),
                  pl.BlockSpec((tile, MEM_WIDTH), lambda i: (i, 0)),
                  pl.BlockSpec((6, D_MODEL), const),
                  pl.BlockSpec((MEM_WIDTH, D_MODEL), const, **once),
                  pl.BlockSpec((D_MODEL, 2 * d_ff), const, **once),
                  pl.BlockSpec((d_ff, D_MODEL), const, **once)],
        out_specs=pl.BlockSpec((tile, D_MODEL), lambda i: (i, 0)),
        out_shape=jax.ShapeDtypeStruct((rows, D_MODEL), F32),
        compiler_params=_params("arbitrary"),
    )(x1, oa, ng, xo, wi, wo)


PAGES_PER_STEP = 32
FOLD_PAGES = 8
GATHER_SLOTS = 3
NEW_TILE = 8
CMP_GROUP = 16


def _bias_rows(dist, tblt_ref):
    v = jnp.broadcast_to(tblt_ref[:, 0:1], dist.shape)
    for k in range(1, N_BUCKETS):
        v = jnp.where(dist >= _BUCKET_START[k], tblt_ref[:, k:k + 1], v)
    return v


def _sample_bias_kernel(tblt_ref, bw_ref, bc_ref, bn_ref, bs_ref, b0_ref, *, past_len):
    dist = (WINDOW - 1) - _iota2(bw_ref.shape, 1)
    bw_ref[...] = jnp.where((dist >= 0) & (dist < WINDOW), _bias_rows(dist, tblt_ref), NEG)
    dist = past_len - (_iota2(bc_ref.shape, 1) * CMP_BLOCK + (CMP_BLOCK - 1))
    bc_ref[...] = jnp.where(dist >= 0, _bias_rows(dist, tblt_ref), NEG)
    lane = _iota2(bn_ref.shape, 1)
    dist = past_len - ((past_len // CMP_BLOCK + lane) * CMP_BLOCK + (CMP_BLOCK - 1))
    bn_ref[...] = jnp.where((dist >= 0) & (lane < SEL_BLOCK // CMP_BLOCK),
                            _bias_rows(jnp.maximum(dist, 0), tblt_ref), NEG)
    page = lax.broadcasted_iota(I32, bs_ref.shape, 0)
    dist = past_len - page * PAGE_SIZE - lax.broadcasted_iota(I32, bs_ref.shape, 2)
    bs_ref[...] = _bias_rows(dist, tblt_ref)
    b0_ref[...] = _bias_rows(jnp.zeros(b0_ref.shape, I32), tblt_ref)


def _sample_bias(rel_bias, past_len):
    vm = pl.BlockSpec(memory_space=pltpu.VMEM)
    shapes = [(N_HEADS, WINDOW), (N_HEADS, past_len // CMP_BLOCK), (N_HEADS, LANES),
              (past_len // PAGE_SIZE, N_HEADS, PAGE_SIZE), (N_HEADS, LANES)]
    return pl.pallas_call(
        functools.partial(_sample_bias_kernel, past_len=past_len),
        in_specs=[vm], out_specs=[vm] * len(shapes),
        out_shape=[jax.ShapeDtypeStruct(s, F32) for s in shapes],
    )(rel_bias.T)


def _compress_sample_kernel(pt_ref, pool_ref, post_ref, fold_ref, projt_ref, o_ref, buf, sem, *, n_steps):
    i = pl.program_id(0)
    slot = i % GATHER_SLOTS
    ahead = GATHER_SLOTS - 1

    def page_copies(step, slot):
        return [pltpu.make_async_copy(pool_ref.at[pt_ref[step * PAGES_PER_STEP + j]], buf.at[slot, j], sem.at[slot])
                for j in range(PAGES_PER_STEP)]

    @pl.when(i == 0)
    def _():
        for step in range(min(ahead, n_steps)):
            for c in page_copies(step, step):
                c.start()

    @pl.when(i + ahead < n_steps)
    def _():
        for c in page_copies(i + ahead, (i + ahead) % GATHER_SLOTS):
            c.start()

    for c in page_copies(i, slot):
        c.wait()

    post = post_ref[...]
    sums_t = None
    for c in range(0, PAGES_PER_STEP, FOLD_PAGES):
        weighted = jnp.concatenate([(buf[slot, j] * post).astype(BF16) for j in range(c, c + FOLD_PAGES)], axis=1)
        part = jnp.dot(weighted, fold_ref[c * PAGE_SIZE:(c + FOLD_PAGES) * PAGE_SIZE, :],
                       preferred_element_type=F32)
        sums_t = part if sums_t is None else sums_t + part
    o_ref[0] = _dot3(projt_ref[...], sums_t)


def _compress_sample(pool_t, page_table_flat, n_seq, n_pages, pos, proj):
    steps = n_pages // PAGES_PER_STEP
    blocks_per_page = PAGE_SIZE // CMP_BLOCK
    blocks_per_step = PAGES_PER_STEP * blocks_per_page
    assert blocks_per_step == LANES
    pos_t = jnp.tile(pos.T, (1, blocks_per_page))
    row = np.arange(PAGES_PER_STEP * PAGE_SIZE)
    fold = jnp.asarray(row[:, None] // CMP_BLOCK == np.arange(blocks_per_step)[None, :], dtype=BF16)

    const = lambda i, pt: (0, 0)
    grid_spec = pltpu.PrefetchScalarGridSpec(
        num_scalar_prefetch=1,
        grid=(n_seq * steps,),
        in_specs=[pl.BlockSpec(memory_space=pl.ANY),
                  pl.BlockSpec((KV_COLS, PAGE_SIZE), const),
                  pl.BlockSpec(fold.shape, const),
                  pl.BlockSpec((KV_COLS, KV_COLS), const)],
        out_specs=pl.BlockSpec((1, KV_COLS, blocks_per_step), lambda i, pt: (i // steps, 0, i % steps)),
        scratch_shapes=[pltpu.VMEM((GATHER_SLOTS, PAGES_PER_STEP, KV_COLS, PAGE_SIZE), F32),
                        pltpu.SemaphoreType.DMA((GATHER_SLOTS,))],
    )
    return pl.pallas_call(
        functools.partial(_compress_sample_kernel, n_steps=n_seq * steps),
        grid_spec=grid_spec,
        out_shape=jax.ShapeDtypeStruct((n_seq, KV_COLS, steps * blocks_per_step), F32),
        compiler_params=_params("arbitrary"),
    )(page_table_flat, pool_t, pos_t, fold, proj.T)


def _cmp_sample_kernel(qm_ref, blk_ref, kvc_ref, pos_ref, proj_ref, bc_ref, bn_ref, oc_ref, idx_ref,
                       *, past_len):
    n_cmp = blk_ref.shape[2]
    bias = jnp.concatenate([bc_ref[...]] * CMP_GROUP, axis=0)
    mask = bias > MASKED_BELOW
    bias_n = jnp.concatenate([bn_ref[:, :NEW_TILE]] * CMP_GROUP, axis=0)
    mask_n = bias_n > MASKED_BELOW
    new_blk = _dot3(jnp.concatenate([kvc_ref[i] for i in range(CMP_GROUP)], axis=0) * pos_ref[0:1], proj_ref[...])
    first_row = _iota2((NEW_TILE, KV_COLS), 0) == 0
    new_blks = [jnp.where(first_row, jnp.broadcast_to(new_blk[i:i + 1], first_row.shape), 0.0)
                for i in range(CMP_GROUP)]
    s = jnp.concatenate([_dot3(qm_ref[i], blk_ref[i, :KV_HALF, :]) for i in range(CMP_GROUP)], axis=0)
    s_n = jnp.concatenate([_dot3_nt(qm_ref[i], new_blks[i][:, :KV_HALF]) for i in range(CMP_GROUP)], axis=0)
    s = jnp.where(mask, s * SCALE + bias, NEG)
    s_n = jnp.where(mask_n, s_n * SCALE + bias_n, NEG)
    m = jnp.maximum(jnp.max(s, axis=-1, keepdims=True), jnp.max(s_n, axis=-1, keepdims=True))
    e = jnp.where(mask, jnp.exp(s - m), 0.0)
    e_n = jnp.where(mask_n, jnp.exp(s_n - m), 0.0)
    inv = 1.0 / jnp.maximum(jnp.sum(e, axis=-1, keepdims=True) + jnp.sum(e_n, axis=-1, keepdims=True), TINY)
    p = e * inv
    p_n = e_n * inv
    group_ps = []
    for i in range(CMP_GROUP):
        rows = slice(i * N_HEADS, (i + 1) * N_HEADS)
        oc_ref[i] = _bdot_nt(p[rows], blk_ref[i, KV_HALF:, :]) + _bdot(p_n[rows], new_blks[i][:, KV_HALF:])
        group_ps += [jnp.sum(p[i * N_HEADS + g * GQA:i * N_HEADS + (g + 1) * GQA], axis=0, keepdims=True)
                     for g in range(N_KV)]

    ps = jnp.concatenate(group_ps, axis=0)
    lane = _iota2(ps.shape, 1)
    pair = ps + pltpu.roll(ps, n_cmp - 1, 1)
    cur = past_len // SEL_BLOCK
    j = lane // (SEL_BLOCK // CMP_BLOCK)
    forced = (j == 0) | (j == cur) | (j == cur - 1)
    score = jnp.where(lane % 2 == 0, jnp.where(forced, pair + FORCE, pair), -jnp.inf)
    slot = _iota2((ps.shape[0], LANES), 1)
    ids = jnp.zeros((ps.shape[0], LANES), I32)
    for it in range(N_SEL - 1):
        top = jnp.max(score, axis=-1, keepdims=True)
        first = jnp.min(jnp.where(score == top, lane, n_cmp), axis=-1, keepdims=True)
        ids = jnp.where(slot == it, first // (SEL_BLOCK // CMP_BLOCK), ids)
        score = jnp.where(lane == first, -jnp.inf, score)
    idx_ref[...] = jnp.where(slot == N_SEL - 1, cur, ids)


def _cmp_sample(qm, blocks_t, kvc_new, pos, proj, bc, bn, past_len):
    n_seq, _, n_cmp = blocks_t.shape
    assert past_len // SEL_BLOCK > N_SEL and past_len % SEL_BLOCK == 0
    const = lambda b: (0, 0)
    per_seq = lambda b: (b, 0, 0)
    return pl.pallas_call(
        functools.partial(_cmp_sample_kernel, past_len=past_len),
        grid=(n_seq // CMP_GROUP,),
        in_specs=[pl.BlockSpec((CMP_GROUP, N_HEADS, LANES), per_seq),
                  pl.BlockSpec((CMP_GROUP, KV_COLS, n_cmp), per_seq),
                  pl.BlockSpec((CMP_GROUP, 1, KV_COLS), per_seq),
                  pl.BlockSpec((CMP_BLOCK, KV_COLS), const),
                  pl.BlockSpec((KV_COLS, KV_COLS), const),
                  pl.BlockSpec((N_HEADS, n_cmp), const),
                  pl.BlockSpec((N_HEADS, LANES), const)],
        out_specs=[pl.BlockSpec((CMP_GROUP, N_HEADS, LANES), per_seq),
                   pl.BlockSpec((CMP_GROUP * N_KV, LANES), lambda b: (b, 0))],
        out_shape=[jax.ShapeDtypeStruct((n_seq, N_HEADS, LANES), F32),
                   jax.ShapeDtypeStruct((n_seq * N_KV, LANES), I32)],
        compiler_params=_params("arbitrary"),
    )(qm, blocks_t, kvc_new, pos, proj, bc, bn)


def _sel_win_sample_kernel(idx_ref, pt_ref, pool_ref, buf_ref, qm_ref, kvs_ref, kvw_ref, gate_ref, oc_ref, bw_ref,
                           bs_ref, b0_ref, o_ref, nbuf_ref, pages, sem, *, n_past_blocks, n_seq):
    b = pl.program_id(0)
    slot = b % 2
    bpp = PAGE_SIZE // SEL_BLOCK

    def page_copies(seq, slot):
        return [pltpu.make_async_copy(pool_ref.at[pt_ref[(seq * N_KV + g) * N_SEL + k], :, g],
                                      pages.at[slot, g * N_SEL + k], sem.at[slot])
                for g in range(N_KV) for k in range(N_SEL)]

    @pl.when(b == 0)
    def _():
        for c in page_copies(0, 0):
            c.start()

    @pl.when(b + 1 < n_seq)
    def _():
        for c in page_copies(b + 1, 1 - slot):
            c.start()

    qm = qm_ref[0]

    buf = buf_ref[0]
    shifted = pltpu.roll(buf, WINDOW - 1, 1)
    nbuf = jnp.where(_iota2(buf.shape, 1) == WINDOW - 1, jnp.broadcast_to(kvw_ref[0], buf.shape), shifted)
    nbuf_ref[0] = nbuf
    bias = bw_ref[...]
    s_win = jnp.where(bias > MASKED_BELOW, _bdot(qm, nbuf[:KV_HALF]) * SCALE + bias, NEG)

    for c in page_copies(b, slot):
        c.wait()
    lane_half = _iota2((GQA, PAGE_SIZE), 1) // SEL_BLOCK
    n_keys = N_SEL * PAGE_SIZE
    scores = [jnp.concatenate([s_win, jnp.full((N_HEADS, n_keys - WINDOW), NEG, F32)], axis=1)]
    new_scores = [jnp.full((N_HEADS, 1), NEG, F32)]
    values, new_values = [], []
    for g in range(N_KV):
        cols = slice(g * HEAD_DIM, (g + 1) * HEAD_DIM)
        rows = slice(g * GQA, (g + 1) * GQA)
        qg = qm[rows, cols]
        ks, vs, biases = [], [], []
        has_new = False
        for k in range(N_SEL):
            i = idx_ref[(b * N_KV + g) * N_SEL + k]
            is_past = i < n_past_blocks
            has_new = jnp.logical_or(has_new, jnp.logical_not(is_past))
            i = jnp.minimum(i, n_past_blocks - 1)
            ks.append(pages[slot, g * N_SEL + k, 0])
            vs.append(pages[slot, g * N_SEL + k, 1])
            tile = bs_ref[i // bpp]
            biases.append(jnp.where(is_past & (lane_half == i % bpp), tile[rows], NEG))
        bias = jnp.concatenate(biases, axis=1)
        scores.append(jnp.where(bias > MASKED_BELOW, _bdot(qg, jnp.concatenate(ks, axis=1)) * SCALE + bias, NEG))
        values.append(jnp.concatenate(vs, axis=1))
        k_new = kvs_ref[0][:, cols]
        new_values.append(kvs_ref[0][:, KV_HALF + g * HEAD_DIM:KV_HALF + (g + 1) * HEAD_DIM])
        new_scores.append(jnp.where(
            has_new, jnp.sum(qg * k_new, axis=-1, keepdims=True) * SCALE + b0_ref[rows, 0:1], NEG))

    s = jnp.concatenate(scores, axis=0)
    s_new = jnp.concatenate(new_scores, axis=0)
    mask, mask_new = s > MASKED_BELOW, s_new > MASKED_BELOW
    m = jnp.maximum(jnp.max(s, axis=-1, keepdims=True), s_new)
    e = jnp.where(mask, jnp.exp(s - m), 0.0)
    e_new = jnp.where(mask_new, jnp.exp(s_new - m), 0.0)
    inv = 1.0 / jnp.maximum(jnp.sum(e, axis=-1, keepdims=True) + e_new, TINY)
    p, p_new = e * inv, e_new * inv
    o_win = _bdot_nt(p[:N_HEADS, :WINDOW], nbuf[KV_HALF:])
    o_sel = jnp.concatenate(
        [_bdot_nt(p[N_HEADS + g * GQA:N_HEADS + (g + 1) * GQA], values[g])
         + p_new[N_HEADS + g * GQA:N_HEADS + (g + 1) * GQA] * new_values[g] for g in range(N_KV)], axis=0)

    first_group = _iota2((N_HEADS, HEAD_DIM), 0) < GQA
    oc = oc_ref[0]
    o_cmp = jnp.where(first_group, oc[:, :HEAD_DIM], oc[:, HEAD_DIM:])
    o_win = jnp.where(first_group, o_win[:, :HEAD_DIM], o_win[:, HEAD_DIM:])
    gate = gate_ref[0]
    o_ref[0] = gate[:, 0:1] * o_cmp + gate[:, 1:2] * o_sel + gate[:, 2:3] * o_win


def _sel_win_sample(idx_flat, pt_flat, pool_t, buf_t, qm, kvs_new, kvw_new_col, gates, o_cmp, bw, bs, b0,
                    n_pages):
    n_seq = buf_t.shape[0]
    bpp = PAGE_SIZE // SEL_BLOCK
    n_past_blocks = n_pages * bpp

    per_seq3 = lambda b, idx, pt: (b, 0, 0)
    const2 = lambda b, idx, pt: (0, 0)
    head_tile = pl.BlockSpec((1, N_HEADS, LANES), per_seq3)
    grid_spec = pltpu.PrefetchScalarGridSpec(
        num_scalar_prefetch=2,
        grid=(n_seq,),
        in_specs=[pl.BlockSpec(memory_space=pl.ANY),
                  pl.BlockSpec((1, KV_COLS, WINDOW), per_seq3), head_tile,
                  pl.BlockSpec((1, 1, KV_COLS), per_seq3), pl.BlockSpec((1, KV_COLS, 1), per_seq3), head_tile,
                  head_tile,
                  pl.BlockSpec((N_HEADS, WINDOW), const2),
                  pl.BlockSpec((n_pages, N_HEADS, PAGE_SIZE), lambda b, idx, pt: (0, 0, 0)),
                  pl.BlockSpec((N_HEADS, LANES), const2)],
        out_specs=[pl.BlockSpec((1, N_HEADS, HEAD_DIM), per_seq3),
                   pl.BlockSpec((1, KV_COLS, WINDOW), per_seq3)],
        scratch_shapes=[pltpu.VMEM((2, N_KV * N_SEL, 2, HEAD_DIM, PAGE_SIZE), F32),
                        pltpu.SemaphoreType.DMA((2,))],
    )
    return pl.pallas_call(
        functools.partial(_sel_win_sample_kernel, n_past_blocks=n_past_blocks, n_seq=n_seq),
        grid_spec=grid_spec,
        out_shape=[jax.ShapeDtypeStruct((n_seq, N_HEADS, HEAD_DIM), F32),
                   jax.ShapeDtypeStruct(buf_t.shape, F32)],
        compiler_params=_params("arbitrary"),
    )(idx_flat, pt_flat, pool_t, buf_t, qm, kvs_new, kvw_new_col, gates, o_cmp, bw, bs, b0)


PROJ_TILE = 512
ROW_TILE = 256
TAIL_TILE = 512
SCAN_STEPS = 128


def kernel(x_prompt, x_sample, cache_kv_cmp, cache_kv_sel, cache_kv_win, state_ssm_re, state_ssm_im,
           cache_mem_kv, page_table, mem_prompt, w_in, w_out, norm_g, grp_norm_g, cmp_pos, cmp_proj,
           rel_bias, ssm_a_re, ssm_a_im, ssm_b_re, ssm_b_im, ssm_c_re, ssm_c_im, ssm_d, ssm_log_dt,
           glu_w, glu_b, mem_norm_g, xq, xkv, xo, ffn_wi, ffn_wo):
    depth = w_in.shape[0]
    n_seq, seq, _ = x_prompt.shape
    n_dec, dec_seq, _ = x_sample.shape
    assert dec_seq == 1, "the sample kernels handle one new token per sequence"
    n_pages = page_table.shape[1]
    past_len = n_pages * PAGE_SIZE
    n_mem = mem_prompt.shape[1]
    assert cache_kv_win.shape[2] == WINDOW and seq >= WINDOW and WINDOW % Q_TILE == 0
    win_tiles = WINDOW // Q_TILE

    bcmp, near = _prompt_bias(rel_bias, seq)
    bw_s, bc_s, bn_s, bs_s, b0_s = _sample_bias(rel_bias, past_len)
    pt_flat = page_table.reshape(-1)
    kv5 = (2, N_KV, HEAD_DIM)

    xp = x_prompt.reshape(n_seq * seq, D_MODEL)
    xs = x_sample.reshape(n_dec, D_MODEL)
    outs = [[] for _ in range(11)]
    for l in range(depth):
        ng = norm_g[l]
        w_row, w_t = _pack_w_prompt(w_in[l])
        w_sample = _pack_w_sample(w_in[l])
        w_out_b, xq_b, xkv_b, xo_b = (w[l].astype(BF16) for w in (w_out, xq, xkv, xo))
        wi_b, wo_b, glu_w_b = ffn_wi[l].astype(BF16), ffn_wo[l].astype(BF16), glu_w[l].astype(BF16)
        gg = grp_norm_g[l][None]
        ssm_w = _ssm_weights(ssm_a_re[l], ssm_a_im[l], ssm_b_re[l], ssm_b_im[l], ssm_c_re[l], ssm_c_im[l],
                             ssm_log_dt[l])
        ssm_tail = (ssm_d[l][None], glu_w_b, glu_b[l][None])
        pos, proj_g = _pack_compress(cmp_pos[l], cmp_proj[l], group_major=True)
        _, proj_c = _pack_compress(cmp_pos[l], cmp_proj[l], group_major=False)

        (kvc, ks_aug, kw_aug, u, q_t, kvc_t, kvs_t, kvw_t, vs_t, vw_t, gates_t) = _inproj_prompt(
            xp, ng[0:1], w_row, w_t, n_seq, PROJ_TILE)
        kblk, vblk_t = _compress_prompt(kvc.reshape(n_seq, seq, KV_COLS), pos, proj_g)
        kw_aug = jnp.pad(kw_aug.reshape(n_seq, seq, N_KV * LANES), ((0, 0), (WINDOW, 0), (0, 0)))
        vw_t = jnp.pad(vw_t, ((0, 0), (win_tiles, 0), (0, 0), (0, 0)))
        ks_aug = jnp.pad(ks_aug.reshape(n_seq, seq, N_KV * LANES), ((0, 0), (Q_TILE, 0), (0, 0)))
        vs_t = jnp.pad(vs_t, ((0, 0), (1, 0), (0, 0), (0, 0)))
        o_nsa = _nsa_prompt(rel_bias, q_t, gates_t, kblk, vblk_t, bcmp, near, ks_aug, vs_t, kw_aug, vw_t)
        zeros = jnp.zeros((n_seq, N_STATE), F32)
        o_ssm, h_re, h_im = _ssm(u, zeros, zeros, ssm_w, *ssm_tail, n_seq, SCAN_STEPS)
        mkv = _norm_matmul(mem_prompt.reshape(n_seq * n_mem, D_MODEL), mem_norm_g[l][None], xkv_b, ROW_TILE)
        xp = _tail_prompt(o_nsa.reshape(n_seq * seq, NSA_WIDTH), o_ssm, xp,
                          mkv.reshape(n_seq, n_mem, 2 * MEM_WIDTH), gg, w_out_b, ng, xq_b, xo_b, wi_b, wo_b,
                          n_seq, TAIL_TILE)
        outs[0].append(_rows_minor(kvc_t))
        outs[1].append(_rows_minor(kvs_t))
        outs[2].append(_rows_minor(kvw_t[:, :, seq - WINDOW:]))
        outs[3].append(h_re.reshape(n_seq, N_SSM_GROUPS, SSM_STATE))
        outs[4].append(h_im.reshape(n_seq, N_SSM_GROUPS, SSM_STATE))
        outs[5].append(mkv.reshape(n_seq, n_mem, 2, MEM_HEADS, MEM_HEAD_DIM))

        q, kv, gates, u = _inproj_sample(xs, ng[0:1], w_sample)
        kvc, kvs, kvw = (kv[:, i * KV_COLS:(i + 1) * KV_COLS] for i in range(3))
        head_group = (jnp.arange(N_HEADS)[:, None] // GQA == jnp.arange(N_KV)[None, :]).astype(F32)
        qm = (q.reshape(n_dec, N_HEADS, 1, HEAD_DIM) * head_group[None, :, :, None]).reshape(n_dec, N_HEADS, LANES)
        gates_h = gates[:, :N_HEADS * N_BRANCH].reshape(n_dec, N_HEADS, N_BRANCH)
        gates_h = jnp.pad(gates_h, ((0, 0), (0, 0), (0, LANES - N_BRANCH)))
        blocks_t = _compress_sample(_rows_minor_view(cache_kv_cmp[l]), pt_flat, n_dec, n_pages, pos, proj_c)
        o_cmp, ids = _cmp_sample(qm, blocks_t, kvc.reshape(n_dec, 1, KV_COLS), pos, proj_c, bc_s, bn_s, past_len)
        ids = ids[:, :N_SEL]
        past_page = jnp.minimum(ids, past_len // SEL_BLOCK - 1) // (PAGE_SIZE // SEL_BLOCK)
        sel_pages = jnp.take_along_axis(jnp.repeat(page_table, N_KV, axis=0), past_page, axis=1)
        pool_sel = cache_kv_sel[l].transpose(0, 2, 3, 4, 1)
        o_nsa, new_buf_t = _sel_win_sample(
            ids.reshape(-1), sel_pages.reshape(-1), pool_sel, _rows_minor_view(cache_kv_win[l]), qm, kvs.reshape(n_dec, 1, KV_COLS),
            kvw.reshape(n_dec, KV_COLS, 1), gates_h, o_cmp, bw_s, bs_s, b0_s, n_pages)
        o_ssm, h_re, h_im = _ssm(u, state_ssm_re[l].reshape(n_dec, N_STATE), state_ssm_im[l].reshape(n_dec, N_STATE),
                                 ssm_w, *ssm_tail, n_dec, 1)
        x1, qx = _merge(o_nsa.reshape(n_dec, NSA_WIDTH), o_ssm, xs, gg, w_out_b, ng, xq_b, 1, n_dec)
        oa = _cross_sample(qx, cache_mem_kv[l].reshape(n_dec, n_mem * 2 * MEM_HEADS, MEM_HEAD_DIM), n_mem)
        xs = _ffn(x1, oa, ng, xo_b, wi_b, wo_b, n_dec)
        outs[6].append(kvc.reshape(n_dec, 1, *kv5))
        outs[7].append(kvs.reshape(n_dec, 1, *kv5))
        outs[8].append(_rows_minor(new_buf_t))
        outs[9].append(h_re.reshape(n_dec, N_SSM_GROUPS, SSM_STATE))
        outs[10].append(h_im.reshape(n_dec, N_SSM_GROUPS, SSM_STATE))

    stacked = [jnp.stack(o, axis=0) for o in outs]
    return (xp.reshape(x_prompt.shape), xs.reshape(x_sample.shape), *stacked)
```

```python
import functools
import math

import numpy as np
import jax
import jax.numpy as jnp
from jax import lax
from jax.experimental import pallas as pl
from jax.experimental.pallas import tpu as pltpu

F32 = jnp.float32
BF16 = jnp.bfloat16
I32 = jnp.int32

D_MODEL = 1024
HEAD_DIM = 64
N_HEADS = 8
N_KV = 2
GQA = N_HEADS // N_KV
NSA_WIDTH = N_HEADS * HEAD_DIM
SSM_WIDTH = 512
KV_COLS = 2 * N_KV * HEAD_DIM
KV_HALF = N_KV * HEAD_DIM
N_BRANCH = 3
CMP_BLOCK = 32
SEL_BLOCK = 64
N_SEL = 16
WINDOW = 512
PAGE_SIZE = 128
SSM_GROUP = 16
N_SSM_GROUPS = 32
SSM_STATE = 64
N_STATE = N_SSM_GROUPS * SSM_STATE
N_BUCKETS = 32
MAX_DISTANCE = 128
MEM_HEADS = 4
MEM_HEAD_DIM = 128
MEM_WIDTH = MEM_HEADS * MEM_HEAD_DIM
EPS = 1e-6
NEG = -1e30
MASKED_BELOW = -1e29
TINY = 1e-30
FORCE = 1e4
SCALE = HEAD_DIM ** -0.5
MEM_SCALE = MEM_HEAD_DIM ** -0.5

LANES = 128
Q_TILE = 256
FAR_TILES = 2
FFN_CHUNKS = 2
GATHER_SLOTS = 3
VMEM_LIMIT = 56 * 1024 * 1024

AUG_SEL = HEAD_DIM
AUG_ONE = AUG_SEL + 32
GATE_ROWS = 16
VT_ROWS = HEAD_DIM + 16
LOG2E = math.log2(math.e)


def _bucket_starts():
    n = np.arange(0, 4 * MAX_DISTANCE)
    exact = N_BUCKETS // 2
    nf = np.maximum(n, exact).astype(np.float32)
    big = exact + (np.log(nf / exact) / np.float32(math.log(MAX_DISTANCE / exact))
                   * (N_BUCKETS - exact)).astype(np.int32)
    bucket = np.where(n < exact, n, np.minimum(big, N_BUCKETS - 1))
    return [int(np.argmax(bucket >= k)) for k in range(N_BUCKETS)]


_BUCKET_START = _bucket_starts()
assert _BUCKET_START[-1] <= Q_TILE


def _params(*sem):
    return pltpu.CompilerParams(dimension_semantics=sem, vmem_limit_bytes=VMEM_LIMIT)


def _bdot(a, b):
    return jnp.dot(a.astype(BF16), b.astype(BF16), preferred_element_type=F32)


def _bdot_nt(a, b):
    return lax.dot_general(a.astype(BF16), b.astype(BF16), (((1,), (1,)), ((), ())),
                           preferred_element_type=F32)


def _split(a):
    hi = a.astype(BF16)
    return hi, (a - hi.astype(F32)).astype(BF16)


def _dot3(a, b):
    ah, al = _split(a)
    bh, bl = _split(b)
    d = functools.partial(jnp.dot, preferred_element_type=F32)
    return d(ah, bh) + (d(ah, bl) + d(al, bh))


def _dot3_nt(a, b):
    ah, al = _split(a)
    bh, bl = _split(b)
    d = functools.partial(lax.dot_general, dimension_numbers=(((1,), (1,)), ((), ())),
                          preferred_element_type=F32)
    return d(ah, bh) + (d(ah, bl) + d(al, bh))


def _rms(x, g):
    return x * lax.rsqrt(jnp.mean(x * x, axis=-1, keepdims=True) + EPS) * g


def _masked_softmax(s, mask, axis=-1):
    s = jnp.where(mask, s, NEG)
    m = jnp.max(s, axis=axis, keepdims=True)
    e = jnp.where(mask, jnp.exp(s - m), 0.0)
    return e * (1.0 / jnp.maximum(jnp.sum(e, axis=axis, keepdims=True), TINY))


def _bias_of_dist(dist, tbl_ref, h):
    v = jnp.full(dist.shape, tbl_ref[0, h], F32)
    for k in range(1, N_BUCKETS):
        v = jnp.where(dist >= _BUCKET_START[k], tbl_ref[k, h], v)
    return v


def _iota2(shape, dim):
    return lax.broadcasted_iota(I32, shape, dim)


def _rows_minor(a):
    n, _, rows = a.shape
    return a.reshape(n, 2, N_KV, HEAD_DIM, rows).transpose(0, 4, 1, 2, 3)


def _rows_minor_view(a):
    n, rows = a.shape[:2]
    return a.transpose(0, 2, 3, 4, 1).reshape(n, KV_COLS, rows)


def _cmp_block_of_row(row, n_cmp):
    return jnp.where(row < n_cmp // 2, 2 * row, 2 * (row - n_cmp // 2) + 1)


def _prompt_bias_kernel(tbl_ref, bcmp_ref, near_ref, *, n_cmp, nq):
    h = pl.program_id(0)
    last = tbl_ref[N_BUCKETS - 1, h]

    blk = _cmp_block_of_row(_iota2((n_cmp, Q_TILE), 0), n_cmp)
    for qi in range(nq):
        dist = qi * Q_TILE + _iota2((n_cmp, Q_TILE), 1) - (blk * CMP_BLOCK + (CMP_BLOCK - 1))
        bcmp_ref[qi, 0] = jnp.where(dist >= 0, _bias_of_dist(dist, tbl_ref, h), NEG)

    key = _iota2((Q_TILE, Q_TILE), 0)
    qry = _iota2((Q_TILE, Q_TILE), 1)
    d0 = qry - key
    near_ref[0, 0] = jnp.where(d0 >= 0, (_bias_of_dist(d0, tbl_ref, h) - last) * LOG2E, NEG)
    d1 = Q_TILE + qry - key
    near_ref[0, 1] = (_bias_of_dist(d1, tbl_ref, h) - last) * LOG2E
    near_ref[0, 2] = jnp.where(key > qry, 0.0, NEG)


def _prompt_bias(rel_bias, seq):
    nq = seq // Q_TILE
    n_cmp = seq // CMP_BLOCK
    return pl.pallas_call(
        functools.partial(_prompt_bias_kernel, n_cmp=n_cmp, nq=nq),
        grid=(N_HEADS,),
        in_specs=[pl.BlockSpec(memory_space=pltpu.SMEM)],
        out_specs=[pl.BlockSpec((nq, 1, n_cmp, Q_TILE), lambda h: (0, h // GQA, 0, h % GQA)),
                   pl.BlockSpec((1, 3, Q_TILE, Q_TILE), lambda h: (h // GQA, 0, 0, h % GQA))],
        out_shape=[jax.ShapeDtypeStruct((nq, N_KV, n_cmp, GQA * Q_TILE), F32),
                   jax.ShapeDtypeStruct((N_KV, 3, Q_TILE, GQA * Q_TILE), F32)],
        compiler_params=_params("arbitrary"),
    )(rel_bias)


OFF_KV = NSA_WIDTH
OFF_GATE = OFF_KV + 3 * KV_COLS
OFF_U = OFF_GATE + N_HEADS * N_BRANCH


def _k_cols(w_in, which):
    base = OFF_KV + which * KV_COLS
    return [w_in[:, base + g * HEAD_DIM: base + (g + 1) * HEAD_DIM] for g in range(N_KV)]


def _pack_w_prompt(w_in):
    w_row = jnp.concatenate([w_in[:, OFF_KV:OFF_KV + KV_COLS]] + _k_cols(w_in, 1) + _k_cols(w_in, 2)
                            + [w_in[:, OFF_U:]], axis=1)
    gates = w_in[:, OFF_GATE:OFF_U].reshape(-1, N_KV, GQA * N_BRANCH)
    gates = jnp.pad(gates, ((0, 0), (0, 0), (0, GATE_ROWS - GQA * N_BRANCH))).reshape(-1, N_KV * GATE_ROWS)
    w_t = jnp.concatenate([w_in[:, :OFF_GATE], gates], axis=1).T
    return w_row.astype(BF16), w_t.astype(BF16)


def _inproj_prompt_kernel(x_ref, g_ref, wr_ref, wt_ref, kvc_ref, ksa_ref, kwa_ref, u_ref, qt_ref, kvct_ref,
                          kvst_ref, kvwt_ref, vst_ref, vwt_ref, gt_ref, *, tile):
    hb = _rms(x_ref[...], g_ref[...]).astype(BF16)
    z = jnp.dot(hb, wr_ref[...], preferred_element_type=F32)
    kvc_ref[...] = z[:, :KV_COLS]
    u_ref[...] = z[:, KV_COLS + 4 * HEAD_DIM:]

    pos = pl.program_id(1) * tile + _iota2((tile, LANES - HEAD_DIM), 0)
    col = _iota2((tile, LANES - HEAD_DIM), 1)
    ones = ((col >= AUG_ONE - HEAD_DIM) & (col < AUG_ONE - HEAD_DIM + 2)).astype(F32)
    aug_sel = jnp.where(col == pos // SEL_BLOCK, -NEG, ones)
    k0 = KV_COLS
    ksa_ref[...] = jnp.concatenate(
        [z[:, k0:k0 + HEAD_DIM], aug_sel, z[:, k0 + HEAD_DIM:k0 + 2 * HEAD_DIM], aug_sel], axis=1).astype(BF16)
    k0 = KV_COLS + 2 * HEAD_DIM
    kwa_ref[...] = jnp.concatenate(
        [z[:, k0:k0 + HEAD_DIM], ones, z[:, k0 + HEAD_DIM:k0 + 2 * HEAD_DIM], ones], axis=1).astype(BF16)

    zt = lax.dot_general(wt_ref[...], hb, (((1,), (1,)), ((), ())), preferred_element_type=F32)
    qt_ref[0] = zt[:NSA_WIDTH]
    kvct_ref[0] = zt[OFF_KV:OFF_KV + KV_COLS]
    kvst_ref[0] = zt[OFF_KV + KV_COLS:OFF_KV + 2 * KV_COLS]
    kvwt_ref[0] = zt[OFF_KV + 2 * KV_COLS:OFF_GATE]
    gt_ref[0] = jax.nn.sigmoid(zt[OFF_GATE:])
    v_sel = OFF_KV + KV_COLS + KV_HALF
    v_win = OFF_KV + 2 * KV_COLS + KV_HALF
    sum_rows = (_iota2((VT_ROWS - HEAD_DIM, Q_TILE), 0) == 0).astype(F32)
    for j in range(tile // Q_TILE):
        cols = slice(j * Q_TILE, (j + 1) * Q_TILE)
        for ref, v0 in ((vst_ref, v_sel), (vwt_ref, v_win)):
            parts = []
            for g in range(N_KV):
                parts += [zt[v0 + g * HEAD_DIM:v0 + (g + 1) * HEAD_DIM, cols], sum_rows]
            ref[0, j] = jnp.concatenate(parts, axis=0).astype(BF16)


def _inproj_prompt(x, g, w_row, w_t, n_seq, tile):
    rows = x.shape[0]
    seq = rows // n_seq
    assert seq // SEL_BLOCK <= AUG_ONE - AUG_SEL
    nt = seq // tile
    row_map = lambda b, t: (b * nt + t, 0)
    t_map = lambda b, t: (b, 0, t)
    tiles = tile // Q_TILE
    n_gate = N_KV * GATE_ROWS
    out = [
        (pl.BlockSpec((tile, KV_COLS), row_map), jax.ShapeDtypeStruct((rows, KV_COLS), F32)),
        (pl.BlockSpec((tile, N_KV * LANES), row_map), jax.ShapeDtypeStruct((rows, N_KV * LANES), BF16)),
        (pl.BlockSpec((tile, N_KV * LANES), row_map), jax.ShapeDtypeStruct((rows, N_KV * LANES), BF16)),
        (pl.BlockSpec((tile, SSM_WIDTH), lambda b, t: (t, b)), jax.ShapeDtypeStruct((seq, n_seq * SSM_WIDTH), F32)),
        (pl.BlockSpec((1, NSA_WIDTH, tile), t_map), jax.ShapeDtypeStruct((n_seq, NSA_WIDTH, seq), F32)),
        (pl.BlockSpec((1, KV_COLS, tile), t_map), jax.ShapeDtypeStruct((n_seq, KV_COLS, seq), F32)),
        (pl.BlockSpec((1, KV_COLS, tile), t_map), jax.ShapeDtypeStruct((n_seq, KV_COLS, seq), F32)),
        (pl.BlockSpec((1, KV_COLS, tile), t_map), jax.ShapeDtypeStruct((n_seq, KV_COLS, seq), F32)),
        (pl.BlockSpec((1, tiles, N_KV * VT_ROWS, Q_TILE), lambda b, t: (b, t, 0, 0)),
         jax.ShapeDtypeStruct((n_seq, seq // Q_TILE, N_KV * VT_ROWS, Q_TILE), BF16)),
        (pl.BlockSpec((1, tiles, N_KV * VT_ROWS, Q_TILE), lambda b, t: (b, t, 0, 0)),
         jax.ShapeDtypeStruct((n_seq, seq // Q_TILE, N_KV * VT_ROWS, Q_TILE), BF16)),
        (pl.BlockSpec((1, n_gate, tile), t_map), jax.ShapeDtypeStruct((n_seq, n_gate, seq), F32)),
    ]
    return pl.pallas_call(
        functools.partial(_inproj_prompt_kernel, tile=tile),
        grid=(n_seq, nt),
        in_specs=[pl.BlockSpec((tile, D_MODEL), row_map),
                  pl.BlockSpec((1, D_MODEL), lambda b, t: (0, 0)),
                  pl.BlockSpec(w_row.shape, lambda b, t: (0, 0)),
                  pl.BlockSpec(w_t.shape, lambda b, t: (0, 0))],
        out_specs=[o[0] for o in out],
        out_shape=[o[1] for o in out],
        compiler_params=_params("arbitrary", "arbitrary"),
    )(x, g, w_row, w_t)


S_Q = 0
S_KV = S_Q + NSA_WIDTH
S_GATE = S_KV + 3 * KV_COLS
S_U = S_GATE + LANES
S_END = S_U + SSM_WIDTH


def _pack_w_sample(w_in):
    gates = jnp.pad(w_in[:, OFF_GATE:OFF_U], ((0, 0), (0, LANES - N_HEADS * N_BRANCH)))
    return jnp.concatenate([w_in[:, :OFF_GATE], gates, w_in[:, OFF_U:]], axis=1).astype(BF16)


def _inproj_sample_kernel(x_ref, g_ref, w_ref, q_ref, kv_ref, gate_ref, u_ref):
    z = jnp.dot(_rms(x_ref[...], g_ref[...]).astype(BF16), w_ref[...], preferred_element_type=F32)
    q_ref[...] = z[:, S_Q:S_KV]
    kv_ref[...] = z[:, S_KV:S_GATE]
    gate_ref[...] = jax.nn.sigmoid(z[:, S_GATE:S_U])
    u_ref[...] = z[:, S_U:S_END]


def _inproj_sample(x, g, w):
    rows = x.shape[0]
    vm = pl.BlockSpec(memory_space=pltpu.VMEM)
    widths = [NSA_WIDTH, 3 * KV_COLS, LANES, SSM_WIDTH]
    return pl.pallas_call(
        _inproj_sample_kernel,
        in_specs=[vm] * 3, out_specs=[vm] * 4,
        out_shape=[jax.ShapeDtypeStruct((rows, w_), F32) for w_ in widths],
        compiler_params=pltpu.CompilerParams(vmem_limit_bytes=VMEM_LIMIT),
    )(x, g, w)


def _pack_compress(cmp_pos, cmp_proj, group_major):
    pos = jnp.broadcast_to(cmp_pos.transpose(1, 0, 2)[:, :, None, :],
                           (CMP_BLOCK, 2, N_KV, HEAD_DIM)).reshape(CMP_BLOCK, KV_COLS)
    eye = jnp.eye(2 * N_KV, dtype=F32).reshape(2, N_KV, 2, N_KV)
    order = 'cgdGCe' if group_major else 'cgdCGe'
    proj = jnp.einsum('cde,cgCG->' + order, cmp_proj, eye).reshape(KV_COLS, KV_COLS)
    return pos, proj


def _compress_prompt_kernel(kv_ref, pos_ref, proj_ref, projt_ref, kblk_ref, vblkt_ref):
    x = kv_ref[0]
    n_sel = x.shape[0] // SEL_BLOCK
    x = x.reshape(n_sel, SEL_BLOCK // CMP_BLOCK, CMP_BLOCK, KV_COLS)
    pos = pos_ref[...][None]
    sums = jnp.concatenate([jnp.sum(x[:, i] * pos, axis=1) for i in range(SEL_BLOCK // CMP_BLOCK)], axis=0)
    blk = _dot3(sums, proj_ref[...])
    blk_t = _dot3_nt(projt_ref[...], sums)
    for g in range(N_KV):
        kblk_ref[0, g] = blk[:, g * KV_HALF:g * KV_HALF + HEAD_DIM]
        vblkt_ref[0, g] = blk_t[g * KV_HALF + HEAD_DIM:(g + 1) * KV_HALF]


def _compress_prompt(kvc, pos, proj):
    n_seq, seq, _ = kvc.shape
    n_cmp = seq // CMP_BLOCK
    return pl.pallas_call(
        _compress_prompt_kernel,
        grid=(n_seq,),
        in_specs=[pl.BlockSpec((1, seq, KV_COLS), lambda b: (b, 0, 0)),
                  pl.BlockSpec((CMP_BLOCK, KV_COLS), lambda b: (0, 0)),
                  pl.BlockSpec((KV_COLS, KV_COLS), lambda b: (0, 0)),
                  pl.BlockSpec((KV_COLS, KV_COLS), lambda b: (0, 0))],
        out_specs=[pl.BlockSpec((1, N_KV, n_cmp, HEAD_DIM), lambda b: (b, 0, 0, 0)),
                   pl.BlockSpec((1, N_KV, HEAD_DIM, n_cmp), lambda b: (b, 0, 0, 0))],
        out_shape=[jax.ShapeDtypeStruct((n_seq, N_KV, n_cmp, HEAD_DIM), F32),
                   jax.ShapeDtypeStruct((n_seq, N_KV, HEAD_DIM, n_cmp), F32)],
        compiler_params=_params("arbitrary"),
    )(kvc, pos, proj, proj.T)


def _select_blocks_t(pair, q0):
    row = _iota2(pair.shape, 0)
    n_sel = pair.shape[0]
    cur = (q0 + _iota2(pair.shape, 1)) // SEL_BLOCK
    forced = (row == 0) | (row == cur) | (row == cur - 1)
    score = jnp.where(row <= cur, jnp.where(forced, pair + FORCE, pair), -jnp.inf)
    chosen = jnp.zeros(pair.shape, F32)
    for _ in range(N_SEL):
        m = jnp.max(score, axis=0, keepdims=True)
        hit = (score == m) & (m > -jnp.inf)
        first = jnp.min(jnp.where(hit, row, n_sel), axis=0, keepdims=True)
        pick = row == first
        chosen = jnp.where(pick, 1.0, chosen)
        score = jnp.where(pick, -jnp.inf, score)
    return chosen


def _attend(state, k_aug, v_t, q_aug, extras):
    m, acc = state
    s = jnp.dot(k_aug, q_aug, preferred_element_type=F32)
    if any(t is not None or c is not None for t, c in extras):
        rows = []
        for t, c in extras:
            if t is None:
                rows.append(jnp.full((Q_TILE, s.shape[1]), 0.0 if c is None else c, F32))
            else:
                rows.append(t if c is None else t + c)
        s = s + jnp.concatenate(rows, axis=0)
    m_new = jnp.maximum(m, jnp.max(s, axis=0, keepdims=True))
    e = jnp.exp2(s - m_new)
    acc = jnp.exp2(m - m_new) * acc + jnp.dot(v_t, e.astype(BF16), preferred_element_type=F32)
    return m_new, acc


def _attend_result(state):
    acc = state[1]
    return acc[:HEAD_DIM] * (1.0 / jnp.maximum(acc[HEAD_DIM:HEAD_DIM + 1], TINY))


def _nsa_prompt_kernel(tbl_ref, qt_ref, gt_ref, kblk_ref, vblkt_ref, bcmp_ref, near_ref, ks_ref, vst_ref,
                       kw_ref, vwt_ref, o_ref):
    g = pl.program_id(1)
    qi = pl.program_id(2)
    q0 = qi * Q_TILE
    win_tiles = WINDOW // Q_TILE
    v_rows = pl.ds(pl.multiple_of(g * VT_ROWS, VT_ROWS), VT_ROWS)

    width = GQA * Q_TILE
    q = jnp.concatenate([qt_ref[0, r * HEAD_DIM:(r + 1) * HEAD_DIM, :] for r in range(GQA)], axis=1) * SCALE

    row = _iota2((LANES - AUG_ONE, Q_TILE), 0)
    consts = []
    for r in range(GQA):
        last = jnp.full((LANES - AUG_ONE, Q_TILE), tbl_ref[N_BUCKETS - 1, g * GQA + r] * LOG2E, F32)
        hi = last.astype(BF16).astype(F32)
        consts.append(jnp.where(row == 0, hi, jnp.where(row == 1, last - hi, 0.0)).astype(BF16))
    consts = jnp.concatenate(consts, axis=1)
    q_log2 = (q * LOG2E).astype(BF16)
    near0, near1, oldest = near_ref[0, 0], near_ref[0, 1], near_ref[0, 2]
    init = (jnp.full((1, width), NEG, F32), jnp.zeros((VT_ROWS, width), F32))

    def key_rows(ref, tile, n_tiles):
        return ref[0, pl.ds(pl.multiple_of(tile * Q_TILE, Q_TILE), n_tiles * Q_TILE), :]

    def value_cols(ref, tile, n_tiles):
        return jnp.concatenate([ref[0, tile + j, v_rows, :] for j in range(n_tiles)], axis=1)

    def tile_mask(ok):
        return jnp.where(ok, 0.0, NEG)

    q_aug = jnp.concatenate([q_log2, jnp.zeros((AUG_ONE - AUG_SEL, width), BF16), consts], axis=0)
    extras = []
    for mt in range(win_tiles):
        table = oldest if mt == 0 else near1 if mt == win_tiles - 1 else None
        extras.append((table, tile_mask(qi + mt >= win_tiles)))
    extras.append((near0, None))
    st = _attend(init, key_rows(kw_ref, qi, win_tiles + 1), value_cols(vwt_ref, qi, win_tiles + 1), q_aug, extras)
    o_win = _attend_result(st)

    k_cmp, v_cmp_t = kblk_ref[0, 0], vblkt_ref[0, 0]
    n_cmp = k_cmp.shape[0]
    bias = bcmp_ref[0, 0]
    p = _masked_softmax(_dot3(k_cmp, q) + bias, bias > MASKED_BELOW, axis=0)
    o_cmp = _bdot(v_cmp_t, p)
    ps = p[:, :Q_TILE]
    for r in range(1, GQA):
        ps = ps + p[:, r * Q_TILE:(r + 1) * Q_TILE]
    chosen = _select_blocks_t(ps[:n_cmp // 2] + ps[n_cmp // 2:], q0)
    not_chosen = (chosen - 1.0).astype(BF16)
    pad_rows = (AUG_ONE - AUG_SEL) - not_chosen.shape[0]
    if pad_rows:
        not_chosen = jnp.concatenate([not_chosen, jnp.zeros((pad_rows, Q_TILE), BF16)], axis=0)
    q_aug = jnp.concatenate([q_log2, jnp.concatenate([not_chosen] * GQA, axis=1), consts], axis=0)

    extras = [(near1, tile_mask(qi >= 1)), (near0, None)]
    st = _attend(init, key_rows(ks_ref, qi, 2), value_cols(vst_ref, qi, 2), q_aug, extras)
    n_far = jnp.maximum(qi - 1, 0)

    def far_chunk(c, st, masked=False):
        t0 = c * FAR_TILES
        extras = [(None, tile_mask(t0 + j < n_far) if masked else None) for j in range(FAR_TILES)]
        return _attend(st, key_rows(ks_ref, t0 + 1, FAR_TILES), value_cols(vst_ref, t0 + 1, FAR_TILES), q_aug, extras)

    full = n_far // FAR_TILES
    st = lax.fori_loop(0, full, far_chunk, st)
    st = lax.cond(n_far % FAR_TILES != 0, lambda st=st: far_chunk(full, st, masked=True), lambda st=st: st)
    o_sel = _attend_result(st)

    def gate(branch):
        return jnp.concatenate([gt_ref[0, r * N_BRANCH + branch:r * N_BRANCH + branch + 1, :]
                                for r in range(GQA)], axis=1)

    o = gate(0) * o_cmp + gate(1) * o_sel + gate(2) * o_win
    o_ref[0] = jnp.concatenate([o[:, r * Q_TILE:(r + 1) * Q_TILE] for r in range(GQA)], axis=0).T


def _nsa_prompt(rel_bias, q_t, gates_t, kblk, vblk_t, bcmp, near, ks_aug, vs_t, kw_aug, vw_t):
    n_seq, _, seq = q_t.shape
    nq = seq // Q_TILE
    assert nq % FAR_TILES == 0
    n_cmp = kblk.shape[2]
    gw = GQA * HEAD_DIM
    wt = WINDOW // Q_TILE
    return pl.pallas_call(
        _nsa_prompt_kernel,
        grid=(n_seq, N_KV, nq),
        in_specs=[pl.BlockSpec(memory_space=pltpu.SMEM),
                  pl.BlockSpec((1, gw, Q_TILE), lambda b, g, qi: (b, g, qi)),
                  pl.BlockSpec((1, GATE_ROWS, Q_TILE), lambda b, g, qi: (b, g, qi)),
                  pl.BlockSpec((1, 1, n_cmp, HEAD_DIM), lambda b, g, qi: (b, g, 0, 0)),
                  pl.BlockSpec((1, 1, HEAD_DIM, n_cmp), lambda b, g, qi: (b, g, 0, 0)),
                  pl.BlockSpec((1, 1, n_cmp, GQA * Q_TILE), lambda b, g, qi: (qi, g, 0, 0)),
                  pl.BlockSpec((1, 3, Q_TILE, GQA * Q_TILE), lambda b, g, qi: (g, 0, 0, 0)),
                  pl.BlockSpec((1, Q_TILE + seq, LANES), lambda b, g, qi: (b, 0, g)),
                  pl.BlockSpec((1, 1 + nq, N_KV * VT_ROWS, Q_TILE), lambda b, g, qi: (b, 0, 0, 0)),
                  pl.BlockSpec((1, WINDOW + seq, LANES), lambda b, g, qi: (b, 0, g)),
                  pl.BlockSpec((1, wt + nq, N_KV * VT_ROWS, Q_TILE), lambda b, g, qi: (b, 0, 0, 0))],
        out_specs=pl.BlockSpec((1, Q_TILE, gw), lambda b, g, qi: (b, qi, g)),
        out_shape=jax.ShapeDtypeStruct((n_seq, seq, NSA_WIDTH), F32),
        compiler_params=_params("arbitrary", "arbitrary", "arbitrary"),
    )(rel_bias, q_t, gates_t, kblk, vblk_t, bcmp, near, ks_aug, vs_t, kw_aug, vw_t)


STATE_CHUNK = 512
SCAN_UNROLL = 4
SSM_KB = SSM_WIDTH // LANES
GROUPS_PER_KB = N_SSM_GROUPS // SSM_KB


def _ssm_disc_kernel(are_ref, aim_ref, ldt_ref, bre_ref, bim_ref, abr_ref, abi_ref, bbr_ref, bbi_ref):
    a_re, a_im = are_ref[...], aim_ref[...]
    dt = jnp.exp(ldt_ref[...])
    mag = jnp.exp(a_re * dt)
    ab_re = mag * jnp.cos(a_im * dt)
    ab_im = mag * jnp.sin(a_im * dt)
    den = a_re * a_re + a_im * a_im
    co_re = ((ab_re - 1.0) * a_re + ab_im * a_im) / den
    co_im = (ab_im * a_re - (ab_re - 1.0) * a_im) / den
    abr_ref[...] = ab_re
    abi_ref[...] = ab_im
    b_re, b_im = bre_ref[...], bim_ref[...]
    bbr_ref[...] = co_re[:, None, :] * b_re - co_im[:, None, :] * b_im
    bbi_ref[...] = co_re[:, None, :] * b_im + co_im[:, None, :] * b_re


def _block_diag(w):
    a, b = w.shape[1:]
    w = w.reshape(SSM_KB, GROUPS_PER_KB, a, b)
    eye = jnp.eye(GROUPS_PER_KB, dtype=w.dtype)
    return jnp.einsum('kgab,gh->kgahb', w, eye).reshape(SSM_KB, GROUPS_PER_KB * a, GROUPS_PER_KB * b)


def _ssm_weights(a_re, a_im, b_re, b_im, c_re, c_im, log_dt):
    ng, p = a_re.shape
    vm = pl.BlockSpec(memory_space=pltpu.VMEM)
    ab_re, ab_im, bb_re, bb_im = pl.pallas_call(
        _ssm_disc_kernel,
        in_specs=[vm] * 5, out_specs=[vm] * 4,
        out_shape=[jax.ShapeDtypeStruct((ng, p), F32)] * 2
        + [jax.ShapeDtypeStruct((ng, SSM_GROUP, p), F32)] * 2,
    )(a_re, a_im, log_dt.reshape(ng, 1), b_re.transpose(0, 2, 1), b_im.transpose(0, 2, 1))
    return (ab_re.reshape(1, N_STATE), ab_im.reshape(1, N_STATE),
            _block_diag(bb_re).astype(BF16), _block_diag(bb_im).astype(BF16),
            _block_diag(c_re.transpose(0, 2, 1)).astype(BF16),
            _block_diag(c_im.transpose(0, 2, 1)).astype(BF16))


def _ssm_kernel(u_ref, h0r_ref, h0i_ref, abr_ref, abi_ref, bbr_ref, bbi_ref, ccr_ref, cci_ref,
                d_ref, gw_ref, gb_ref, o_ref, hr_ref, hi_ref, sre, sim, *rows_scratch, nb, steps):
    @pl.when(pl.program_id(0) == 0)
    def _():
        hr_ref[...] = h0r_ref[...]
        hi_ref[...] = h0i_ref[...]

    if rows_scratch:
        rows_ref, = rows_scratch
        for b in range(nb):
            for c in range(SSM_KB):
                col = b * SSM_WIDTH + c * LANES
                rows_ref[c, pl.ds(b, steps, stride=nb), :] = u_ref[:, col:col + LANES]
        u = jnp.concatenate([rows_ref[c] for c in range(SSM_KB)], axis=1)
    else:
        u = u_ref[...]
    ub = u.astype(BF16)
    kw = N_STATE // SSM_KB
    for kb in range(SSM_KB):
        uk = ub[:, kb * LANES:(kb + 1) * LANES]
        sre[:, kb * kw:(kb + 1) * kw] = jnp.dot(uk, bbr_ref[kb], preferred_element_type=F32)
        sim[:, kb * kw:(kb + 1) * kw] = jnp.dot(uk, bbi_ref[kb], preferred_element_type=F32)

    if steps == 1:
        ar, ai = abr_ref[...], abi_ref[...]
        hr, hi = hr_ref[...], hi_ref[...]
        nr = ar * hr - ai * hi + sre[...]
        ni = ar * hi + ai * hr + sim[...]
        sre[...] = nr
        sim[...] = ni
        hr_ref[...] = nr
        hi_ref[...] = ni
    else:
        for cb in range(N_STATE // STATE_CHUNK):
            cols = slice(cb * STATE_CHUNK, (cb + 1) * STATE_CHUNK)
            ar = jnp.broadcast_to(abr_ref[:, cols], (nb, STATE_CHUNK))
            ai = jnp.broadcast_to(abi_ref[:, cols], (nb, STATE_CHUNK))

            def step(t, carry, cols=cols, ar=ar, ai=ai):
                hr, hi = carry
                rows = pl.ds(pl.multiple_of(t * nb, nb), nb)
                nr = ar * hr - ai * hi + sre[rows, cols]
                ni = ar * hi + ai * hr + sim[rows, cols]
                sre[rows, cols] = nr
                sim[rows, cols] = ni
                return nr, ni

            hr, hi = lax.fori_loop(0, steps, step, (hr_ref[:, cols], hi_ref[:, cols]), unroll=SCAN_UNROLL)
            hr_ref[:, cols] = hr
            hi_ref[:, cols] = hi

    ys = []
    for kb in range(SSM_KB):
        cols = slice(kb * kw, (kb + 1) * kw)
        ys.append(_bdot(sre[:, cols], ccr_ref[kb]) - _bdot(sim[:, cols], cci_ref[kb]))
    y = jnp.concatenate(ys, axis=1) + d_ref[...] * u
    g = jax.nn.gelu(y)
    out = g * jax.nn.sigmoid(_bdot(g, gw_ref[...]) + gb_ref[...])
    if rows_scratch:
        for c in range(SSM_KB):
            rows_ref[c] = out[:, c * LANES:(c + 1) * LANES]
        for b in range(nb):
            for c in range(SSM_KB):
                col = b * SSM_WIDTH + c * LANES
                o_ref[:, col:col + LANES] = rows_ref[c, pl.ds(b, steps, stride=nb), :]
    else:
        o_ref[...] = out


def _ssm(u, h0_re, h0_im, wts, d, glu_w, glu_b, nb, steps_per_call):
    ab_re, ab_im, bb_re, bb_im, cc_re, cc_im = wts
    rows = nb * steps_per_call
    wide = steps_per_call > 1
    n_calls = u.shape[0] // steps_per_call if wide else 1
    block = (steps_per_call, nb * SSM_WIDTH) if wide else (rows, SSM_WIDTH)
    const2 = lambda i: (0, 0)
    const3 = lambda i: (0, 0, 0)
    kw = N_STATE // SSM_KB
    scratch = [pltpu.VMEM((rows, N_STATE), F32), pltpu.VMEM((rows, N_STATE), F32)]
    if wide:
        scratch.append(pltpu.VMEM((SSM_KB, rows, LANES), F32))
    return pl.pallas_call(
        functools.partial(_ssm_kernel, nb=nb, steps=steps_per_call),
        grid=(n_calls,),
        in_specs=[pl.BlockSpec(block, lambda i: (i, 0)),
                  pl.BlockSpec((nb, N_STATE), const2), pl.BlockSpec((nb, N_STATE), const2),
                  pl.BlockSpec((1, N_STATE), const2), pl.BlockSpec((1, N_STATE), const2),
                  pl.BlockSpec((SSM_KB, LANES, kw), const3), pl.BlockSpec((SSM_KB, LANES, kw), const3),
                  pl.BlockSpec((SSM_KB, kw, LANES), const3), pl.BlockSpec((SSM_KB, kw, LANES), const3),
                  pl.BlockSpec((1, SSM_WIDTH), const2),
                  pl.BlockSpec((SSM_WIDTH, SSM_WIDTH), const2),
                  pl.BlockSpec((1, SSM_WIDTH), const2)],
        out_specs=[pl.BlockSpec(block, lambda i: (i, 0)),
                   pl.BlockSpec((nb, N_STATE), const2), pl.BlockSpec((nb, N_STATE), const2)],
        out_shape=[jax.ShapeDtypeStruct(u.shape, F32),
                   jax.ShapeDtypeStruct((nb, N_STATE), F32), jax.ShapeDtypeStruct((nb, N_STATE), F32)],
        scratch_shapes=scratch,
        compiler_params=_params("arbitrary"),
    )(u, h0_re, h0_im, ab_re, ab_im, bb_re, bb_im, cc_re, cc_im, d, glu_w, glu_b)


def _merge_rows(o_nsa, o_ssm, x, gg_ref, wout_ref, ng_ref, xq_ref):
    gg = gg_ref[...]
    a = _rms(o_nsa, gg[:, :NSA_WIDTH])
    b = _rms(o_ssm, gg[:, NSA_WIDTH:])
    m = (jnp.dot(a.astype(BF16), wout_ref[:NSA_WIDTH], preferred_element_type=F32)
         + jnp.dot(b.astype(BF16), wout_ref[NSA_WIDTH:], preferred_element_type=F32))
    x1 = x + _rms(m, ng_ref[1:2])
    return x1, _bdot(_rms(x1, ng_ref[2:3]), xq_ref[...])


def _cross_rows(qx, mkv_ref):
    outs = []
    for h in range(MEM_HEADS):
        cols = slice(h * MEM_HEAD_DIM, (h + 1) * MEM_HEAD_DIM)
        k = mkv_ref[0, :, cols]
        v = mkv_ref[0, :, MEM_WIDTH + h * MEM_HEAD_DIM: MEM_WIDTH + (h + 1) * MEM_HEAD_DIM]
        s = _bdot_nt(qx[:, cols], k) * MEM_SCALE
        e = jnp.exp(s - jnp.max(s, axis=-1, keepdims=True))
        p = e * (1.0 / jnp.sum(e, axis=-1, keepdims=True))
        outs.append(_bdot(p, v))
    return jnp.concatenate(outs, axis=1)


def _ffn_rows(x1, oa, ng_ref, xo_ref, wi_ref, wo_ref, d_ff):
    x2 = x1 + _rms(_bdot(oa, xo_ref[...]), ng_ref[3:4])
    h = _rms(x2, ng_ref[4:5]).astype(BF16)
    chunk = d_ff // FFN_CHUNKS
    y = None
    for c in range(0, d_ff, chunk):
        z1 = jnp.dot(h, wi_ref[:, c:c + chunk], preferred_element_type=F32)
        z2 = jnp.dot(h, wi_ref[:, d_ff + c:d_ff + c + chunk], preferred_element_type=F32)
        part = _bdot(z1 * jax.nn.sigmoid(z1) * z2, wo_ref[c:c + chunk, :])
        y = part if y is None else y + part
    return x2 + _rms(y, ng_ref[5:6])


def _tail_prompt_kernel(onsa_ref, ossm_ref, x_ref, mkv_ref, gg_ref, wout_ref, ng_ref, xq_ref, xo_ref, wi_ref,
                        wo_ref, o_ref, *, d_ff):
    x1, qx = _merge_rows(onsa_ref[...], ossm_ref[...], x_ref[...], gg_ref, wout_ref, ng_ref, xq_ref)
    o_ref[...] = _ffn_rows(x1, _cross_rows(qx, mkv_ref), ng_ref, xo_ref, wi_ref, wo_ref, d_ff)


def _tail_prompt(o_nsa, o_ssm, x, mkv, gg, w_out, ng, xq, xo, wi, wo, n_seq, tile):
    rows = x.shape[0]
    nt = rows // n_seq // tile
    d_ff = wo.shape[0]
    assert d_ff % (FFN_CHUNKS * LANES) == 0
    row_map = lambda b, t: (b * nt + t, 0)
    const = lambda b, t: (0, 0)
    once = dict(pipeline_mode=pl.Buffered(1))
    return pl.pallas_call(
        functools.partial(_tail_prompt_kernel, d_ff=d_ff),
        grid=(n_seq, nt),
        in_specs=[pl.BlockSpec((tile, NSA_WIDTH), row_map),
                  pl.BlockSpec((tile, SSM_WIDTH), lambda b, t: (t, b)),
                  pl.BlockSpec((tile, D_MODEL), row_map),
                  pl.BlockSpec((1,) + mkv.shape[1:], lambda b, t: (b, 0, 0)),
                  pl.BlockSpec((1, D_MODEL), const),
                  pl.BlockSpec((D_MODEL, D_MODEL), const, **once),
                  pl.BlockSpec((6, D_MODEL), const),
                  pl.BlockSpec((D_MODEL, MEM_WIDTH), const, **once),
                  pl.BlockSpec((MEM_WIDTH, D_MODEL), const, **once),
                  pl.BlockSpec((D_MODEL, 2 * d_ff), const, **once),
                  pl.BlockSpec((d_ff, D_MODEL), const, **once)],
        out_specs=pl.BlockSpec((tile, D_MODEL), row_map),
        out_shape=jax.ShapeDtypeStruct((rows, D_MODEL), F32),
        compiler_params=_params("arbitrary", "arbitrary"),
    )(o_nsa, o_ssm, x, mkv, gg, w_out, ng, xq, xo, wi, wo)


def _merge_kernel(onsa_ref, ossm_ref, x_ref, gg_ref, wout_ref, ng_ref, xq_ref, x1_ref, qx_ref):
    x1_ref[...], qx_ref[...] = _merge_rows(onsa_ref[...], ossm_ref[...], x_ref[...], gg_ref, wout_ref, ng_ref,
                                           xq_ref)


def _merge(o_nsa, o_ssm, x, gg, w_out, ng, xq, n_seq, tile):
    rows = x.shape[0]
    nt = rows // n_seq // tile
    row_map = lambda b, t: (b * nt + t, 0)
    const = lambda b, t: (0, 0)
    return pl.pallas_call(
        _merge_kernel,
        grid=(n_seq, nt),
        in_specs=[pl.BlockSpec((tile, NSA_WIDTH), row_map),
                  pl.BlockSpec((tile, SSM_WIDTH), lambda b, t: (t, b)),
                  pl.BlockSpec((tile, D_MODEL), row_map),
                  pl.BlockSpec((1, D_MODEL), const),
                  pl.BlockSpec((D_MODEL, D_MODEL), const),
                  pl.BlockSpec((6, D_MODEL), const),
                  pl.BlockSpec((D_MODEL, MEM_WIDTH), const)],
        out_specs=[pl.BlockSpec((tile, D_MODEL), row_map), pl.BlockSpec((tile, MEM_WIDTH), row_map)],
        out_shape=[jax.ShapeDtypeStruct((rows, D_MODEL), F32),
                   jax.ShapeDtypeStruct((rows, MEM_WIDTH), F32)],
        compiler_params=_params("arbitrary", "arbitrary"),
    )(o_nsa, o_ssm, x, gg, w_out, ng, xq)


def _norm_matmul_kernel(x_ref, g_ref, w_ref, o_ref):
    o_ref[...] = _bdot(_rms(x_ref[...], g_ref[...]), w_ref[...])


def _norm_matmul(x, g, w, tile):
    rows, k = x.shape
    n = w.shape[1]
    return pl.pallas_call(
        _norm_matmul_kernel,
        grid=(rows // tile,),
        in_specs=[pl.BlockSpec((tile, k), lambda i: (i, 0)),
                  pl.BlockSpec((1, k), lambda i: (0, 0)),
                  pl.BlockSpec((k, n), lambda i: (0, 0))],
        out_specs=pl.BlockSpec((tile, n), lambda i: (i, 0)),
        out_shape=jax.ShapeDtypeStruct((rows, n), F32),
        compiler_params=_params("arbitrary"),
    )(x, g, w)


SAMPLE_GROUP = 8


def _cross_sample_kernel(qx_ref, mkv_ref, o_ref, *, n_mem):
    per_tok = 2 * MEM_HEADS
    sub = _iota2((SAMPLE_GROUP, MEM_HEAD_DIM), 0)
    qx = qx_ref[...]
    outs = []
    for i in range(SAMPLE_GROUP):
        s = jnp.zeros((SAMPLE_GROUP, n_mem), F32)
        for h in range(MEM_HEADS):
            qh = jnp.broadcast_to(qx[i:i + 1, h * MEM_HEAD_DIM:(h + 1) * MEM_HEAD_DIM], sub.shape)
            k = mkv_ref[i, pl.ds(h, n_mem, stride=per_tok), :]
            s = s + _bdot_nt(jnp.where(sub == h, qh, 0.0), k)
        s = s * MEM_SCALE
        e = jnp.exp(s - jnp.max(s, axis=-1, keepdims=True))
        p = e * (1.0 / jnp.sum(e, axis=-1, keepdims=True))
        heads = []
        for h in range(MEM_HEADS):
            v = mkv_ref[i, pl.ds(MEM_HEADS + h, n_mem, stride=per_tok), :]
            heads.append(_bdot(p, v)[h:h + 1])
        outs.append(jnp.concatenate(heads, axis=1))
    o_ref[...] = jnp.concatenate(outs, axis=0)


def _cross_sample(qx, mkv, n_mem):
    rows = qx.shape[0]
    return pl.pallas_call(
        functools.partial(_cross_sample_kernel, n_mem=n_mem),
        grid=(rows // SAMPLE_GROUP,),
        in_specs=[pl.BlockSpec((SAMPLE_GROUP, MEM_WIDTH), lambda i: (i, 0)),
                  pl.BlockSpec((SAMPLE_GROUP,) + mkv.shape[1:], lambda i: (i, 0, 0))],
        out_specs=pl.BlockSpec((SAMPLE_GROUP, MEM_WIDTH), lambda i: (i, 0)),
        out_shape=jax.ShapeDtypeStruct((rows, MEM_WIDTH), F32),
        compiler_params=_params("arbitrary"),
    )(qx, mkv)


def _ffn_kernel(x1_ref, oa_ref, ng_ref, xo_ref, wi_ref, wo_ref, o_ref, *, d_ff):
    o_ref[...] = _ffn_rows(x1_ref[...], oa_ref[...], ng_ref, xo_ref, wi_ref, wo_ref, d_ff)


def _ffn(x1, oa, ng, xo, wi, wo, tile):
    rows = x1.shape[0]
    d_ff = wo.shape[0]
    assert d_ff % (FFN_CHUNKS * LANES) == 0
    const = lambda i: (0, 0)
    once = dict(pipeline_mode=pl.Buffered(1))
    return pl.pallas_call(
        functools.partial(_ffn_kernel, d_ff=d_ff),
        grid=(rows // tile,),
        in_specs=[pl.BlockSpec((tile, D_MODEL), lambda i: (i, 0)),
                  pl.BlockSpec((tile, MEM_WIDTH), lambda i: (i, 0)),
                  pl.BlockSpec((6, D_MODEL), const),
                  pl.BlockSpec((MEM_WIDTH, D_MODEL), const, **once),
                  pl.BlockSpec((D_MODEL, 2 * d_ff), const, **once),
                  pl.BlockSpec((d_ff, D_MODEL), const, **once)],
        out_specs=pl.BlockSpec((tile, D_MODEL), lambda i: (i, 0)),
        out_shape=jax.ShapeDtypeStruct((rows, D_MODEL), F32),
        compiler_params=_params("arbitrary"),
    )(x1, oa, ng, xo, wi, wo)


PAGES_PER_STEP = 32
FOLD_PAGES = 8
NEW_TILE = 8
CMP_GROUP = 16


def _bias_rows(dist, tblt_ref):
    v = jnp.broadcast_to(tblt_ref[:, 0:1], dist.shape)
    for k in range(1, N_BUCKETS):
        v = jnp.where(dist >= _BUCKET_START[k], tblt_ref[:, k:k + 1], v)
    return v


def _sample_bias_kernel(tblt_ref, bw_ref, bc_ref, bn_ref, bs_ref, b0_ref, *, past_len):
    dist = (WINDOW - 1) - _iota2(bw_ref.shape, 1)
    bw_ref[...] = jnp.where((dist >= 0) & (dist < WINDOW), _bias_rows(dist, tblt_ref), NEG)
    dist = past_len - (_iota2(bc_ref.shape, 1) * CMP_BLOCK + (CMP_BLOCK - 1))
    bc_ref[...] = jnp.where(dist >= 0, _bias_rows(dist, tblt_ref), NEG)
    lane = _iota2(bn_ref.shape, 1)
    dist = past_len - ((past_len // CMP_BLOCK + lane) * CMP_BLOCK + (CMP_BLOCK - 1))
    bn_ref[...] = jnp.where((dist >= 0) & (lane < SEL_BLOCK // CMP_BLOCK),
                            _bias_rows(jnp.maximum(dist, 0), tblt_ref), NEG)
    page = lax.broadcasted_iota(I32, bs_ref.shape, 0)
    dist = past_len - page * PAGE_SIZE - lax.broadcasted_iota(I32, bs_ref.shape, 2)
    bs_ref[...] = _bias_rows(dist, tblt_ref)
    b0_ref[...] = _bias_rows(jnp.zeros(b0_ref.shape, I32), tblt_ref)


def _sample_bias(rel_bias, past_len):
    vm = pl.BlockSpec(memory_space=pltpu.VMEM)
    shapes = [(N_HEADS, WINDOW), (N_HEADS, past_len // CMP_BLOCK), (N_HEADS, LANES),
              (past_len // PAGE_SIZE, N_HEADS, PAGE_SIZE), (N_HEADS, LANES)]
    return pl.pallas_call(
        functools.partial(_sample_bias_kernel, past_len=past_len),
        in_specs=[vm], out_specs=[vm] * len(shapes),
        out_shape=[jax.ShapeDtypeStruct(s, F32) for s in shapes],
    )(rel_bias.T)


def _compress_sample_kernel(pt_ref, pool_ref, post_ref, fold_ref, projt_ref, o_ref, buf, sem, *, n_steps):
    i = pl.program_id(0)
    slot = i % GATHER_SLOTS
    ahead = GATHER_SLOTS - 1

    def page_copies(step, slot):
        return [pltpu.make_async_copy(pool_ref.at[pt_ref[step * PAGES_PER_STEP + j]], buf.at[slot, j], sem.at[slot])
                for j in range(PAGES_PER_STEP)]

    @pl.when(i == 0)
    def _():
        for step in range(min(ahead, n_steps)):
            for c in page_copies(step, step):
                c.start()

    @pl.when(i + ahead < n_steps)
    def _():
        for c in page_copies(i + ahead, (i + ahead) % GATHER_SLOTS):
            c.start()

    for c in page_copies(i, slot):
        c.wait()

    post = post_ref[...]
    sums_t = None
    for c in range(0, PAGES_PER_STEP, FOLD_PAGES):
        weighted = jnp.concatenate([(buf[slot, j] * post).astype(BF16) for j in range(c, c + FOLD_PAGES)], axis=1)
        part = jnp.dot(weighted, fold_ref[c * PAGE_SIZE:(c + FOLD_PAGES) * PAGE_SIZE, :],
                       preferred_element_type=F32)
        sums_t = part if sums_t is None else sums_t + part
    o_ref[0] = _dot3(projt_ref[...], sums_t)


def _compress_sample(pool_t, page_table_flat, n_seq, n_pages, pos, proj):
    steps = n_pages // PAGES_PER_STEP
    blocks_per_page = PAGE_SIZE // CMP_BLOCK
    blocks_per_step = PAGES_PER_STEP * blocks_per_page
    assert blocks_per_step == LANES
    pos_t = jnp.tile(pos.T, (1, blocks_per_page))
    row = np.arange(PAGES_PER_STEP * PAGE_SIZE)
    fold = jnp.asarray(row[:, None] // CMP_BLOCK == np.arange(blocks_per_step)[None, :], dtype=BF16)

    const = lambda i, pt: (0, 0)
    grid_spec = pltpu.PrefetchScalarGridSpec(
        num_scalar_prefetch=1,
        grid=(n_seq * steps,),
        in_specs=[pl.BlockSpec(memory_space=pl.ANY),
                  pl.BlockSpec((KV_COLS, PAGE_SIZE), const),
                  pl.BlockSpec(fold.shape, const),
                  pl.BlockSpec((KV_COLS, KV_COLS), const)],
        out_specs=pl.BlockSpec((1, KV_COLS, blocks_per_step), lambda i, pt: (i // steps, 0, i % steps)),
        scratch_shapes=[pltpu.VMEM((GATHER_SLOTS, PAGES_PER_STEP, KV_COLS, PAGE_SIZE), F32),
                        pltpu.SemaphoreType.DMA((GATHER_SLOTS,))],
    )
    return pl.pallas_call(
        functools.partial(_compress_sample_kernel, n_steps=n_seq * steps),
        grid_spec=grid_spec,
        out_shape=jax.ShapeDtypeStruct((n_seq, KV_COLS, steps * blocks_per_step), F32),
        compiler_params=_params("arbitrary"),
    )(page_table_flat, pool_t, pos_t, fold, proj.T)


def _cmp_sample_kernel(qm_ref, blk_ref, kvc_ref, pos_ref, proj_ref, bc_ref, bn_ref, oc_ref, idx_ref,
                       *, past_len):
    n_cmp = blk_ref.shape[2]
    bias = jnp.concatenate([bc_ref[...]] * CMP_GROUP, axis=0)
    mask = bias > MASKED_BELOW
    bias_n = jnp.concatenate([bn_ref[:, :NEW_TILE]] * CMP_GROUP, axis=0)
    mask_n = bias_n > MASKED_BELOW
    new_blk = _dot3(jnp.concatenate([kvc_ref[i] for i in range(CMP_GROUP)], axis=0) * pos_ref[0:1], proj_ref[...])
    first_row = _iota2((NEW_TILE, KV_COLS), 0) == 0
    new_blks = [jnp.where(first_row, jnp.broadcast_to(new_blk[i:i + 1], first_row.shape), 0.0)
                for i in range(CMP_GROUP)]
    s = jnp.concatenate([_dot3(qm_ref[i], blk_ref[i, :KV_HALF, :]) for i in range(CMP_GROUP)], axis=0)
    s_n = jnp.concatenate([_dot3_nt(qm_ref[i], new_blks[i][:, :KV_HALF]) for i in range(CMP_GROUP)], axis=0)
    s = jnp.where(mask, s * SCALE + bias, NEG)
    s_n = jnp.where(mask_n, s_n * SCALE + bias_n, NEG)
    m = jnp.maximum(jnp.max(s, axis=-1, keepdims=True), jnp.max(s_n, axis=-1, keepdims=True))
    e = jnp.where(mask, jnp.exp(s - m), 0.0)
    e_n = jnp.where(mask_n, jnp.exp(s_n - m), 0.0)
    inv = 1.0 / jnp.maximum(jnp.sum(e, axis=-1, keepdims=True) + jnp.sum(e_n, axis=-1, keepdims=True), TINY)
    p = e * inv
    p_n = e_n * inv
    group_ps = []
    for i in range(CMP_GROUP):
        rows = slice(i * N_HEADS, (i + 1) * N_HEADS)
        oc_ref[i] = _bdot_nt(p[rows], blk_ref[i, KV_HALF:, :]) + _bdot(p_n[rows], new_blks[i][:, KV_HALF:])
        group_ps += [jnp.sum(p[i * N_HEADS + g * GQA:i * N_HEADS + (g + 1) * GQA], axis=0, keepdims=True)
                     for g in range(N_KV)]

    ps = jnp.concatenate(group_ps, axis=0)
    lane = _iota2(ps.shape, 1)
    pair = ps + pltpu.roll(ps, n_cmp - 1, 1)
    cur = past_len // SEL_BLOCK
    j = lane // (SEL_BLOCK // CMP_BLOCK)
    forced = (j == 0) | (j == cur) | (j == cur - 1)
    score = jnp.where(lane % 2 == 0, jnp.where(forced, pair + FORCE, pair), -jnp.inf)
    slot = _iota2((ps.shape[0], LANES), 1)
    ids = jnp.zeros((ps.shape[0], LANES), I32)
    for it in range(N_SEL - 1):
        top = jnp.max(score, axis=-1, keepdims=True)
        first = jnp.min(jnp.where(score == top, lane, n_cmp), axis=-1, keepdims=True)
        ids = jnp.where(slot == it, first // (SEL_BLOCK // CMP_BLOCK), ids)
        score = jnp.where(lane == first, -jnp.inf, score)
    idx_ref[...] = jnp.where(slot == N_SEL - 1, cur, ids)


def _cmp_sample(qm, blocks_t, kvc_new, pos, proj, bc, bn, past_len):
    n_seq, _, n_cmp = blocks_t.shape
    assert past_len // SEL_BLOCK > N_SEL and past_len % SEL_BLOCK == 0
    const = lambda b: (0, 0)
    per_seq = lambda b: (b, 0, 0)
    return pl.pallas_call(
        functools.partial(_cmp_sample_kernel, past_len=past_len),
        grid=(n_seq // CMP_GROUP,),
        in_specs=[pl.BlockSpec((CMP_GROUP, N_HEADS, LANES), per_seq),
                  pl.BlockSpec((CMP_GROUP, KV_COLS, n_cmp), per_seq),
                  pl.BlockSpec((CMP_GROUP, 1, KV_COLS), per_seq),
                  pl.BlockSpec((CMP_BLOCK, KV_COLS), const),
                  pl.BlockSpec((KV_COLS, KV_COLS), const),
                  pl.BlockSpec((N_HEADS, n_cmp), const),
                  pl.BlockSpec((N_HEADS, LANES), const)],
        out_specs=[pl.BlockSpec((CMP_GROUP, N_HEADS, LANES), per_seq),
                   pl.BlockSpec((CMP_GROUP * N_KV, LANES), lambda b: (b, 0))],
        out_shape=[jax.ShapeDtypeStruct((n_seq, N_HEADS, LANES), F32),
                   jax.ShapeDtypeStruct((n_seq * N_KV, LANES), I32)],
        compiler_params=_params("arbitrary"),
    )(qm, blocks_t, kvc_new, pos, proj, bc, bn)


def _sel_win_sample_kernel(idx_ref, pt_ref, pool_ref, buf_ref, qm_ref, kvs_ref, kvw_ref, gate_ref, oc_ref, bw_ref,
                           bs_ref, b0_ref, o_ref, nbuf_ref, pages, sem, *, n_past_blocks, n_seq):
    b = pl.program_id(0)
    slot = b % GATHER_SLOTS
    ahead = GATHER_SLOTS - 1
    bpp = PAGE_SIZE // SEL_BLOCK

    def page_copies(seq, slot):
        return [pltpu.make_async_copy(pool_ref.at[pt_ref[(seq * N_KV + g) * N_SEL + k], :, g],
                                      pages.at[slot, g * N_SEL + k], sem.at[slot])
                for g in range(N_KV) for k in range(N_SEL)]

    @pl.when(b == 0)
    def _():
        for seq in range(min(ahead, n_seq)):
            for c in page_copies(seq, seq):
                c.start()

    @pl.when(b + ahead < n_seq)
    def _():
        for c in page_copies(b + ahead, (b + ahead) % GATHER_SLOTS):
            c.start()

    qm = qm_ref[0]

    buf = buf_ref[0]
    shifted = pltpu.roll(buf, WINDOW - 1, 1)
    nbuf = jnp.where(_iota2(buf.shape, 1) == WINDOW - 1, jnp.broadcast_to(kvw_ref[0], buf.shape), shifted)
    nbuf_ref[0] = nbuf
    bias = bw_ref[...]
    s_win = jnp.where(bias > MASKED_BELOW, _bdot(qm, nbuf[:KV_HALF]) * SCALE + bias, NEG)

    for c in page_copies(b, slot):
        c.wait()
    lane_half = _iota2((GQA, PAGE_SIZE), 1) // SEL_BLOCK
    n_keys = N_SEL * PAGE_SIZE
    scores = [jnp.concatenate([s_win, jnp.full((N_HEADS, n_keys - WINDOW), NEG, F32)], axis=1)]
    new_scores = [jnp.full((N_HEADS, 1), NEG, F32)]
    values, new_values = [], []
    for g in range(N_KV):
        cols = slice(g * HEAD_DIM, (g + 1) * HEAD_DIM)
        rows = slice(g * GQA, (g + 1) * GQA)
        qg = qm[rows, cols]
        ks, vs, biases = [], [], []
        has_new = False
        for k in range(N_SEL):
            i = idx_ref[(b * N_KV + g) * N_SEL + k]
            is_past = i < n_past_blocks
            has_new = jnp.logical_or(has_new, jnp.logical_not(is_past))
            i = jnp.minimum(i, n_past_blocks - 1)
            ks.append(pages[slot, g * N_SEL + k, 0])
            vs.append(pages[slot, g * N_SEL + k, 1])
            tile = bs_ref[i // bpp]
            biases.append(jnp.where(is_past & (lane_half == i % bpp), tile[rows], NEG))
        bias = jnp.concatenate(biases, axis=1)
        scores.append(jnp.where(bias > MASKED_BELOW, _bdot(qg, jnp.concatenate(ks, axis=1)) * SCALE + bias, NEG))
        values.append(jnp.concatenate(vs, axis=1))
        k_new = kvs_ref[0][:, cols]
        new_values.append(kvs_ref[0][:, KV_HALF + g * HEAD_DIM:KV_HALF + (g + 1) * HEAD_DIM])
        new_scores.append(jnp.where(
            has_new, jnp.sum(qg * k_new, axis=-1, keepdims=True) * SCALE + b0_ref[rows, 0:1], NEG))

    s = jnp.concatenate(scores, axis=0)
    s_new = jnp.concatenate(new_scores, axis=0)
    mask, mask_new = s > MASKED_BELOW, s_new > MASKED_BELOW
    m = jnp.maximum(jnp.max(s, axis=-1, keepdims=True), s_new)
    e = jnp.where(mask, jnp.exp(s - m), 0.0)
    e_new = jnp.where(mask_new, jnp.exp(s_new - m), 0.0)
    inv = 1.0 / jnp.maximum(jnp.sum(e, axis=-1, keepdims=True) + e_new, TINY)
    p, p_new = e * inv, e_new * inv
    o_win = _bdot_nt(p[:N_HEADS, :WINDOW], nbuf[KV_HALF:])
    o_sel = jnp.concatenate(
        [_bdot_nt(p[N_HEADS + g * GQA:N_HEADS + (g + 1) * GQA], values[g])
         + p_new[N_HEADS + g * GQA:N_HEADS + (g + 1) * GQA] * new_values[g] for g in range(N_KV)], axis=0)

    first_group = _iota2((N_HEADS, HEAD_DIM), 0) < GQA
    oc = oc_ref[0]
    o_cmp = jnp.where(first_group, oc[:, :HEAD_DIM], oc[:, HEAD_DIM:])
    o_win = jnp.where(first_group, o_win[:, :HEAD_DIM], o_win[:, HEAD_DIM:])
    gate = gate_ref[0]
    o_ref[0] = gate[:, 0:1] * o_cmp + gate[:, 1:2] * o_sel + gate[:, 2:3] * o_win


def _sel_win_sample(idx_flat, pt_flat, pool_t, buf_t, qm, kvs_new, kvw_new_col, gates, o_cmp, bw, bs, b0,
                    n_pages):
    n_seq = buf_t.shape[0]
    bpp = PAGE_SIZE // SEL_BLOCK
    n_past_blocks = n_pages * bpp

    per_seq3 = lambda b, idx, pt: (b, 0, 0)
    const2 = lambda b, idx, pt: (0, 0)
    head_tile = pl.BlockSpec((1, N_HEADS, LANES), per_seq3)
    grid_spec = pltpu.PrefetchScalarGridSpec(
        num_scalar_prefetch=2,
        grid=(n_seq,),
        in_specs=[pl.BlockSpec(memory_space=pl.ANY),
                  pl.BlockSpec((1, KV_COLS, WINDOW), per_seq3), head_tile,
                  pl.BlockSpec((1, 1, KV_COLS), per_seq3), pl.BlockSpec((1, KV_COLS, 1), per_seq3), head_tile,
                  head_tile,
                  pl.BlockSpec((N_HEADS, WINDOW), const2),
                  pl.BlockSpec((n_pages, N_HEADS, PAGE_SIZE), lambda b, idx, pt: (0, 0, 0)),
                  pl.BlockSpec((N_HEADS, LANES), const2)],
        out_specs=[pl.BlockSpec((1, N_HEADS, HEAD_DIM), per_seq3),
                   pl.BlockSpec((1, KV_COLS, WINDOW), per_seq3)],
        scratch_shapes=[pltpu.VMEM((GATHER_SLOTS, N_KV * N_SEL, 2, HEAD_DIM, PAGE_SIZE), F32),
                        pltpu.SemaphoreType.DMA((GATHER_SLOTS,))],
    )
    return pl.pallas_call(
        functools.partial(_sel_win_sample_kernel, n_past_blocks=n_past_blocks, n_seq=n_seq),
        grid_spec=grid_spec,
        out_shape=[jax.ShapeDtypeStruct((n_seq, N_HEADS, HEAD_DIM), F32),
                   jax.ShapeDtypeStruct(buf_t.shape, F32)],
        compiler_params=_params("arbitrary"),
    )(idx_flat, pt_flat, pool_t, buf_t, qm, kvs_new, kvw_new_col, gates, o_cmp, bw, bs, b0)


PROJ_TILE = 512
ROW_TILE = 256
TAIL_TILE = 512
SCAN_STEPS = 128


def kernel(x_prompt, x_sample, cache_kv_cmp, cache_kv_sel, cache_kv_win, state_ssm_re, state_ssm_im,
           cache_mem_kv, page_table, mem_prompt, w_in, w_out, norm_g, grp_norm_g, cmp_pos, cmp_proj,
           rel_bias, ssm_a_re, ssm_a_im, ssm_b_re, ssm_b_im, ssm_c_re, ssm_c_im, ssm_d, ssm_log_dt,
           glu_w, glu_b, mem_norm_g, xq, xkv, xo, ffn_wi, ffn_wo):
    depth = w_in.shape[0]
    n_seq, seq, _ = x_prompt.shape
    n_dec, dec_seq, _ = x_sample.shape
    assert dec_seq == 1, "the sample kernels handle one new token per sequence"
    n_pages = page_table.shape[1]
    past_len = n_pages * PAGE_SIZE
    n_mem = mem_prompt.shape[1]
    assert cache_kv_win.shape[2] == WINDOW and seq >= WINDOW and WINDOW % Q_TILE == 0
    win_tiles = WINDOW // Q_TILE

    bcmp, near = _prompt_bias(rel_bias, seq)
    bw_s, bc_s, bn_s, bs_s, b0_s = _sample_bias(rel_bias, past_len)
    pt_flat = page_table.reshape(-1)
    kv5 = (2, N_KV, HEAD_DIM)

    xp = x_prompt.reshape(n_seq * seq, D_MODEL)
    xs = x_sample.reshape(n_dec, D_MODEL)
    outs = [[] for _ in range(11)]
    for l in range(depth):
        ng = norm_g[l]
        w_row, w_t = _pack_w_prompt(w_in[l])
        w_sample = _pack_w_sample(w_in[l])
        w_out_b, xq_b, xkv_b, xo_b = (w[l].astype(BF16) for w in (w_out, xq, xkv, xo))
        wi_b, wo_b, glu_w_b = ffn_wi[l].astype(BF16), ffn_wo[l].astype(BF16), glu_w[l].astype(BF16)
        gg = grp_norm_g[l][None]
        ssm_w = _ssm_weights(ssm_a_re[l], ssm_a_im[l], ssm_b_re[l], ssm_b_im[l], ssm_c_re[l], ssm_c_im[l],
                             ssm_log_dt[l])
        ssm_tail = (ssm_d[l][None], glu_w_b, glu_b[l][None])
        pos, proj_g = _pack_compress(cmp_pos[l], cmp_proj[l], group_major=True)
        _, proj_c = _pack_compress(cmp_pos[l], cmp_proj[l], group_major=False)

        (kvc, ks_aug, kw_aug, u, q_t, kvc_t, kvs_t, kvw_t, vs_t, vw_t, gates_t) = _inproj_prompt(
            xp, ng[0:1], w_row, w_t, n_seq, PROJ_TILE)
        kblk, vblk_t = _compress_prompt(kvc.reshape(n_seq, seq, KV_COLS), pos, proj_g)
        kw_aug = jnp.pad(kw_aug.reshape(n_seq, seq, N_KV * LANES), ((0, 0), (WINDOW, 0), (0, 0)))
        vw_t = jnp.pad(vw_t, ((0, 0), (win_tiles, 0), (0, 0), (0, 0)))
        ks_aug = jnp.pad(ks_aug.reshape(n_seq, seq, N_KV * LANES), ((0, 0), (Q_TILE, 0), (0, 0)))
        vs_t = jnp.pad(vs_t, ((0, 0), (1, 0), (0, 0), (0, 0)))
        o_nsa = _nsa_prompt(rel_bias, q_t, gates_t, kblk, vblk_t, bcmp, near, ks_aug, vs_t, kw_aug, vw_t)
        zeros = jnp.zeros((n_seq, N_STATE), F32)
        o_ssm, h_re, h_im = _ssm(u, zeros, zeros, ssm_w, *ssm_tail, n_seq, SCAN_STEPS)
        mkv = _norm_matmul(mem_prompt.reshape(n_seq * n_mem, D_MODEL), mem_norm_g[l][None], xkv_b, ROW_TILE)
        xp = _tail_prompt(o_nsa.reshape(n_seq * seq, NSA_WIDTH), o_ssm, xp,
                          mkv.reshape(n_seq, n_mem, 2 * MEM_WIDTH), gg, w_out_b, ng, xq_b, xo_b, wi_b, wo_b,
                          n_seq, TAIL_TILE)
        outs[0].append(_rows_minor(kvc_t))
        outs[1].append(_rows_minor(kvs_t))
        outs[2].append(_rows_minor(kvw_t[:, :, seq - WINDOW:]))
        outs[3].append(h_re.reshape(n_seq, N_SSM_GROUPS, SSM_STATE))
        outs[4].append(h_im.reshape(n_seq, N_SSM_GROUPS, SSM_STATE))
        outs[5].append(mkv.reshape(n_seq, n_mem, 2, MEM_HEADS, MEM_HEAD_DIM))

        q, kv, gates, u = _inproj_sample(xs, ng[0:1], w_sample)
        kvc, kvs, kvw = (kv[:, i * KV_COLS:(i + 1) * KV_COLS] for i in range(3))
        head_group = (jnp.arange(N_HEADS)[:, None] // GQA == jnp.arange(N_KV)[None, :]).astype(F32)
        qm = (q.reshape(n_dec, N_HEADS, 1, HEAD_DIM) * head_group[None, :, :, None]).reshape(n_dec, N_HEADS, LANES)
        gates_h = gates[:, :N_HEADS * N_BRANCH].reshape(n_dec, N_HEADS, N_BRANCH)
        gates_h = jnp.pad(gates_h, ((0, 0), (0, 0), (0, LANES - N_BRANCH)))
        blocks_t = _compress_sample(_rows_minor_view(cache_kv_cmp[l]), pt_flat, n_dec, n_pages, pos, proj_c)
        o_cmp, ids = _cmp_sample(qm, blocks_t, kvc.reshape(n_dec, 1, KV_COLS), pos, proj_c, bc_s, bn_s, past_len)
        ids = ids[:, :N_SEL]
        past_page = jnp.minimum(ids, past_len // SEL_BLOCK - 1) // (PAGE_SIZE // SEL_BLOCK)
        sel_pages = jnp.take_along_axis(jnp.repeat(page_table, N_KV, axis=0), past_page, axis=1)
        pool_sel = cache_kv_sel[l].transpose(0, 2, 3, 4, 1)
        o_nsa, new_buf_t = _sel_win_sample(
            ids.reshape(-1), sel_pages.reshape(-1), pool_sel, _rows_minor_view(cache_kv_win[l]), qm, kvs.reshape(n_dec, 1, KV_COLS),
            kvw.reshape(n_dec, KV_COLS, 1), gates_h, o_cmp, bw_s, bs_s, b0_s, n_pages)
        o_ssm, h_re, h_im = _ssm(u, state_ssm_re[l].reshape(n_dec, N_STATE), state_ssm_im[l].reshape(n_dec, N_STATE),
                                 ssm_w, *ssm_tail, n_dec, 1)
        x1, qx = _merge(o_nsa.reshape(n_dec, NSA_WIDTH), o_ssm, xs, gg, w_out_b, ng, xq_b, 1, n_dec)
        oa = _cross_sample(qx, cache_mem_kv[l].reshape(n_dec, n_mem * 2 * MEM_HEADS, MEM_HEAD_DIM), n_mem)
        xs = _ffn(x1, oa, ng, xo_b, wi_b, wo_b, n_dec)
        outs[6].append(kvc.reshape(n_dec, 1, *kv5))
        outs[7].append(kvs.reshape(n_dec, 1, *kv5))
        outs[8].append(_rows_minor(new_buf_t))
        outs[9].append(h_re.reshape(n_dec, N_SSM_GROUPS, SSM_STATE))
        outs[10].append(h_im.reshape(n_dec, N_SSM_GROUPS, SSM_STATE))

    stacked = [jnp.stack(o, axis=0) for o in outs]
    return (xp.reshape(x_prompt.shape), xs.reshape(x_sample.shape), *stacked)
```

```python
import functools
import math

import numpy as np
import jax
import jax.numpy as jnp
from jax import lax
from jax.experimental import pallas as pl
from jax.experimental.pallas import tpu as pltpu

F32 = jnp.float32
BF16 = jnp.bfloat16
I32 = jnp.int32

D_MODEL = 1024
HEAD_DIM = 64
N_HEADS = 8
N_KV = 2
GQA = N_HEADS // N_KV
NSA_WIDTH = N_HEADS * HEAD_DIM
SSM_WIDTH = 512
KV_COLS = 2 * N_KV * HEAD_DIM
KV_HALF = N_KV * HEAD_DIM
N_BRANCH = 3
CMP_BLOCK = 32
SEL_BLOCK = 64
N_SEL = 16
WINDOW = 512
PAGE_SIZE = 128
SSM_GROUP = 16
N_SSM_GROUPS = 32
SSM_STATE = 64
N_STATE = N_SSM_GROUPS * SSM_STATE
N_BUCKETS = 32
MAX_DISTANCE = 128
MEM_HEADS = 4
MEM_HEAD_DIM = 128
MEM_WIDTH = MEM_HEADS * MEM_HEAD_DIM
EPS = 1e-6
NEG = -1e30
MASKED_BELOW = -1e29
TINY = 1e-30
FORCE = 1e4
SCALE = HEAD_DIM ** -0.5
MEM_SCALE = MEM_HEAD_DIM ** -0.5

LANES = 128
Q_TILE = 256
FAR_TILES = 2
FFN_CHUNKS = 2
GATHER_SLOTS = 4
VMEM_LIMIT = 56 * 1024 * 1024

AUG_SEL = HEAD_DIM
AUG_ONE = AUG_SEL + 32
GATE_ROWS = 16
VT_ROWS = HEAD_DIM + 16
LOG2E = math.log2(math.e)


def _bucket_starts():
    n = np.arange(0, 4 * MAX_DISTANCE)
    exact = N_BUCKETS // 2
    nf = np.maximum(n, exact).astype(np.float32)
    big = exact + (np.log(nf / exact) / np.float32(math.log(MAX_DISTANCE / exact))
                   * (N_BUCKETS - exact)).astype(np.int32)
    bucket = np.where(n < exact, n, np.minimum(big, N_BUCKETS - 1))
    return [int(np.argmax(bucket >= k)) for k in range(N_BUCKETS)]


_BUCKET_START = _bucket_starts()
assert _BUCKET_START[-1] <= Q_TILE


def _params(*sem):
    return pltpu.CompilerParams(dimension_semantics=sem, vmem_limit_bytes=VMEM_LIMIT)


def _bdot(a, b):
    return jnp.dot(a.astype(BF16), b.astype(BF16), preferred_element_type=F32)


def _bdot_nt(a, b):
    return lax.dot_general(a.astype(BF16), b.astype(BF16), (((1,), (1,)), ((), ())),
                           preferred_element_type=F32)


def _split(a):
    hi = a.astype(BF16)
    return hi, (a - hi.astype(F32)).astype(BF16)


def _dot3(a, b):
    ah, al = _split(a)
    bh, bl = _split(b)
    d = functools.partial(jnp.dot, preferred_element_type=F32)
    return d(ah, bh) + (d(ah, bl) + d(al, bh))


def _dot3_nt(a, b):
    ah, al = _split(a)
    bh, bl = _split(b)
    d = functools.partial(lax.dot_general, dimension_numbers=(((1,), (1,)), ((), ())),
                          preferred_element_type=F32)
    return d(ah, bh) + (d(ah, bl) + d(al, bh))


def _rms(x, g):
    return x * lax.rsqrt(jnp.mean(x * x, axis=-1, keepdims=True) + EPS) * g


def _masked_softmax(s, mask, axis=-1):
    s = jnp.where(mask, s, NEG)
    m = jnp.max(s, axis=axis, keepdims=True)
    e = jnp.where(mask, jnp.exp(s - m), 0.0)
    return e * (1.0 / jnp.maximum(jnp.sum(e, axis=axis, keepdims=True), TINY))


def _bias_of_dist(dist, tbl_ref, h):
    v = jnp.full(dist.shape, tbl_ref[0, h], F32)
    for k in range(1, N_BUCKETS):
        v = jnp.where(dist >= _BUCKET_START[k], tbl_ref[k, h], v)
    return v


def _iota2(shape, dim):
    return lax.broadcasted_iota(I32, shape, dim)


def _rows_minor(a):
    n, _, rows = a.shape
    return a.reshape(n, 2, N_KV, HEAD_DIM, rows).transpose(0, 4, 1, 2, 3)


def _rows_minor_view(a):
    n, rows = a.shape[:2]
    return a.transpose(0, 2, 3, 4, 1).reshape(n, KV_COLS, rows)


def _cmp_block_of_row(row, n_cmp):
    return jnp.where(row < n_cmp // 2, 2 * row, 2 * (row - n_cmp // 2) + 1)


def _prompt_bias_kernel(tbl_ref, bcmp_ref, near_ref, *, n_cmp, nq):
    h = pl.program_id(0)
    last = tbl_ref[N_BUCKETS - 1, h]

    blk = _cmp_block_of_row(_iota2((n_cmp, Q_TILE), 0), n_cmp)
    for qi in range(nq):
        dist = qi * Q_TILE + _iota2((n_cmp, Q_TILE), 1) - (blk * CMP_BLOCK + (CMP_BLOCK - 1))
        bcmp_ref[qi, 0] = jnp.where(dist >= 0, _bias_of_dist(dist, tbl_ref, h), NEG)

    key = _iota2((Q_TILE, Q_TILE), 0)
    qry = _iota2((Q_TILE, Q_TILE), 1)
    d0 = qry - key
    near_ref[0, 0] = jnp.where(d0 >= 0, (_bias_of_dist(d0, tbl_ref, h) - last) * LOG2E, NEG)
    d1 = Q_TILE + qry - key
    near_ref[0, 1] = (_bias_of_dist(d1, tbl_ref, h) - last) * LOG2E
    near_ref[0, 2] = jnp.where(key > qry, 0.0, NEG)


def _prompt_bias(rel_bias, seq):
    nq = seq // Q_TILE
    n_cmp = seq // CMP_BLOCK
    return pl.pallas_call(
        functools.partial(_prompt_bias_kernel, n_cmp=n_cmp, nq=nq),
        grid=(N_HEADS,),
        in_specs=[pl.BlockSpec(memory_space=pltpu.SMEM)],
        out_specs=[pl.BlockSpec((nq, 1, n_cmp, Q_TILE), lambda h: (0, h // GQA, 0, h % GQA)),
                   pl.BlockSpec((1, 3, Q_TILE, Q_TILE), lambda h: (h // GQA, 0, 0, h % GQA))],
        out_shape=[jax.ShapeDtypeStruct((nq, N_KV, n_cmp, GQA * Q_TILE), F32),
                   jax.ShapeDtypeStruct((N_KV, 3, Q_TILE, GQA * Q_TILE), F32)],
        compiler_params=_params("arbitrary"),
    )(rel_bias)


OFF_KV = NSA_WIDTH
OFF_GATE = OFF_KV + 3 * KV_COLS
OFF_U = OFF_GATE + N_HEADS * N_BRANCH


def _k_cols(w_in, which):
    base = OFF_KV + which * KV_COLS
    return [w_in[:, base + g * HEAD_DIM: base + (g + 1) * HEAD_DIM] for g in range(N_KV)]


def _pack_w_prompt(w_in):
    w_row = jnp.concatenate([w_in[:, OFF_KV:OFF_KV + KV_COLS]] + _k_cols(w_in, 1) + _k_cols(w_in, 2)
                            + [w_in[:, OFF_U:]], axis=1)
    gates = w_in[:, OFF_GATE:OFF_U].reshape(-1, N_KV, GQA * N_BRANCH)
    gates = jnp.pad(gates, ((0, 0), (0, 0), (0, GATE_ROWS - GQA * N_BRANCH))).reshape(-1, N_KV * GATE_ROWS)
    w_t = jnp.concatenate([w_in[:, :OFF_GATE], gates], axis=1).T
    return w_row.astype(BF16), w_t.astype(BF16)


def _inproj_prompt_kernel(x_ref, g_ref, wr_ref, wt_ref, kvc_ref, ksa_ref, kwa_ref, u_ref, qt_ref, kvct_ref,
                          kvst_ref, kvwt_ref, vst_ref, vwt_ref, gt_ref, *, tile):
    hb = _rms(x_ref[...], g_ref[...]).astype(BF16)
    z = jnp.dot(hb, wr_ref[...], preferred_element_type=F32)
    kvc_ref[...] = z[:, :KV_COLS]
    u_ref[...] = z[:, KV_COLS + 4 * HEAD_DIM:]

    pos = pl.program_id(1) * tile + _iota2((tile, LANES - HEAD_DIM), 0)
    col = _iota2((tile, LANES - HEAD_DIM), 1)
    ones = ((col >= AUG_ONE - HEAD_DIM) & (col < AUG_ONE - HEAD_DIM + 2)).astype(F32)
    aug_sel = jnp.where(col == pos // SEL_BLOCK, -NEG, ones)
    k0 = KV_COLS
    ksa_ref[...] = jnp.concatenate(
        [z[:, k0:k0 + HEAD_DIM], aug_sel, z[:, k0 + HEAD_DIM:k0 + 2 * HEAD_DIM], aug_sel], axis=1).astype(BF16)
    k0 = KV_COLS + 2 * HEAD_DIM
    kwa_ref[...] = jnp.concatenate(
        [z[:, k0:k0 + HEAD_DIM], ones, z[:, k0 + HEAD_DIM:k0 + 2 * HEAD_DIM], ones], axis=1).astype(BF16)

    zt = lax.dot_general(wt_ref[...], hb, (((1,), (1,)), ((), ())), preferred_element_type=F32)
    qt_ref[0] = zt[:NSA_WIDTH]
    kvct_ref[0] = zt[OFF_KV:OFF_KV + KV_COLS]
    kvst_ref[0] = zt[OFF_KV + KV_COLS:OFF_KV + 2 * KV_COLS]
    kvwt_ref[0] = zt[OFF_KV + 2 * KV_COLS:OFF_GATE]
    gt_ref[0] = jax.nn.sigmoid(zt[OFF_GATE:])
    v_sel = OFF_KV + KV_COLS + KV_HALF
    v_win = OFF_KV + 2 * KV_COLS + KV_HALF
    sum_rows = (_iota2((VT_ROWS - HEAD_DIM, Q_TILE), 0) == 0).astype(F32)
    for j in range(tile // Q_TILE):
        cols = slice(j * Q_TILE, (j + 1) * Q_TILE)
        for ref, v0 in ((vst_ref, v_sel), (vwt_ref, v_win)):
            parts = []
            for g in range(N_KV):
                parts += [zt[v0 + g * HEAD_DIM:v0 + (g + 1) * HEAD_DIM, cols], sum_rows]
            ref[0, j] = jnp.concatenate(parts, axis=0).astype(BF16)


def _inproj_prompt(x, g, w_row, w_t, n_seq, tile):
    rows = x.shape[0]
    seq = rows // n_seq
    assert seq // SEL_BLOCK <= AUG_ONE - AUG_SEL
    nt = seq // tile
    row_map = lambda b, t: (b * nt + t, 0)
    t_map = lambda b, t: (b, 0, t)
    tiles = tile // Q_TILE
    n_gate = N_KV * GATE_ROWS
    out = [
        (pl.BlockSpec((tile, KV_COLS), row_map), jax.ShapeDtypeStruct((rows, KV_COLS), F32)),
        (pl.BlockSpec((tile, N_KV * LANES), row_map), jax.ShapeDtypeStruct((rows, N_KV * LANES), BF16)),
        (pl.BlockSpec((tile, N_KV * LANES), row_map), jax.ShapeDtypeStruct((rows, N_KV * LANES), BF16)),
        (pl.BlockSpec((tile, SSM_WIDTH), lambda b, t: (t, b)), jax.ShapeDtypeStruct((seq, n_seq * SSM_WIDTH), F32)),
        (pl.BlockSpec((1, NSA_WIDTH, tile), t_map), jax.ShapeDtypeStruct((n_seq, NSA_WIDTH, seq), F32)),
        (pl.BlockSpec((1, KV_COLS, tile), t_map), jax.ShapeDtypeStruct((n_seq, KV_COLS, seq), F32)),
        (pl.BlockSpec((1, KV_COLS, tile), t_map), jax.ShapeDtypeStruct((n_seq, KV_COLS, seq), F32)),
        (pl.BlockSpec((1, KV_COLS, tile), t_map), jax.ShapeDtypeStruct((n_seq, KV_COLS, seq), F32)),
        (pl.BlockSpec((1, tiles, N_KV * VT_ROWS, Q_TILE), lambda b, t: (b, t, 0, 0)),
         jax.ShapeDtypeStruct((n_seq, seq // Q_TILE, N_KV * VT_ROWS, Q_TILE), BF16)),
        (pl.BlockSpec((1, tiles, N_KV * VT_ROWS, Q_TILE), lambda b, t: (b, t, 0, 0)),
         jax.ShapeDtypeStruct((n_seq, seq // Q_TILE, N_KV * VT_ROWS, Q_TILE), BF16)),
        (pl.BlockSpec((1, n_gate, tile), t_map), jax.ShapeDtypeStruct((n_seq, n_gate, seq), F32)),
    ]
    return pl.pallas_call(
        functools.partial(_inproj_prompt_kernel, tile=tile),
        grid=(n_seq, nt),
        in_specs=[pl.BlockSpec((tile, D_MODEL), row_map),
                  pl.BlockSpec((1, D_MODEL), lambda b, t: (0, 0)),
                  pl.BlockSpec(w_row.shape, lambda b, t: (0, 0)),
                  pl.BlockSpec(w_t.shape, lambda b, t: (0, 0))],
        out_specs=[o[0] for o in out],
        out_shape=[o[1] for o in out],
        compiler_params=_params("arbitrary", "arbitrary"),
    )(x, g, w_row, w_t)


S_Q = 0
S_KV = S_Q + NSA_WIDTH
S_GATE = S_KV + 3 * KV_COLS
S_U = S_GATE + LANES
S_END = S_U + SSM_WIDTH


def _pack_w_sample(w_in):
    gates = jnp.pad(w_in[:, OFF_GATE:OFF_U], ((0, 0), (0, LANES - N_HEADS * N_BRANCH)))
    return jnp.concatenate([w_in[:, :OFF_GATE], gates, w_in[:, OFF_U:]], axis=1).astype(BF16)


def _inproj_sample_kernel(x_ref, g_ref, w_ref, q_ref, kv_ref, gate_ref, u_ref):
    z = jnp.dot(_rms(x_ref[...], g_ref[...]).astype(BF16), w_ref[...], preferred_element_type=F32)
    q_ref[...] = z[:, S_Q:S_KV]
    kv_ref[...] = z[:, S_KV:S_GATE]
    gate_ref[...] = jax.nn.sigmoid(z[:, S_GATE:S_U])
    u_ref[...] = z[:, S_U:S_END]


def _inproj_sample(x, g, w):
    rows = x.shape[0]
    vm = pl.BlockSpec(memory_space=pltpu.VMEM)
    widths = [NSA_WIDTH, 3 * KV_COLS, LANES, SSM_WIDTH]
    return pl.pallas_call(
        _inproj_sample_kernel,
        in_specs=[vm] * 3, out_specs=[vm] * 4,
        out_shape=[jax.ShapeDtypeStruct((rows, w_), F32) for w_ in widths],
        compiler_params=pltpu.CompilerParams(vmem_limit_bytes=VMEM_LIMIT),
    )(x, g, w)


def _pack_compress(cmp_pos, cmp_proj, group_major):
    pos = jnp.broadcast_to(cmp_pos.transpose(1, 0, 2)[:, :, None, :],
                           (CMP_BLOCK, 2, N_KV, HEAD_DIM)).reshape(CMP_BLOCK, KV_COLS)
    eye = jnp.eye(2 * N_KV, dtype=F32).reshape(2, N_KV, 2, N_KV)
    order = 'cgdGCe' if group_major else 'cgdCGe'
    proj = jnp.einsum('cde,cgCG->' + order, cmp_proj, eye).reshape(KV_COLS, KV_COLS)
    return pos, proj


def _compress_prompt_kernel(kv_ref, pos_ref, proj_ref, projt_ref, kblk_ref, vblkt_ref):
    x = kv_ref[0]
    n_sel = x.shape[0] // SEL_BLOCK
    x = x.reshape(n_sel, SEL_BLOCK // CMP_BLOCK, CMP_BLOCK, KV_COLS)
    pos = pos_ref[...][None]
    sums = jnp.concatenate([jnp.sum(x[:, i] * pos, axis=1) for i in range(SEL_BLOCK // CMP_BLOCK)], axis=0)
    blk = _dot3(sums, proj_ref[...])
    blk_t = _dot3_nt(projt_ref[...], sums)
    for g in range(N_KV):
        kblk_ref[0, g] = blk[:, g * KV_HALF:g * KV_HALF + HEAD_DIM]
        vblkt_ref[0, g] = blk_t[g * KV_HALF + HEAD_DIM:(g + 1) * KV_HALF]


def _compress_prompt(kvc, pos, proj):
    n_seq, seq, _ = kvc.shape
    n_cmp = seq // CMP_BLOCK
    return pl.pallas_call(
        _compress_prompt_kernel,
        grid=(n_seq,),
        in_specs=[pl.BlockSpec((1, seq, KV_COLS), lambda b: (b, 0, 0)),
                  pl.BlockSpec((CMP_BLOCK, KV_COLS), lambda b: (0, 0)),
                  pl.BlockSpec((KV_COLS, KV_COLS), lambda b: (0, 0)),
                  pl.BlockSpec((KV_COLS, KV_COLS), lambda b: (0, 0))],
        out_specs=[pl.BlockSpec((1, N_KV, n_cmp, HEAD_DIM), lambda b: (b, 0, 0, 0)),
                   pl.BlockSpec((1, N_KV, HEAD_DIM, n_cmp), lambda b: (b, 0, 0, 0))],
        out_shape=[jax.ShapeDtypeStruct((n_seq, N_KV, n_cmp, HEAD_DIM), F32),
                   jax.ShapeDtypeStruct((n_seq, N_KV, HEAD_DIM, n_cmp), F32)],
        compiler_params=_params("arbitrary"),
    )(kvc, pos, proj, proj.T)


def _select_blocks_t(pair, q0):
    row = _iota2(pair.shape, 0)
    n_sel = pair.shape[0]
    cur = (q0 + _iota2(pair.shape, 1)) // SEL_BLOCK
    forced = (row == 0) | (row == cur) | (row == cur - 1)
    score = jnp.where(row <= cur, jnp.where(forced, pair + FORCE, pair), -jnp.inf)
    chosen = jnp.zeros(pair.shape, F32)
    for _ in range(N_SEL):
        m = jnp.max(score, axis=0, keepdims=True)
        hit = (score == m) & (m > -jnp.inf)
        first = jnp.min(jnp.where(hit, row, n_sel), axis=0, keepdims=True)
        pick = row == first
        chosen = jnp.where(pick, 1.0, chosen)
        score = jnp.where(pick, -jnp.inf, score)
    return chosen


def _attend(state, k_aug, v_t, q_aug, extras):
    m, acc = state
    s = jnp.dot(k_aug, q_aug, preferred_element_type=F32)
    if any(t is not None or c is not None for t, c in extras):
        rows = []
        for t, c in extras:
            if t is None:
                rows.append(jnp.full((Q_TILE, s.shape[1]), 0.0 if c is None else c, F32))
            else:
                rows.append(t if c is None else t + c)
        s = s + jnp.concatenate(rows, axis=0)
    m_new = jnp.maximum(m, jnp.max(s, axis=0, keepdims=True))
    e = jnp.exp2(s - m_new)
    acc = jnp.exp2(m - m_new) * acc + jnp.dot(v_t, e.astype(BF16), preferred_element_type=F32)
    return m_new, acc


def _attend_result(state):
    acc = state[1]
    return acc[:HEAD_DIM] * (1.0 / jnp.maximum(acc[HEAD_DIM:HEAD_DIM + 1], TINY))


def _nsa_prompt_kernel(tbl_ref, qt_ref, gt_ref, kblk_ref, vblkt_ref, bcmp_ref, near_ref, ks_ref, vst_ref,
                       kw_ref, vwt_ref, o_ref):
    g = pl.program_id(1)
    qi = pl.program_id(2)
    q0 = qi * Q_TILE
    win_tiles = WINDOW // Q_TILE
    v_rows = pl.ds(pl.multiple_of(g * VT_ROWS, VT_ROWS), VT_ROWS)

    width = GQA * Q_TILE
    q = jnp.concatenate([qt_ref[0, r * HEAD_DIM:(r + 1) * HEAD_DIM, :] for r in range(GQA)], axis=1) * SCALE

    row = _iota2((LANES - AUG_ONE, Q_TILE), 0)
    consts = []
    for r in range(GQA):
        last = jnp.full((LANES - AUG_ONE, Q_TILE), tbl_ref[N_BUCKETS - 1, g * GQA + r] * LOG2E, F32)
        hi = last.astype(BF16).astype(F32)
        consts.append(jnp.where(row == 0, hi, jnp.where(row == 1, last - hi, 0.0)).astype(BF16))
    consts = jnp.concatenate(consts, axis=1)
    q_log2 = (q * LOG2E).astype(BF16)
    near0, near1, oldest = near_ref[0, 0], near_ref[0, 1], near_ref[0, 2]
    init = (jnp.full((1, width), NEG, F32), jnp.zeros((VT_ROWS, width), F32))

    def key_rows(ref, tile, n_tiles):
        return ref[0, pl.ds(pl.multiple_of(tile * Q_TILE, Q_TILE), n_tiles * Q_TILE), :]

    def value_cols(ref, tile, n_tiles):
        return jnp.concatenate([ref[0, tile + j, v_rows, :] for j in range(n_tiles)], axis=1)

    def tile_mask(ok):
        return jnp.where(ok, 0.0, NEG)

    q_aug = jnp.concatenate([q_log2, jnp.zeros((AUG_ONE - AUG_SEL, width), BF16), consts], axis=0)
    extras = []
    for mt in range(win_tiles):
        table = oldest if mt == 0 else near1 if mt == win_tiles - 1 else None
        extras.append((table, tile_mask(qi + mt >= win_tiles)))
    extras.append((near0, None))
    st = _attend(init, key_rows(kw_ref, qi, win_tiles + 1), value_cols(vwt_ref, qi, win_tiles + 1), q_aug, extras)
    o_win = _attend_result(st)

    k_cmp, v_cmp_t = kblk_ref[0, 0], vblkt_ref[0, 0]
    n_cmp = k_cmp.shape[0]
    bias = bcmp_ref[0, 0]
    p = _masked_softmax(_dot3(k_cmp, q) + bias, bias > MASKED_BELOW, axis=0)
    o_cmp = _bdot(v_cmp_t, p)
    ps = p[:, :Q_TILE]
    for r in range(1, GQA):
        ps = ps + p[:, r * Q_TILE:(r + 1) * Q_TILE]
    chosen = _select_blocks_t(ps[:n_cmp // 2] + ps[n_cmp // 2:], q0)
    not_chosen = (chosen - 1.0).astype(BF16)
    pad_rows = (AUG_ONE - AUG_SEL) - not_chosen.shape[0]
    if pad_rows:
        not_chosen = jnp.concatenate([not_chosen, jnp.zeros((pad_rows, Q_TILE), BF16)], axis=0)
    q_aug = jnp.concatenate([q_log2, jnp.concatenate([not_chosen] * GQA, axis=1), consts], axis=0)

    extras = [(near1, tile_mask(qi >= 1)), (near0, None)]
    st = _attend(init, key_rows(ks_ref, qi, 2), value_cols(vst_ref, qi, 2), q_aug, extras)
    n_far = jnp.maximum(qi - 1, 0)

    def far_chunk(c, st, masked=False):
        t0 = c * FAR_TILES
        extras = [(None, tile_mask(t0 + j < n_far) if masked else None) for j in range(FAR_TILES)]
        return _attend(st, key_rows(ks_ref, t0 + 1, FAR_TILES), value_cols(vst_ref, t0 + 1, FAR_TILES), q_aug, extras)

    full = n_far // FAR_TILES
    st = lax.fori_loop(0, full, far_chunk, st)
    st = lax.cond(n_far % FAR_TILES != 0, lambda st=st: far_chunk(full, st, masked=True), lambda st=st: st)
    o_sel = _attend_result(st)

    def gate(branch):
        return jnp.concatenate([gt_ref[0, r * N_BRANCH + branch:r * N_BRANCH + branch + 1, :]
                                for r in range(GQA)], axis=1)

    o = gate(0) * o_cmp + gate(1) * o_sel + gate(2) * o_win
    o_ref[0] = jnp.concatenate([o[:, r * Q_TILE:(r + 1) * Q_TILE] for r in range(GQA)], axis=0).T


def _nsa_prompt(rel_bias, q_t, gates_t, kblk, vblk_t, bcmp, near, ks_aug, vs_t, kw_aug, vw_t):
    n_seq, _, seq = q_t.shape
    nq = seq // Q_TILE
    assert nq % FAR_TILES == 0
    n_cmp = kblk.shape[2]
    gw = GQA * HEAD_DIM
    wt = WINDOW // Q_TILE
    return pl.pallas_call(
        _nsa_prompt_kernel,
        grid=(n_seq, N_KV, nq),
        in_specs=[pl.BlockSpec(memory_space=pltpu.SMEM),
                  pl.BlockSpec((1, gw, Q_TILE), lambda b, g, qi: (b, g, qi)),
                  pl.BlockSpec((1, GATE_ROWS, Q_TILE), lambda b, g, qi: (b, g, qi)),
                  pl.BlockSpec((1, 1, n_cmp, HEAD_DIM), lambda b, g, qi: (b, g, 0, 0)),
                  pl.BlockSpec((1, 1, HEAD_DIM, n_cmp), lambda b, g, qi: (b, g, 0, 0)),
                  pl.BlockSpec((1, 1, n_cmp, GQA * Q_TILE), lambda b, g, qi: (qi, g, 0, 0)),
                  pl.BlockSpec((1, 3, Q_TILE, GQA * Q_TILE), lambda b, g, qi: (g, 0, 0, 0)),
                  pl.BlockSpec((1, Q_TILE + seq, LANES), lambda b, g, qi: (b, 0, g)),
                  pl.BlockSpec((1, 1 + nq, N_KV * VT_ROWS, Q_TILE), lambda b, g, qi: (b, 0, 0, 0)),
                  pl.BlockSpec((1, WINDOW + seq, LANES), lambda b, g, qi: (b, 0, g)),
                  pl.BlockSpec((1, wt + nq, N_KV * VT_ROWS, Q_TILE), lambda b, g, qi: (b, 0, 0, 0))],
        out_specs=pl.BlockSpec((1, Q_TILE, gw), lambda b, g, qi: (b, qi, g)),
        out_shape=jax.ShapeDtypeStruct((n_seq, seq, NSA_WIDTH), F32),
        compiler_params=_params("arbitrary", "arbitrary", "arbitrary"),
    )(rel_bias, q_t, gates_t, kblk, vblk_t, bcmp, near, ks_aug, vs_t, kw_aug, vw_t)


STATE_CHUNK = 512
SCAN_UNROLL = 4
SSM_KB = SSM_WIDTH // LANES
GROUPS_PER_KB = N_SSM_GROUPS // SSM_KB


def _ssm_disc_kernel(are_ref, aim_ref, ldt_ref, bre_ref, bim_ref, abr_ref, abi_ref, bbr_ref, bbi_ref):
    a_re, a_im = are_ref[...], aim_ref[...]
    dt = jnp.exp(ldt_ref[...])
    mag = jnp.exp(a_re * dt)
    ab_re = mag * jnp.cos(a_im * dt)
    ab_im = mag * jnp.sin(a_im * dt)
    den = a_re * a_re + a_im * a_im
    co_re = ((ab_re - 1.0) * a_re + ab_im * a_im) / den
    co_im = (ab_im * a_re - (ab_re - 1.0) * a_im) / den
    abr_ref[...] = ab_re
    abi_ref[...] = ab_im
    b_re, b_im = bre_ref[...], bim_ref[...]
    bbr_ref[...] = co_re[:, None, :] * b_re - co_im[:, None, :] * b_im
    bbi_ref[...] = co_re[:, None, :] * b_im + co_im[:, None, :] * b_re


def _block_diag(w):
    a, b = w.shape[1:]
    w = w.reshape(SSM_KB, GROUPS_PER_KB, a, b)
    eye = jnp.eye(GROUPS_PER_KB, dtype=w.dtype)
    return jnp.einsum('kgab,gh->kgahb', w, eye).reshape(SSM_KB, GROUPS_PER_KB * a, GROUPS_PER_KB * b)


def _ssm_weights(a_re, a_im, b_re, b_im, c_re, c_im, log_dt):
    ng, p = a_re.shape
    vm = pl.BlockSpec(memory_space=pltpu.VMEM)
    ab_re, ab_im, bb_re, bb_im = pl.pallas_call(
        _ssm_disc_kernel,
        in_specs=[vm] * 5, out_specs=[vm] * 4,
        out_shape=[jax.ShapeDtypeStruct((ng, p), F32)] * 2
        + [jax.ShapeDtypeStruct((ng, SSM_GROUP, p), F32)] * 2,
    )(a_re, a_im, log_dt.reshape(ng, 1), b_re.transpose(0, 2, 1), b_im.transpose(0, 2, 1))
    return (ab_re.reshape(1, N_STATE), ab_im.reshape(1, N_STATE),
            _block_diag(bb_re).astype(BF16), _block_diag(bb_im).astype(BF16),
            _block_diag(c_re.transpose(0, 2, 1)).astype(BF16),
            _block_diag(c_im.transpose(0, 2, 1)).astype(BF16))


def _ssm_kernel(u_ref, h0r_ref, h0i_ref, abr_ref, abi_ref, bbr_ref, bbi_ref, ccr_ref, cci_ref,
                d_ref, gw_ref, gb_ref, o_ref, hr_ref, hi_ref, sre, sim, *rows_scratch, nb, steps):
    @pl.when(pl.program_id(0) == 0)
    def _():
        hr_ref[...] = h0r_ref[...]
        hi_ref[...] = h0i_ref[...]

    if rows_scratch:
        rows_ref, = rows_scratch
        for b in range(nb):
            for c in range(SSM_KB):
                col = b * SSM_WIDTH + c * LANES
                rows_ref[c, pl.ds(b, steps, stride=nb), :] = u_ref[:, col:col + LANES]
        u = jnp.concatenate([rows_ref[c] for c in range(SSM_KB)], axis=1)
    else:
        u = u_ref[...]
    ub = u.astype(BF16)
    kw = N_STATE // SSM_KB
    for kb in range(SSM_KB):
        uk = ub[:, kb * LANES:(kb + 1) * LANES]
        sre[:, kb * kw:(kb + 1) * kw] = jnp.dot(uk, bbr_ref[kb], preferred_element_type=F32)
        sim[:, kb * kw:(kb + 1) * kw] = jnp.dot(uk, bbi_ref[kb], preferred_element_type=F32)

    if steps == 1:
        ar, ai = abr_ref[...], abi_ref[...]
        hr, hi = hr_ref[...], hi_ref[...]
        nr = ar * hr - ai * hi + sre[...]
        ni = ar * hi + ai * hr + sim[...]
        sre[...] = nr
        sim[...] = ni
        hr_ref[...] = nr
        hi_ref[...] = ni
    else:
        for cb in range(N_STATE // STATE_CHUNK):
            cols = slice(cb * STATE_CHUNK, (cb + 1) * STATE_CHUNK)
            ar = jnp.broadcast_to(abr_ref[:, cols], (nb, STATE_CHUNK))
            ai = jnp.broadcast_to(abi_ref[:, cols], (nb, STATE_CHUNK))

            def step(t, carry, cols=cols, ar=ar, ai=ai):
                hr, hi = carry
                rows = pl.ds(pl.multiple_of(t * nb, nb), nb)
                nr = ar * hr - ai * hi + sre[rows, cols]
                ni = ar * hi + ai * hr + sim[rows, cols]
                sre[rows, cols] = nr
                sim[rows, cols] = ni
                return nr, ni

            hr, hi = lax.fori_loop(0, steps, step, (hr_ref[:, cols], hi_ref[:, cols]), unroll=SCAN_UNROLL)
            hr_ref[:, cols] = hr
            hi_ref[:, cols] = hi

    ys = []
    for kb in range(SSM_KB):
        cols = slice(kb * kw, (kb + 1) * kw)
        ys.append(_bdot(sre[:, cols], ccr_ref[kb]) - _bdot(sim[:, cols], cci_ref[kb]))
    y = jnp.concatenate(ys, axis=1) + d_ref[...] * u
    g = jax.nn.gelu(y)
    out = g * jax.nn.sigmoid(_bdot(g, gw_ref[...]) + gb_ref[...])
    if rows_scratch:
        for c in range(SSM_KB):
            rows_ref[c] = out[:, c * LANES:(c + 1) * LANES]
        for b in range(nb):
            for c in range(SSM_KB):
                col = b * SSM_WIDTH + c * LANES
                o_ref[:, col:col + LANES] = rows_ref[c, pl.ds(b, steps, stride=nb), :]
    else:
        o_ref[...] = out


def _ssm(u, h0_re, h0_im, wts, d, glu_w, glu_b, nb, steps_per_call):
    ab_re, ab_im, bb_re, bb_im, cc_re, cc_im = wts
    rows = nb * steps_per_call
    wide = steps_per_call > 1
    n_calls = u.shape[0] // steps_per_call if wide else 1
    block = (steps_per_call, nb * SSM_WIDTH) if wide else (rows, SSM_WIDTH)
    const2 = lambda i: (0, 0)
    const3 = lambda i: (0, 0, 0)
    kw = N_STATE // SSM_KB
    scratch = [pltpu.VMEM((rows, N_STATE), F32), pltpu.VMEM((rows, N_STATE), F32)]
    if wide:
        scratch.append(pltpu.VMEM((SSM_KB, rows, LANES), F32))
    return pl.pallas_call(
        functools.partial(_ssm_kernel, nb=nb, steps=steps_per_call),
        grid=(n_calls,),
        in_specs=[pl.BlockSpec(block, lambda i: (i, 0)),
                  pl.BlockSpec((nb, N_STATE), const2), pl.BlockSpec((nb, N_STATE), const2),
                  pl.BlockSpec((1, N_STATE), const2), pl.BlockSpec((1, N_STATE), const2),
                  pl.BlockSpec((SSM_KB, LANES, kw), const3), pl.BlockSpec((SSM_KB, LANES, kw), const3),
                  pl.BlockSpec((SSM_KB, kw, LANES), const3), pl.BlockSpec((SSM_KB, kw, LANES), const3),
                  pl.BlockSpec((1, SSM_WIDTH), const2),
                  pl.BlockSpec((SSM_WIDTH, SSM_WIDTH), const2),
                  pl.BlockSpec((1, SSM_WIDTH), const2)],
        out_specs=[pl.BlockSpec(block, lambda i: (i, 0)),
                   pl.BlockSpec((nb, N_STATE), const2), pl.BlockSpec((nb, N_STATE), const2)],
        out_shape=[jax.ShapeDtypeStruct(u.shape, F32),
                   jax.ShapeDtypeStruct((nb, N_STATE), F32), jax.ShapeDtypeStruct((nb, N_STATE), F32)],
        scratch_shapes=scratch,
        compiler_params=_params("arbitrary"),
    )(u, h0_re, h0_im, ab_re, ab_im, bb_re, bb_im, cc_re, cc_im, d, glu_w, glu_b)


def _merge_rows(o_nsa, o_ssm, x, gg_ref, wout_ref, ng_ref, xq_ref):
    gg = gg_ref[...]
    a = _rms(o_nsa, gg[:, :NSA_WIDTH])
    b = _rms(o_ssm, gg[:, NSA_WIDTH:])
    m = (jnp.dot(a.astype(BF16), wout_ref[:NSA_WIDTH], preferred_element_type=F32)
         + jnp.dot(b.astype(BF16), wout_ref[NSA_WIDTH:], preferred_element_type=F32))
    x1 = x + _rms(m, ng_ref[1:2])
    return x1, _bdot(_rms(x1, ng_ref[2:3]), xq_ref[...])


def _cross_rows(qx, mkv_ref):
    outs = []
    for h in range(MEM_HEADS):
        cols = slice(h * MEM_HEAD_DIM, (h + 1) * MEM_HEAD_DIM)
        k = mkv_ref[0, :, cols]
        v = mkv_ref[0, :, MEM_WIDTH + h * MEM_HEAD_DIM: MEM_WIDTH + (h + 1) * MEM_HEAD_DIM]
        s = _bdot_nt(qx[:, cols], k) * MEM_SCALE
        e = jnp.exp(s - jnp.max(s, axis=-1, keepdims=True))
        p = e * (1.0 / jnp.sum(e, axis=-1, keepdims=True))
        outs.append(_bdot(p, v))
    return jnp.concatenate(outs, axis=1)


def _ffn_rows(x1, oa, ng_ref, xo_ref, wi_ref, wo_ref, d_ff):
    x2 = x1 + _rms(_bdot(oa, xo_ref[...]), ng_ref[3:4])
    h = _rms(x2, ng_ref[4:5]).astype(BF16)
    chunk = d_ff // FFN_CHUNKS
    y = None
    for c in range(0, d_ff, chunk):
        z1 = jnp.dot(h, wi_ref[:, c:c + chunk], preferred_element_type=F32)
        z2 = jnp.dot(h, wi_ref[:, d_ff + c:d_ff + c + chunk], preferred_element_type=F32)
        part = _bdot(z1 * jax.nn.sigmoid(z1) * z2, wo_ref[c:c + chunk, :])
        y = part if y is None else y + part
    return x2 + _rms(y, ng_ref[5:6])


def _tail_prompt_kernel(onsa_ref, ossm_ref, x_ref, mkv_ref, gg_ref, wout_ref, ng_ref, xq_ref, xo_ref, wi_ref,
                        wo_ref, o_ref, *, d_ff):
    x1, qx = _merge_rows(onsa_ref[...], ossm_ref[...], x_ref[...], gg_ref, wout_ref, ng_ref, xq_ref)
    o_ref[...] = _ffn_rows(x1, _cross_rows(qx, mkv_ref), ng_ref, xo_ref, wi_ref, wo_ref, d_ff)


def _tail_prompt(o_nsa, o_ssm, x, mkv, gg, w_out, ng, xq, xo, wi, wo, n_seq, tile):
    rows = x.shape[0]
    nt = rows // n_seq // tile
    d_ff = wo.shape[0]
    assert d_ff % (FFN_CHUNKS * LANES) == 0
    row_map = lambda b, t: (b * nt + t, 0)
    const = lambda b, t: (0, 0)
    once = dict(pipeline_mode=pl.Buffered(1))
    return pl.pallas_call(
        functools.partial(_tail_prompt_kernel, d_ff=d_ff),
        grid=(n_seq, nt),
        in_specs=[pl.BlockSpec((tile, NSA_WIDTH), row_map),
                  pl.BlockSpec((tile, SSM_WIDTH), lambda b, t: (t, b)),
                  pl.BlockSpec((tile, D_MODEL), row_map),
                  pl.BlockSpec((1,) + mkv.shape[1:], lambda b, t: (b, 0, 0)),
                  pl.BlockSpec((1, D_MODEL), const),
                  pl.BlockSpec((D_MODEL, D_MODEL), const, **once),
                  pl.BlockSpec((6, D_MODEL), const),
                  pl.BlockSpec((D_MODEL, MEM_WIDTH), const, **once),
                  pl.BlockSpec((MEM_WIDTH, D_MODEL), const, **once),
                  pl.BlockSpec((D_MODEL, 2 * d_ff), const, **once),
                  pl.BlockSpec((d_ff, D_MODEL), const, **once)],
        out_specs=pl.BlockSpec((tile, D_MODEL), row_map),
        out_shape=jax.ShapeDtypeStruct((rows, D_MODEL), F32),
        compiler_params=_params("arbitrary", "arbitrary"),
    )(o_nsa, o_ssm, x, mkv, gg, w_out, ng, xq, xo, wi, wo)


def _merge_kernel(onsa_ref, ossm_ref, x_ref, gg_ref, wout_ref, ng_ref, xq_ref, x1_ref, qx_ref):
    x1_ref[...], qx_ref[...] = _merge_rows(onsa_ref[...], ossm_ref[...], x_ref[...], gg_ref, wout_ref, ng_ref,
                                           xq_ref)


def _merge(o_nsa, o_ssm, x, gg, w_out, ng, xq, n_seq, tile):
    rows = x.shape[0]
    nt = rows // n_seq // tile
    row_map = lambda b, t: (b * nt + t, 0)
    const = lambda b, t: (0, 0)
    return pl.pallas_call(
        _merge_kernel,
        grid=(n_seq, nt),
        in_specs=[pl.BlockSpec((tile, NSA_WIDTH), row_map),
                  pl.BlockSpec((tile, SSM_WIDTH), lambda b, t: (t, b)),
                  pl.BlockSpec((tile, D_MODEL), row_map),
                  pl.BlockSpec((1, D_MODEL), const),
                  pl.BlockSpec((D_MODEL, D_MODEL), const),
                  pl.BlockSpec((6, D_MODEL), const),
                  pl.BlockSpec((D_MODEL, MEM_WIDTH), const)],
        out_specs=[pl.BlockSpec((tile, D_MODEL), row_map), pl.BlockSpec((tile, MEM_WIDTH), row_map)],
        out_shape=[jax.ShapeDtypeStruct((rows, D_MODEL), F32),
                   jax.ShapeDtypeStruct((rows, MEM_WIDTH), F32)],
        compiler_params=_params("arbitrary", "arbitrary"),
    )(o_nsa, o_ssm, x, gg, w_out, ng, xq)


def _norm_matmul_kernel(x_ref, g_ref, w_ref, o_ref):
    o_ref[...] = _bdot(_rms(x_ref[...], g_ref[...]), w_ref[...])


def _norm_matmul(x, g, w, tile):
    rows, k = x.shape
    n = w.shape[1]
    return pl.pallas_call(
        _norm_matmul_kernel,
        grid=(rows // tile,),
        in_specs=[pl.BlockSpec((tile, k), lambda i: (i, 0)),
                  pl.BlockSpec((1, k), lambda i: (0, 0)),
                  pl.BlockSpec((k, n), lambda i: (0, 0))],
        out_specs=pl.BlockSpec((tile, n), lambda i: (i, 0)),
        out_shape=jax.ShapeDtypeStruct((rows, n), F32),
        compiler_params=_params("arbitrary"),
    )(x, g, w)


SAMPLE_GROUP = 8


def _cross_sample_kernel(qx_ref, mkv_ref, o_ref, *, n_mem):
    per_tok = 2 * MEM_HEADS
    sub = _iota2((SAMPLE_GROUP, MEM_HEAD_DIM), 0)
    qx = qx_ref[...]
    outs = []
    for i in range(SAMPLE_GROUP):
        s = jnp.zeros((SAMPLE_GROUP, n_mem), F32)
        for h in range(MEM_HEADS):
            qh = jnp.broadcast_to(qx[i:i + 1, h * MEM_HEAD_DIM:(h + 1) * MEM_HEAD_DIM], sub.shape)
            k = mkv_ref[i, pl.ds(h, n_mem, stride=per_tok), :]
            s = s + _bdot_nt(jnp.where(sub == h, qh, 0.0), k)
        s = s * MEM_SCALE
        e = jnp.exp(s - jnp.max(s, axis=-1, keepdims=True))
        p = e * (1.0 / jnp.sum(e, axis=-1, keepdims=True))
        heads = []
        for h in range(MEM_HEADS):
            v = mkv_ref[i, pl.ds(MEM_HEADS + h, n_mem, stride=per_tok), :]
            heads.append(_bdot(p, v)[h:h + 1])
        outs.append(jnp.concatenate(heads, axis=1))
    o_ref[...] = jnp.concatenate(outs, axis=0)


def _cross_sample(qx, mkv, n_mem):
    rows = qx.shape[0]
    return pl.pallas_call(
        functools.partial(_cross_sample_kernel, n_mem=n_mem),
        grid=(rows // SAMPLE_GROUP,),
        in_specs=[pl.BlockSpec((SAMPLE_GROUP, MEM_WIDTH), lambda i: (i, 0)),
                  pl.BlockSpec((SAMPLE_GROUP,) + mkv.shape[1:], lambda i: (i, 0, 0))],
        out_specs=pl.BlockSpec((SAMPLE_GROUP, MEM_WIDTH), lambda i: (i, 0)),
        out_shape=jax.ShapeDtypeStruct((rows, MEM_WIDTH), F32),
        compiler_params=_params("arbitrary"),
    )(qx, mkv)


def _ffn_kernel(x1_ref, oa_ref, ng_ref, xo_ref, wi_ref, wo_ref, o_ref, *, d_ff):
    o_ref[...] = _ffn_rows(x1_ref[...], oa_ref[...], ng_ref, xo_ref, wi_ref, wo_ref, d_ff)


def _ffn(x1, oa, ng, xo, wi, wo, tile):
    rows = x1.shape[0]
    d_ff = wo.shape[0]
    assert d_ff % (FFN_CHUNKS * LANES) == 0
    const = lambda i: (0, 0)
    once = dict(pipeline_mode=pl.Buffered(1))
    return pl.pallas_call(
        functools.partial(_ffn_kernel, d_ff=d_ff),
        grid=(rows // tile,),
        in_specs=[pl.BlockSpec((tile, D_MODEL), lambda i: (i, 0)),
                  pl.BlockSpec((tile, MEM_WIDTH), lambda i: (i, 0)),
                  pl.BlockSpec((6, D_MODEL), const),
                  pl.BlockSpec((MEM_WIDTH, D_MODEL), const, **once),
                  pl.BlockSpec((D_MODEL, 2 * d_ff), const, **once),
                  pl.BlockSpec((d_ff, D_MODEL), const, **once)],
        out_specs=pl.BlockSpec((tile, D_MODEL), lambda i: (i, 0)),
        out_shape=jax.ShapeDtypeStruct((rows, D_MODEL), F32),
        compiler_params=_params("arbitrary"),
    )(x1, oa, ng, xo, wi, wo)


PAGES_PER_STEP = 32
FOLD_PAGES = 8
NEW_TILE = 8
CMP_GROUP = 16


def _bias_rows(dist, tblt_ref):
    v = jnp.broadcast_to(tblt_ref[:, 0:1], dist.shape)
    for k in range(1, N_BUCKETS):
        v = jnp.where(dist >= _BUCKET_START[k], tblt_ref[:, k:k + 1], v)
    return v


def _sample_bias_kernel(tblt_ref, bw_ref, bc_ref, bn_ref, bs_ref, b0_ref, *, past_len):
    dist = (WINDOW - 1) - _iota2(bw_ref.shape, 1)
    bw_ref[...] = jnp.where((dist >= 0) & (dist < WINDOW), _bias_rows(dist, tblt_ref), NEG)
    dist = past_len - (_iota2(bc_ref.shape, 1) * CMP_BLOCK + (CMP_BLOCK - 1))
    bc_ref[...] = jnp.where(dist >= 0, _bias_rows(dist, tblt_ref), NEG)
    lane = _iota2(bn_ref.shape, 1)
    dist = past_len - ((past_len // CMP_BLOCK + lane) * CMP_BLOCK + (CMP_BLOCK - 1))
    bn_ref[...] = jnp.where((dist >= 0) & (lane < SEL_BLOCK // CMP_BLOCK),
                            _bias_rows(jnp.maximum(dist, 0), tblt_ref), NEG)
    page = lax.broadcasted_iota(I32, bs_ref.shape, 0)
    dist = past_len - page * PAGE_SIZE - lax.broadcasted_iota(I32, bs_ref.shape, 2)
    bs_ref[...] = _bias_rows(dist, tblt_ref)
    b0_ref[...] = _bias_rows(jnp.zeros(b0_ref.shape, I32), tblt_ref)


def _sample_bias(rel_bias, past_len):
    vm = pl.BlockSpec(memory_space=pltpu.VMEM)
    shapes = [(N_HEADS, WINDOW), (N_HEADS, past_len // CMP_BLOCK), (N_HEADS, LANES),
              (past_len // PAGE_SIZE, N_HEADS, PAGE_SIZE), (N_HEADS, LANES)]
    return pl.pallas_call(
        functools.partial(_sample_bias_kernel, past_len=past_len),
        in_specs=[vm], out_specs=[vm] * len(shapes),
        out_shape=[jax.ShapeDtypeStruct(s, F32) for s in shapes],
    )(rel_bias.T)


def _compress_sample_kernel(pt_ref, pool_ref, post_ref, fold_ref, projt_ref, o_ref, buf, sem, *, n_steps):
    i = pl.program_id(0)
    slot = i % GATHER_SLOTS
    ahead = GATHER_SLOTS - 1

    def page_copies(step, slot):
        return [pltpu.make_async_copy(pool_ref.at[pt_ref[step * PAGES_PER_STEP + j]], buf.at[slot, j], sem.at[slot])
                for j in range(PAGES_PER_STEP)]

    @pl.when(i == 0)
    def _():
        for step in range(min(ahead, n_steps)):
            for c in page_copies(step, step):
                c.start()

    @pl.when(i + ahead < n_steps)
    def _():
        for c in page_copies(i + ahead, (i + ahead) % GATHER_SLOTS):
            c.start()

    for c in page_copies(i, slot):
        c.wait()

    post = post_ref[...]
    sums_t = None
    for c in range(0, PAGES_PER_STEP, FOLD_PAGES):
        weighted = jnp.concatenate([(buf[slot, j] * post).astype(BF16) for j in range(c, c + FOLD_PAGES)], axis=1)
        part = jnp.dot(weighted, fold_ref[c * PAGE_SIZE:(c + FOLD_PAGES) * PAGE_SIZE, :],
                       preferred_element_type=F32)
        sums_t = part if sums_t is None else sums_t + part
    o_ref[0] = _dot3(projt_ref[...], sums_t)


def _compress_sample(pool_t, page_table_flat, n_seq, n_pages, pos, proj):
    steps = n_pages // PAGES_PER_STEP
    blocks_per_page = PAGE_SIZE // CMP_BLOCK
    blocks_per_step = PAGES_PER_STEP * blocks_per_page
    assert blocks_per_step == LANES
    pos_t = jnp.tile(pos.T, (1, blocks_per_page))
    row = np.arange(PAGES_PER_STEP * PAGE_SIZE)
    fold = jnp.asarray(row[:, None] // CMP_BLOCK == np.arange(blocks_per_step)[None, :], dtype=BF16)

    const = lambda i, pt: (0, 0)
    grid_spec = pltpu.PrefetchScalarGridSpec(
        num_scalar_prefetch=1,
        grid=(n_seq * steps,),
        in_specs=[pl.BlockSpec(memory_space=pl.ANY),
                  pl.BlockSpec((KV_COLS, PAGE_SIZE), const),
                  pl.BlockSpec(fold.shape, const),
                  pl.BlockSpec((KV_COLS, KV_COLS), const)],
        out_specs=pl.BlockSpec((1, KV_COLS, blocks_per_step), lambda i, pt: (i // steps, 0, i % steps)),
        scratch_shapes=[pltpu.VMEM((GATHER_SLOTS, PAGES_PER_STEP, KV_COLS, PAGE_SIZE), F32),
                        pltpu.SemaphoreType.DMA((GATHER_SLOTS,))],
    )
    return pl.pallas_call(
        functools.partial(_compress_sample_kernel, n_steps=n_seq * steps),
        grid_spec=grid_spec,
        out_shape=jax.ShapeDtypeStruct((n_seq, KV_COLS, steps * blocks_per_step), F32),
        compiler_params=_params("arbitrary"),
    )(page_table_flat, pool_t, pos_t, fold, proj.T)


def _cmp_sample_kernel(qm_ref, blk_ref, kvc_ref, pos_ref, proj_ref, bc_ref, bn_ref, oc_ref, idx_ref,
                       *, past_len):
    n_cmp = blk_ref.shape[2]
    bias = jnp.concatenate([bc_ref[...]] * CMP_GROUP, axis=0)
    mask = bias > MASKED_BELOW
    bias_n = jnp.concatenate([bn_ref[:, :NEW_TILE]] * CMP_GROUP, axis=0)
    mask_n = bias_n > MASKED_BELOW
    new_blk = _dot3(jnp.concatenate([kvc_ref[i] for i in range(CMP_GROUP)], axis=0) * pos_ref[0:1], proj_ref[...])
    first_row = _iota2((NEW_TILE, KV_COLS), 0) == 0
    new_blks = [jnp.where(first_row, jnp.broadcast_to(new_blk[i:i + 1], first_row.shape), 0.0)
                for i in range(CMP_GROUP)]
    s = jnp.concatenate([_dot3(qm_ref[i], blk_ref[i, :KV_HALF, :]) for i in range(CMP_GROUP)], axis=0)
    s_n = jnp.concatenate([_dot3_nt(qm_ref[i], new_blks[i][:, :KV_HALF]) for i in range(CMP_GROUP)], axis=0)
    s = jnp.where(mask, s * SCALE + bias, NEG)
    s_n = jnp.where(mask_n, s_n * SCALE + bias_n, NEG)
    m = jnp.maximum(jnp.max(s, axis=-1, keepdims=True), jnp.max(s_n, axis=-1, keepdims=True))
    e = jnp.where(mask, jnp.exp(s - m), 0.0)
    e_n = jnp.where(mask_n, jnp.exp(s_n - m), 0.0)
    inv = 1.0 / jnp.maximum(jnp.sum(e, axis=-1, keepdims=True) + jnp.sum(e_n, axis=-1, keepdims=True), TINY)
    p = e * inv
    p_n = e_n * inv
    group_ps = []
    for i in range(CMP_GROUP):
        rows = slice(i * N_HEADS, (i + 1) * N_HEADS)
        oc_ref[i] = _bdot_nt(p[rows], blk_ref[i, KV_HALF:, :]) + _bdot(p_n[rows], new_blks[i][:, KV_HALF:])
        group_ps += [jnp.sum(p[i * N_HEADS + g * GQA:i * N_HEADS + (g + 1) * GQA], axis=0, keepdims=True)
                     for g in range(N_KV)]

    ps = jnp.concatenate(group_ps, axis=0)
    lane = _iota2(ps.shape, 1)
    pair = ps + pltpu.roll(ps, n_cmp - 1, 1)
    cur = past_len // SEL_BLOCK
    j = lane // (SEL_BLOCK // CMP_BLOCK)
    forced = (j == 0) | (j == cur) | (j == cur - 1)
    score = jnp.where(lane % 2 == 0, jnp.where(forced, pair + FORCE, pair), -jnp.inf)
    slot = _iota2((ps.shape[0], LANES), 1)
    ids = jnp.zeros((ps.shape[0], LANES), I32)
    for it in range(N_SEL - 1):
        top = jnp.max(score, axis=-1, keepdims=True)
        first = jnp.min(jnp.where(score == top, lane, n_cmp), axis=-1, keepdims=True)
        ids = jnp.where(slot == it, first // (SEL_BLOCK // CMP_BLOCK), ids)
        score = jnp.where(lane == first, -jnp.inf, score)
    idx_ref[...] = jnp.where(slot == N_SEL - 1, cur, ids)


def _cmp_sample(qm, blocks_t, kvc_new, pos, proj, bc, bn, past_len):
    n_seq, _, n_cmp = blocks_t.shape
    assert past_len // SEL_BLOCK > N_SEL and past_len % SEL_BLOCK == 0
    const = lambda b: (0, 0)
    per_seq = lambda b: (b, 0, 0)
    return pl.pallas_call(
        functools.partial(_cmp_sample_kernel, past_len=past_len),
        grid=(n_seq // CMP_GROUP,),
        in_specs=[pl.BlockSpec((CMP_GROUP, N_HEADS, LANES), per_seq),
                  pl.BlockSpec((CMP_GROUP, KV_COLS, n_cmp), per_seq),
                  pl.BlockSpec((CMP_GROUP, 1, KV_COLS), per_seq),
                  pl.BlockSpec((CMP_BLOCK, KV_COLS), const),
                  pl.BlockSpec((KV_COLS, KV_COLS), const),
                  pl.BlockSpec((N_HEADS, n_cmp), const),
                  pl.BlockSpec((N_HEADS, LANES), const)],
        out_specs=[pl.BlockSpec((CMP_GROUP, N_HEADS, LANES), per_seq),
                   pl.BlockSpec((CMP_GROUP * N_KV, LANES), lambda b: (b, 0))],
        out_shape=[jax.ShapeDtypeStruct((n_seq, N_HEADS, LANES), F32),
                   jax.ShapeDtypeStruct((n_seq * N_KV, LANES), I32)],
        compiler_params=_params("arbitrary"),
    )(qm, blocks_t, kvc_new, pos, proj, bc, bn)


def _sel_win_sample_kernel(idx_ref, pt_ref, pool_ref, buf_ref, qm_ref, kvs_ref, kvw_ref, gate_ref, oc_ref, bw_ref,
                           bs_ref, b0_ref, o_ref, nbuf_ref, pages, sem, *, n_past_blocks, n_seq):
    b = pl.program_id(0)
    slot = b % GATHER_SLOTS
    ahead = GATHER_SLOTS - 1
    bpp = PAGE_SIZE // SEL_BLOCK

    def page_copies(seq, slot):
        return [pltpu.make_async_copy(pool_ref.at[pt_ref[(seq * N_KV + g) * N_SEL + k], :, g],
                                      pages.at[slot, g * N_SEL + k], sem.at[slot])
                for g in range(N_KV) for k in range(N_SEL)]

    @pl.when(b == 0)
    def _():
        for seq in range(min(ahead, n_seq)):
            for c in page_copies(seq, seq):
                c.start()

    @pl.when(b + ahead < n_seq)
    def _():
        for c in page_copies(b + ahead, (b + ahead) % GATHER_SLOTS):
            c.start()

    qm = qm_ref[0]

    buf = buf_ref[0]
    shifted = pltpu.roll(buf, WINDOW - 1, 1)
    nbuf = jnp.where(_iota2(buf.shape, 1) == WINDOW - 1, jnp.broadcast_to(kvw_ref[0], buf.shape), shifted)
    nbuf_ref[0] = nbuf
    bias = bw_ref[...]
    s_win = jnp.where(bias > MASKED_BELOW, _bdot(qm, nbuf[:KV_HALF]) * SCALE + bias, NEG)

    for c in page_copies(b, slot):
        c.wait()
    lane_half = _iota2((GQA, PAGE_SIZE), 1) // SEL_BLOCK
    n_keys = N_SEL * PAGE_SIZE
    scores = [jnp.concatenate([s_win, jnp.full((N_HEADS, n_keys - WINDOW), NEG, F32)], axis=1)]
    new_scores = [jnp.full((N_HEADS, 1), NEG, F32)]
    values, new_values = [], []
    for g in range(N_KV):
        cols = slice(g * HEAD_DIM, (g + 1) * HEAD_DIM)
        rows = slice(g * GQA, (g + 1) * GQA)
        qg = qm[rows, cols]
        ks, vs, biases = [], [], []
        has_new = False
        for k in range(N_SEL):
            i = idx_ref[(b * N_KV + g) * N_SEL + k]
            is_past = i < n_past_blocks
            has_new = jnp.logical_or(has_new, jnp.logical_not(is_past))
            i = jnp.minimum(i, n_past_blocks - 1)
            ks.append(pages[slot, g * N_SEL + k, 0])
            vs.append(pages[slot, g * N_SEL + k, 1])
            tile = bs_ref[i // bpp]
            biases.append(jnp.where(is_past & (lane_half == i % bpp), tile[rows], NEG))
        bias = jnp.concatenate(biases, axis=1)
        scores.append(jnp.where(bias > MASKED_BELOW, _bdot(qg, jnp.concatenate(ks, axis=1)) * SCALE + bias, NEG))
        values.append(jnp.concatenate(vs, axis=1))
        k_new = kvs_ref[0][:, cols]
        new_values.append(kvs_ref[0][:, KV_HALF + g * HEAD_DIM:KV_HALF + (g + 1) * HEAD_DIM])
        new_scores.append(jnp.where(
            has_new, jnp.sum(qg * k_new, axis=-1, keepdims=True) * SCALE + b0_ref[rows, 0:1], NEG))

    s = jnp.concatenate(scores, axis=0)
    s_new = jnp.concatenate(new_scores, axis=0)
    mask, mask_new = s > MASKED_BELOW, s_new > MASKED_BELOW
    m = jnp.maximum(jnp.max(s, axis=-1, keepdims=True), s_new)
    e = jnp.where(mask, jnp.exp(s - m), 0.0)
    e_new = jnp.where(mask_new, jnp.exp(s_new - m), 0.0)
    inv = 1.0 / jnp.maximum(jnp.sum(e, axis=-1, keepdims=True) + e_new, TINY)
    p, p_new = e * inv, e_new * inv
    o_win = _bdot_nt(p[:N_HEADS, :WINDOW], nbuf[KV_HALF:])
    o_sel = jnp.concatenate(
        [_bdot_nt(p[N_HEADS + g * GQA:N_HEADS + (g + 1) * GQA], values[g])
         + p_new[N_HEADS + g * GQA:N_HEADS + (g + 1) * GQA] * new_values[g] for g in range(N_KV)], axis=0)

    first_group = _iota2((N_HEADS, HEAD_DIM), 0) < GQA
    oc = oc_ref[0]
    o_cmp = jnp.where(first_group, oc[:, :HEAD_DIM], oc[:, HEAD_DIM:])
    o_win = jnp.where(first_group, o_win[:, :HEAD_DIM], o_win[:, HEAD_DIM:])
    gate = gate_ref[0]
    o_ref[0] = gate[:, 0:1] * o_cmp + gate[:, 1:2] * o_sel + gate[:, 2:3] * o_win


def _sel_win_sample(idx_flat, pt_flat, pool_t, buf_t, qm, kvs_new, kvw_new_col, gates, o_cmp, bw, bs, b0,
                    n_pages):
    n_seq = buf_t.shape[0]
    bpp = PAGE_SIZE // SEL_BLOCK
    n_past_blocks = n_pages * bpp

    per_seq3 = lambda b, idx, pt: (b, 0, 0)
    const2 = lambda b, idx, pt: (0, 0)
    head_tile = pl.BlockSpec((1, N_HEADS, LANES), per_seq3)
    grid_spec = pltpu.PrefetchScalarGridSpec(
        num_scalar_prefetch=2,
        grid=(n_seq,),
        in_specs=[pl.BlockSpec(memory_space=pl.ANY),
                  pl.BlockSpec((1, KV_COLS, WINDOW), per_seq3), head_tile,
                  pl.BlockSpec((1, 1, KV_COLS), per_seq3), pl.BlockSpec((1, KV_COLS, 1), per_seq3), head_tile,
                  head_tile,
                  pl.BlockSpec((N_HEADS, WINDOW), const2),
                  pl.BlockSpec((n_pages, N_HEADS, PAGE_SIZE), lambda b, idx, pt: (0, 0, 0)),
                  pl.BlockSpec((N_HEADS, LANES), const2)],
        out_specs=[pl.BlockSpec((1, N_HEADS, HEAD_DIM), per_seq3),
                   pl.BlockSpec((1, KV_COLS, WINDOW), per_seq3)],
        scratch_shapes=[pltpu.VMEM((GATHER_SLOTS, N_KV * N_SEL, 2, HEAD_DIM, PAGE_SIZE), F32),
                        pltpu.SemaphoreType.DMA((GATHER_SLOTS,))],
    )
    return pl.pallas_call(
        functools.partial(_sel_win_sample_kernel, n_past_blocks=n_past_blocks, n_seq=n_seq),
        grid_spec=grid_spec,
        out_shape=[jax.ShapeDtypeStruct((n_seq, N_HEADS, HEAD_DIM), F32),
                   jax.ShapeDtypeStruct(buf_t.shape, F32)],
        compiler_params=_params("arbitrary"),
    )(idx_flat, pt_flat, pool_t, buf_t, qm, kvs_new, kvw_new_col, gates, o_cmp, bw, bs, b0)


PROJ_TILE = 512
ROW_TILE = 256
TAIL_TILE = 512
SCAN_STEPS = 128


def kernel(x_prompt, x_sample, cache_kv_cmp, cache_kv_sel, cache_kv_win, state_ssm_re, state_ssm_im,
           cache_mem_kv, page_table, mem_prompt, w_in, w_out, norm_g, grp_norm_g, cmp_pos, cmp_proj,
           rel_bias, ssm_a_re, ssm_a_im, ssm_b_re, ssm_b_im, ssm_c_re, ssm_c_im, ssm_d, ssm_log_dt,
           glu_w, glu_b, mem_norm_g, xq, xkv, xo, ffn_wi, ffn_wo):
    depth = w_in.shape[0]
    n_seq, seq, _ = x_prompt.shape
    n_dec, dec_seq, _ = x_sample.shape
    assert dec_seq == 1, "the sample kernels handle one new token per sequence"
    n_pages = page_table.shape[1]
    past_len = n_pages * PAGE_SIZE
    n_mem = mem_prompt.shape[1]
    assert cache_kv_win.shape[2] == WINDOW and seq >= WINDOW and WINDOW % Q_TILE == 0
    win_tiles = WINDOW // Q_TILE

    bcmp, near = _prompt_bias(rel_bias, seq)
    bw_s, bc_s, bn_s, bs_s, b0_s = _sample_bias(rel_bias, past_len)
    pt_flat = page_table.reshape(-1)
    kv5 = (2, N_KV, HEAD_DIM)

    xp = x_prompt.reshape(n_seq * seq, D_MODEL)
    xs = x_sample.reshape(n_dec, D_MODEL)
    outs = [[] for _ in range(11)]
    for l in range(depth):
        ng = norm_g[l]
        w_row, w_t = _pack_w_prompt(w_in[l])
        w_sample = _pack_w_sample(w_in[l])
        w_out_b, xq_b, xkv_b, xo_b = (w[l].astype(BF16) for w in (w_out, xq, xkv, xo))
        wi_b, wo_b, glu_w_b = ffn_wi[l].astype(BF16), ffn_wo[l].astype(BF16), glu_w[l].astype(BF16)
        gg = grp_norm_g[l][None]
        ssm_w = _ssm_weights(ssm_a_re[l], ssm_a_im[l], ssm_b_re[l], ssm_b_im[l], ssm_c_re[l], ssm_c_im[l],
                             ssm_log_dt[l])
        ssm_tail = (ssm_d[l][None], glu_w_b, glu_b[l][None])
        pos, proj_g = _pack_compress(cmp_pos[l], cmp_proj[l], group_major=True)
        _, proj_c = _pack_compress(cmp_pos[l], cmp_proj[l], group_major=False)

        (kvc, ks_aug, kw_aug, u, q_t, kvc_t, kvs_t, kvw_t, vs_t, vw_t, gates_t) = _inproj_prompt(
            xp, ng[0:1], w_row, w_t, n_seq, PROJ_TILE)
        kblk, vblk_t = _compress_prompt(kvc.reshape(n_seq, seq, KV_COLS), pos, proj_g)
        kw_aug = jnp.pad(kw_aug.reshape(n_seq, seq, N_KV * LANES), ((0, 0), (WINDOW, 0), (0, 0)))
        vw_t = jnp.pad(vw_t, ((0, 0), (win_tiles, 0), (0, 0), (0, 0)))
        ks_aug = jnp.pad(ks_aug.reshape(n_seq, seq, N_KV * LANES), ((0, 0), (Q_TILE, 0), (0, 0)))
        vs_t = jnp.pad(vs_t, ((0, 0), (1, 0), (0, 0), (0, 0)))
        o_nsa = _nsa_prompt(rel_bias, q_t, gates_t, kblk, vblk_t, bcmp, near, ks_aug, vs_t, kw_aug, vw_t)
        zeros = jnp.zeros((n_seq, N_STATE), F32)
        o_ssm, h_re, h_im = _ssm(u, zeros, zeros, ssm_w, *ssm_tail, n_seq, SCAN_STEPS)
        mkv = _norm_matmul(mem_prompt.reshape(n_seq * n_mem, D_MODEL), mem_norm_g[l][None], xkv_b, ROW_TILE)
        xp = _tail_prompt(o_nsa.reshape(n_seq * seq, NSA_WIDTH), o_ssm, xp,
                          mkv.reshape(n_seq, n_mem, 2 * MEM_WIDTH), gg, w_out_b, ng, xq_b, xo_b, wi_b, wo_b,
                          n_seq, TAIL_TILE)
        outs[0].append(_rows_minor(kvc_t))
        outs[1].append(_rows_minor(kvs_t))
        outs[2].append(_rows_minor(kvw_t[:, :, seq - WINDOW:]))
        outs[3].append(h_re.reshape(n_seq, N_SSM_GROUPS, SSM_STATE))
        outs[4].append(h_im.reshape(n_seq, N_SSM_GROUPS, SSM_STATE))
        outs[5].append(mkv.reshape(n_seq, n_mem, 2, MEM_HEADS, MEM_HEAD_DIM))

        q, kv, gates, u = _inproj_sample(xs, ng[0:1], w_sample)
        kvc, kvs, kvw = (kv[:, i * KV_COLS:(i + 1) * KV_COLS] for i in range(3))
        head_group = (jnp.arange(N_HEADS)[:, None] // GQA == jnp.arange(N_KV)[None, :]).astype(F32)
        qm = (q.reshape(n_dec, N_HEADS, 1, HEAD_DIM) * head_group[None, :, :, None]).reshape(n_dec, N_HEADS, LANES)
        gates_h = gates[:, :N_HEADS * N_BRANCH].reshape(n_dec, N_HEADS, N_BRANCH)
        gates_h = jnp.pad(gates_h, ((0, 0), (0, 0), (0, LANES - N_BRANCH)))
        blocks_t = _compress_sample(_rows_minor_view(cache_kv_cmp[l]), pt_flat, n_dec, n_pages, pos, proj_c)
        o_cmp, ids = _cmp_sample(qm, blocks_t, kvc.reshape(n_dec, 1, KV_COLS), pos, proj_c, bc_s, bn_s, past_len)
        ids = ids[:, :N_SEL]
        past_page = jnp.minimum(ids, past_len // SEL_BLOCK - 1) // (PAGE_SIZE // SEL_BLOCK)
        sel_pages = jnp.take_along_axis(jnp.repeat(page_table, N_KV, axis=0), past_page, axis=1)
        pool_sel = cache_kv_sel[l].transpose(0, 2, 3, 4, 1)
        o_nsa, new_buf_t = _sel_win_sample(
            ids.reshape(-1), sel_pages.reshape(-1), pool_sel, _rows_minor_view(cache_kv_win[l]), qm, kvs.reshape(n_dec, 1, KV_COLS),
            kvw.reshape(n_dec, KV_COLS, 1), gates_h, o_cmp, bw_s, bs_s, b0_s, n_pages)
        o_ssm, h_re, h_im = _ssm(u, state_ssm_re[l].reshape(n_dec, N_STATE), state_ssm_im[l].reshape(n_dec, N_STATE),
                                 ssm_w, *ssm_tail, n_dec, 1)
        x1, qx = _merge(o_nsa.reshape(n_dec, NSA_WIDTH), o_ssm, xs, gg, w_out_b, ng, xq_b, 1, n_dec)
        oa = _cross_sample(qx, cache_mem_kv[l].reshape(n_dec, n_mem * 2 * MEM_HEADS, MEM_HEAD_DIM), n_mem)
        xs = _ffn(x1, oa, ng, xo_b, wi_b, wo_b, n_dec)
        outs[6].append(kvc.reshape(n_dec, 1, *kv5))
        outs[7].append(kvs.reshape(n_dec, 1, *kv5))
        outs[8].append(_rows_minor(new_buf_t))
        outs[9].append(h_re.reshape(n_dec, N_SSM_GROUPS, SSM_STATE))
        outs[10].append(h_im.reshape(n_dec, N_SSM_GROUPS, SSM_STATE))

    stacked = [jnp.stack(o, axis=0) for o in outs]
    return (xp.reshape(x_prompt.shape), xs.reshape(x_sample.shape), *stacked)
```
